```python
import math
import jax, jax.numpy as jnp
from jax import lax
import numpy as np

D_MODEL = 1024
BATCH = 32
SEQ = 256
DEPTH = 2
DEC_BATCH = 2
DEC_SEQ = 1024
PAST_LEN = 256

GRID_W = 64
MLA_HEADS = 4
MLA_NOPE = 64
MLA_ROPE = 32
MLA_V = 64
MLA_Q_LORA = 256
MLA_KV_LORA = 128
W_MLA = MLA_HEADS * MLA_V
W_CONV = 256
CONV_K = 3
DIFF_HEADS = 4
DIFF_QK = 64
DIFF_V = 2 * DIFF_QK
W_DIFF = DIFF_HEADS * DIFF_V
D_MIX = W_MLA + W_CONV + W_DIFF
IN_SIZES = (MLA_Q_LORA, MLA_KV_LORA, MLA_ROPE,
            W_CONV, W_CONV, W_CONV,
            DIFF_HEADS * 2 * DIFF_QK, DIFF_HEADS * 2 * DIFF_QK, W_DIFF,
            D_MIX)
D_IN = sum(IN_SIZES)
ROPE_THETA = 10000.0
Q_BLOCK = 128
EPS = 1e-6

kernel_name = "hybrid_mla_shortconv_diffattn_prefix_dit"


def _split_offsets():
    acc, offs = 0, []
    for s in IN_SIZES[:-1]:
        acc += s
        offs.append(acc)
    return offs


def rmsnorm(x, g):
    xf = x.astype(jnp.float32)
    y = xf * lax.rsqrt(jnp.mean(xf * xf, axis=-1, keepdims=True) + EPS)
    return (y * g.astype(jnp.float32)).astype(x.dtype)


def grid_positions(n):
    rows = n // GRID_W
    row = jnp.broadcast_to(jnp.arange(rows)[:, None], (rows, GRID_W)).reshape(n)
    col = jnp.broadcast_to(jnp.arange(GRID_W)[None, :], (rows, GRID_W)).reshape(n)
    return row, col


def axial_tables(n, rot_dim):
    half = rot_dim // 2
    inv = ROPE_THETA ** (-(jnp.arange(0, half, 2, dtype=jnp.float32) / half))
    row, col = grid_positions(n)
    ang_r = row.astype(jnp.float32)[:, None] * inv
    ang_c = col.astype(jnp.float32)[:, None] * inv
    ang = jnp.concatenate([ang_r, ang_r, ang_c, ang_c], axis=-1)
    return jnp.cos(ang), jnp.sin(ang)


def _rot_half_axial(x):
    half = x.shape[-1] // 2
    q = half // 2
    xr, xc = x[..., :half], x[..., half:]
    rh = lambda t: jnp.concatenate([-t[..., q:], t[..., :q]], axis=-1)
    return jnp.concatenate([rh(xr), rh(xc)], axis=-1)


def apply_rope(x, cos, sin):
    xf = x.astype(jnp.float32)
    return (xf * cos + _rot_half_axial(xf) * sin).astype(x.dtype)


def map_query_blocks(f, q):
    b, n = q.shape[:2]
    nb = n // Q_BLOCK
    qb = q.reshape((b, nb, Q_BLOCK) + q.shape[2:]).swapaxes(0, 1)
    out = lax.map(f, qb)
    return out.swapaxes(0, 1).reshape((b, n) + out.shape[3:])


def mla_branch(c_q, c_kv, k_rope, q_norm, w_uq, kv_norm, w_ukv, rope, ctx):
    b, n = c_q.shape[:2]
    q = (rmsnorm(c_q, q_norm) @ w_uq).reshape(b, n, MLA_HEADS, MLA_NOPE + MLA_ROPE)
    ckv = rmsnorm(c_kv, kv_norm)
    if rope is not None:
        cos, sin = rope
        q = jnp.concatenate([q[..., :MLA_NOPE],
                             apply_rope(q[..., MLA_NOPE:], cos[:, None, :], sin[:, None, :])], axis=-1)
        k_rope_pos = apply_rope(k_rope, cos, sin)
    else:
        k_rope_pos = k_rope
    ckv_all, krope_all = ckv, k_rope_pos
    if ctx is not None:
        ckv_ctx, krope_ctx = ctx
        ckv_all = jnp.concatenate([ckv_ctx.astype(ckv.dtype), ckv], axis=1)
        krope_all = jnp.concatenate([krope_ctx.astype(k_rope_pos.dtype), k_rope_pos], axis=1)
    m = ckv_all.shape[1]
    kv = (ckv_all @ w_ukv).reshape(b, m, MLA_HEADS, MLA_NOPE + MLA_V)
    k_nope, v = kv[..., :MLA_NOPE], kv[..., MLA_NOPE:]
    scale = (MLA_NOPE + MLA_ROPE) ** -0.5
    k_nope_f = k_nope.astype(jnp.float32)
    krope_f = krope_all.astype(jnp.float32)

    def blk(qb):
        qf = qb.astype(jnp.float32)
        s = (jnp.einsum('bqhd,bkhd->bhqk', qf[..., :MLA_NOPE], k_nope_f)
             + jnp.einsum('bqhr,bkr->bhqk', qf[..., MLA_NOPE:], krope_f))
        p = jax.nn.softmax(s * scale, axis=-1)
        return jnp.einsum('bhqk,bkhd->bqhd', p.astype(v.dtype), v)

    o = map_query_blocks(blk, q).reshape(b, n, W_MLA)
    return o, ckv, k_rope


def conv_branch(bg, cg, xc, conv_w):
    u = cg * xc
    n = u.shape[1]
    pad = CONV_K // 2
    up = jnp.pad(u, ((0, 0), (pad, pad), (0, 0)))
    y = sum(up[:, j:j + n] * conv_w[j] for j in range(CONV_K))
    return bg * y


def diff_branch(q, k, v, lam, subln, lambda_init, rope, ctx):
    b, n = q.shape[:2]
    q = q.reshape(b, n, DIFF_HEADS, 2, DIFF_QK)
    k = k.reshape(b, n, DIFF_HEADS, 2, DIFF_QK)
    v = v.reshape(b, n, DIFF_HEADS, DIFF_V)
    k_ctx_out = k
    if rope is not None:
        cos, sin = rope
        cq, sq = cos[:, None, None, :], sin[:, None, None, :]
        q = apply_rope(q, cq, sq)
        k = apply_rope(k, cq, sq)
    k_all, v_all = k, v
    if ctx is not None:
        k_ctx, v_ctx = ctx
        k_all = jnp.concatenate([k_ctx.astype(k.dtype), k], axis=1)
        v_all = jnp.concatenate([v_ctx.astype(v.dtype), v], axis=1)
    lf = lam.astype(jnp.float32)
    lam_full = jnp.exp(jnp.sum(lf[0] * lf[1])) - jnp.exp(jnp.sum(lf[2] * lf[3])) + lambda_init
    k_f = k_all.astype(jnp.float32)
    scale = DIFF_QK ** -0.5

    def blk(qb):
        s = jnp.einsum('bqhad,bkhad->bhaqk', qb.astype(jnp.float32), k_f) * scale
        p = jax.nn.softmax(s, axis=-1)
        a = p[:, :, 0] - lam_full * p[:, :, 1]
        return jnp.einsum('bhqk,bkhe->bqhe', a.astype(v_all.dtype), v_all)

    o = map_query_blocks(blk, q)
    o = rmsnorm(o, subln) * (1.0 - lambda_init)
    return o.reshape(b, n, W_DIFF), k_ctx_out, v


def trunk_layer(x, cond, layer_idx, p, rope_mla, rope_diff, ctx):
    mod = jax.nn.silu(cond) @ p['ada_w'] + p['ada_b']
    shift, scale, gate = jnp.split(mod, 3, axis=-1)
    h = rmsnorm(x, p['norm_g']) * (1.0 + scale) + shift
    parts = jnp.split(h @ p['w_in'], _split_offsets(), axis=-1)
    c_q, c_kv, k_rope, cb, cc, cx, dq, dk, dv, z = parts
    if ctx is None:
        ctx_mla, ctx_diff = None, None
    else:
        ctx_mla, ctx_diff = (ctx[0], ctx[1]), (ctx[2], ctx[3])
    o_mla, ckv, kr = mla_branch(c_q, c_kv, k_rope, p['q_norm'], p['w_uq'], p['kv_norm'], p['w_ukv'],
                                rope_mla, ctx_mla)
    o_conv = conv_branch(cb, cc, cx, p['conv_w'])
    lambda_init = 0.8 - 0.6 * math.exp(-0.3 * layer_idx)
    o_diff, k_d, v_d = diff_branch(dq, dk, dv, p['lam'], p['subln'], lambda_init, rope_diff, ctx_diff)
    mix = jnp.concatenate([o_mla, o_conv, o_diff], axis=-1) * jax.nn.silu(z)
    x = x + gate * (mix @ p['w_out'])
    return x, (ckv, kr, k_d, v_d)


def setup_inputs(seed: int = 0) -> dict:
    key = jax.random.key(seed)
    ks = jax.random.split(key, 24)
    f32 = jnp.float32
    nrm = lambda k, shape, s: jax.random.normal(k, shape, f32) * s
    gain = lambda k, shape: 1.0 + 0.1 * jax.random.normal(k, shape, f32)
    return {
        "x_prompt": nrm(ks[0], (BATCH, SEQ, D_MODEL), 1.0),
        "x_sample": nrm(ks[1], (DEC_BATCH, DEC_SEQ, D_MODEL), 1.0),
        "c": nrm(ks[2], (DEC_BATCH, D_MODEL), 1.0),
        "cache_mla_ckv": nrm(ks[3], (DEC_BATCH, DEPTH, PAST_LEN, MLA_KV_LORA), 1.0),
        "cache_mla_krope": nrm(ks[4], (DEC_BATCH, DEPTH, PAST_LEN, MLA_ROPE), 1.0),
        "cache_diff_k": nrm(ks[5], (DEC_BATCH, DEPTH, PAST_LEN, DIFF_HEADS, 2, DIFF_QK), 1.0),
        "cache_diff_v": nrm(ks[6], (DEC_BATCH, DEPTH, PAST_LEN, DIFF_HEADS, DIFF_V), 1.0),
        "c_ctx": nrm(ks[7], (D_MODEL,), 1.0),
        "norm_g": gain(ks[8], (DEPTH, D_MODEL)),
        "ada_w": nrm(ks[9], (DEPTH, D_MODEL, 3 * D_MODEL), D_MODEL ** -0.5),
        "ada_b": nrm(ks[10], (DEPTH, 3 * D_MODEL), 0.02),
        "w_in": nrm(ks[11], (DEPTH, D_MODEL, D_IN), D_MODEL ** -0.5),
        "mla_q_norm": gain(ks[12], (DEPTH, MLA_Q_LORA)),
        "w_uq": nrm(ks[13], (DEPTH, MLA_Q_LORA, MLA_HEADS * (MLA_NOPE + MLA_ROPE)), MLA_Q_LORA ** -0.5),
        "mla_kv_norm": gain(ks[14], (DEPTH, MLA_KV_LORA)),
        "w_ukv": nrm(ks[15], (DEPTH, MLA_KV_LORA, MLA_HEADS * (MLA_NOPE + MLA_V)), MLA_KV_LORA ** -0.5),
        "conv_w": nrm(ks[16], (DEPTH, CONV_K, W_CONV), CONV_K ** -0.5),
        "diff_lambda": nrm(ks[17], (DEPTH, 4, DIFF_QK), 0.1),
        "diff_subln": gain(ks[18], (DEPTH, DIFF_V)),
        "w_out": nrm(ks[19], (DEPTH, D_MIX, D_MODEL), D_MIX ** -0.5),
        "final_norm": gain(ks[20], (D_MODEL,)),
    }


def reference(x_prompt, x_sample, c, cache_mla_ckv, cache_mla_krope, cache_diff_k, cache_diff_v,
              c_ctx, norm_g, ada_w, ada_b, w_in, mla_q_norm, w_uq, mla_kv_norm, w_ukv,
              conv_w, diff_lambda, diff_subln, w_out, final_norm):
    n_lat = x_sample.shape[1]
    rope_mla = axial_tables(n_lat, MLA_ROPE)
    rope_diff = axial_tables(n_lat, DIFF_QK)
    cond_ctx = c_ctx[None, None, :]
    cond_lat = c[:, None, :]
    h_ctx, h_lat = x_prompt, x_sample
    ckvs, krs, dks, dvs = [], [], [], []
    for l in range(DEPTH):
        p = dict(norm_g=norm_g[l], ada_w=ada_w[l], ada_b=ada_b[l], w_in=w_in[l],
                 q_norm=mla_q_norm[l], w_uq=w_uq[l], kv_norm=mla_kv_norm[l], w_ukv=w_ukv[l],
                 conv_w=conv_w[l], lam=diff_lambda[l], subln=diff_subln[l], w_out=w_out[l])
        h_ctx, (ckv, kr, dk, dv) = trunk_layer(h_ctx, cond_ctx, l, p, None, None, None)
        ckvs.append(ckv)
        krs.append(kr)
        dks.append(dk)
        dvs.append(dv)
        ctx = (cache_mla_ckv[:, l], cache_mla_krope[:, l], cache_diff_k[:, l], cache_diff_v[:, l])
        h_lat, _ = trunk_layer(h_lat, cond_lat, l, p, rope_mla, rope_diff, ctx)
    y_prompt = rmsnorm(h_ctx, final_norm)
    y_sample = rmsnorm(h_lat, final_norm)
    return (y_prompt, y_sample, jnp.stack(ckvs, axis=1), jnp.stack(krs, axis=1),
            jnp.stack(dks, axis=1), jnp.stack(dvs, axis=1))
```

```python
import functools
import math

import jax
import jax.numpy as jnp
from jax import lax
from jax.experimental import pallas as pl
from jax.experimental.pallas import tpu as pltpu

D_MODEL = 1024
DEPTH = 2
GRID_W = 64
MLA_HEADS = 4
MLA_NOPE = 64
MLA_ROPE = 32
MLA_V = 64
MLA_Q_LORA = 256
MLA_KV_LORA = 128
W_MLA = MLA_HEADS * MLA_V
W_CONV = 256
DIFF_HEADS = 4
DIFF_QK = 64
DIFF_V = 2 * DIFF_QK
W_DIFF = DIFF_HEADS * DIFF_V
D_MIX = W_MLA + W_CONV + W_DIFF
ROPE_THETA = 10000.0
EPS = 1e-6

LANES = 128
Q_TILE = 256
MLA_SCALE = (MLA_NOPE + MLA_ROPE) ** -0.5
DIFF_SCALE = DIFF_QK ** -0.5
VMEM_LIMIT_BYTES = 58 * 1024 * 1024

C_Q = 0
C_KV = C_Q + MLA_Q_LORA
C_CB = C_KV + MLA_KV_LORA
C_CC = C_CB + W_CONV
C_CX = C_CC + W_CONV
C_DQ = C_CX + W_CONV
C_DK = C_DQ + W_DIFF
C_DV = C_DK + W_DIFF
C_Z = C_DV + W_DIFF
C_KR = C_Z + D_MIX
D_IN_P = C_KR + LANES

BF16 = jnp.bfloat16
F32 = jnp.float32


def _lambda_init(layer):
    return 0.8 - 0.6 * math.exp(-0.3 * layer)


def _dot(a, b):
    return jnp.dot(a, b, preferred_element_type=F32)


def _dot_nt(a, b):
    return lax.dot_general(a, b, (((1,), (1,)), ((), ())), preferred_element_type=F32)


def _rms(x, g):
    ms = jnp.mean(x * x, axis=-1, keepdims=True)
    return x * lax.rsqrt(ms + EPS) * g


def _softmax(s):
    m = jnp.max(s, axis=-1, keepdims=True)
    e = jnp.exp(s - m)
    return e * (1.0 / jnp.sum(e, axis=-1, keepdims=True))


def _lane_mask(lo, hi):
    lane = lax.broadcasted_iota(jnp.int32, (1, LANES), 1)
    return jnp.where((lane >= lo) & (lane < hi), 1.0, 0.0).astype(F32)


def _rope(x, tables, q):
    cos, sin_a, sin_b = tables
    tiles = []
    for t in range(x.shape[1] // LANES):
        xt = x[:, LANES * t:LANES * (t + 1)]
        tiles.append(xt * cos + pltpu.roll(xt, LANES - q, 1) * sin_a + pltpu.roll(xt, q, 1) * sin_b)
    return tiles[0] if len(tiles) == 1 else jnp.concatenate(tiles, axis=1)


def _attend(q, dq, keys, kcat_ref, vm_ref, dka_ref, dvb_ref, lam, subln, lam_init):
    qr = q[:, 2 * LANES:3 * LANES]
    o_mla = []
    for t in range(2):
        qcat = jnp.concatenate([q[:, LANES * t:LANES * (t + 1)], qr], axis=1)
        acc = None
        for hh in range(2):
            h = 2 * t + hh
            p = _softmax(_dot_nt(qcat, kcat_ref[h, keys, :])).astype(BF16)
            o = _dot(p, vm_ref[h, keys, :])
            acc = o if acc is None else acc + o
        o_mla.append(acc)
    o_diff = []
    for h in range(DIFF_HEADS):
        qt = dq[:, LANES * h:LANES * (h + 1)]
        p0 = _softmax(_dot_nt(qt, dka_ref[2 * h, keys, :]))
        p1 = _softmax(_dot_nt(qt, dka_ref[2 * h + 1, keys, :]))
        amap = (p0 - lam * p1).astype(BF16)
        o = _dot(amap, dvb_ref[keys, LANES * h:LANES * (h + 1)])
        o_diff.append(_rms(o, subln) * (1.0 - lam_init))
    return jnp.concatenate(o_mla, axis=1), jnp.concatenate(o_diff, axis=1)


def _trunk_layer(l, wl, x, n_seq, mod, w, scr, rope, ctx, state):
    m_rows = x.shape[0]
    n_b = m_rows // n_seq
    n_ctx = 0 if ctx is None else ctx["ckv"].shape[0]
    nk = n_ctx + n_seq
    shift, scale, gate = mod
    hb_ref, o_ref, mix_ref = scr["hb"], scr["o"], scr["mix"]
    qb_ref, dqb_ref = scr["qb"], scr["dqb"]
    kcat_ref, vm_ref, dka_ref, dvb_ref = scr["kcat"], scr["vm"], scr["dka"], scr["dvb"]
    w_in, w_ukv = w["w_in"], w["w_ukv"]
    half = (_lane_mask(0, 64), _lane_mask(64, 128))
    if rope is not None:
        rope_m = tuple(r[...] for r in rope[0])
        rope_d = tuple(r[...] for r in rope[1])

    def seq_rows(b):
        return slice(b * n_seq, (b + 1) * n_seq)

    gs = w["norm_g"][wl] * (1.0 + scale)
    ms = jnp.mean(x * x, axis=-1, keepdims=True)
    hb_ref[...] = (x * lax.rsqrt(ms + EPS) * gs + shift).astype(BF16)

    def proj(c0, c1):
        return _dot(hb_ref[...], w_in[wl, :, c0:c1])

    cqn = _rms(proj(C_Q, C_KV), w["q_norm"][wl]).astype(BF16)
    q = _dot(cqn, w["w_uq"][wl]) * MLA_SCALE
    if rope is not None:
        q = jnp.concatenate(
            [q[:, 0:2 * LANES], _rope(q[:, 2 * LANES:3 * LANES], rope_m, MLA_ROPE // 4)], axis=1)
    qb_ref[...] = q.astype(BF16)

    def fill_kv(b, dst, kv):
        for h in range(MLA_HEADS):
            t, hh = divmod(h, 2)
            kcat_ref[b, h, dst, 0:LANES] = (kv[:, LANES * t:LANES * (t + 1)] * half[hh]).astype(BF16)
            vm_ref[b, h, dst, :] = (
                kv[:, 2 * LANES + LANES * t:2 * LANES + LANES * (t + 1)] * half[hh]).astype(BF16)

    def fill_kr(b, dst, kr):
        for h in range(MLA_HEADS):
            kcat_ref[b, h, dst, LANES:2 * LANES] = (
                kr * _lane_mask(MLA_ROPE * h, MLA_ROPE * (h + 1))).astype(BF16)

    def fill_dk(b, dst, dk):
        for h in range(DIFF_HEADS):
            for a in range(2):
                dka_ref[b, 2 * h + a, dst, :] = (
                    dk[:, LANES * h:LANES * (h + 1)] * half[a]).astype(BF16)

    ckv = _rms(proj(C_KV, C_CB), w["kv_norm"][wl])
    kv = _dot(ckv.astype(BF16), w_ukv[wl])
    for b in range(n_b):
        if state is not None:
            state[0][b, l] = ckv[seq_rows(b)]
        fill_kv(b, pl.ds(n_ctx, n_seq), kv[seq_rows(b)])
        if ctx is not None:
            fill_kv(b, pl.ds(0, n_ctx), _dot(ctx["ckv"][...].astype(BF16), w_ukv[wl]))

    krr = proj(C_KR, D_IN_P)
    krr_pos = krr if rope is None else _rope(krr, rope_m, MLA_ROPE // 4)
    for b in range(n_b):
        if state is not None:
            state[1][b, l] = krr[seq_rows(b), 0:MLA_ROPE]
        fill_kr(b, pl.ds(n_ctx, n_seq), krr_pos[seq_rows(b)])
        if ctx is not None:
            fill_kr(b, pl.ds(0, n_ctx), ctx["kr"][...])

    dq = proj(C_DQ, C_DK) * DIFF_SCALE
    if rope is not None:
        dq = _rope(dq, rope_d, DIFF_QK // 4)
    dqb_ref[...] = dq.astype(BF16)

    dk = proj(C_DK, C_DV)
    dk_pos = dk if rope is None else _rope(dk, rope_d, DIFF_QK // 4)
    for b in range(n_b):
        if state is not None:
            state[2][b, l] = dk[seq_rows(b)]
        fill_dk(b, pl.ds(n_ctx, n_seq), dk_pos[seq_rows(b)])
        if ctx is not None:
            fill_dk(b, pl.ds(0, n_ctx), ctx["dk"][...])

    dv = proj(C_DV, C_Z)
    for b in range(n_b):
        if state is not None:
            state[3][b, l] = dv[seq_rows(b)]
        dvb_ref[b, pl.ds(n_ctx, n_seq), :] = dv[seq_rows(b)].astype(BF16)
        if ctx is not None:
            dvb_ref[b, pl.ds(0, n_ctx), :] = ctx["dv"][...].astype(BF16)

    lf = w["lam"][wl]
    lam = (jnp.exp(jnp.sum(lf[0:1] * lf[1:2], axis=-1, keepdims=True))
           - jnp.exp(jnp.sum(lf[2:3] * lf[3:4], axis=-1, keepdims=True)) + _lambda_init(l))
    subln = w["subln"][wl]
    keys = pl.ds(0, nk)
    for b in range(n_b):
        def q_block(start, b=b):
            qrows = pl.ds(start, Q_TILE)
            o_mla, o_diff = _attend(qb_ref[qrows, :], dqb_ref[qrows, :], keys,
                                    kcat_ref.at[b], vm_ref.at[b], dka_ref.at[b], dvb_ref.at[b],
                                    lam, subln, _lambda_init(l))
            o_ref[qrows, 0:W_MLA] = o_mla
            o_ref[qrows, W_MLA + W_CONV:D_MIX] = o_diff

        n_qb = n_seq // Q_TILE
        if n_qb == 1:
            q_block(b * n_seq)
        else:
            def q_step(i, carry, b=b, q_block=q_block):
                q_block(pl.multiple_of(b * n_seq + i * Q_TILE, Q_TILE))
                return carry
            lax.fori_loop(0, n_qb, q_step, 0)

    u = proj(C_CC, C_CX) * proj(C_CX, C_DQ)
    pos = lax.broadcasted_iota(jnp.int32, (m_rows, 1), 0) % n_seq
    u_prev = jnp.where(pos == 0, 0.0, pltpu.roll(u, 1, 0))
    u_next = jnp.where(pos == n_seq - 1, 0.0, pltpu.roll(u, m_rows - 1, 0))
    cw = w["conv_w"][wl]
    o_ref[:, W_MLA:W_MLA + W_CONV] = proj(C_CB, C_CC) * (
        u_prev * cw[0:1] + u * cw[1:2] + u_next * cw[2:3])

    for c0 in range(0, D_MIX, 512):
        z = proj(C_Z + c0, C_Z + c0 + 512)
        mix_ref[:, c0:c0 + 512] = (o_ref[:, c0:c0 + 512] * (z * jax.nn.sigmoid(z))).astype(BF16)
    return x + gate * _dot(mix_ref[...], w["w_out"][wl])


_W_NAMES = ("norm_g", "w_in", "q_norm", "w_uq", "kv_norm", "w_ukv", "conv_w", "lam", "subln",
            "w_out", "final_norm")
_SCR_NAMES = ("hb", "o", "mix", "qb", "dqb", "kcat", "vm", "dka", "dvb")
_CTX_NAMES = ("ckv", "kr", "dk", "dv")
N_ROPE = 6


def _ctx_kernel(x_ref, mod_ref, *refs, n_seq):
    w = dict(zip(_W_NAMES, refs[:len(_W_NAMES)]))
    rest = refs[len(_W_NAMES):]
    y_ref, state = rest[0], rest[1:5]
    scr = dict(zip(_SCR_NAMES, rest[5:]))
    n_b = x_ref.shape[0]
    x = x_ref[...].reshape(n_b * n_seq, D_MODEL)
    for l in range(DEPTH):
        mod = tuple(mod_ref[l, j:j + 1, :] for j in range(3))
        x = _trunk_layer(l, l, x, n_seq, mod, w, scr, None, None, state)
    y_ref[...] = _rms(x, w["final_norm"][...]).reshape(n_b, n_seq, D_MODEL)


def _lat_kernel(x_ref, mod_ref, *refs, n_seq, layer):
    ctx = {k: r.at[0, 0] for k, r in zip(_CTX_NAMES, refs[:4])}
    rope = (refs[4:7], refs[7:10])
    w = dict(zip(_W_NAMES, refs[4 + N_ROPE:4 + N_ROPE + len(_W_NAMES)]))
    rest = refs[4 + N_ROPE + len(_W_NAMES):]
    y_ref = rest[0]
    scr = dict(zip(_SCR_NAMES, rest[1:]))
    mod = tuple(mod_ref[0, 0, j:j + 1, :] for j in range(3))
    x = _trunk_layer(layer, 0, x_ref[0], n_seq, mod, w, scr, rope, ctx, None)
    if layer == DEPTH - 1:
        x = _rms(x, w["final_norm"][...])
    y_ref[0] = x


def _ada_kernel(cond_ref, w_ref, b_ref, o_ref):
    c = cond_ref[...]
    s = (c * jax.nn.sigmoid(c)).astype(BF16)
    o_ref[0] = _dot(s, w_ref[0].astype(BF16)) + b_ref[0]


def _const_spec(shape, layer=None):
    if layer is None:
        zeros = (0,) * len(shape)
        return pl.BlockSpec(shape, lambda *_: zeros, pipeline_mode=pl.Buffered(1))
    idx = (layer,) + (0,) * (len(shape) - 1)
    return pl.BlockSpec((1,) + tuple(shape[1:]), lambda *_: idx, pipeline_mode=pl.Buffered(1))


def _scratch(n_b, n_seq, nk):
    m_rows = n_b * n_seq
    return [
        pltpu.VMEM((m_rows, D_MODEL), BF16),
        pltpu.VMEM((m_rows, D_MIX), F32),
        pltpu.VMEM((m_rows, D_MIX), BF16),
        pltpu.VMEM((m_rows, 3 * LANES), BF16),
        pltpu.VMEM((m_rows, W_DIFF), BF16),
        pltpu.VMEM((n_b, MLA_HEADS, nk, 2 * LANES), BF16),
        pltpu.VMEM((n_b, MLA_HEADS, nk, LANES), BF16),
        pltpu.VMEM((n_b, 2 * DIFF_HEADS, nk, LANES), BF16),
        pltpu.VMEM((n_b, nk, W_DIFF), BF16),
    ]


def _axial_tables(n, rot_dim):
    half = rot_dim // 2
    inv = ROPE_THETA ** (-(jnp.arange(0, half, 2, dtype=F32) / half))
    rows = n // GRID_W
    row = jnp.broadcast_to(jnp.arange(rows)[:, None], (rows, GRID_W)).reshape(n)
    col = jnp.broadcast_to(jnp.arange(GRID_W)[None, :], (rows, GRID_W)).reshape(n)
    ang_r = row.astype(F32)[:, None] * inv
    ang_c = col.astype(F32)[:, None] * inv
    ang = jnp.concatenate([ang_r, ang_r, ang_c, ang_c], axis=-1)
    return jnp.cos(ang), jnp.sin(ang)


def _rope_tables(n, rot_dim):
    cos, sin = _axial_tables(n, rot_dim)
    q = rot_dim // 4
    first = (jnp.arange(rot_dim) % (2 * q)) < q
    sin_a = jnp.where(first, -sin, 0.0)
    sin_b = jnp.where(first, 0.0, sin)
    reps = LANES // rot_dim
    return tuple(jnp.tile(t, (1, reps)) for t in (cos, sin_a, sin_b))


def _prep_weights(w_in, w_uq, w_ukv, w_out):
    offs = [0]
    for s in (MLA_Q_LORA, MLA_KV_LORA, MLA_ROPE, W_CONV, W_CONV, W_CONV, W_DIFF, W_DIFF, W_DIFF,
              D_MIX):
        offs.append(offs[-1] + s)
    seg = lambda i: w_in[:, :, offs[i]:offs[i + 1]]
    w_in_p = jnp.concatenate(
        [seg(0), seg(1), seg(3), seg(4), seg(5), seg(6), seg(7), seg(8), seg(9)]
        + [seg(2)] * (LANES // MLA_ROPE), axis=-1).astype(BF16)
    uq = w_uq.reshape(DEPTH, MLA_Q_LORA, MLA_HEADS, MLA_NOPE + MLA_ROPE)
    w_uq_p = jnp.concatenate(
        [uq[..., :MLA_NOPE].reshape(DEPTH, MLA_Q_LORA, -1),
         uq[..., MLA_NOPE:].reshape(DEPTH, MLA_Q_LORA, -1)], axis=-1).astype(BF16)
    ukv = w_ukv.reshape(DEPTH, MLA_KV_LORA, MLA_HEADS, MLA_NOPE + MLA_V)
    w_ukv_p = jnp.concatenate(
        [ukv[..., :MLA_NOPE].reshape(DEPTH, MLA_KV_LORA, -1),
         ukv[..., MLA_NOPE:].reshape(DEPTH, MLA_KV_LORA, -1)], axis=-1).astype(BF16)
    return w_in_p, w_uq_p, w_ukv_p, w_out.astype(BF16)


def kernel(x_prompt, x_sample, c, cache_mla_ckv, cache_mla_krope, cache_diff_k, cache_diff_v,
           c_ctx, norm_g, ada_w, ada_b, w_in, mla_q_norm, w_uq, mla_kv_norm, w_ukv,
           conv_w, diff_lambda, diff_subln, w_out, final_norm):
    batch, seq, _ = x_prompt.shape
    dec_batch, dec_seq, _ = x_sample.shape
    past = cache_mla_ckv.shape[2]
    cparams = pltpu.CompilerParams(dimension_semantics=("arbitrary",),
                                   vmem_limit_bytes=VMEM_LIMIT_BYTES)

    n_cond = 8
    cond = jnp.concatenate(
        [c_ctx[None], c, jnp.zeros((n_cond - 1 - dec_batch, D_MODEL), F32)], axis=0)
    tn = 1024
    mod = pl.pallas_call(
        _ada_kernel,
        grid=(DEPTH, 3 * D_MODEL // tn),
        in_specs=[pl.BlockSpec((n_cond, D_MODEL), lambda l, j: (0, 0)),
                  pl.BlockSpec((1, D_MODEL, tn), lambda l, j: (l, 0, j)),
                  pl.BlockSpec((1, 1, tn), lambda l, j: (l, 0, j))],
        out_specs=pl.BlockSpec((1, n_cond, tn), lambda l, j: (l, 0, j)),
        out_shape=jax.ShapeDtypeStruct((DEPTH, n_cond, 3 * D_MODEL), F32),
        compiler_params=pltpu.CompilerParams(dimension_semantics=("arbitrary", "arbitrary")),
        name="ada",
    )(cond, ada_w, ada_b.reshape(DEPTH, 1, 3 * D_MODEL))
    mod = mod.reshape(DEPTH, n_cond, 3, D_MODEL)
    mod_ctx = mod[:, 0]
    mod_lat = jnp.swapaxes(mod[:, 1:1 + dec_batch], 0, 1)

    w_in_p, w_uq_p, w_ukv_p, w_out_b = _prep_weights(w_in, w_uq, w_ukv, w_out)
    row3 = lambda a: a.reshape(DEPTH, 1, a.shape[-1])
    weights = (row3(norm_g), w_in_p, row3(mla_q_norm), w_uq_p, row3(mla_kv_norm), w_ukv_p,
               conv_w, diff_lambda, row3(diff_subln), w_out_b, final_norm.reshape(1, D_MODEL))

    tb = 2
    y_prompt, s_ckv, s_kr, s_dk, s_dv = pl.pallas_call(
        functools.partial(_ctx_kernel, n_seq=seq),
        grid=(batch // tb,),
        in_specs=[pl.BlockSpec((tb, seq, D_MODEL), lambda i: (i, 0, 0)),
                  _const_spec(mod_ctx.shape)] + [_const_spec(a.shape) for a in weights],
        out_specs=[pl.BlockSpec((tb, seq, D_MODEL), lambda i: (i, 0, 0)),
                   pl.BlockSpec((tb, DEPTH, seq, MLA_KV_LORA), lambda i: (i, 0, 0, 0)),
                   pl.BlockSpec((tb, DEPTH, seq, MLA_ROPE), lambda i: (i, 0, 0, 0)),
                   pl.BlockSpec((tb, DEPTH, seq, W_DIFF), lambda i: (i, 0, 0, 0)),
                   pl.BlockSpec((tb, DEPTH, seq, W_DIFF), lambda i: (i, 0, 0, 0))],
        out_shape=[jax.ShapeDtypeStruct((batch, seq, D_MODEL), F32),
                   jax.ShapeDtypeStruct((batch, DEPTH, seq, MLA_KV_LORA), F32),
                   jax.ShapeDtypeStruct((batch, DEPTH, seq, MLA_ROPE), F32),
                   jax.ShapeDtypeStruct((batch, DEPTH, seq, W_DIFF), F32),
                   jax.ShapeDtypeStruct((batch, DEPTH, seq, W_DIFF), F32)],
        scratch_shapes=_scratch(tb, seq, seq),
        compiler_params=cparams,
        name="ctx",
    )(x_prompt, mod_ctx, *weights)

    tables = _rope_tables(dec_seq, MLA_ROPE) + _rope_tables(dec_seq, DIFF_QK)
    ctx_in = (cache_mla_ckv,
              jnp.tile(cache_mla_krope, (1, 1, 1, LANES // MLA_ROPE)),
              cache_diff_k.reshape(dec_batch, DEPTH, past, W_DIFF),
              cache_diff_v.reshape(dec_batch, DEPTH, past, W_DIFF))
    h_lat = x_sample
    for l in range(DEPTH):
        h_lat = pl.pallas_call(
            functools.partial(_lat_kernel, n_seq=dec_seq, layer=l),
            grid=(dec_batch,),
            in_specs=[pl.BlockSpec((1, dec_seq, D_MODEL), lambda i: (i, 0, 0)),
                      pl.BlockSpec((1, 1, 3, D_MODEL), lambda i, l=l: (i, l, 0, 0))]
            + [pl.BlockSpec((1, 1) + a.shape[2:], lambda i, l=l: (i, l, 0, 0)) for a in ctx_in]
            + [_const_spec(t.shape) for t in tables]
            + [_const_spec(a.shape, layer=l) for a in weights[:-1]]
            + [_const_spec(weights[-1].shape)],
            out_specs=pl.BlockSpec((1, dec_seq, D_MODEL), lambda i: (i, 0, 0)),
            out_shape=jax.ShapeDtypeStruct((dec_batch, dec_seq, D_MODEL), F32),
            scratch_shapes=_scratch(1, dec_seq, past + dec_seq),
            compiler_params=cparams,
            name=f"latent_{l}",
        )(h_lat, mod_lat, *ctx_in, *tables, *weights)

    return (y_prompt, h_lat, s_ckv, s_kr,
            s_dk.reshape(batch, DEPTH, seq, DIFF_HEADS, 2, DIFF_QK),
            s_dv.reshape(batch, DEPTH, seq, DIFF_HEADS, DIFF_V))
```

```python
import functools
import math

import jax
import jax.numpy as jnp
from jax import lax
from jax.experimental import pallas as pl
from jax.experimental.pallas import tpu as pltpu

D_MODEL = 1024
DEPTH = 2
GRID_W = 64
MLA_HEADS = 4
MLA_NOPE = 64
MLA_ROPE = 32
MLA_V = 64
MLA_Q_LORA = 256
MLA_KV_LORA = 128
W_MLA = MLA_HEADS * MLA_V
W_CONV = 256
DIFF_HEADS = 4
DIFF_QK = 64
DIFF_V = 2 * DIFF_QK
W_DIFF = DIFF_HEADS * DIFF_V
D_MIX = W_MLA + W_CONV + W_DIFF
ROPE_THETA = 10000.0
EPS = 1e-6

LANES = 128
Q_TILE = 256
LOG2E = math.log2(math.e)
MLA_SCALE = (MLA_NOPE + MLA_ROPE) ** -0.5 * LOG2E
DIFF_SCALE = DIFF_QK ** -0.5 * LOG2E
VMEM_LIMIT_BYTES = 58 * 1024 * 1024

C_Q = 0
C_KV = C_Q + MLA_Q_LORA
C_CB = C_KV + MLA_KV_LORA
C_CC = C_CB + W_CONV
C_CX = C_CC + W_CONV
C_DQ = C_CX + W_CONV
C_DK = C_DQ + W_DIFF
C_DV = C_DK + W_DIFF
C_Z = C_DV + W_DIFF
C_KR = C_Z + D_MIX
D_IN_P = C_KR + LANES

BF16 = jnp.bfloat16
F32 = jnp.float32


def _lambda_init(layer):
    return 0.8 - 0.6 * math.exp(-0.3 * layer)


def _dot(a, b):
    return jnp.dot(a, b, preferred_element_type=F32)


def _dot_nt(a, b):
    return lax.dot_general(a, b, (((1,), (1,)), ((), ())), preferred_element_type=F32)


def _rms(x, g):
    ms = jnp.mean(x * x, axis=-1, keepdims=True)
    return x * lax.rsqrt(ms + EPS) * g


def _exp_scores(s):
    m = s[:, 0:LANES]
    for t in range(1, s.shape[1] // LANES):
        m = jnp.maximum(m, s[:, LANES * t:LANES * (t + 1)])
    return jnp.exp2(s - jnp.max(m, axis=-1, keepdims=True)).astype(BF16)


def _weighted(e, v_ones):
    r = _dot(e, v_ones)
    return r[:, 0:LANES], 1.0 / r[:, LANES:2 * LANES]


def _lane_mask(lo, hi):
    lane = lax.broadcasted_iota(jnp.int32, (1, LANES), 1)
    return jnp.where((lane >= lo) & (lane < hi), 1.0, 0.0).astype(F32)


def _rope(x, tables, q):
    cos, sin_a, sin_b = tables
    tiles = []
    for t in range(x.shape[1] // LANES):
        xt = x[:, LANES * t:LANES * (t + 1)]
        tiles.append(xt * cos + pltpu.roll(xt, LANES - q, 1) * sin_a + pltpu.roll(xt, q, 1) * sin_b)
    return tiles[0] if len(tiles) == 1 else jnp.concatenate(tiles, axis=1)


def _attend(q, dq, keys, kcat_ref, vm_ref, dka_ref, dvb_ref, lam, subln, lam_init):
    qr = q[:, 2 * LANES:3 * LANES]
    o_mla = []
    for t in range(2):
        qcat = jnp.concatenate([q[:, LANES * t:LANES * (t + 1)], qr], axis=1)
        acc = None
        for hh in range(2):
            h = 2 * t + hh
            o, rinv = _weighted(_exp_scores(_dot_nt(qcat, kcat_ref[h, keys, :])),
                                vm_ref[h, keys, :])
            acc = o * rinv if acc is None else acc + o * rinv
        o_mla.append(acc)
    o_diff = []
    for h in range(DIFF_HEADS):
        qt = dq[:, LANES * h:LANES * (h + 1)]
        o0, rinv0 = _weighted(_exp_scores(_dot_nt(qt, dka_ref[2 * h, keys, :])),
                              dvb_ref[h, keys, :])
        o1, rinv1 = _weighted(_exp_scores(_dot_nt(qt, dka_ref[2 * h + 1, keys, :])),
                              dvb_ref[h, keys, :])
        o = o0 * rinv0 - o1 * (lam * rinv1)
        o_diff.append(_rms(o, subln) * (1.0 - lam_init))
    return jnp.concatenate(o_mla, axis=1), jnp.concatenate(o_diff, axis=1)


def _trunk_layer(l, wl, x, n_seq, mod, w, scr, rope, ctx, state):
    m_rows = x.shape[0]
    n_b = m_rows // n_seq
    n_ctx = 0 if ctx is None else ctx["ckv"].shape[0]
    nk = n_ctx + n_seq
    shift, scale, gate = mod
    hb_ref, o_ref, mix_ref = scr["hb"], scr["o"], scr["mix"]
    qb_ref, dqb_ref = scr["qb"], scr["dqb"]
    kcat_ref, vm_ref, dka_ref, dvb_ref = scr["kcat"], scr["vm"], scr["dka"], scr["dvb"]
    w_in, w_ukv = w["w_in"], w["w_ukv"]
    half = (_lane_mask(0, 64), _lane_mask(64, 128))
    if rope is not None:
        rope_m = tuple(r[...] for r in rope[0])
        rope_d = tuple(r[...] for r in rope[1])

    def seq_rows(b):
        return slice(b * n_seq, (b + 1) * n_seq)

    gs = w["norm_g"][wl] * (1.0 + scale)
    ms = jnp.mean(x * x, axis=-1, keepdims=True)
    hb_ref[...] = (x * lax.rsqrt(ms + EPS) * gs + shift).astype(BF16)

    def proj(c0, c1):
        return _dot(hb_ref[...], w_in[wl, :, c0:c1])

    cqn = _rms(proj(C_Q, C_KV), w["q_norm"][wl]).astype(BF16)
    q = _dot(cqn, w["w_uq"][wl]) * MLA_SCALE
    if rope is not None:
        q = jnp.concatenate(
            [q[:, 0:2 * LANES], _rope(q[:, 2 * LANES:3 * LANES], rope_m, MLA_ROPE // 4)], axis=1)
    qb_ref[...] = q.astype(BF16)

    def fill_kv(b, dst, kv):
        for h in range(MLA_HEADS):
            t, hh = divmod(h, 2)
            kcat_ref[b, h, dst, 0:LANES] = (kv[:, LANES * t:LANES * (t + 1)] * half[hh]).astype(BF16)
            vm_ref[b, h, dst, 0:LANES] = (
                kv[:, 2 * LANES + LANES * t:2 * LANES + LANES * (t + 1)] * half[hh]).astype(BF16)

    def fill_kr(b, dst, kr):
        for h in range(MLA_HEADS):
            kcat_ref[b, h, dst, LANES:2 * LANES] = (
                kr * _lane_mask(MLA_ROPE * h, MLA_ROPE * (h + 1))).astype(BF16)

    def fill_dk(b, dst, dk):
        for h in range(DIFF_HEADS):
            for a in range(2):
                dka_ref[b, 2 * h + a, dst, :] = (
                    dk[:, LANES * h:LANES * (h + 1)] * half[a]).astype(BF16)

    ckv = _rms(proj(C_KV, C_CB), w["kv_norm"][wl])
    kv = _dot(ckv.astype(BF16), w_ukv[wl])
    for b in range(n_b):
        if state is not None:
            state[0][b, l] = ckv[seq_rows(b)]
        fill_kv(b, pl.ds(n_ctx, n_seq), kv[seq_rows(b)])
        if ctx is not None:
            fill_kv(b, pl.ds(0, n_ctx), _dot(ctx["ckv"][...].astype(BF16), w_ukv[wl]))

    krr = proj(C_KR, D_IN_P)
    krr_pos = krr if rope is None else _rope(krr, rope_m, MLA_ROPE // 4)
    for b in range(n_b):
        if state is not None:
            state[1][b, l] = krr[seq_rows(b), 0:MLA_ROPE]
        fill_kr(b, pl.ds(n_ctx, n_seq), krr_pos[seq_rows(b)])
        if ctx is not None:
            fill_kr(b, pl.ds(0, n_ctx), ctx["kr"][...])

    dq = proj(C_DQ, C_DK) * DIFF_SCALE
    if rope is not None:
        dq = _rope(dq, rope_d, DIFF_QK // 4)
    dqb_ref[...] = dq.astype(BF16)

    dk = proj(C_DK, C_DV)
    dk_pos = dk if rope is None else _rope(dk, rope_d, DIFF_QK // 4)
    for b in range(n_b):
        if state is not None:
            state[2][b, l] = dk[seq_rows(b)]
        fill_dk(b, pl.ds(n_ctx, n_seq), dk_pos[seq_rows(b)])
        if ctx is not None:
            fill_dk(b, pl.ds(0, n_ctx), ctx["dk"][...])

    dv = proj(C_DV, C_Z)
    for b in range(n_b):
        if state is not None:
            state[3][b, l] = dv[seq_rows(b)]
        for h in range(DIFF_HEADS):
            cols = slice(LANES * h, LANES * (h + 1))
            dvb_ref[b, h, pl.ds(n_ctx, n_seq), 0:LANES] = dv[seq_rows(b), cols].astype(BF16)
            if ctx is not None:
                dvb_ref[b, h, pl.ds(0, n_ctx), 0:LANES] = ctx["dv"][:, cols].astype(BF16)
            ones = jnp.ones((nk, LANES), BF16)
            dvb_ref[b, h, :, LANES:2 * LANES] = ones
            vm_ref[b, h, :, LANES:2 * LANES] = ones

    lf = w["lam"][wl]
    lam = (jnp.exp(jnp.sum(lf[0:1] * lf[1:2], axis=-1, keepdims=True))
           - jnp.exp(jnp.sum(lf[2:3] * lf[3:4], axis=-1, keepdims=True)) + _lambda_init(l))
    subln = w["subln"][wl]
    keys = pl.ds(0, nk)
    for b in range(n_b):
        def q_block(start, b=b):
            qrows = pl.ds(start, Q_TILE)
            o_mla, o_diff = _attend(qb_ref[qrows, :], dqb_ref[qrows, :], keys,
                                    kcat_ref.at[b], vm_ref.at[b], dka_ref.at[b], dvb_ref.at[b],
                                    lam, subln, _lambda_init(l))
            o_ref[qrows, 0:W_MLA] = o_mla
            o_ref[qrows, W_MLA + W_CONV:D_MIX] = o_diff

        n_qb = n_seq // Q_TILE
        if n_qb == 1:
            q_block(b * n_seq)
        else:
            def q_step(i, carry, b=b, q_block=q_block):
                q_block(pl.multiple_of(b * n_seq + i * Q_TILE, Q_TILE))
                return carry
            lax.fori_loop(0, n_qb, q_step, 0)

    u = proj(C_CC, C_CX) * proj(C_CX, C_DQ)
    pos = lax.broadcasted_iota(jnp.int32, (m_rows, 1), 0) % n_seq
    u_prev = jnp.where(pos == 0, 0.0, pltpu.roll(u, 1, 0))
    u_next = jnp.where(pos == n_seq - 1, 0.0, pltpu.roll(u, m_rows - 1, 0))
    cw = w["conv_w"][wl]
    o_ref[:, W_MLA:W_MLA + W_CONV] = proj(C_CB, C_CC) * (
        u_prev * cw[0:1] + u * cw[1:2] + u_next * cw[2:3])

    for c0 in range(0, D_MIX, 512):
        z = proj(C_Z + c0, C_Z + c0 + 512)
        mix_ref[:, c0:c0 + 512] = (o_ref[:, c0:c0 + 512] * (z * jax.nn.sigmoid(z))).astype(BF16)
    return x + gate * _dot(mix_ref[...], w["w_out"][wl])


_W_NAMES = ("norm_g", "w_in", "q_norm", "w_uq", "kv_norm", "w_ukv", "conv_w", "lam", "subln",
            "w_out", "final_norm")
_SCR_NAMES = ("hb", "o", "mix", "qb", "dqb", "kcat", "vm", "dka", "dvb")
_CTX_NAMES = ("ckv", "kr", "dk", "dv")
N_ROPE = 6


def _ctx_kernel(x_ref, mod_ref, *refs, n_seq):
    w = dict(zip(_W_NAMES, refs[:len(_W_NAMES)]))
    rest = refs[len(_W_NAMES):]
    y_ref, state = rest[0], rest[1:5]
    scr = dict(zip(_SCR_NAMES, rest[5:]))
    n_b = x_ref.shape[0]
    x = x_ref[...].reshape(n_b * n_seq, D_MODEL)
    for l in range(DEPTH):
        mod = tuple(mod_ref[l, j:j + 1, :] for j in range(3))
        x = _trunk_layer(l, l, x, n_seq, mod, w, scr, None, None, state)
    y_ref[...] = _rms(x, w["final_norm"][...]).reshape(n_b, n_seq, D_MODEL)


def _lat_kernel(x_ref, mod_ref, *refs, n_seq, layer):
    ctx = {k: r.at[0, 0] for k, r in zip(_CTX_NAMES, refs[:4])}
    rope = (refs[4:7], refs[7:10])
    w = dict(zip(_W_NAMES, refs[4 + N_ROPE:4 + N_ROPE + len(_W_NAMES)]))
    rest = refs[4 + N_ROPE + len(_W_NAMES):]
    y_ref = rest[0]
    scr = dict(zip(_SCR_NAMES, rest[1:]))
    mod = tuple(mod_ref[0, 0, j:j + 1, :] for j in range(3))
    x = _trunk_layer(layer, 0, x_ref[0], n_seq, mod, w, scr, rope, ctx, None)
    if layer == DEPTH - 1:
        x = _rms(x, w["final_norm"][...])
    y_ref[0] = x


def _ada_kernel(cond_ref, w_ref, b_ref, o_ref):
    c = cond_ref[...]
    s = (c * jax.nn.sigmoid(c)).astype(BF16)
    o_ref[0] = _dot(s, w_ref[0].astype(BF16)) + b_ref[0]


def _const_spec(shape, layer=None):
    if layer is None:
        zeros = (0,) * len(shape)
        return pl.BlockSpec(shape, lambda *_: zeros, pipeline_mode=pl.Buffered(1))
    idx = (layer,) + (0,) * (len(shape) - 1)
    return pl.BlockSpec((1,) + tuple(shape[1:]), lambda *_: idx, pipeline_mode=pl.Buffered(1))


def _scratch(n_b, n_seq, nk):
    m_rows = n_b * n_seq
    return [
        pltpu.VMEM((m_rows, D_MODEL), BF16),
        pltpu.VMEM((m_rows, D_MIX), F32),
        pltpu.VMEM((m_rows, D_MIX), BF16),
        pltpu.VMEM((m_rows, 3 * LANES), BF16),
        pltpu.VMEM((m_rows, W_DIFF), BF16),
        pltpu.VMEM((n_b, MLA_HEADS, nk, 2 * LANES), BF16),
        pltpu.VMEM((n_b, MLA_HEADS, nk, 2 * LANES), BF16),
        pltpu.VMEM((n_b, 2 * DIFF_HEADS, nk, LANES), BF16),
        pltpu.VMEM((n_b, DIFF_HEADS, nk, 2 * LANES), BF16),
    ]


def _axial_tables(n, rot_dim):
    half = rot_dim // 2
    inv = ROPE_THETA ** (-(jnp.arange(0, half, 2, dtype=F32) / half))
    rows = n // GRID_W
    row = jnp.broadcast_to(jnp.arange(rows)[:, None], (rows, GRID_W)).reshape(n)
    col = jnp.broadcast_to(jnp.arange(GRID_W)[None, :], (rows, GRID_W)).reshape(n)
    ang_r = row.astype(F32)[:, None] * inv
    ang_c = col.astype(F32)[:, None] * inv
    ang = jnp.concatenate([ang_r, ang_r, ang_c, ang_c], axis=-1)
    return jnp.cos(ang), jnp.sin(ang)


def _rope_tables(n, rot_dim):
    cos, sin = _axial_tables(n, rot_dim)
    q = rot_dim // 4
    first = (jnp.arange(rot_dim) % (2 * q)) < q
    sin_a = jnp.where(first, -sin, 0.0)
    sin_b = jnp.where(first, 0.0, sin)
    reps = LANES // rot_dim
    return tuple(jnp.tile(t, (1, reps)) for t in (cos, sin_a, sin_b))


def _prep_weights(w_in, w_uq, w_ukv, w_out):
    offs = [0]
    for s in (MLA_Q_LORA, MLA_KV_LORA, MLA_ROPE, W_CONV, W_CONV, W_CONV, W_DIFF, W_DIFF, W_DIFF,
              D_MIX):
        offs.append(offs[-1] + s)
    seg = lambda i: w_in[:, :, offs[i]:offs[i + 1]]
    w_in_p = jnp.concatenate(
        [seg(0), seg(1), seg(3), seg(4), seg(5), seg(6), seg(7), seg(8), seg(9)]
        + [seg(2)] * (LANES // MLA_ROPE), axis=-1).astype(BF16)
    uq = w_uq.reshape(DEPTH, MLA_Q_LORA, MLA_HEADS, MLA_NOPE + MLA_ROPE)
    w_uq_p = jnp.concatenate(
        [uq[..., :MLA_NOPE].reshape(DEPTH, MLA_Q_LORA, -1),
         uq[..., MLA_NOPE:].reshape(DEPTH, MLA_Q_LORA, -1)], axis=-1).astype(BF16)
    ukv = w_ukv.reshape(DEPTH, MLA_KV_LORA, MLA_HEADS, MLA_NOPE + MLA_V)
    w_ukv_p = jnp.concatenate(
        [ukv[..., :MLA_NOPE].reshape(DEPTH, MLA_KV_LORA, -1),
         ukv[..., MLA_NOPE:].reshape(DEPTH, MLA_KV_LORA, -1)], axis=-1).astype(BF16)
    return w_in_p, w_uq_p, w_ukv_p, w_out.astype(BF16)


def kernel(x_prompt, x_sample, c, cache_mla_ckv, cache_mla_krope, cache_diff_k, cache_diff_v,
           c_ctx, norm_g, ada_w, ada_b, w_in, mla_q_norm, w_uq, mla_kv_norm, w_ukv,
           conv_w, diff_lambda, diff_subln, w_out, final_norm):
    batch, seq, _ = x_prompt.shape
    dec_batch, dec_seq, _ = x_sample.shape
    past = cache_mla_ckv.shape[2]
    cparams = pltpu.CompilerParams(dimension_semantics=("arbitrary",),
                                   vmem_limit_bytes=VMEM_LIMIT_BYTES)

    n_cond = 8
    cond = jnp.concatenate(
        [c_ctx[None], c, jnp.zeros((n_cond - 1 - dec_batch, D_MODEL), F32)], axis=0)
    tn = 1024
    mod = pl.pallas_call(
        _ada_kernel,
        grid=(DEPTH, 3 * D_MODEL // tn),
        in_specs=[pl.BlockSpec((n_cond, D_MODEL), lambda l, j: (0, 0)),
                  pl.BlockSpec((1, D_MODEL, tn), lambda l, j: (l, 0, j)),
                  pl.BlockSpec((1, 1, tn), lambda l, j: (l, 0, j))],
        out_specs=pl.BlockSpec((1, n_cond, tn), lambda l, j: (l, 0, j)),
        out_shape=jax.ShapeDtypeStruct((DEPTH, n_cond, 3 * D_MODEL), F32),
        compiler_params=pltpu.CompilerParams(dimension_semantics=("arbitrary", "arbitrary")),
        name="ada",
    )(cond, ada_w, ada_b.reshape(DEPTH, 1, 3 * D_MODEL))
    mod = mod.reshape(DEPTH, n_cond, 3, D_MODEL)
    mod_ctx = mod[:, 0]
    mod_lat = jnp.swapaxes(mod[:, 1:1 + dec_batch], 0, 1)

    w_in_p, w_uq_p, w_ukv_p, w_out_b = _prep_weights(w_in, w_uq, w_ukv, w_out)
    row3 = lambda a: a.reshape(DEPTH, 1, a.shape[-1])
    weights = (row3(norm_g), w_in_p, row3(mla_q_norm), w_uq_p, row3(mla_kv_norm), w_ukv_p,
               conv_w, diff_lambda, row3(diff_subln), w_out_b, final_norm.reshape(1, D_MODEL))

    tb = 2
    y_prompt, s_ckv, s_kr, s_dk, s_dv = pl.pallas_call(
        functools.partial(_ctx_kernel, n_seq=seq),
        grid=(batch // tb,),
        in_specs=[pl.BlockSpec((tb, seq, D_MODEL), lambda i: (i, 0, 0)),
                  _const_spec(mod_ctx.shape)] + [_const_spec(a.shape) for a in weights],
        out_specs=[pl.BlockSpec((tb, seq, D_MODEL), lambda i: (i, 0, 0)),
                   pl.BlockSpec((tb, DEPTH, seq, MLA_KV_LORA), lambda i: (i, 0, 0, 0)),
                   pl.BlockSpec((tb, DEPTH, seq, MLA_ROPE), lambda i: (i, 0, 0, 0)),
                   pl.BlockSpec((tb, DEPTH, seq, W_DIFF), lambda i: (i, 0, 0, 0)),
                   pl.BlockSpec((tb, DEPTH, seq, W_DIFF), lambda i: (i, 0, 0, 0))],
        out_shape=[jax.ShapeDtypeStruct((batch, seq, D_MODEL), F32),
                   jax.ShapeDtypeStruct((batch, DEPTH, seq, MLA_KV_LORA), F32),
                   jax.ShapeDtypeStruct((batch, DEPTH, seq, MLA_ROPE), F32),
                   jax.ShapeDtypeStruct((batch, DEPTH, seq, W_DIFF), F32),
                   jax.ShapeDtypeStruct((batch, DEPTH, seq, W_DIFF), F32)],
        scratch_shapes=_scratch(tb, seq, seq),
        compiler_params=cparams,
        name="ctx",
    )(x_prompt, mod_ctx, *weights)

    tables = _rope_tables(dec_seq, MLA_ROPE) + _rope_tables(dec_seq, DIFF_QK)
    ctx_in = (cache_mla_ckv,
              jnp.tile(cache_mla_krope, (1, 1, 1, LANES // MLA_ROPE)),
              cache_diff_k.reshape(dec_batch, DEPTH, past, W_DIFF),
              cache_diff_v.reshape(dec_batch, DEPTH, past, W_DIFF))
    h_lat = x_sample
    for l in range(DEPTH):
        h_lat = pl.pallas_call(
            functools.partial(_lat_kernel, n_seq=dec_seq, layer=l),
            grid=(dec_batch,),
            in_specs=[pl.BlockSpec((1, dec_seq, D_MODEL), lambda i: (i, 0, 0)),
                      pl.BlockSpec((1, 1, 3, D_MODEL), lambda i, l=l: (i, l, 0, 0))]
            + [pl.BlockSpec((1, 1) + a.shape[2:], lambda i, l=l: (i, l, 0, 0)) for a in ctx_in]
            + [_const_spec(t.shape) for t in tables]
            + [_const_spec(a.shape, layer=l) for a in weights[:-1]]
            + [_const_spec(weights[-1].shape)],
            out_specs=pl.BlockSpec((1, dec_seq, D_MODEL), lambda i: (i, 0, 0),
                                   pipeline_mode=pl.Buffered(1)),
            out_shape=jax.ShapeDtypeStruct((dec_batch, dec_seq, D_MODEL), F32),
            scratch_shapes=_scratch(1, dec_seq, past + dec_seq),
            compiler_params=cparams,
            name=f"latent_{l}",
        )(h_lat, mod_lat, *ctx_in, *tables, *weights)

    return (y_prompt, h_lat, s_ckv, s_kr,
            s_dk.reshape(batch, DEPTH, seq, DIFF_HEADS, 2, DIFF_QK),
            s_dv.reshape(batch, DEPTH, seq, DIFF_HEADS, DIFF_V))
```

```python
import functools
import math

import numpy as np
import jax
import jax.numpy as jnp
from jax import lax
from jax.experimental import pallas as pl
from jax.experimental.pallas import tpu as pltpu

D_MODEL = 1024
DEPTH = 2
GRID_W = 64
MLA_HEADS = 4
MLA_NOPE = 64
MLA_ROPE = 32
MLA_V = 64
MLA_Q_LORA = 256
MLA_KV_LORA = 128
W_MLA = MLA_HEADS * MLA_V
W_CONV = 256
DIFF_HEADS = 4
DIFF_QK = 64
DIFF_V = 2 * DIFF_QK
W_DIFF = DIFF_HEADS * DIFF_V
D_MIX = W_MLA + W_CONV + W_DIFF
ROPE_THETA = 10000.0
EPS = 1e-6

LANES = 128
Q_TILE = 256
LOG2E = math.log2(math.e)
MLA_SCALE = (MLA_NOPE + MLA_ROPE) ** -0.5 * LOG2E
DIFF_SCALE = DIFF_QK ** -0.5 * LOG2E
VMEM_LIMIT_BYTES = 58 * 1024 * 1024

R_Q = 0
R_KV = R_Q + MLA_Q_LORA
R_KR = R_KV + MLA_KV_LORA
R_CB = R_KR + MLA_ROPE
R_CC = R_CB + W_CONV
R_CX = R_CC + W_CONV
R_DQ = R_CX + W_CONV
R_DK = R_DQ + W_DIFF
R_DV = R_DK + W_DIFF
R_Z = R_DV + W_DIFF
D_IN = R_Z + D_MIX

BF16 = jnp.bfloat16
F32 = jnp.float32


def _lambda_init(layer):
    return 0.8 - 0.6 * math.exp(-0.3 * layer)


def _dot(a, b):
    return jnp.dot(a, b, preferred_element_type=F32)


def _dot_nt(a, b):
    return lax.dot_general(a, b, (((1,), (1,)), ((), ())), preferred_element_type=F32)


def _rms(x, g):
    ms = jnp.mean(x * x, axis=-1, keepdims=True)
    return x * lax.rsqrt(ms + EPS) * g


def _exp_scores(s):
    m = s[:, 0:LANES]
    for t in range(1, s.shape[1] // LANES):
        m = jnp.maximum(m, s[:, LANES * t:LANES * (t + 1)])
    return jnp.exp2(s - jnp.max(m, axis=-1, keepdims=True)).astype(BF16)


def _weighted(e, v_ones):
    r = _dot(e, v_ones)
    return r[:, 0:LANES], 1.0 / r[:, LANES:2 * LANES]


def _lane_mask(lo, hi):
    lane = lax.broadcasted_iota(jnp.int32, (1, LANES), 1)
    return jnp.where((lane >= lo) & (lane < hi), 1.0, 0.0).astype(F32)


def _rope(x, tables, q):
    cos, sin_a, sin_b = tables
    tiles = []
    for t in range(x.shape[1] // LANES):
        xt = x[:, LANES * t:LANES * (t + 1)]
        tiles.append(xt * cos + pltpu.roll(xt, LANES - q, 1) * sin_a + pltpu.roll(xt, q, 1) * sin_b)
    return tiles[0] if len(tiles) == 1 else jnp.concatenate(tiles, axis=1)


def _attend(q, dq, keys, kcat_ref, vm_ref, dka_ref, dvb_ref, lam, subln, lam_init):
    qr = q[:, 2 * LANES:3 * LANES]
    o_mla = []
    for t in range(2):
        qcat = jnp.concatenate([q[:, LANES * t:LANES * (t + 1)], qr], axis=1)
        acc = None
        for hh in range(2):
            h = 2 * t + hh
            o, rinv = _weighted(_exp_scores(_dot_nt(qcat, kcat_ref[h, keys, :])),
                                vm_ref[h, keys, :])
            acc = o * rinv if acc is None else acc + o * rinv
        o_mla.append(acc)
    o_diff = []
    for h in range(DIFF_HEADS):
        qt = dq[:, LANES * h:LANES * (h + 1)]
        o0, rinv0 = _weighted(_exp_scores(_dot_nt(qt, dka_ref[2 * h, keys, :])),
                              dvb_ref[h, keys, :])
        o1, rinv1 = _weighted(_exp_scores(_dot_nt(qt, dka_ref[2 * h + 1, keys, :])),
                              dvb_ref[h, keys, :])
        o = o0 * rinv0 - o1 * (lam * rinv1)
        o_diff.append(_rms(o, subln) * (1.0 - lam_init))
    return jnp.concatenate(o_mla, axis=1), jnp.concatenate(o_diff, axis=1)


def _trunk_layer(l, wl, x, n_seq, mod, w, scr, rope, ctx, state):
    m_rows = x.shape[0]
    n_b = m_rows // n_seq
    n_ctx = 0 if ctx is None else ctx["ckv"].shape[0]
    nk = n_ctx + n_seq
    shift, scale, gate = mod
    hb_ref, o_ref, mix_ref = scr["hb"], scr["o"], scr["mix"]
    qb_ref, dqb_ref = scr["qb"], scr["dqb"]
    kcat_ref, vm_ref, dka_ref, dvb_ref = scr["kcat"], scr["vm"], scr["dka"], scr["dvb"]
    w_in_t, w_ukv = w["w_in_t"], w["w_ukv"]
    half = (_lane_mask(0, 64), _lane_mask(64, 128))
    new_keys, old_keys = pl.ds(n_ctx, n_seq), pl.ds(0, n_ctx)
    if rope is not None:
        rope_m = tuple(r[...] for r in rope[0])
        rope_d = tuple(r[...] for r in rope[1])

    def seq_rows(b):
        return slice(b * n_seq, (b + 1) * n_seq)

    gs = w["norm_g"][wl] * (1.0 + scale)
    ms = jnp.mean(x * x, axis=-1, keepdims=True)
    hb_ref[...] = (x * lax.rsqrt(ms + EPS) * gs + shift).astype(BF16)

    def proj(r0, r1):
        return _dot_nt(hb_ref[...], w_in_t[wl, r0:r1, :])

    cqn = _rms(proj(R_Q, R_KV), w["q_norm"][wl]).astype(BF16)
    q = _dot(cqn, w["w_uq"][wl]) * MLA_SCALE
    if rope is not None:
        q = jnp.concatenate(
            [q[:, 0:2 * LANES], _rope(q[:, 2 * LANES:3 * LANES], rope_m, MLA_ROPE // 4)], axis=1)
    qb_ref[...] = q.astype(BF16)

    def fill_kv(b, dst, kv):
        for h in range(MLA_HEADS):
            t, hh = divmod(h, 2)
            kcat_ref[b, h, dst, 0:LANES] = (kv[:, LANES * t:LANES * (t + 1)] * half[hh]).astype(BF16)
            vm_ref[b, h, dst, 0:LANES] = (
                kv[:, 2 * LANES + LANES * t:2 * LANES + LANES * (t + 1)] * half[hh]).astype(BF16)

    def fill_kr(b, dst, kr):
        for h in range(MLA_HEADS):
            kcat_ref[b, h, dst, LANES:2 * LANES] = (
                kr * _lane_mask(MLA_ROPE * h, MLA_ROPE * (h + 1))).astype(BF16)

    def fill_dk(b, dst, dk):
        for h in range(DIFF_HEADS):
            for a in range(2):
                dka_ref[b, 2 * h + a, dst, :] = (
                    dk[:, LANES * h:LANES * (h + 1)] * half[a]).astype(BF16)

    ckv = _rms(proj(R_KV, R_KR), w["kv_norm"][wl])
    kv = _dot(ckv.astype(BF16), w_ukv[wl])
    for b in range(n_b):
        if state is not None:
            state[0][b, l] = ckv[seq_rows(b)]
        fill_kv(b, new_keys, kv[seq_rows(b)])
        if ctx is not None:
            fill_kv(b, old_keys, _dot(ctx["ckv"][...].astype(BF16), w_ukv[wl]))

    krr = _dot_nt(hb_ref[...], w["w_kr4"][wl])
    krr_pos = krr if rope is None else _rope(krr, rope_m, MLA_ROPE // 4)
    for b in range(n_b):
        if state is not None:
            state[1][b, l] = krr[seq_rows(b)].T[0:MLA_ROPE, :]
        fill_kr(b, new_keys, krr_pos[seq_rows(b)])
        if ctx is not None:
            fill_kr(b, old_keys, jnp.concatenate([ctx["kr"][...]] * (LANES // MLA_ROPE), axis=0).T)

    dq = proj(R_DQ, R_DK) * DIFF_SCALE
    if rope is not None:
        dq = _rope(dq, rope_d, DIFF_QK // 4)
    dqb_ref[...] = dq.astype(BF16)

    dk = proj(R_DK, R_DV)
    dk_pos = dk if rope is None else _rope(dk, rope_d, DIFF_QK // 4)
    for b in range(n_b):
        if state is not None:
            state[2][b, l] = dk[seq_rows(b)].T
        fill_dk(b, new_keys, dk_pos[seq_rows(b)])
        if ctx is not None:
            fill_dk(b, old_keys, ctx["dk"][...].T)

    dv = proj(R_DV, R_Z)
    for b in range(n_b):
        for h in range(DIFF_HEADS):
            dv_h = dv[seq_rows(b), LANES * h:LANES * (h + 1)]
            if state is not None:
                state[3][b, l, pl.ds(h, n_seq, stride=DIFF_HEADS), :] = dv_h
            dvb_ref[b, h, new_keys, 0:LANES] = dv_h.astype(BF16)
            if ctx is not None:
                dvb_ref[b, h, old_keys, 0:LANES] = (
                    ctx["dv"][pl.ds(h, n_ctx, stride=DIFF_HEADS), :].astype(BF16))
            ones = jnp.ones((nk, LANES), BF16)
            dvb_ref[b, h, :, LANES:2 * LANES] = ones
            vm_ref[b, h, :, LANES:2 * LANES] = ones

    lf = w["lam"][wl]
    lam = (jnp.exp(jnp.sum(lf[0:1] * lf[1:2], axis=-1, keepdims=True))
           - jnp.exp(jnp.sum(lf[2:3] * lf[3:4], axis=-1, keepdims=True)) + _lambda_init(l))
    subln = w["subln"][wl]
    keys = pl.ds(0, nk)
    for b in range(n_b):
        def q_block(start, b=b):
            qrows = pl.ds(start, Q_TILE)
            o_mla, o_diff = _attend(qb_ref[qrows, :], dqb_ref[qrows, :], keys,
                                    kcat_ref.at[b], vm_ref.at[b], dka_ref.at[b], dvb_ref.at[b],
                                    lam, subln, _lambda_init(l))
            o_ref[qrows, 0:W_MLA] = o_mla
            o_ref[qrows, W_MLA + W_CONV:D_MIX] = o_diff

        n_qb = n_seq // Q_TILE
        if n_qb == 1:
            q_block(b * n_seq)
        else:
            def q_step(i, carry, b=b, q_block=q_block):
                q_block(pl.multiple_of(b * n_seq + i * Q_TILE, Q_TILE))
                return carry
            lax.fori_loop(0, n_qb, q_step, 0)

    u = proj(R_CC, R_CX) * proj(R_CX, R_DQ)
    pos = lax.broadcasted_iota(jnp.int32, (m_rows, 1), 0) % n_seq
    u_prev = jnp.where(pos == 0, 0.0, pltpu.roll(u, 1, 0))
    u_next = jnp.where(pos == n_seq - 1, 0.0, pltpu.roll(u, m_rows - 1, 0))
    cw = w["conv_w"][wl]
    o_ref[:, W_MLA:W_MLA + W_CONV] = proj(R_CB, R_CC) * (
        u_prev * cw[0:1] + u * cw[1:2] + u_next * cw[2:3])

    for c0 in range(0, D_MIX, 512):
        z = proj(R_Z + c0, R_Z + c0 + 512)
        mix_ref[:, c0:c0 + 512] = (o_ref[:, c0:c0 + 512] * (z * jax.nn.sigmoid(z))).astype(BF16)
    return x + gate * _dot(mix_ref[...], w["w_out"][wl])


_W_NAMES = ("norm_g", "w_in_t", "w_kr4", "q_norm", "w_uq", "kv_norm", "w_ukv", "conv_w", "lam",
            "subln", "w_out", "final_norm")
_SCR_NAMES = ("hb", "o", "mix", "qb", "dqb", "kcat", "vm", "dka", "dvb")
_CTX_NAMES = ("ckv", "kr", "dk", "dv")
N_ROPE = 6


def _ctx_kernel(x_ref, mod_ref, *refs, n_seq):
    w = dict(zip(_W_NAMES, refs[:len(_W_NAMES)]))
    rest = refs[len(_W_NAMES):]
    y_ref, state = rest[0], rest[1:5]
    scr = dict(zip(_SCR_NAMES, rest[5:]))
    n_b = x_ref.shape[0]
    x = x_ref[...].reshape(n_b * n_seq, D_MODEL)
    for l in range(DEPTH):
        mod = tuple(mod_ref[l, j:j + 1, :] for j in range(3))
        x = _trunk_layer(l, l, x, n_seq, mod, w, scr, None, None, state)
    y_ref[...] = _rms(x, w["final_norm"][...]).reshape(n_b, n_seq, D_MODEL)


def _lat_kernel(x_ref, mod_ref, *refs, n_seq, layer):
    ctx = {k: r.at[0, 0] for k, r in zip(_CTX_NAMES, refs[:4])}
    rope = (refs[4:7], refs[7:10])
    w = dict(zip(_W_NAMES, refs[4 + N_ROPE:4 + N_ROPE + len(_W_NAMES)]))
    rest = refs[4 + N_ROPE + len(_W_NAMES):]
    y_ref = rest[0]
    scr = dict(zip(_SCR_NAMES, rest[1:]))
    mod = tuple(mod_ref[0, 0, j:j + 1, :] for j in range(3))
    x = _trunk_layer(layer, 0, x_ref[0], n_seq, mod, w, scr, rope, ctx, None)
    if layer == DEPTH - 1:
        x = _rms(x, w["final_norm"][...])
    y_ref[0] = x


def _ada_kernel(cond_ref, w_ref, b_ref, o_ref):
    c = cond_ref[...]
    s = (c * jax.nn.sigmoid(c)).astype(BF16)
    o_ref[0] = _dot(s, w_ref[0].astype(BF16)) + b_ref[0]


def _const_spec(shape, layer=None):
    if layer is None:
        zeros = (0,) * len(shape)
        return pl.BlockSpec(shape, lambda *_: zeros, pipeline_mode=pl.Buffered(1))
    idx = (layer,) + (0,) * (len(shape) - 1)
    return pl.BlockSpec((1,) + tuple(shape[1:]), lambda *_: idx, pipeline_mode=pl.Buffered(1))


def _scratch(n_b, n_seq, nk):
    m_rows = n_b * n_seq
    return [
        pltpu.VMEM((m_rows, D_MODEL), BF16),
        pltpu.VMEM((m_rows, D_MIX), F32),
        pltpu.VMEM((m_rows, D_MIX), BF16),
        pltpu.VMEM((m_rows, 3 * LANES), BF16),
        pltpu.VMEM((m_rows, W_DIFF), BF16),
        pltpu.VMEM((n_b, MLA_HEADS, nk, 2 * LANES), BF16),
        pltpu.VMEM((n_b, MLA_HEADS, nk, 2 * LANES), BF16),
        pltpu.VMEM((n_b, 2 * DIFF_HEADS, nk, LANES), BF16),
        pltpu.VMEM((n_b, DIFF_HEADS, nk, 2 * LANES), BF16),
    ]


def _rope_tables(n, rot_dim):
    half = rot_dim // 2
    inv = ROPE_THETA ** (-(np.arange(0, half, 2, dtype=np.float32) / half))
    pos = np.arange(n)
    ang_r = (pos // GRID_W).astype(np.float32)[:, None] * inv
    ang_c = (pos % GRID_W).astype(np.float32)[:, None] * inv
    ang = np.concatenate([ang_r, ang_r, ang_c, ang_c], axis=-1)
    cos, sin = np.cos(ang), np.sin(ang)
    q = rot_dim // 4
    first = (np.arange(rot_dim) % (2 * q)) < q
    sin_a = np.where(first, -sin, 0.0)
    sin_b = np.where(first, 0.0, sin)
    reps = LANES // rot_dim
    return tuple(jnp.asarray(np.tile(t, (1, reps)), dtype=F32) for t in (cos, sin_a, sin_b))


def _prep_weights(w_in, w_uq, w_ukv, w_out):
    w_in_t = jnp.swapaxes(w_in, 1, 2).astype(BF16)
    w_kr4 = jnp.tile(w_in_t[:, R_KR:R_CB, :], (1, LANES // MLA_ROPE, 1))
    uq = w_uq.reshape(DEPTH, MLA_Q_LORA, MLA_HEADS, MLA_NOPE + MLA_ROPE)
    w_uq_p = jnp.concatenate(
        [uq[..., :MLA_NOPE].reshape(DEPTH, MLA_Q_LORA, -1),
         uq[..., MLA_NOPE:].reshape(DEPTH, MLA_Q_LORA, -1)], axis=-1).astype(BF16)
    ukv = w_ukv.reshape(DEPTH, MLA_KV_LORA, MLA_HEADS, MLA_NOPE + MLA_V)
    w_ukv_p = jnp.concatenate(
        [ukv[..., :MLA_NOPE].reshape(DEPTH, MLA_KV_LORA, -1),
         ukv[..., MLA_NOPE:].reshape(DEPTH, MLA_KV_LORA, -1)], axis=-1).astype(BF16)
    return w_in_t, w_kr4, w_uq_p, w_ukv_p, w_out.astype(BF16)


def kernel(x_prompt, x_sample, c, cache_mla_ckv, cache_mla_krope, cache_diff_k, cache_diff_v,
           c_ctx, norm_g, ada_w, ada_b, w_in, mla_q_norm, w_uq, mla_kv_norm, w_ukv,
           conv_w, diff_lambda, diff_subln, w_out, final_norm):
    batch, seq, _ = x_prompt.shape
    dec_batch, dec_seq, _ = x_sample.shape
    past = cache_mla_ckv.shape[2]
    cparams = pltpu.CompilerParams(dimension_semantics=("arbitrary",),
                                   vmem_limit_bytes=VMEM_LIMIT_BYTES)

    n_cond = 8
    cond = jnp.concatenate(
        [c_ctx[None], c, jnp.zeros((n_cond - 1 - dec_batch, D_MODEL), F32)], axis=0)
    tn = 1024
    mod = pl.pallas_call(
        _ada_kernel,
        grid=(DEPTH, 3 * D_MODEL // tn),
        in_specs=[pl.BlockSpec((n_cond, D_MODEL), lambda l, j: (0, 0)),
                  pl.BlockSpec((1, D_MODEL, tn), lambda l, j: (l, 0, j)),
                  pl.BlockSpec((1, 1, tn), lambda l, j: (l, 0, j))],
        out_specs=pl.BlockSpec((1, n_cond, tn), lambda l, j: (l, 0, j)),
        out_shape=jax.ShapeDtypeStruct((DEPTH, n_cond, 3 * D_MODEL), F32),
        compiler_params=pltpu.CompilerParams(dimension_semantics=("arbitrary", "arbitrary")),
        name="ada",
    )(cond, ada_w, ada_b.reshape(DEPTH, 1, 3 * D_MODEL))
    mod = mod.reshape(DEPTH, n_cond, 3, D_MODEL)
    mod_ctx = mod[:, 0]
    mod_lat = jnp.swapaxes(mod[:, 1:1 + dec_batch], 0, 1)

    w_in_t, w_kr4, w_uq_p, w_ukv_p, w_out_b = _prep_weights(w_in, w_uq, w_ukv, w_out)
    row3 = lambda a: a.reshape(DEPTH, 1, a.shape[-1])
    weights = (row3(norm_g), w_in_t, w_kr4, row3(mla_q_norm), w_uq_p, row3(mla_kv_norm), w_ukv_p,
               conv_w, diff_lambda, row3(diff_subln), w_out_b, final_norm.reshape(1, D_MODEL))

    tb = 2
    state_shapes = ((seq, MLA_KV_LORA), (MLA_ROPE, seq), (W_DIFF, seq),
                    (DIFF_HEADS * seq, DIFF_V))
    y_prompt, s_ckv, s_kr, s_dk, s_dv = pl.pallas_call(
        functools.partial(_ctx_kernel, n_seq=seq),
        grid=(batch // tb,),
        in_specs=[pl.BlockSpec((tb, seq, D_MODEL), lambda i: (i, 0, 0)),
                  _const_spec(mod_ctx.shape)] + [_const_spec(a.shape) for a in weights],
        out_specs=[pl.BlockSpec((tb, seq, D_MODEL), lambda i: (i, 0, 0))]
        + [pl.BlockSpec((tb, DEPTH) + s, lambda i: (i, 0, 0, 0)) for s in state_shapes],
        out_shape=[jax.ShapeDtypeStruct((batch, seq, D_MODEL), F32)]
        + [jax.ShapeDtypeStruct((batch, DEPTH) + s, F32) for s in state_shapes],
        scratch_shapes=_scratch(tb, seq, seq),
        compiler_params=cparams,
        name="ctx",
    )(x_prompt, mod_ctx, *weights)

    tables = _rope_tables(dec_seq, MLA_ROPE) + _rope_tables(dec_seq, DIFF_QK)
    ctx_in = (cache_mla_ckv,
              jnp.swapaxes(cache_mla_krope, 2, 3),
              jnp.transpose(cache_diff_k, (0, 1, 3, 4, 5, 2)).reshape(dec_batch, DEPTH, W_DIFF, past),
              cache_diff_v.reshape(dec_batch, DEPTH, past * DIFF_HEADS, DIFF_V))
    h_lat = x_sample
    for l in range(DEPTH):
        h_lat = pl.pallas_call(
            functools.partial(_lat_kernel, n_seq=dec_seq, layer=l),
            grid=(dec_batch,),
            in_specs=[pl.BlockSpec((1, dec_seq, D_MODEL), lambda i: (i, 0, 0)),
                      pl.BlockSpec((1, 1, 3, D_MODEL), lambda i, l=l: (i, l, 0, 0))]
            + [pl.BlockSpec((1, 1) + a.shape[2:], lambda i, l=l: (i, l, 0, 0)) for a in ctx_in]
            + [_const_spec(t.shape) for t in tables]
            + [_const_spec(a.shape, layer=l) for a in weights[:-1]]
            + [_const_spec(weights[-1].shape)],
            out_specs=pl.BlockSpec((1, dec_seq, D_MODEL), lambda i: (i, 0, 0),
                                   pipeline_mode=pl.Buffered(1)),
            out_shape=jax.ShapeDtypeStruct((dec_batch, dec_seq, D_MODEL), F32),
            scratch_shapes=_scratch(1, dec_seq, past + dec_seq),
            compiler_params=cparams,
            name=f"latent_{l}",
        )(h_lat, mod_lat, *ctx_in, *tables, *weights)

    return (y_prompt, h_lat, s_ckv,
            jnp.swapaxes(s_kr, 2, 3),
            jnp.transpose(s_dk.reshape(batch, DEPTH, DIFF_HEADS, 2, DIFF_QK, seq),
                          (0, 1, 5, 2, 3, 4)),
            s_dv.reshape(batch, DEPTH, seq, DIFF_HEADS, DIFF_V))
```

```python
import functools
import math

import numpy as np
import jax
import jax.numpy as jnp
from jax import lax
from jax.experimental import pallas as pl
from jax.experimental.pallas import tpu as pltpu

D_MODEL = 1024
DEPTH = 2
GRID_W = 64
MLA_HEADS = 4
MLA_NOPE = 64
MLA_ROPE = 32
MLA_V = 64
MLA_Q_LORA = 256
MLA_KV_LORA = 128
W_MLA = MLA_HEADS * MLA_V
W_CONV = 256
DIFF_HEADS = 4
DIFF_QK = 64
DIFF_V = 2 * DIFF_QK
W_DIFF = DIFF_HEADS * DIFF_V
D_MIX = W_MLA + W_CONV + W_DIFF
ROPE_THETA = 10000.0
EPS = 1e-6

LANES = 128
Q_TILE = 256
LOG2E = math.log2(math.e)
MLA_SCALE = (MLA_NOPE + MLA_ROPE) ** -0.5 * LOG2E
DIFF_SCALE = DIFF_QK ** -0.5 * LOG2E
VMEM_LIMIT_BYTES = 54 * 1024 * 1024

R_Q = 0
R_KV = R_Q + MLA_Q_LORA
R_KR = R_KV + MLA_KV_LORA
R_CB = R_KR + MLA_ROPE
R_CC = R_CB + W_CONV
R_CX = R_CC + W_CONV
R_DQ = R_CX + W_CONV
R_DK = R_DQ + W_DIFF
R_DV = R_DK + W_DIFF
R_Z = R_DV + W_DIFF
D_IN = R_Z + D_MIX

BF16 = jnp.bfloat16
F32 = jnp.float32


def _lambda_init(layer):
    return 0.8 - 0.6 * math.exp(-0.3 * layer)


def _dot(a, b):
    return jnp.dot(a, b, preferred_element_type=F32)


def _dot_nt(a, b):
    return lax.dot_general(a, b, (((1,), (1,)), ((), ())), preferred_element_type=F32)


def _rms(x, g):
    ms = jnp.mean(x * x, axis=-1, keepdims=True)
    return x * lax.rsqrt(ms + EPS) * g


def _exp_scores(s):
    m = s[:, 0:LANES]
    for t in range(1, s.shape[1] // LANES):
        m = jnp.maximum(m, s[:, LANES * t:LANES * (t + 1)])
    return jnp.exp2(s - jnp.max(m, axis=-1, keepdims=True)).astype(BF16)


def _weighted(e, v_ones):
    r = _dot(e, v_ones)
    return r[:, 0:LANES], 1.0 / r[:, LANES:2 * LANES]


def _lane_mask(lo, hi):
    lane = lax.broadcasted_iota(jnp.int32, (1, LANES), 1)
    return jnp.where((lane >= lo) & (lane < hi), 1.0, 0.0).astype(F32)


def _rope(x, tables, q):
    cos, sin_a, sin_b = tables
    tiles = []
    for t in range(x.shape[1] // LANES):
        xt = x[:, LANES * t:LANES * (t + 1)]
        tiles.append(xt * cos + pltpu.roll(xt, LANES - q, 1) * sin_a + pltpu.roll(xt, q, 1) * sin_b)
    return tiles[0] if len(tiles) == 1 else jnp.concatenate(tiles, axis=1)


def _attend(q, dq, keys, kcat_ref, vm_ref, dka_ref, dvb_ref, lam, subln, lam_init):
    qr = q[:, 2 * LANES:3 * LANES]
    o_mla = []
    for t in range(2):
        qcat = jnp.concatenate([q[:, LANES * t:LANES * (t + 1)], qr], axis=1)
        acc = None
        for hh in range(2):
            h = 2 * t + hh
            o, rinv = _weighted(_exp_scores(_dot_nt(qcat, kcat_ref[h, keys, :])),
                                vm_ref[h, keys, :])
            acc = o * rinv if acc is None else acc + o * rinv
        o_mla.append(acc)
    o_diff = []
    for h in range(DIFF_HEADS):
        qt = dq[:, LANES * h:LANES * (h + 1)]
        o0, rinv0 = _weighted(_exp_scores(_dot_nt(qt, dka_ref[2 * h, keys, :])),
                              dvb_ref[h, keys, :])
        o1, rinv1 = _weighted(_exp_scores(_dot_nt(qt, dka_ref[2 * h + 1, keys, :])),
                              dvb_ref[h, keys, :])
        o = o0 * rinv0 - o1 * (lam * rinv1)
        o_diff.append(_rms(o, subln) * (1.0 - lam_init))
    return jnp.concatenate(o_mla, axis=1), jnp.concatenate(o_diff, axis=1)


def _trunk_layer(l, wl, x, n_seq, mod, w, scr, rope, ctx, state):
    m_rows = x.shape[0]
    n_b = m_rows // n_seq
    n_ctx = 0 if ctx is None else ctx["ckv"].shape[0]
    nk = n_ctx + n_seq
    shift, scale, gate = mod
    hb_ref, o_ref, mix_ref = scr["hb"], scr["o"], scr["mix"]
    qb_ref, dqb_ref = scr["qb"], scr["dqb"]
    kcat_ref, vm_ref, dka_ref, dvb_ref = scr["kcat"], scr["vm"], scr["dka"], scr["dvb"]
    w_in_t, w_ukv = w["w_in_t"], w["w_ukv"]
    half = (_lane_mask(0, 64), _lane_mask(64, 128))
    new_keys, old_keys = pl.ds(n_ctx, n_seq), pl.ds(0, n_ctx)
    if rope is not None:
        rope_m = tuple(r[...] for r in rope[0])
        rope_d = tuple(r[...] for r in rope[1])

    def seq_rows(b):
        return slice(b * n_seq, (b + 1) * n_seq)

    gs = w["norm_g"][wl] * (1.0 + scale)
    ms = jnp.mean(x * x, axis=-1, keepdims=True)
    hb_ref[...] = (x * lax.rsqrt(ms + EPS) * gs + shift).astype(BF16)

    def proj(r0, r1):
        return _dot_nt(hb_ref[...], w_in_t[wl, r0:r1, :])

    head = _dot_nt(hb_ref[...], w["w_head"][wl])

    cqn = _rms(head[:, 0:MLA_Q_LORA], w["q_norm"][wl]).astype(BF16)
    q = _dot(cqn, w["w_uq"][wl]) * MLA_SCALE
    if rope is not None:
        q = jnp.concatenate(
            [q[:, 0:2 * LANES], _rope(q[:, 2 * LANES:3 * LANES], rope_m, MLA_ROPE // 4)], axis=1)
    qb_ref[...] = q.astype(BF16)

    def fill_kv(b, dst, kv):
        for h in range(MLA_HEADS):
            t, hh = divmod(h, 2)
            kcat_ref[b, h, dst, 0:LANES] = (kv[:, LANES * t:LANES * (t + 1)] * half[hh]).astype(BF16)
            vm_ref[b, h, dst, 0:LANES] = (
                kv[:, 2 * LANES + LANES * t:2 * LANES + LANES * (t + 1)] * half[hh]).astype(BF16)

    def fill_kr(b, dst, kr):
        for h in range(MLA_HEADS):
            kcat_ref[b, h, dst, LANES:2 * LANES] = (
                kr * _lane_mask(MLA_ROPE * h, MLA_ROPE * (h + 1))).astype(BF16)

    def fill_dk(b, dst, dk):
        for h in range(DIFF_HEADS):
            for a in range(2):
                dka_ref[b, 2 * h + a, dst, :] = (
                    dk[:, LANES * h:LANES * (h + 1)] * half[a]).astype(BF16)

    ckv = _rms(head[:, R_KV:R_KR], w["kv_norm"][wl])
    kv = _dot(ckv.astype(BF16), w_ukv[wl])
    for b in range(n_b):
        if state is not None:
            state[0][b, l] = ckv[seq_rows(b)]
        fill_kv(b, new_keys, kv[seq_rows(b)])
        if ctx is not None:
            fill_kv(b, old_keys, _dot(ctx["ckv"][...].astype(BF16), w_ukv[wl]))

    krr = head[:, R_KR:R_KR + LANES]
    krr_pos = krr if rope is None else _rope(krr, rope_m, MLA_ROPE // 4)
    for b in range(n_b):
        if state is not None:
            state[1][b, l] = krr[seq_rows(b)].T[0:MLA_ROPE, :]
        fill_kr(b, new_keys, krr_pos[seq_rows(b)])
        if ctx is not None:
            fill_kr(b, old_keys, jnp.concatenate([ctx["kr"][...]] * (LANES // MLA_ROPE), axis=0).T)

    dq = proj(R_DQ, R_DK) * DIFF_SCALE
    if rope is not None:
        dq = _rope(dq, rope_d, DIFF_QK // 4)
    dqb_ref[...] = dq.astype(BF16)

    dk = proj(R_DK, R_DV)
    dk_pos = dk if rope is None else _rope(dk, rope_d, DIFF_QK // 4)
    for b in range(n_b):
        if state is not None:
            state[2][b, l] = dk[seq_rows(b)].T
        fill_dk(b, new_keys, dk_pos[seq_rows(b)])
        if ctx is not None:
            fill_dk(b, old_keys, ctx["dk"][...].T)

    dv = proj(R_DV, R_Z)
    for b in range(n_b):
        for h in range(DIFF_HEADS):
            dv_h = dv[seq_rows(b), LANES * h:LANES * (h + 1)]
            if state is not None:
                state[3][b, l, pl.ds(h, n_seq, stride=DIFF_HEADS), :] = dv_h
            dvb_ref[b, h, new_keys, 0:LANES] = dv_h.astype(BF16)
            if ctx is not None:
                dvb_ref[b, h, old_keys, 0:LANES] = (
                    ctx["dv"][pl.ds(h, n_ctx, stride=DIFF_HEADS), :].astype(BF16))
            ones = jnp.ones((nk, LANES), BF16)
            dvb_ref[b, h, :, LANES:2 * LANES] = ones
            vm_ref[b, h, :, LANES:2 * LANES] = ones

    lf = w["lam"][wl]
    lam = (jnp.exp(jnp.sum(lf[0:1] * lf[1:2], axis=-1, keepdims=True))
           - jnp.exp(jnp.sum(lf[2:3] * lf[3:4], axis=-1, keepdims=True)) + _lambda_init(l))
    subln = w["subln"][wl]
    keys = pl.ds(0, nk)
    for b in range(n_b):
        def q_block(start, b=b):
            qrows = pl.ds(start, Q_TILE)
            o_mla, o_diff = _attend(qb_ref[qrows, :], dqb_ref[qrows, :], keys,
                                    kcat_ref.at[b], vm_ref.at[b], dka_ref.at[b], dvb_ref.at[b],
                                    lam, subln, _lambda_init(l))
            o_ref[qrows, 0:W_MLA] = o_mla
            o_ref[qrows, W_MLA + W_CONV:D_MIX] = o_diff

        n_qb = n_seq // Q_TILE
        if n_qb == 1:
            q_block(b * n_seq)
        else:
            def q_step(i, carry, b=b, q_block=q_block):
                q_block(pl.multiple_of(b * n_seq + i * Q_TILE, Q_TILE))
                return carry
            lax.fori_loop(0, n_qb, q_step, 0)

    conv = proj(R_CB, R_DQ)
    u = conv[:, W_CONV:2 * W_CONV] * conv[:, 2 * W_CONV:3 * W_CONV]
    pos = lax.broadcasted_iota(jnp.int32, (m_rows, 1), 0) % n_seq
    u_prev = jnp.where(pos == 0, 0.0, pltpu.roll(u, 1, 0))
    u_next = jnp.where(pos == n_seq - 1, 0.0, pltpu.roll(u, m_rows - 1, 0))
    cw = w["conv_w"][wl]
    o_ref[:, W_MLA:W_MLA + W_CONV] = conv[:, 0:W_CONV] * (
        u_prev * cw[0:1] + u * cw[1:2] + u_next * cw[2:3])

    for c0 in range(0, D_MIX, 512):
        z = proj(R_Z + c0, R_Z + c0 + 512)
        mix_ref[:, c0:c0 + 512] = (o_ref[:, c0:c0 + 512] * (z * jax.nn.sigmoid(z))).astype(BF16)
    return x + gate * _dot(mix_ref[...], w["w_out"][wl])


_W_NAMES = ("norm_g", "w_in_t", "w_head", "q_norm", "w_uq", "kv_norm", "w_ukv", "conv_w", "lam",
            "subln", "w_out", "final_norm")
_SCR_NAMES = ("hb", "o", "mix", "qb", "dqb", "kcat", "vm", "dka", "dvb")
_CTX_NAMES = ("ckv", "kr", "dk", "dv")
N_ROPE = 6


def _ctx_kernel(x_ref, mod_ref, *refs, n_seq):
    w = dict(zip(_W_NAMES, refs[:len(_W_NAMES)]))
    rest = refs[len(_W_NAMES):]
    y_ref, state = rest[0], rest[1:5]
    scr = dict(zip(_SCR_NAMES, rest[5:]))
    n_b = x_ref.shape[0]
    x = x_ref[...].reshape(n_b * n_seq, D_MODEL)
    for l in range(DEPTH):
        mod = tuple(mod_ref[l, j:j + 1, :] for j in range(3))
        x = _trunk_layer(l, l, x, n_seq, mod, w, scr, None, None, state)
    y_ref[...] = _rms(x, w["final_norm"][...]).reshape(n_b, n_seq, D_MODEL)


def _lat_kernel(x_ref, mod_ref, *refs, n_seq, layer):
    ctx = {k: r.at[0, 0] for k, r in zip(_CTX_NAMES, refs[:4])}
    rope = (refs[4:7], refs[7:10])
    w = dict(zip(_W_NAMES, refs[4 + N_ROPE:4 + N_ROPE + len(_W_NAMES)]))
    rest = refs[4 + N_ROPE + len(_W_NAMES):]
    y_ref = rest[0]
    scr = dict(zip(_SCR_NAMES, rest[1:]))
    mod = tuple(mod_ref[0, 0, j:j + 1, :] for j in range(3))
    x = _trunk_layer(layer, 0, x_ref[0], n_seq, mod, w, scr, rope, ctx, None)
    if layer == DEPTH - 1:
        x = _rms(x, w["final_norm"][...])
    y_ref[0] = x


def _ada_kernel(cond_ref, w_ref, b_ref, o_ref):
    c = cond_ref[...]
    s = (c * jax.nn.sigmoid(c)).astype(BF16)
    o_ref[0] = _dot(s, w_ref[0].astype(BF16)) + b_ref[0]


def _const_spec(shape, layer=None):
    if layer is None:
        zeros = (0,) * len(shape)
        return pl.BlockSpec(shape, lambda *_: zeros, pipeline_mode=pl.Buffered(1))
    idx = (layer,) + (0,) * (len(shape) - 1)
    return pl.BlockSpec((1,) + tuple(shape[1:]), lambda *_: idx, pipeline_mode=pl.Buffered(1))


def _scratch(n_b, n_seq, nk):
    m_rows = n_b * n_seq
    return [
        pltpu.VMEM((m_rows, D_MODEL), BF16),
        pltpu.VMEM((m_rows, D_MIX), F32),
        pltpu.VMEM((m_rows, D_MIX), BF16),
        pltpu.VMEM((m_rows, 3 * LANES), BF16),
        pltpu.VMEM((m_rows, W_DIFF), BF16),
        pltpu.VMEM((n_b, MLA_HEADS, nk, 2 * LANES), BF16),
        pltpu.VMEM((n_b, MLA_HEADS, nk, 2 * LANES), BF16),
        pltpu.VMEM((n_b, 2 * DIFF_HEADS, nk, LANES), BF16),
        pltpu.VMEM((n_b, DIFF_HEADS, nk, 2 * LANES), BF16),
    ]


def _rope_tables(n, rot_dim):
    half = rot_dim // 2
    inv = ROPE_THETA ** (-(np.arange(0, half, 2, dtype=np.float32) / half))
    pos = np.arange(n)
    ang_r = (pos // GRID_W).astype(np.float32)[:, None] * inv
    ang_c = (pos % GRID_W).astype(np.float32)[:, None] * inv
    ang = np.concatenate([ang_r, ang_r, ang_c, ang_c], axis=-1)
    cos, sin = np.cos(ang), np.sin(ang)
    q = rot_dim // 4
    first = (np.arange(rot_dim) % (2 * q)) < q
    sin_a = np.where(first, -sin, 0.0)
    sin_b = np.where(first, 0.0, sin)
    reps = LANES // rot_dim
    return tuple(jnp.asarray(np.tile(t, (1, reps)), dtype=F32) for t in (cos, sin_a, sin_b))


def _prep_weights(w_in, w_uq, w_ukv, w_out):
    w_in_t = jnp.swapaxes(w_in, 1, 2).astype(BF16)
    w_head = jnp.concatenate(
        [w_in_t[:, R_Q:R_KR, :]] + [w_in_t[:, R_KR:R_CB, :]] * (LANES // MLA_ROPE), axis=1)
    uq = w_uq.reshape(DEPTH, MLA_Q_LORA, MLA_HEADS, MLA_NOPE + MLA_ROPE)
    w_uq_p = jnp.concatenate(
        [uq[..., :MLA_NOPE].reshape(DEPTH, MLA_Q_LORA, -1),
         uq[..., MLA_NOPE:].reshape(DEPTH, MLA_Q_LORA, -1)], axis=-1).astype(BF16)
    ukv = w_ukv.reshape(DEPTH, MLA_KV_LORA, MLA_HEADS, MLA_NOPE + MLA_V)
    w_ukv_p = jnp.concatenate(
        [ukv[..., :MLA_NOPE].reshape(DEPTH, MLA_KV_LORA, -1),
         ukv[..., MLA_NOPE:].reshape(DEPTH, MLA_KV_LORA, -1)], axis=-1).astype(BF16)
    return w_in_t, w_head, w_uq_p, w_ukv_p, w_out.astype(BF16)


def kernel(x_prompt, x_sample, c, cache_mla_ckv, cache_mla_krope, cache_diff_k, cache_diff_v,
           c_ctx, norm_g, ada_w, ada_b, w_in, mla_q_norm, w_uq, mla_kv_norm, w_ukv,
           conv_w, diff_lambda, diff_subln, w_out, final_norm):
    batch, seq, _ = x_prompt.shape
    dec_batch, dec_seq, _ = x_sample.shape
    past = cache_mla_ckv.shape[2]
    cparams = pltpu.CompilerParams(dimension_semantics=("arbitrary",),
                                   vmem_limit_bytes=VMEM_LIMIT_BYTES)

    n_cond = 8
    cond = jnp.concatenate(
        [c_ctx[None], c, jnp.zeros((n_cond - 1 - dec_batch, D_MODEL), F32)], axis=0)
    tn = 1024
    mod = pl.pallas_call(
        _ada_kernel,
        grid=(DEPTH, 3 * D_MODEL // tn),
        in_specs=[pl.BlockSpec((n_cond, D_MODEL), lambda l, j: (0, 0)),
                  pl.BlockSpec((1, D_MODEL, tn), lambda l, j: (l, 0, j)),
                  pl.BlockSpec((1, 1, tn), lambda l, j: (l, 0, j))],
        out_specs=pl.BlockSpec((1, n_cond, tn), lambda l, j: (l, 0, j)),
        out_shape=jax.ShapeDtypeStruct((DEPTH, n_cond, 3 * D_MODEL), F32),
        compiler_params=pltpu.CompilerParams(dimension_semantics=("arbitrary", "arbitrary")),
        name="ada",
    )(cond, ada_w, ada_b.reshape(DEPTH, 1, 3 * D_MODEL))
    mod = mod.reshape(DEPTH, n_cond, 3, D_MODEL)
    mod_ctx = mod[:, 0]
    mod_lat = jnp.swapaxes(mod[:, 1:1 + dec_batch], 0, 1)

    w_in_t, w_head, w_uq_p, w_ukv_p, w_out_b = _prep_weights(w_in, w_uq, w_ukv, w_out)
    row3 = lambda a: a.reshape(DEPTH, 1, a.shape[-1])
    weights = (row3(norm_g), w_in_t, w_head, row3(mla_q_norm), w_uq_p, row3(mla_kv_norm), w_ukv_p,
               conv_w, diff_lambda, row3(diff_subln), w_out_b, final_norm.reshape(1, D_MODEL))

    tb = 2
    state_shapes = ((seq, MLA_KV_LORA), (MLA_ROPE, seq), (W_DIFF, seq),
                    (DIFF_HEADS * seq, DIFF_V))
    y_prompt, s_ckv, s_kr, s_dk, s_dv = pl.pallas_call(
        functools.partial(_ctx_kernel, n_seq=seq),
        grid=(batch // tb,),
        in_specs=[pl.BlockSpec((tb, seq, D_MODEL), lambda i: (i, 0, 0)),
                  _const_spec(mod_ctx.shape)] + [_const_spec(a.shape) for a in weights],
        out_specs=[pl.BlockSpec((tb, seq, D_MODEL), lambda i: (i, 0, 0))]
        + [pl.BlockSpec((tb, DEPTH) + s, lambda i: (i, 0, 0, 0)) for s in state_shapes],
        out_shape=[jax.ShapeDtypeStruct((batch, seq, D_MODEL), F32)]
        + [jax.ShapeDtypeStruct((batch, DEPTH) + s, F32) for s in state_shapes],
        scratch_shapes=_scratch(tb, seq, seq),
        compiler_params=cparams,
        name="ctx",
    )(x_prompt, mod_ctx, *weights)

    tables = _rope_tables(dec_seq, MLA_ROPE) + _rope_tables(dec_seq, DIFF_QK)
    ctx_in = (cache_mla_ckv,
              jnp.swapaxes(cache_mla_krope, 2, 3),
              jnp.transpose(cache_diff_k, (0, 1, 3, 4, 5, 2)).reshape(dec_batch, DEPTH, W_DIFF, past),
              cache_diff_v.reshape(dec_batch, DEPTH, past * DIFF_HEADS, DIFF_V))
    h_lat = x_sample
    for l in range(DEPTH):
        h_lat = pl.pallas_call(
            functools.partial(_lat_kernel, n_seq=dec_seq, layer=l),
            grid=(dec_batch,),
            in_specs=[pl.BlockSpec((1, dec_seq, D_MODEL), lambda i: (i, 0, 0),
                                   pipeline_mode=pl.Buffered(1)),
                      pl.BlockSpec((1, 1, 3, D_MODEL), lambda i, l=l: (i, l, 0, 0))]
            + [pl.BlockSpec((1, 1) + a.shape[2:], lambda i, l=l: (i, l, 0, 0)) for a in ctx_in]
            + [_const_spec(t.shape) for t in tables]
            + [_const_spec(a.shape, layer=l) for a in weights[:-1]]
            + [_const_spec(weights[-1].shape)],
            out_specs=pl.BlockSpec((1, dec_seq, D_MODEL), lambda i: (i, 0, 0),
                                   pipeline_mode=pl.Buffered(1)),
            out_shape=jax.ShapeDtypeStruct((dec_batch, dec_seq, D_MODEL), F32),
            scratch_shapes=_scratch(1, dec_seq, past + dec_seq),
            compiler_params=cparams,
            name=f"latent_{l}",
        )(h_lat, mod_lat, *ctx_in, *tables, *weights)

    return (y_prompt, h_lat, s_ckv,
            jnp.swapaxes(s_kr, 2, 3),
            jnp.transpose(s_dk.reshape(batch, DEPTH, DIFF_HEADS, 2, DIFF_QK, seq),
                          (0, 1, 5, 2, 3, 4)),
            s_dv.reshape(batch, DEPTH, seq, DIFF_HEADS, DIFF_V))
```

```python
import functools
import math

import numpy as np
import jax
import jax.numpy as jnp
from jax import lax
from jax.experimental import pallas as pl
from jax.experimental.pallas import tpu as pltpu

D_MODEL = 1024
DEPTH = 2
GRID_W = 64
MLA_HEADS = 4
MLA_NOPE = 64
MLA_ROPE = 32
MLA_V = 64
MLA_Q_LORA = 256
MLA_KV_LORA = 128
W_MLA = MLA_HEADS * MLA_V
W_CONV = 256
DIFF_HEADS = 4
DIFF_QK = 64
DIFF_V = 2 * DIFF_QK
W_DIFF = DIFF_HEADS * DIFF_V
D_MIX = W_MLA + W_CONV + W_DIFF
ROPE_THETA = 10000.0
EPS = 1e-6

LANES = 128
Q_TILE = 256
LOG2E = math.log2(math.e)
MLA_SCALE = (MLA_NOPE + MLA_ROPE) ** -0.5 * LOG2E
DIFF_SCALE = DIFF_QK ** -0.5 * LOG2E
VMEM_LIMIT_BYTES = 54 * 1024 * 1024

R_Q = 0
R_KV = R_Q + MLA_Q_LORA
R_KR = R_KV + MLA_KV_LORA
R_CB = R_KR + MLA_ROPE
R_CC = R_CB + W_CONV
R_CX = R_CC + W_CONV
R_DQ = R_CX + W_CONV
R_DK = R_DQ + W_DIFF
R_DV = R_DK + W_DIFF
R_Z = R_DV + W_DIFF
D_IN = R_Z + D_MIX

BF16 = jnp.bfloat16
F32 = jnp.float32


def _lambda_init(layer):
    return 0.8 - 0.6 * math.exp(-0.3 * layer)


def _dot(a, b):
    return jnp.dot(a, b, preferred_element_type=F32)


def _dot_nt(a, b):
    return lax.dot_general(a, b, (((1,), (1,)), ((), ())), preferred_element_type=F32)


def _rms(x, g):
    ms = jnp.mean(x * x, axis=-1, keepdims=True)
    return x * lax.rsqrt(ms + EPS) * g


def _exp_scores(s):
    m = s[:, 0:LANES]
    for t in range(1, s.shape[1] // LANES):
        m = jnp.maximum(m, s[:, LANES * t:LANES * (t + 1)])
    return jnp.exp2(s - jnp.max(m, axis=-1, keepdims=True)).astype(BF16)


def _weighted(e, v_ones):
    r = _dot(e, v_ones)
    return r[:, 0:LANES], 1.0 / r[:, LANES:2 * LANES]


def _lane_mask(lo, hi):
    lane = lax.broadcasted_iota(jnp.int32, (1, LANES), 1)
    return jnp.where((lane >= lo) & (lane < hi), 1.0, 0.0).astype(F32)


def _rope(x, tables, q):
    cos, sin_a, sin_b = tables
    tiles = []
    for t in range(x.shape[1] // LANES):
        xt = x[:, LANES * t:LANES * (t + 1)]
        tiles.append(xt * cos + pltpu.roll(xt, LANES - q, 1) * sin_a + pltpu.roll(xt, q, 1) * sin_b)
    return tiles[0] if len(tiles) == 1 else jnp.concatenate(tiles, axis=1)


def _attend(q, dq, keys, kcat_ref, vm_ref, dka_ref, dvb_ref, lam, subln, lam_init):
    qr = q[:, 2 * LANES:3 * LANES]
    o_mla = []
    for t in range(2):
        qcat = jnp.concatenate([q[:, LANES * t:LANES * (t + 1)], qr], axis=1)
        acc = None
        for hh in range(2):
            h = 2 * t + hh
            o, rinv = _weighted(_exp_scores(_dot_nt(qcat, kcat_ref[h, keys, :])),
                                vm_ref[h, keys, :])
            acc = o * rinv if acc is None else acc + o * rinv
        o_mla.append(acc)
    o_diff = []
    for h in range(DIFF_HEADS):
        qt = dq[:, LANES * h:LANES * (h + 1)]
        o0, rinv0 = _weighted(_exp_scores(_dot_nt(qt, dka_ref[2 * h, keys, :])),
                              dvb_ref[h, keys, :])
        o1, rinv1 = _weighted(_exp_scores(_dot_nt(qt, dka_ref[2 * h + 1, keys, :])),
                              dvb_ref[h, keys, :])
        o = o0 * rinv0 - o1 * (lam * rinv1)
        o_diff.append(_rms(o, subln) * (1.0 - lam_init))
    return jnp.concatenate(o_mla, axis=1), jnp.concatenate(o_diff, axis=1)


def _trunk_layer(l, wl, x, n_seq, mod, w, scr, rope, ctx, state):
    m_rows = x.shape[0]
    n_b = m_rows // n_seq
    n_ctx = 0 if ctx is None else ctx["ckv"].shape[0]
    nk = n_ctx + n_seq
    shift, scale, gate = mod
    hb_ref, o_ref, mix_ref = scr["hb"], scr["o"], scr["mix"]
    qb_ref, dqb_ref = scr["qb"], scr["dqb"]
    kcat_ref, vm_ref, dka_ref, dvb_ref = scr["kcat"], scr["vm"], scr["dka"], scr["dvb"]
    w_in_t, w_ukv = w["w_in_t"], w["w_ukv"]
    half = (_lane_mask(0, 64), _lane_mask(64, 128))
    new_keys, old_keys = pl.ds(n_ctx, n_seq), pl.ds(0, n_ctx)
    if rope is not None:
        rope_m = tuple(r[...] for r in rope[0])
        rope_d = tuple(r[...] for r in rope[1])

    def seq_rows(b):
        return slice(b * n_seq, (b + 1) * n_seq)

    gs = w["norm_g"][l:l + 1, :] * (1.0 + scale)
    ms = jnp.mean(x * x, axis=-1, keepdims=True)
    hb_ref[...] = (x * lax.rsqrt(ms + EPS) * gs + shift).astype(BF16)

    def proj(r0, r1):
        return _dot_nt(hb_ref[...], w_in_t[wl, r0:r1, :])

    head = _dot_nt(hb_ref[...], w["w_head"][wl])

    cqn = _rms(head[:, 0:MLA_Q_LORA], w["q_norm"][l:l + 1, :]).astype(BF16)
    q = _dot(cqn, w["w_uq"][wl]) * MLA_SCALE
    if rope is not None:
        q = jnp.concatenate(
            [q[:, 0:2 * LANES], _rope(q[:, 2 * LANES:3 * LANES], rope_m, MLA_ROPE // 4)], axis=1)
    qb_ref[...] = q.astype(BF16)

    def fill_kv(b, dst, kv):
        for h in range(MLA_HEADS):
            t, hh = divmod(h, 2)
            kcat_ref[b, h, dst, 0:LANES] = (kv[:, LANES * t:LANES * (t + 1)] * half[hh]).astype(BF16)
            vm_ref[b, h, dst, 0:LANES] = (
                kv[:, 2 * LANES + LANES * t:2 * LANES + LANES * (t + 1)] * half[hh]).astype(BF16)

    def fill_kr(b, dst, kr):
        for h in range(MLA_HEADS):
            kcat_ref[b, h, dst, LANES:2 * LANES] = (
                kr * _lane_mask(MLA_ROPE * h, MLA_ROPE * (h + 1))).astype(BF16)

    def fill_dk(b, dst, dk):
        for h in range(DIFF_HEADS):
            for a in range(2):
                dka_ref[b, 2 * h + a, dst, :] = (
                    dk[:, LANES * h:LANES * (h + 1)] * half[a]).astype(BF16)

    ckv = _rms(head[:, R_KV:R_KR], w["kv_norm"][l:l + 1, :])
    kv = _dot(ckv.astype(BF16), w_ukv[wl])
    for b in range(n_b):
        if state is not None:
            state[0][b, l] = ckv[seq_rows(b)]
        fill_kv(b, new_keys, kv[seq_rows(b)])
        if ctx is not None:
            fill_kv(b, old_keys, _dot(ctx["ckv"][...].astype(BF16), w_ukv[wl]))

    krr = head[:, R_KR:R_KR + LANES]
    krr_pos = krr if rope is None else _rope(krr, rope_m, MLA_ROPE // 4)
    for b in range(n_b):
        if state is not None:
            state[1][b, l] = krr[seq_rows(b)].T[0:MLA_ROPE, :]
        fill_kr(b, new_keys, krr_pos[seq_rows(b)])
        if ctx is not None:
            fill_kr(b, old_keys, jnp.concatenate([ctx["kr"][...]] * (LANES // MLA_ROPE), axis=0).T)

    dq = proj(R_DQ, R_DK) * DIFF_SCALE
    if rope is not None:
        dq = _rope(dq, rope_d, DIFF_QK // 4)
    dqb_ref[...] = dq.astype(BF16)

    dk = proj(R_DK, R_DV)
    dk_pos = dk if rope is None else _rope(dk, rope_d, DIFF_QK // 4)
    for b in range(n_b):
        if state is not None:
            state[2][b, l] = dk[seq_rows(b)].T
        fill_dk(b, new_keys, dk_pos[seq_rows(b)])
        if ctx is not None:
            fill_dk(b, old_keys, ctx["dk"][...].T)

    dv = proj(R_DV, R_Z)
    for b in range(n_b):
        for h in range(DIFF_HEADS):
            dv_h = dv[seq_rows(b), LANES * h:LANES * (h + 1)]
            if state is not None:
                state[3][b, l, pl.ds(h, n_seq, stride=DIFF_HEADS), :] = dv_h
            dvb_ref[b, h, new_keys, 0:LANES] = dv_h.astype(BF16)
            if ctx is not None:
                dvb_ref[b, h, old_keys, 0:LANES] = (
                    ctx["dv"][pl.ds(h, n_ctx, stride=DIFF_HEADS), :].astype(BF16))
            ones = jnp.ones((nk, LANES), BF16)
            dvb_ref[b, h, :, LANES:2 * LANES] = ones
            vm_ref[b, h, :, LANES:2 * LANES] = ones

    lf = w["lam"][l]
    lam = (jnp.exp(jnp.sum(lf[0:1] * lf[1:2], axis=-1, keepdims=True))
           - jnp.exp(jnp.sum(lf[2:3] * lf[3:4], axis=-1, keepdims=True)) + _lambda_init(l))
    subln = w["subln"][l:l + 1, :]
    keys = pl.ds(0, nk)
    for b in range(n_b):
        def q_block(start, b=b):
            qrows = pl.ds(start, Q_TILE)
            o_mla, o_diff = _attend(qb_ref[qrows, :], dqb_ref[qrows, :], keys,
                                    kcat_ref.at[b], vm_ref.at[b], dka_ref.at[b], dvb_ref.at[b],
                                    lam, subln, _lambda_init(l))
            o_ref[qrows, 0:W_MLA] = o_mla
            o_ref[qrows, W_MLA + W_CONV:D_MIX] = o_diff

        n_qb = n_seq // Q_TILE
        if n_qb == 1:
            q_block(b * n_seq)
        else:
            def q_step(i, carry, b=b, q_block=q_block):
                q_block(pl.multiple_of(b * n_seq + i * Q_TILE, Q_TILE))
                return carry
            lax.fori_loop(0, n_qb, q_step, 0)

    conv = proj(R_CB, R_DQ)
    u = conv[:, W_CONV:2 * W_CONV] * conv[:, 2 * W_CONV:3 * W_CONV]
    pos = lax.broadcasted_iota(jnp.int32, (m_rows, 1), 0) % n_seq
    u_prev = jnp.where(pos == 0, 0.0, pltpu.roll(u, 1, 0))
    u_next = jnp.where(pos == n_seq - 1, 0.0, pltpu.roll(u, m_rows - 1, 0))
    cw = w["conv_w"][:, l, :]
    o_ref[:, W_MLA:W_MLA + W_CONV] = conv[:, 0:W_CONV] * (
        u_prev * cw[0:1] + u * cw[1:2] + u_next * cw[2:3])

    for c0 in range(0, D_MIX, 512):
        z = proj(R_Z + c0, R_Z + c0 + 512)
        mix_ref[:, c0:c0 + 512] = (o_ref[:, c0:c0 + 512] * (z * jax.nn.sigmoid(z))).astype(BF16)
    return x + gate * _dot(mix_ref[...], w["w_out"][wl])


_W_NAMES = ("norm_g", "w_in_t", "w_head", "q_norm", "w_uq", "kv_norm", "w_ukv", "conv_w", "lam",
            "subln", "w_out", "final_norm")
_PER_LAYER_BLOCKS = ("w_in_t", "w_head", "w_uq", "w_ukv", "w_out")
_SCR_NAMES = ("hb", "o", "mix", "qb", "dqb", "kcat", "vm", "dka", "dvb")
_CTX_NAMES = ("ckv", "kr", "dk", "dv")
N_ROPE = 6


def _ctx_kernel(x_ref, mod_ref, *refs, n_seq):
    w = dict(zip(_W_NAMES, refs[:len(_W_NAMES)]))
    rest = refs[len(_W_NAMES):]
    y_ref, state = rest[0], rest[1:5]
    scr = dict(zip(_SCR_NAMES, rest[5:]))
    n_b = x_ref.shape[0]
    x = x_ref[...].reshape(n_b * n_seq, D_MODEL)
    for l in range(DEPTH):
        mod = tuple(mod_ref[l, j, 0:1, :] for j in range(3))
        x = _trunk_layer(l, l, x, n_seq, mod, w, scr, None, None, state)
    y_ref[...] = _rms(x, w["final_norm"][...]).reshape(n_b, n_seq, D_MODEL)


def _lat_kernel(x_ref, mod_ref, *refs, n_seq, layer):
    ctx = {k: r.at[0, 0] for k, r in zip(_CTX_NAMES, refs[:4])}
    rope = (refs[4:7], refs[7:10])
    w = dict(zip(_W_NAMES, refs[4 + N_ROPE:4 + N_ROPE + len(_W_NAMES)]))
    rest = refs[4 + N_ROPE + len(_W_NAMES):]
    y_ref = rest[0]
    scr = dict(zip(_SCR_NAMES, rest[1:]))
    cond_row = pl.ds(1 + pl.program_id(0), 1)
    mod = tuple(mod_ref[layer, j, cond_row, :] for j in range(3))
    x = _trunk_layer(layer, 0, x_ref[0], n_seq, mod, w, scr, rope, ctx, None)
    if layer == DEPTH - 1:
        x = _rms(x, w["final_norm"][...])
    y_ref[0] = x


def _ada_kernel(cctx_ref, c_ref, w_ref, b_ref, o_ref):
    n_rows = o_ref.shape[2]
    pad = jnp.zeros((n_rows - 1 - c_ref.shape[0], D_MODEL), F32)
    c = jnp.concatenate([cctx_ref[...], c_ref[...], pad], axis=0)
    s = (c * jax.nn.sigmoid(c)).astype(BF16)
    o_ref[0, 0] = _dot(s, w_ref[0].astype(BF16)) + b_ref[pl.ds(pl.program_id(0), 1), :]


def _const_spec(shape, layer=None):
    if layer is None:
        zeros = (0,) * len(shape)
        return pl.BlockSpec(shape, lambda *_: zeros, pipeline_mode=pl.Buffered(1))
    idx = (layer,) + (0,) * (len(shape) - 1)
    return pl.BlockSpec((1,) + tuple(shape[1:]), lambda *_: idx, pipeline_mode=pl.Buffered(1))


def _scratch(n_b, n_seq, nk):
    m_rows = n_b * n_seq
    return [
        pltpu.VMEM((m_rows, D_MODEL), BF16),
        pltpu.VMEM((m_rows, D_MIX), F32),
        pltpu.VMEM((m_rows, D_MIX), BF16),
        pltpu.VMEM((m_rows, 3 * LANES), BF16),
        pltpu.VMEM((m_rows, W_DIFF), BF16),
        pltpu.VMEM((n_b, MLA_HEADS, nk, 2 * LANES), BF16),
        pltpu.VMEM((n_b, MLA_HEADS, nk, 2 * LANES), BF16),
        pltpu.VMEM((n_b, 2 * DIFF_HEADS, nk, LANES), BF16),
        pltpu.VMEM((n_b, DIFF_HEADS, nk, 2 * LANES), BF16),
    ]


def _rope_tables(n, rot_dim):
    half = rot_dim // 2
    inv = ROPE_THETA ** (-(np.arange(0, half, 2, dtype=np.float32) / half))
    pos = np.arange(n)
    ang_r = (pos // GRID_W).astype(np.float32)[:, None] * inv
    ang_c = (pos % GRID_W).astype(np.float32)[:, None] * inv
    ang = np.concatenate([ang_r, ang_r, ang_c, ang_c], axis=-1)
    cos, sin = np.cos(ang), np.sin(ang)
    q = rot_dim // 4
    first = (np.arange(rot_dim) % (2 * q)) < q
    sin_a = np.where(first, -sin, 0.0)
    sin_b = np.where(first, 0.0, sin)
    reps = LANES // rot_dim
    return tuple(jnp.asarray(np.tile(t, (1, reps)), dtype=F32) for t in (cos, sin_a, sin_b))


def _prep_weights(w_in, w_uq, w_ukv, w_out):
    w_in_t = jnp.swapaxes(w_in, 1, 2).astype(BF16)
    w_head = jnp.concatenate(
        [w_in_t[:, R_Q:R_KR, :]] + [w_in_t[:, R_KR:R_CB, :]] * (LANES // MLA_ROPE), axis=1)
    uq = w_uq.reshape(DEPTH, MLA_Q_LORA, MLA_HEADS, MLA_NOPE + MLA_ROPE)
    w_uq_p = jnp.concatenate(
        [uq[..., :MLA_NOPE].reshape(DEPTH, MLA_Q_LORA, -1),
         uq[..., MLA_NOPE:].reshape(DEPTH, MLA_Q_LORA, -1)], axis=-1).astype(BF16)
    ukv = w_ukv.reshape(DEPTH, MLA_KV_LORA, MLA_HEADS, MLA_NOPE + MLA_V)
    w_ukv_p = jnp.concatenate(
        [ukv[..., :MLA_NOPE].reshape(DEPTH, MLA_KV_LORA, -1),
         ukv[..., MLA_NOPE:].reshape(DEPTH, MLA_KV_LORA, -1)], axis=-1).astype(BF16)
    return w_in_t, w_head, w_uq_p, w_ukv_p, w_out.astype(BF16)


def kernel(x_prompt, x_sample, c, cache_mla_ckv, cache_mla_krope, cache_diff_k, cache_diff_v,
           c_ctx, norm_g, ada_w, ada_b, w_in, mla_q_norm, w_uq, mla_kv_norm, w_ukv,
           conv_w, diff_lambda, diff_subln, w_out, final_norm):
    batch, seq, _ = x_prompt.shape
    dec_batch, dec_seq, _ = x_sample.shape
    past = cache_mla_ckv.shape[2]
    cparams = pltpu.CompilerParams(dimension_semantics=("arbitrary",),
                                   vmem_limit_bytes=VMEM_LIMIT_BYTES)

    n_cond = 8
    mod = pl.pallas_call(
        _ada_kernel,
        grid=(DEPTH, 3),
        in_specs=[pl.BlockSpec((1, D_MODEL), lambda l, j: (0, 0)),
                  pl.BlockSpec((dec_batch, D_MODEL), lambda l, j: (0, 0)),
                  pl.BlockSpec((1, D_MODEL, D_MODEL), lambda l, j: (l, 0, j)),
                  pl.BlockSpec((DEPTH, D_MODEL), lambda l, j: (0, j))],
        out_specs=pl.BlockSpec((1, 1, n_cond, D_MODEL), lambda l, j: (l, j, 0, 0)),
        out_shape=jax.ShapeDtypeStruct((DEPTH, 3, n_cond, D_MODEL), F32),
        compiler_params=pltpu.CompilerParams(dimension_semantics=("arbitrary", "arbitrary")),
        name="ada",
    )(c_ctx.reshape(1, D_MODEL), c, ada_w, ada_b)

    w_in_t, w_head, w_uq_p, w_ukv_p, w_out_b = _prep_weights(w_in, w_uq, w_ukv, w_out)
    weights = (norm_g, w_in_t, w_head, mla_q_norm, w_uq_p, mla_kv_norm, w_ukv_p,
               jnp.swapaxes(conv_w, 0, 1), diff_lambda, diff_subln, w_out_b,
               final_norm.reshape(1, D_MODEL))

    tb = 2
    state_shapes = ((seq, MLA_KV_LORA), (MLA_ROPE, seq), (W_DIFF, seq),
                    (DIFF_HEADS * seq, DIFF_V))
    y_prompt, s_ckv, s_kr, s_dk, s_dv = pl.pallas_call(
        functools.partial(_ctx_kernel, n_seq=seq),
        grid=(batch // tb,),
        in_specs=[pl.BlockSpec((tb, seq, D_MODEL), lambda i: (i, 0, 0)),
                  _const_spec(mod.shape)] + [_const_spec(a.shape) for a in weights],
        out_specs=[pl.BlockSpec((tb, seq, D_MODEL), lambda i: (i, 0, 0))]
        + [pl.BlockSpec((tb, DEPTH) + s, lambda i: (i, 0, 0, 0)) for s in state_shapes],
        out_shape=[jax.ShapeDtypeStruct((batch, seq, D_MODEL), F32)]
        + [jax.ShapeDtypeStruct((batch, DEPTH) + s, F32) for s in state_shapes],
        scratch_shapes=_scratch(tb, seq, seq),
        compiler_params=cparams,
        name="ctx",
    )(x_prompt, mod, *weights)

    tables = _rope_tables(dec_seq, MLA_ROPE) + _rope_tables(dec_seq, DIFF_QK)
    ctx_in = (cache_mla_ckv,
              jnp.swapaxes(cache_mla_krope, 2, 3),
              jnp.transpose(cache_diff_k, (0, 1, 3, 4, 5, 2)).reshape(dec_batch, DEPTH, W_DIFF, past),
              cache_diff_v.reshape(dec_batch, DEPTH, past * DIFF_HEADS, DIFF_V))
    h_lat = x_sample
    for l in range(DEPTH):
        h_lat = pl.pallas_call(
            functools.partial(_lat_kernel, n_seq=dec_seq, layer=l),
            grid=(dec_batch,),
            in_specs=[pl.BlockSpec((1, dec_seq, D_MODEL), lambda i: (i, 0, 0),
                                   pipeline_mode=pl.Buffered(1)),
                      _const_spec(mod.shape)]
            + [pl.BlockSpec((1, 1) + a.shape[2:], lambda i, l=l: (i, l, 0, 0)) for a in ctx_in]
            + [_const_spec(t.shape) for t in tables]
            + [_const_spec(a.shape, layer=l if name in _PER_LAYER_BLOCKS else None)
               for name, a in zip(_W_NAMES, weights)],
            out_specs=pl.BlockSpec((1, dec_seq, D_MODEL), lambda i: (i, 0, 0),
                                   pipeline_mode=pl.Buffered(1)),
            out_shape=jax.ShapeDtypeStruct((dec_batch, dec_seq, D_MODEL), F32),
            scratch_shapes=_scratch(1, dec_seq, past + dec_seq),
            compiler_params=cparams,
            name=f"latent_{l}",
        )(h_lat, mod, *ctx_in, *tables, *weights)

    return (y_prompt, h_lat, s_ckv,
            jnp.swapaxes(s_kr, 2, 3),
            jnp.transpose(s_dk.reshape(batch, DEPTH, DIFF_HEADS, 2, DIFF_QK, seq),
                          (0, 1, 5, 2, 3, 4)),
            s_dv.reshape(batch, DEPTH, seq, DIFF_HEADS, DIFF_V))
```

```python
import functools
import math

import numpy as np
import jax
import jax.numpy as jnp
from jax import lax
from jax.experimental import pallas as pl
from jax.experimental.pallas import tpu as pltpu

D_MODEL = 1024
DEPTH = 2
GRID_W = 64
MLA_HEADS = 4
MLA_NOPE = 64
MLA_ROPE = 32
MLA_V = 64
MLA_Q_LORA = 256
MLA_KV_LORA = 128
W_MLA = MLA_HEADS * MLA_V
W_CONV = 256
DIFF_HEADS = 4
DIFF_QK = 64
DIFF_V = 2 * DIFF_QK
W_DIFF = DIFF_HEADS * DIFF_V
D_MIX = W_MLA + W_CONV + W_DIFF
ROPE_THETA = 10000.0
EPS = 1e-6

LANES = 128
Q_TILE = 256
LOG2E = math.log2(math.e)
MLA_SCALE = (MLA_NOPE + MLA_ROPE) ** -0.5 * LOG2E
DIFF_SCALE = DIFF_QK ** -0.5 * LOG2E
VMEM_LIMIT_BYTES = 54 * 1024 * 1024

R_Q = 0
R_KV = R_Q + MLA_Q_LORA
R_KR = R_KV + MLA_KV_LORA
R_CB = R_KR + MLA_ROPE
R_CC = R_CB + W_CONV
R_CX = R_CC + W_CONV
R_DQ = R_CX + W_CONV
R_DK = R_DQ + W_DIFF
R_DV = R_DK + W_DIFF
R_Z = R_DV + W_DIFF
D_IN = R_Z + D_MIX

BF16 = jnp.bfloat16
F32 = jnp.float32


def _lambda_init(layer):
    return 0.8 - 0.6 * math.exp(-0.3 * layer)


def _dot(a, b):
    return jnp.dot(a, b, preferred_element_type=F32)


def _dot_nt(a, b):
    return lax.dot_general(a, b, (((1,), (1,)), ((), ())), preferred_element_type=F32)


def _rms(x, g):
    ms = jnp.mean(x * x, axis=-1, keepdims=True)
    return x * lax.rsqrt(ms + EPS) * g


def _exp_scores(s):
    m = s[:, 0:LANES]
    for t in range(1, s.shape[1] // LANES):
        m = jnp.maximum(m, s[:, LANES * t:LANES * (t + 1)])
    return jnp.exp2(s - jnp.max(m, axis=-1, keepdims=True)).astype(BF16)


def _weighted(e, v_ones):
    r = _dot(e, v_ones)
    return r[:, 0:LANES], 1.0 / r[:, LANES:2 * LANES]


def _lane_mask(lo, hi):
    lane = lax.broadcasted_iota(jnp.int32, (1, LANES), 1)
    return jnp.where((lane >= lo) & (lane < hi), 1.0, 0.0).astype(F32)


def _rope(x, tables, q):
    cos, sin_a, sin_b = tables
    tiles = []
    for t in range(x.shape[1] // LANES):
        xt = x[:, LANES * t:LANES * (t + 1)]
        tiles.append(xt * cos + pltpu.roll(xt, LANES - q, 1) * sin_a + pltpu.roll(xt, q, 1) * sin_b)
    return tiles[0] if len(tiles) == 1 else jnp.concatenate(tiles, axis=1)


def _attend(q, dq, keys, kcat_ref, vm_ref, dka_ref, dvb_ref, lam, subln, lam_init):
    qr = q[:, 2 * LANES:3 * LANES]
    o_mla = []
    for t in range(2):
        qcat = jnp.concatenate([q[:, LANES * t:LANES * (t + 1)], qr], axis=1)
        acc = None
        for hh in range(2):
            h = 2 * t + hh
            o, rinv = _weighted(_exp_scores(_dot_nt(qcat, kcat_ref[h, keys, :])),
                                vm_ref[h, keys, :])
            acc = o * rinv if acc is None else acc + o * rinv
        o_mla.append(acc)
    o_diff = []
    for h in range(DIFF_HEADS):
        qt = dq[:, LANES * h:LANES * (h + 1)]
        o0, rinv0 = _weighted(_exp_scores(_dot_nt(qt, dka_ref[2 * h, keys, :])),
                              dvb_ref[h, keys, :])
        o1, rinv1 = _weighted(_exp_scores(_dot_nt(qt, dka_ref[2 * h + 1, keys, :])),
                              dvb_ref[h, keys, :])
        o = o0 * rinv0 - o1 * (lam * rinv1)
        o_diff.append(_rms(o, subln) * (1.0 - lam_init))
    return jnp.concatenate(o_mla, axis=1), jnp.concatenate(o_diff, axis=1)


def _trunk_layer(l, wl, x, n_seq, mod, w, scr, rope, ctx, state):
    m_rows = x.shape[0]
    n_b = m_rows // n_seq
    n_ctx = 0 if ctx is None else ctx["ckv"].shape[0]
    nk = n_ctx + n_seq
    shift, scale, gate = mod
    hb_ref, o_ref, mix_ref = scr["hb"], scr["o"], scr["mix"]
    qb_ref, dqb_ref = scr["qb"], scr["dqb"]
    kcat_ref, vm_ref, dka_ref, dvb_ref = scr["kcat"], scr["vm"], scr["dka"], scr["dvb"]
    w_in_t, w_ukv = w["w_in_t"], w["w_ukv"]
    half = (_lane_mask(0, 64), _lane_mask(64, 128))
    new_keys, old_keys = pl.ds(n_ctx, n_seq), pl.ds(0, n_ctx)
    if rope is not None:
        rope_m = tuple(r[...] for r in rope[0])
        rope_d = tuple(r[...] for r in rope[1])

    def seq_rows(b):
        return slice(b * n_seq, (b + 1) * n_seq)

    gs = w["norm_g"][l:l + 1, :] * (1.0 + scale)
    ms = jnp.mean(x * x, axis=-1, keepdims=True)
    hb_ref[...] = (x * lax.rsqrt(ms + EPS) * gs + shift).astype(BF16)

    def proj(r0, r1):
        return _dot_nt(hb_ref[...], w_in_t[wl, r0:r1, :])

    head = _dot_nt(hb_ref[...], w["w_head"][wl])

    cqn = _rms(head[:, 0:MLA_Q_LORA], w["q_norm"][l:l + 1, :]).astype(BF16)
    q = _dot(cqn, w["w_uq"][wl]) * MLA_SCALE
    if rope is not None:
        q = jnp.concatenate(
            [q[:, 0:2 * LANES], _rope(q[:, 2 * LANES:3 * LANES], rope_m, MLA_ROPE // 4)], axis=1)
    qb_ref[...] = q.astype(BF16)

    def fill_kv(b, dst, kv):
        for h in range(MLA_HEADS):
            t, hh = divmod(h, 2)
            kcat_ref[b, h, dst, 0:LANES] = (kv[:, LANES * t:LANES * (t + 1)] * half[hh]).astype(BF16)
            vm_ref[b, h, dst, 0:LANES] = (
                kv[:, 2 * LANES + LANES * t:2 * LANES + LANES * (t + 1)] * half[hh]).astype(BF16)

    def fill_kr(b, dst, kr):
        for h in range(MLA_HEADS):
            kcat_ref[b, h, dst, LANES:2 * LANES] = (
                kr * _lane_mask(MLA_ROPE * h, MLA_ROPE * (h + 1))).astype(BF16)

    def fill_dk(b, dst, dk):
        for h in range(DIFF_HEADS):
            for a in range(2):
                dka_ref[b, 2 * h + a, dst, :] = (
                    dk[:, LANES * h:LANES * (h + 1)] * half[a]).astype(BF16)

    ckv = _rms(head[:, R_KV:R_KR], w["kv_norm"][l:l + 1, :])
    kv = _dot(ckv.astype(BF16), w_ukv[wl])
    for b in range(n_b):
        if state is not None:
            state[0][b, l] = ckv[seq_rows(b)]
        fill_kv(b, new_keys, kv[seq_rows(b)])
        if ctx is not None:
            fill_kv(b, old_keys, _dot(ctx["ckv"][...].astype(BF16), w_ukv[wl]))

    krr = head[:, R_KR:R_KR + LANES]
    krr_pos = krr if rope is None else _rope(krr, rope_m, MLA_ROPE // 4)
    for b in range(n_b):
        if state is not None:
            state[1][b, l] = krr[seq_rows(b)].T[0:MLA_ROPE, :]
        fill_kr(b, new_keys, krr_pos[seq_rows(b)])
        if ctx is not None:
            fill_kr(b, old_keys, jnp.concatenate([ctx["kr"][...]] * (LANES // MLA_ROPE), axis=0).T)

    dq = proj(R_DQ, R_DK) * DIFF_SCALE
    if rope is not None:
        dq = _rope(dq, rope_d, DIFF_QK // 4)
    dqb_ref[...] = dq.astype(BF16)

    dk = proj(R_DK, R_DV)
    dk_pos = dk if rope is None else _rope(dk, rope_d, DIFF_QK // 4)
    for b in range(n_b):
        if state is not None:
            state[2][b, l] = dk[seq_rows(b)].T
        fill_dk(b, new_keys, dk_pos[seq_rows(b)])
        if ctx is not None:
            fill_dk(b, old_keys, ctx["dk"][...].T)

    dv = proj(R_DV, R_Z)
    for b in range(n_b):
        for h in range(DIFF_HEADS):
            dv_h = dv[seq_rows(b), LANES * h:LANES * (h + 1)]
            if state is not None:
                state[3][b, l, pl.ds(h, n_seq, stride=DIFF_HEADS), :] = dv_h
            dvb_ref[b, h, new_keys, 0:LANES] = dv_h.astype(BF16)
            if ctx is not None:
                dvb_ref[b, h, old_keys, 0:LANES] = (
                    ctx["dv"][pl.ds(h, n_ctx, stride=DIFF_HEADS), :].astype(BF16))
            ones = jnp.ones((nk, LANES), BF16)
            dvb_ref[b, h, :, LANES:2 * LANES] = ones
            vm_ref[b, h, :, LANES:2 * LANES] = ones

    lf = w["lam"][l]
    lam = (jnp.exp(jnp.sum(lf[0:1] * lf[1:2], axis=-1, keepdims=True))
           - jnp.exp(jnp.sum(lf[2:3] * lf[3:4], axis=-1, keepdims=True)) + _lambda_init(l))
    subln = w["subln"][l:l + 1, :]
    keys = pl.ds(0, nk)
    for b in range(n_b):
        def q_block(start, b=b):
            qrows = pl.ds(start, Q_TILE)
            o_mla, o_diff = _attend(qb_ref[qrows, :], dqb_ref[qrows, :], keys,
                                    kcat_ref.at[b], vm_ref.at[b], dka_ref.at[b], dvb_ref.at[b],
                                    lam, subln, _lambda_init(l))
            o_ref[qrows, 0:W_MLA] = o_mla
            o_ref[qrows, W_MLA + W_CONV:D_MIX] = o_diff

        n_qb = n_seq // Q_TILE
        if n_qb == 1:
            q_block(b * n_seq)
        else:
            def q_step(i, carry, b=b, q_block=q_block):
                q_block(pl.multiple_of(b * n_seq + i * Q_TILE, Q_TILE))
                return carry
            lax.fori_loop(0, n_qb, q_step, 0)

    conv = proj(R_CB, R_DQ)
    u = conv[:, W_CONV:2 * W_CONV] * conv[:, 2 * W_CONV:3 * W_CONV]
    pos = lax.broadcasted_iota(jnp.int32, (m_rows, 1), 0) % n_seq
    u_prev = jnp.where(pos == 0, 0.0, pltpu.roll(u, 1, 0))
    u_next = jnp.where(pos == n_seq - 1, 0.0, pltpu.roll(u, m_rows - 1, 0))
    cw = w["conv_w"][:, l, :]
    o_ref[:, W_MLA:W_MLA + W_CONV] = conv[:, 0:W_CONV] * (
        u_prev * cw[0:1] + u * cw[1:2] + u_next * cw[2:3])

    for c0 in range(0, D_MIX, 512):
        z = proj(R_Z + c0, R_Z + c0 + 512)
        mix_ref[:, c0:c0 + 512] = (o_ref[:, c0:c0 + 512] * (z * jax.nn.sigmoid(z))).astype(BF16)
    return x + gate * _dot(mix_ref[...], w["w_out"][wl])


_W_NAMES = ("norm_g", "w_in_t", "w_head", "q_norm", "w_uq", "kv_norm", "w_ukv", "conv_w", "lam",
            "subln", "w_out", "final_norm")
_PER_LAYER_BLOCKS = ("w_in_t", "w_head", "w_uq", "w_ukv", "w_out")
_SCR_NAMES = ("hb", "o", "mix", "qb", "dqb", "kcat", "vm", "dka", "dvb")
_CTX_NAMES = ("ckv", "kr", "dk", "dv")
N_ROPE = 6


def _ctx_kernel(x_ref, mod_ref, *refs, n_seq):
    w = dict(zip(_W_NAMES, refs[:len(_W_NAMES)]))
    rest = refs[len(_W_NAMES):]
    y_ref, state = rest[0], rest[1:5]
    scr = dict(zip(_SCR_NAMES, rest[5:]))
    n_b = x_ref.shape[0]
    x = x_ref[...].reshape(n_b * n_seq, D_MODEL)
    for l in range(DEPTH):
        mod = tuple(mod_ref[l, j, 0:1, :] for j in range(3))
        x = _trunk_layer(l, l, x, n_seq, mod, w, scr, None, None, state)
    y_ref[...] = _rms(x, w["final_norm"][...]).reshape(n_b, n_seq, D_MODEL)


def _lat_kernel(x_ref, mod_ref, *refs, n_seq, layer):
    ctx = {k: r.at[0, 0] for k, r in zip(_CTX_NAMES, refs[:4])}
    rope = (refs[4:7], refs[7:10])
    w = dict(zip(_W_NAMES, refs[4 + N_ROPE:4 + N_ROPE + len(_W_NAMES)]))
    rest = refs[4 + N_ROPE + len(_W_NAMES):]
    y_ref = rest[0]
    scr = dict(zip(_SCR_NAMES, rest[1:]))
    cond_row = pl.ds(1 + pl.program_id(0), 1)
    mod = tuple(mod_ref[layer, j, cond_row, :] for j in range(3))
    x = _trunk_layer(layer, 0, x_ref[0], n_seq, mod, w, scr, rope, ctx, None)
    if layer == DEPTH - 1:
        x = _rms(x, w["final_norm"][...])
    y_ref[0] = x


def _prep_kernel(cctx_ref, c_ref, ada_w_ref, ada_b_ref, w_in_ref, w_out_ref,
                 mod_ref, w_in_bf_ref, w_out_bf_ref, w_head_ref):
    n_rows = mod_ref.shape[2]
    pad = jnp.zeros((n_rows - 1 - c_ref.shape[0], D_MODEL), F32)
    c = jnp.concatenate([cctx_ref[...], c_ref[...], pad], axis=0)
    s = (c * jax.nn.sigmoid(c)).astype(BF16)
    mod_ref[0, 0] = (_dot(s, ada_w_ref[0].astype(BF16))
                     + ada_b_ref[pl.ds(pl.program_id(0), 1), :])
    w_in_bf_ref[0] = w_in_ref[0].astype(BF16)

    @pl.when(pl.program_id(1) == 0)
    def _():
        w_out_bf_ref[0] = w_out_ref[0].astype(BF16)
        w_head_ref[0, R_Q:R_KR, :] = w_in_ref[0, R_Q:R_KR, :].astype(BF16)
        kr = w_in_ref[0, R_KR:R_CB, :].astype(BF16)
        for r in range(LANES // MLA_ROPE):
            w_head_ref[0, R_KR + MLA_ROPE * r:R_KR + MLA_ROPE * (r + 1), :] = kr


def _const_spec(shape, layer=None):
    if layer is None:
        zeros = (0,) * len(shape)
        return pl.BlockSpec(shape, lambda *_: zeros, pipeline_mode=pl.Buffered(1))
    idx = (layer,) + (0,) * (len(shape) - 1)
    return pl.BlockSpec((1,) + tuple(shape[1:]), lambda *_: idx, pipeline_mode=pl.Buffered(1))


def _scratch(n_b, n_seq, nk):
    m_rows = n_b * n_seq
    return [
        pltpu.VMEM((m_rows, D_MODEL), BF16),
        pltpu.VMEM((m_rows, D_MIX), F32),
        pltpu.VMEM((m_rows, D_MIX), BF16),
        pltpu.VMEM((m_rows, 3 * LANES), BF16),
        pltpu.VMEM((m_rows, W_DIFF), BF16),
        pltpu.VMEM((n_b, MLA_HEADS, nk, 2 * LANES), BF16),
        pltpu.VMEM((n_b, MLA_HEADS, nk, 2 * LANES), BF16),
        pltpu.VMEM((n_b, 2 * DIFF_HEADS, nk, LANES), BF16),
        pltpu.VMEM((n_b, DIFF_HEADS, nk, 2 * LANES), BF16),
    ]


def _rope_tables(n, rot_dim):
    half = rot_dim // 2
    inv = ROPE_THETA ** (-(np.arange(0, half, 2, dtype=np.float32) / half))
    pos = np.arange(n)
    ang_r = (pos // GRID_W).astype(np.float32)[:, None] * inv
    ang_c = (pos % GRID_W).astype(np.float32)[:, None] * inv
    ang = np.concatenate([ang_r, ang_r, ang_c, ang_c], axis=-1)
    cos, sin = np.cos(ang), np.sin(ang)
    q = rot_dim // 4
    first = (np.arange(rot_dim) % (2 * q)) < q
    sin_a = np.where(first, -sin, 0.0)
    sin_b = np.where(first, 0.0, sin)
    reps = LANES // rot_dim
    return tuple(jnp.asarray(np.tile(t, (1, reps)), dtype=F32) for t in (cos, sin_a, sin_b))


def _prep_up_weights(w_uq, w_ukv):
    uq = w_uq.reshape(DEPTH, MLA_Q_LORA, MLA_HEADS, MLA_NOPE + MLA_ROPE)
    w_uq_p = jnp.concatenate(
        [uq[..., :MLA_NOPE].reshape(DEPTH, MLA_Q_LORA, -1),
         uq[..., MLA_NOPE:].reshape(DEPTH, MLA_Q_LORA, -1)], axis=-1).astype(BF16)
    ukv = w_ukv.reshape(DEPTH, MLA_KV_LORA, MLA_HEADS, MLA_NOPE + MLA_V)
    w_ukv_p = jnp.concatenate(
        [ukv[..., :MLA_NOPE].reshape(DEPTH, MLA_KV_LORA, -1),
         ukv[..., MLA_NOPE:].reshape(DEPTH, MLA_KV_LORA, -1)], axis=-1).astype(BF16)
    return w_uq_p, w_ukv_p


def kernel(x_prompt, x_sample, c, cache_mla_ckv, cache_mla_krope, cache_diff_k, cache_diff_v,
           c_ctx, norm_g, ada_w, ada_b, w_in, mla_q_norm, w_uq, mla_kv_norm, w_ukv,
           conv_w, diff_lambda, diff_subln, w_out, final_norm):
    batch, seq, _ = x_prompt.shape
    dec_batch, dec_seq, _ = x_sample.shape
    past = cache_mla_ckv.shape[2]
    cparams = pltpu.CompilerParams(dimension_semantics=("arbitrary",),
                                   vmem_limit_bytes=VMEM_LIMIT_BYTES)

    n_cond = 8
    n_third = 3
    in_rows = D_IN // n_third
    head_rows = R_KR + LANES
    mod, w_in_t, w_out_b, w_head = pl.pallas_call(
        _prep_kernel,
        grid=(DEPTH, n_third),
        in_specs=[pl.BlockSpec((1, D_MODEL), lambda l, j: (0, 0)),
                  pl.BlockSpec((dec_batch, D_MODEL), lambda l, j: (0, 0)),
                  pl.BlockSpec((1, D_MODEL, D_MODEL), lambda l, j: (l, 0, j)),
                  pl.BlockSpec((DEPTH, D_MODEL), lambda l, j: (0, j)),
                  pl.BlockSpec((1, in_rows, D_MODEL), lambda l, j: (l, j, 0)),
                  pl.BlockSpec((1, D_MIX, D_MODEL), lambda l, j: (l, 0, 0))],
        out_specs=[pl.BlockSpec((1, 1, n_cond, D_MODEL), lambda l, j: (l, j, 0, 0)),
                   pl.BlockSpec((1, in_rows, D_MODEL), lambda l, j: (l, j, 0)),
                   pl.BlockSpec((1, D_MIX, D_MODEL), lambda l, j: (l, 0, 0)),
                   pl.BlockSpec((1, head_rows, D_MODEL), lambda l, j: (l, 0, 0))],
        out_shape=[jax.ShapeDtypeStruct((DEPTH, 3, n_cond, D_MODEL), F32),
                   jax.ShapeDtypeStruct((DEPTH, D_IN, D_MODEL), BF16),
                   jax.ShapeDtypeStruct((DEPTH, D_MIX, D_MODEL), BF16),
                   jax.ShapeDtypeStruct((DEPTH, head_rows, D_MODEL), BF16)],
        compiler_params=pltpu.CompilerParams(dimension_semantics=("arbitrary", "arbitrary"),
                                             vmem_limit_bytes=VMEM_LIMIT_BYTES),
        name="prep",
    )(c_ctx.reshape(1, D_MODEL), c, ada_w, ada_b, jnp.swapaxes(w_in, 1, 2), w_out)

    w_uq_p, w_ukv_p = _prep_up_weights(w_uq, w_ukv)
    weights = (norm_g, w_in_t, w_head, mla_q_norm, w_uq_p, mla_kv_norm, w_ukv_p,
               jnp.swapaxes(conv_w, 0, 1), diff_lambda, diff_subln, w_out_b,
               final_norm.reshape(1, D_MODEL))

    tb = 2
    state_shapes = ((seq, MLA_KV_LORA), (MLA_ROPE, seq), (W_DIFF, seq),
                    (DIFF_HEADS * seq, DIFF_V))
    y_prompt, s_ckv, s_kr, s_dk, s_dv = pl.pallas_call(
        functools.partial(_ctx_kernel, n_seq=seq),
        grid=(batch // tb,),
        in_specs=[pl.BlockSpec((tb, seq, D_MODEL), lambda i: (i, 0, 0)),
                  _const_spec(mod.shape)] + [_const_spec(a.shape) for a in weights],
        out_specs=[pl.BlockSpec((tb, seq, D_MODEL), lambda i: (i, 0, 0))]
        + [pl.BlockSpec((tb, DEPTH) + s, lambda i: (i, 0, 0, 0)) for s in state_shapes],
        out_shape=[jax.ShapeDtypeStruct((batch, seq, D_MODEL), F32)]
        + [jax.ShapeDtypeStruct((batch, DEPTH) + s, F32) for s in state_shapes],
        scratch_shapes=_scratch(tb, seq, seq),
        compiler_params=cparams,
        name="ctx",
    )(x_prompt, mod, *weights)

    tables = _rope_tables(dec_seq, MLA_ROPE) + _rope_tables(dec_seq, DIFF_QK)
    ctx_in = (cache_mla_ckv,
              jnp.swapaxes(cache_mla_krope, 2, 3),
              jnp.transpose(cache_diff_k, (0, 1, 3, 4, 5, 2)).reshape(dec_batch, DEPTH, W_DIFF, past),
              cache_diff_v.reshape(dec_batch, DEPTH, past * DIFF_HEADS, DIFF_V))
    h_lat = x_sample
    for l in range(DEPTH):
        h_lat = pl.pallas_call(
            functools.partial(_lat_kernel, n_seq=dec_seq, layer=l),
            grid=(dec_batch,),
            in_specs=[pl.BlockSpec((1, dec_seq, D_MODEL), lambda i: (i, 0, 0),
                                   pipeline_mode=pl.Buffered(1)),
                      _const_spec(mod.shape)]
            + [pl.BlockSpec((1, 1) + a.shape[2:], lambda i, l=l: (i, l, 0, 0)) for a in ctx_in]
            + [_const_spec(t.shape) for t in tables]
            + [_const_spec(a.shape, layer=l if name in _PER_LAYER_BLOCKS else None)
               for name, a in zip(_W_NAMES, weights)],
            out_specs=pl.BlockSpec((1, dec_seq, D_MODEL), lambda i: (i, 0, 0),
                                   pipeline_mode=pl.Buffered(1)),
            out_shape=jax.ShapeDtypeStruct((dec_batch, dec_seq, D_MODEL), F32),
            scratch_shapes=_scratch(1, dec_seq, past + dec_seq),
            compiler_params=cparams,
            name=f"latent_{l}",
        )(h_lat, mod, *ctx_in, *tables, *weights)

    return (y_prompt, h_lat, s_ckv,
            jnp.swapaxes(s_kr, 2, 3),
            jnp.transpose(s_dk.reshape(batch, DEPTH, DIFF_HEADS, 2, DIFF_QK, seq),
                          (0, 1, 5, 2, 3, 4)),
            s_dv.reshape(batch, DEPTH, seq, DIFF_HEADS, DIFF_V))
```

```python
import functools
import math

import numpy as np
import jax
import jax.numpy as jnp
from jax import lax
from jax.experimental import pallas as pl
from jax.experimental.pallas import tpu as pltpu

D_MODEL = 1024
DEPTH = 2
GRID_W = 64
MLA_HEADS = 4
MLA_NOPE = 64
MLA_ROPE = 32
MLA_V = 64
MLA_Q_LORA = 256
MLA_KV_LORA = 128
W_MLA = MLA_HEADS * MLA_V
W_CONV = 256
DIFF_HEADS = 4
DIFF_QK = 64
DIFF_V = 2 * DIFF_QK
W_DIFF = DIFF_HEADS * DIFF_V
D_MIX = W_MLA + W_CONV + W_DIFF
ROPE_THETA = 10000.0
EPS = 1e-6

LANES = 128
Q_TILE = 256
LOG2E = math.log2(math.e)
MLA_SCALE = (MLA_NOPE + MLA_ROPE) ** -0.5 * LOG2E
DIFF_SCALE = DIFF_QK ** -0.5 * LOG2E
VMEM_LIMIT_BYTES = 54 * 1024 * 1024

R_Q = 0
R_KV = R_Q + MLA_Q_LORA
R_KR = R_KV + MLA_KV_LORA
R_CB = R_KR + MLA_ROPE
R_CC = R_CB + W_CONV
R_CX = R_CC + W_CONV
R_DQ = R_CX + W_CONV
R_DK = R_DQ + W_DIFF
R_DV = R_DK + W_DIFF
R_Z = R_DV + W_DIFF
D_IN = R_Z + D_MIX

BF16 = jnp.bfloat16
F32 = jnp.float32


def _lambda_init(layer):
    return 0.8 - 0.6 * math.exp(-0.3 * layer)


def _dot(a, b):
    return jnp.dot(a, b, preferred_element_type=F32)


def _dot_nt(a, b):
    return lax.dot_general(a, b, (((1,), (1,)), ((), ())), preferred_element_type=F32)


def _rms(x, g):
    ms = jnp.mean(x * x, axis=-1, keepdims=True)
    return x * lax.rsqrt(ms + EPS) * g


def _exp_scores(s):
    m = s[:, 0:LANES]
    for t in range(1, s.shape[1] // LANES):
        m = jnp.maximum(m, s[:, LANES * t:LANES * (t + 1)])
    return jnp.exp2(s - jnp.max(m, axis=-1, keepdims=True)).astype(BF16)


def _weighted(e, v_ones):
    r = _dot(e, v_ones)
    return r[:, 0:LANES], 1.0 / r[:, LANES:2 * LANES]


def _lane_mask(lo, hi):
    lane = lax.broadcasted_iota(jnp.int32, (1, LANES), 1)
    return jnp.where((lane >= lo) & (lane < hi), 1.0, 0.0).astype(F32)


def _rope(x, tables, q):
    cos, sin_a, sin_b = tables
    tiles = []
    for t in range(x.shape[1] // LANES):
        xt = x[:, LANES * t:LANES * (t + 1)]
        tiles.append(xt * cos + pltpu.roll(xt, LANES - q, 1) * sin_a + pltpu.roll(xt, q, 1) * sin_b)
    return tiles[0] if len(tiles) == 1 else jnp.concatenate(tiles, axis=1)


def _attend(q, dq, keys, kcat_ref, vm_ref, dka_ref, dvb_ref, lam, subln, lam_init):
    qr = q[:, 2 * LANES:3 * LANES]
    o_mla = []
    for t in range(2):
        qcat = jnp.concatenate([q[:, LANES * t:LANES * (t + 1)], qr], axis=1)
        acc = None
        for hh in range(2):
            h = 2 * t + hh
            o, rinv = _weighted(_exp_scores(_dot_nt(qcat, kcat_ref[h, keys, :])),
                                vm_ref[h, keys, :])
            acc = o * rinv if acc is None else acc + o * rinv
        o_mla.append(acc)
    o_diff = []
    for h in range(DIFF_HEADS):
        qt = dq[:, LANES * h:LANES * (h + 1)]
        o0, rinv0 = _weighted(_exp_scores(_dot_nt(qt, dka_ref[2 * h, keys, :])),
                              dvb_ref[h, keys, :])
        o1, rinv1 = _weighted(_exp_scores(_dot_nt(qt, dka_ref[2 * h + 1, keys, :])),
                              dvb_ref[h, keys, :])
        o = o0 * rinv0 - o1 * (lam * rinv1)
        o_diff.append(_rms(o, subln) * (1.0 - lam_init))
    return jnp.concatenate(o_mla, axis=1), jnp.concatenate(o_diff, axis=1)


def _trunk_layer(l, wl, x, n_seq, mod, w, scr, rope, ctx, state):
    m_rows = x.shape[0]
    n_b = m_rows // n_seq
    n_ctx = 0 if ctx is None else ctx["ckv"].shape[0]
    nk = n_ctx + n_seq
    shift, scale, gate = mod
    hb_ref, o_ref, mix_ref = scr["hb"], scr["o"], scr["mix"]
    qb_ref, dqb_ref = scr["qb"], scr["dqb"]
    kcat_ref, vm_ref, dka_ref, dvb_ref = scr["kcat"], scr["vm"], scr["dka"], scr["dvb"]
    w_in_t, w_ukv = w["w_in_t"], w["w_ukv"]
    half = (_lane_mask(0, 64), _lane_mask(64, 128))
    new_keys, old_keys = pl.ds(n_ctx, n_seq), pl.ds(0, n_ctx)
    if rope is not None:
        rope_m = tuple(r[...] for r in rope[0])
        rope_d = tuple(r[...] for r in rope[1])

    def seq_rows(b):
        return slice(b * n_seq, (b + 1) * n_seq)

    gs = w["norm_g"][l:l + 1, :] * (1.0 + scale)
    ms = jnp.mean(x * x, axis=-1, keepdims=True)
    hb_ref[...] = (x * lax.rsqrt(ms + EPS) * gs + shift).astype(BF16)

    def proj(r0, r1):
        return _dot_nt(hb_ref[...], w_in_t[wl, r0:r1, :])

    head = _dot_nt(hb_ref[...], w["w_head"][wl])

    cqn = _rms(head[:, 0:MLA_Q_LORA], w["q_norm"][l:l + 1, :]).astype(BF16)
    q = _dot(cqn, w["w_uq"][wl]) * MLA_SCALE
    if rope is not None:
        q = jnp.concatenate(
            [q[:, 0:2 * LANES], _rope(q[:, 2 * LANES:3 * LANES], rope_m, MLA_ROPE // 4)], axis=1)
    qb_ref[...] = q.astype(BF16)

    def fill_kv(b, dst, kv):
        for h in range(MLA_HEADS):
            t, hh = divmod(h, 2)
            kcat_ref[b, h, dst, 0:LANES] = (kv[:, LANES * t:LANES * (t + 1)] * half[hh]).astype(BF16)
            vm_ref[b, h, dst, 0:LANES] = (
                kv[:, 2 * LANES + LANES * t:2 * LANES + LANES * (t + 1)] * half[hh]).astype(BF16)

    def fill_kr(b, dst, kr):
        for h in range(MLA_HEADS):
            kcat_ref[b, h, dst, LANES:2 * LANES] = (
                kr * _lane_mask(MLA_ROPE * h, MLA_ROPE * (h + 1))).astype(BF16)

    def fill_dk(b, dst, dk):
        for h in range(DIFF_HEADS):
            for a in range(2):
                dka_ref[b, 2 * h + a, dst, :] = (
                    dk[:, LANES * h:LANES * (h + 1)] * half[a]).astype(BF16)

    ckv = _rms(head[:, R_KV:R_KR], w["kv_norm"][l:l + 1, :])
    kv = _dot(ckv.astype(BF16), w_ukv[wl])
    for b in range(n_b):
        if state is not None:
            state[0][b, l] = ckv[seq_rows(b)]
        fill_kv(b, new_keys, kv[seq_rows(b)])
        if ctx is not None:
            fill_kv(b, old_keys, _dot(ctx["ckv"][...].astype(BF16), w_ukv[wl]))

    krr = head[:, R_KR:R_KR + LANES]
    krr_pos = krr if rope is None else _rope(krr, rope_m, MLA_ROPE // 4)
    for b in range(n_b):
        if state is not None:
            state[1][b, l] = krr[seq_rows(b)].T[0:MLA_ROPE, :]
        fill_kr(b, new_keys, krr_pos[seq_rows(b)])
        if ctx is not None:
            fill_kr(b, old_keys, jnp.concatenate([ctx["kr"][...]] * (LANES // MLA_ROPE), axis=0).T)

    dq = proj(R_DQ, R_DK) * DIFF_SCALE
    if rope is not None:
        dq = _rope(dq, rope_d, DIFF_QK // 4)
    dqb_ref[...] = dq.astype(BF16)

    dk = proj(R_DK, R_DV)
    dk_pos = dk if rope is None else _rope(dk, rope_d, DIFF_QK // 4)
    for b in range(n_b):
        if state is not None:
            state[2][b, l] = dk[seq_rows(b)].T
        fill_dk(b, new_keys, dk_pos[seq_rows(b)])
        if ctx is not None:
            fill_dk(b, old_keys, ctx["dk"][...].T)

    dv = proj(R_DV, R_Z)
    for b in range(n_b):
        for h in range(DIFF_HEADS):
            dv_h = dv[seq_rows(b), LANES * h:LANES * (h + 1)]
            if state is not None:
                state[3][b, l, pl.ds(h, n_seq, stride=DIFF_HEADS), :] = dv_h
            dvb_ref[b, h, new_keys, 0:LANES] = dv_h.astype(BF16)
            if ctx is not None:
                dvb_ref[b, h, old_keys, 0:LANES] = (
                    ctx["dv"][pl.ds(h, n_ctx, stride=DIFF_HEADS), :].astype(BF16))
            ones = jnp.ones((nk, LANES), BF16)
            dvb_ref[b, h, :, LANES:2 * LANES] = ones
            vm_ref[b, h, :, LANES:2 * LANES] = ones

    lf = w["lam"][l]
    lam = (jnp.exp(jnp.sum(lf[0:1] * lf[1:2], axis=-1, keepdims=True))
           - jnp.exp(jnp.sum(lf[2:3] * lf[3:4], axis=-1, keepdims=True)) + _lambda_init(l))
    subln = w["subln"][l:l + 1, :]
    keys = pl.ds(0, nk)
    for b in range(n_b):
        def q_block(start, b=b):
            qrows = pl.ds(start, Q_TILE)
            o_mla, o_diff = _attend(qb_ref[qrows, :], dqb_ref[qrows, :], keys,
                                    kcat_ref.at[b], vm_ref.at[b], dka_ref.at[b], dvb_ref.at[b],
                                    lam, subln, _lambda_init(l))
            o_ref[qrows, 0:W_MLA] = o_mla
            o_ref[qrows, W_MLA + W_CONV:D_MIX] = o_diff

        n_qb = n_seq // Q_TILE
        if n_qb == 1:
            q_block(b * n_seq)
        else:
            def q_step(i, carry, b=b, q_block=q_block):
                q_block(pl.multiple_of(b * n_seq + i * Q_TILE, Q_TILE))
                return carry
            lax.fori_loop(0, n_qb, q_step, 0)

    conv = proj(R_CB, R_DQ)
    u = conv[:, W_CONV:2 * W_CONV] * conv[:, 2 * W_CONV:3 * W_CONV]
    pos = lax.broadcasted_iota(jnp.int32, (m_rows, 1), 0) % n_seq
    u_prev = jnp.where(pos == 0, 0.0, pltpu.roll(u, 1, 0))
    u_next = jnp.where(pos == n_seq - 1, 0.0, pltpu.roll(u, m_rows - 1, 0))
    cw = w["conv_w"][:, l, :]
    o_ref[:, W_MLA:W_MLA + W_CONV] = conv[:, 0:W_CONV] * (
        u_prev * cw[0:1] + u * cw[1:2] + u_next * cw[2:3])

    for c0 in range(0, D_MIX, 512):
        z = proj(R_Z + c0, R_Z + c0 + 512)
        mix_ref[:, c0:c0 + 512] = (o_ref[:, c0:c0 + 512] * (z * jax.nn.sigmoid(z))).astype(BF16)
    return x + gate * _dot(mix_ref[...], w["w_out"][wl])


_W_NAMES = ("norm_g", "w_in_t", "w_head", "q_norm", "w_uq", "kv_norm", "w_ukv", "conv_w", "lam",
            "subln", "w_out", "final_norm")
_PER_LAYER_BLOCKS = ("w_in_t", "w_head", "w_uq", "w_ukv", "w_out")
_SCR_NAMES = ("hb", "o", "mix", "qb", "dqb", "kcat", "vm", "dka", "dvb")
_CTX_NAMES = ("ckv", "kr", "dk", "dv")
N_ROPE = 6


def _ctx_kernel(x_ref, mod_ref, *refs, n_seq):
    w = dict(zip(_W_NAMES, refs[:len(_W_NAMES)]))
    rest = refs[len(_W_NAMES):]
    y_ref, state = rest[0], rest[1:5]
    scr = dict(zip(_SCR_NAMES, rest[5:]))
    n_b = x_ref.shape[0]
    x = x_ref[...].reshape(n_b * n_seq, D_MODEL)
    for l in range(DEPTH):
        mod = tuple(mod_ref[l, j, 0:1, :] for j in range(3))
        x = _trunk_layer(l, l, x, n_seq, mod, w, scr, None, None, state)
    y_ref[...] = _rms(x, w["final_norm"][...]).reshape(n_b, n_seq, D_MODEL)


def _lat_kernel(x_ref, mod_ref, *refs, n_seq, layer):
    ctx = {k: r.at[0, 0] for k, r in zip(_CTX_NAMES, refs[:4])}
    rope = (refs[4:7], refs[7:10])
    w = dict(zip(_W_NAMES, refs[4 + N_ROPE:4 + N_ROPE + len(_W_NAMES)]))
    rest = refs[4 + N_ROPE + len(_W_NAMES):]
    y_ref = rest[0]
    scr = dict(zip(_SCR_NAMES, rest[1:]))
    cond_row = pl.ds(1 + pl.program_id(0), 1)
    mod = tuple(mod_ref[layer, j, cond_row, :] for j in range(3))
    x = _trunk_layer(layer, 0, x_ref[0], n_seq, mod, w, scr, rope, ctx, None)
    if layer == DEPTH - 1:
        x = _rms(x, w["final_norm"][...])
    y_ref[0] = x


def _prep_kernel(cctx_ref, c_ref, ada_w_ref, ada_b_ref, w_in_ref, w_out_ref, w_uq_ref, w_ukv_ref,
                 mod_ref, w_in_bf_ref, w_out_bf_ref, w_head_ref, w_uq_p_ref, w_ukv_p_ref):
    n_rows = mod_ref.shape[2]
    pad = jnp.zeros((n_rows - 1 - c_ref.shape[0], D_MODEL), F32)
    c = jnp.concatenate([cctx_ref[...], c_ref[...], pad], axis=0)
    s = (c * jax.nn.sigmoid(c)).astype(BF16)
    mod_ref[0, 0] = (_dot(s, ada_w_ref[0].astype(BF16))
                     + ada_b_ref[pl.ds(pl.program_id(0), 1), :])
    w_in_bf_ref[0] = w_in_ref[0].astype(BF16)

    @pl.when(pl.program_id(1) == 0)
    def _():
        w_out_bf_ref[0] = w_out_ref[0].astype(BF16)
        w_head_ref[0, R_Q:R_KR, :] = w_in_ref[0, R_Q:R_KR, :].astype(BF16)
        kr = w_in_ref[0, R_KR:R_CB, :].astype(BF16)
        for r in range(LANES // MLA_ROPE):
            w_head_ref[0, R_KR + MLA_ROPE * r:R_KR + MLA_ROPE * (r + 1), :] = kr
        for src, dst, first in ((w_uq_ref, w_uq_p_ref, MLA_NOPE), (w_ukv_ref, w_ukv_p_ref, MLA_NOPE)):
            a = src[0]
            per = a.shape[1] // MLA_HEADS
            cols = ([a[:, per * h:per * h + first] for h in range(MLA_HEADS)]
                    + [a[:, per * h + first:per * (h + 1)] for h in range(MLA_HEADS)])
            dst[0] = jnp.concatenate(cols, axis=1).astype(BF16)


def _const_spec(shape, layer=None):
    if layer is None:
        zeros = (0,) * len(shape)
        return pl.BlockSpec(shape, lambda *_: zeros, pipeline_mode=pl.Buffered(1))
    idx = (layer,) + (0,) * (len(shape) - 1)
    return pl.BlockSpec((1,) + tuple(shape[1:]), lambda *_: idx, pipeline_mode=pl.Buffered(1))


def _scratch(n_b, n_seq, nk):
    m_rows = n_b * n_seq
    return [
        pltpu.VMEM((m_rows, D_MODEL), BF16),
        pltpu.VMEM((m_rows, D_MIX), F32),
        pltpu.VMEM((m_rows, D_MIX), BF16),
        pltpu.VMEM((m_rows, 3 * LANES), BF16),
        pltpu.VMEM((m_rows, W_DIFF), BF16),
        pltpu.VMEM((n_b, MLA_HEADS, nk, 2 * LANES), BF16),
        pltpu.VMEM((n_b, MLA_HEADS, nk, 2 * LANES), BF16),
        pltpu.VMEM((n_b, 2 * DIFF_HEADS, nk, LANES), BF16),
        pltpu.VMEM((n_b, DIFF_HEADS, nk, 2 * LANES), BF16),
    ]


def _rope_tables(n, rot_dim):
    half = rot_dim // 2
    inv = ROPE_THETA ** (-(np.arange(0, half, 2, dtype=np.float32) / half))
    pos = np.arange(n)
    ang_r = (pos // GRID_W).astype(np.float32)[:, None] * inv
    ang_c = (pos % GRID_W).astype(np.float32)[:, None] * inv
    ang = np.concatenate([ang_r, ang_r, ang_c, ang_c], axis=-1)
    cos, sin = np.cos(ang), np.sin(ang)
    q = rot_dim // 4
    first = (np.arange(rot_dim) % (2 * q)) < q
    sin_a = np.where(first, -sin, 0.0)
    sin_b = np.where(first, 0.0, sin)
    reps = LANES // rot_dim
    return tuple(jnp.asarray(np.tile(t, (1, reps)), dtype=F32) for t in (cos, sin_a, sin_b))


def kernel(x_prompt, x_sample, c, cache_mla_ckv, cache_mla_krope, cache_diff_k, cache_diff_v,
           c_ctx, norm_g, ada_w, ada_b, w_in, mla_q_norm, w_uq, mla_kv_norm, w_ukv,
           conv_w, diff_lambda, diff_subln, w_out, final_norm):
    batch, seq, _ = x_prompt.shape
    dec_batch, dec_seq, _ = x_sample.shape
    past = cache_mla_ckv.shape[2]
    cparams = pltpu.CompilerParams(dimension_semantics=("arbitrary",),
                                   vmem_limit_bytes=VMEM_LIMIT_BYTES)

    n_cond = 8
    n_third = 3
    in_rows = D_IN // n_third
    head_rows = R_KR + LANES
    per_layer_spec = lambda a: pl.BlockSpec((1,) + a.shape[1:], lambda l, j: (l, 0, 0))
    mod, w_in_t, w_out_b, w_head, w_uq_p, w_ukv_p = pl.pallas_call(
        _prep_kernel,
        grid=(DEPTH, n_third),
        in_specs=[pl.BlockSpec((1, D_MODEL), lambda l, j: (0, 0)),
                  pl.BlockSpec((dec_batch, D_MODEL), lambda l, j: (0, 0)),
                  pl.BlockSpec((1, D_MODEL, D_MODEL), lambda l, j: (l, 0, j)),
                  pl.BlockSpec((DEPTH, D_MODEL), lambda l, j: (0, j)),
                  pl.BlockSpec((1, in_rows, D_MODEL), lambda l, j: (l, j, 0)),
                  per_layer_spec(w_out), per_layer_spec(w_uq), per_layer_spec(w_ukv)],
        out_specs=[pl.BlockSpec((1, 1, n_cond, D_MODEL), lambda l, j: (l, j, 0, 0)),
                   pl.BlockSpec((1, in_rows, D_MODEL), lambda l, j: (l, j, 0)),
                   per_layer_spec(w_out),
                   pl.BlockSpec((1, head_rows, D_MODEL), lambda l, j: (l, 0, 0)),
                   per_layer_spec(w_uq), per_layer_spec(w_ukv)],
        out_shape=[jax.ShapeDtypeStruct((DEPTH, 3, n_cond, D_MODEL), F32),
                   jax.ShapeDtypeStruct((DEPTH, D_IN, D_MODEL), BF16),
                   jax.ShapeDtypeStruct(w_out.shape, BF16),
                   jax.ShapeDtypeStruct((DEPTH, head_rows, D_MODEL), BF16),
                   jax.ShapeDtypeStruct(w_uq.shape, BF16),
                   jax.ShapeDtypeStruct(w_ukv.shape, BF16)],
        compiler_params=pltpu.CompilerParams(dimension_semantics=("arbitrary", "arbitrary"),
                                             vmem_limit_bytes=VMEM_LIMIT_BYTES),
        name="prep",
    )(c_ctx.reshape(1, D_MODEL), c, ada_w, ada_b, jnp.swapaxes(w_in, 1, 2), w_out, w_uq, w_ukv)

    weights = (norm_g, w_in_t, w_head, mla_q_norm, w_uq_p, mla_kv_norm, w_ukv_p,
               jnp.swapaxes(conv_w, 0, 1), diff_lambda, diff_subln, w_out_b,
               final_norm.reshape(1, D_MODEL))

    tb = 2
    state_shapes = ((seq, MLA_KV_LORA), (MLA_ROPE, seq), (W_DIFF, seq),
                    (DIFF_HEADS * seq, DIFF_V))
    y_prompt, s_ckv, s_kr, s_dk, s_dv = pl.pallas_call(
        functools.partial(_ctx_kernel, n_seq=seq),
        grid=(batch // tb,),
        in_specs=[pl.BlockSpec((tb, seq, D_MODEL), lambda i: (i, 0, 0)),
                  _const_spec(mod.shape)] + [_const_spec(a.shape) for a in weights],
        out_specs=[pl.BlockSpec((tb, seq, D_MODEL), lambda i: (i, 0, 0))]
        + [pl.BlockSpec((tb, DEPTH) + s, lambda i: (i, 0, 0, 0)) for s in state_shapes],
        out_shape=[jax.ShapeDtypeStruct((batch, seq, D_MODEL), F32)]
        + [jax.ShapeDtypeStruct((batch, DEPTH) + s, F32) for s in state_shapes],
        scratch_shapes=_scratch(tb, seq, seq),
        compiler_params=cparams,
        name="ctx",
    )(x_prompt, mod, *weights)

    tables = _rope_tables(dec_seq, MLA_ROPE) + _rope_tables(dec_seq, DIFF_QK)
    ctx_in = (cache_mla_ckv,
              jnp.swapaxes(cache_mla_krope, 2, 3),
              jnp.transpose(cache_diff_k, (0, 1, 3, 4, 5, 2)).reshape(dec_batch, DEPTH, W_DIFF, past),
              cache_diff_v.reshape(dec_batch, DEPTH, past * DIFF_HEADS, DIFF_V))
    h_lat = x_sample
    for l in range(DEPTH):
        h_lat = pl.pallas_call(
            functools.partial(_lat_kernel, n_seq=dec_seq, layer=l),
            grid=(dec_batch,),
            in_specs=[pl.BlockSpec((1, dec_seq, D_MODEL), lambda i: (i, 0, 0),
                                   pipeline_mode=pl.Buffered(1)),
                      _const_spec(mod.shape)]
            + [pl.BlockSpec((1, 1) + a.shape[2:], lambda i, l=l: (i, l, 0, 0)) for a in ctx_in]
            + [_const_spec(t.shape) for t in tables]
            + [_const_spec(a.shape, layer=l if name in _PER_LAYER_BLOCKS else None)
               for name, a in zip(_W_NAMES, weights)],
            out_specs=pl.BlockSpec((1, dec_seq, D_MODEL), lambda i: (i, 0, 0),
                                   pipeline_mode=pl.Buffered(1)),
            out_shape=jax.ShapeDtypeStruct((dec_batch, dec_seq, D_MODEL), F32),
            scratch_shapes=_scratch(1, dec_seq, past + dec_seq),
            compiler_params=cparams,
            name=f"latent_{l}",
        )(h_lat, mod, *ctx_in, *tables, *weights)

    return (y_prompt, h_lat, s_ckv,
            jnp.swapaxes(s_kr, 2, 3),
            jnp.transpose(s_dk.reshape(batch, DEPTH, DIFF_HEADS, 2, DIFF_QK, seq),
                          (0, 1, 5, 2, 3, 4)),
            s_dv.reshape(batch, DEPTH, seq, DIFF_HEADS, DIFF_V))
```

```python
import functools
import math

import numpy as np
import jax
import jax.numpy as jnp
from jax import lax
from jax.experimental import pallas as pl
from jax.experimental.pallas import tpu as pltpu

D_MODEL = 1024
DEPTH = 2
GRID_W = 64
MLA_HEADS = 4
MLA_NOPE = 64
MLA_ROPE = 32
MLA_V = 64
MLA_Q_LORA = 256
MLA_KV_LORA = 128
W_MLA = MLA_HEADS * MLA_V
W_CONV = 256
DIFF_HEADS = 4
DIFF_QK = 64
DIFF_V = 2 * DIFF_QK
W_DIFF = DIFF_HEADS * DIFF_V
D_MIX = W_MLA + W_CONV + W_DIFF
ROPE_THETA = 10000.0
EPS = 1e-6

LANES = 128
Q_TILE = 256
LOG2E = math.log2(math.e)
MLA_SCALE = (MLA_NOPE + MLA_ROPE) ** -0.5 * LOG2E
DIFF_SCALE = DIFF_QK ** -0.5 * LOG2E
VMEM_LIMIT_BYTES = 54 * 1024 * 1024

R_Q = 0
R_KV = R_Q + MLA_Q_LORA
R_KR = R_KV + MLA_KV_LORA
R_CB = R_KR + MLA_ROPE
R_CC = R_CB + W_CONV
R_CX = R_CC + W_CONV
R_DQ = R_CX + W_CONV
R_DK = R_DQ + W_DIFF
R_DV = R_DK + W_DIFF
R_Z = R_DV + W_DIFF
D_IN = R_Z + D_MIX

BF16 = jnp.bfloat16
F32 = jnp.float32


def _lambda_init(layer):
    return 0.8 - 0.6 * math.exp(-0.3 * layer)


def _dot(a, b):
    return jnp.dot(a, b, preferred_element_type=F32)


def _dot_nt(a, b):
    return lax.dot_general(a, b, (((1,), (1,)), ((), ())), preferred_element_type=F32)


def _rms(x, g):
    ms = jnp.mean(x * x, axis=-1, keepdims=True)
    return x * lax.rsqrt(ms + EPS) * g


def _exp_scores(s):
    m = s[:, 0:LANES]
    for t in range(1, s.shape[1] // LANES):
        m = jnp.maximum(m, s[:, LANES * t:LANES * (t + 1)])
    return jnp.exp2(s - jnp.max(m, axis=-1, keepdims=True)).astype(BF16)


def _weighted(e, v_ones):
    r = _dot(e, v_ones)
    return r[:, 0:LANES], 1.0 / r[:, LANES:2 * LANES]


def _lane_mask(lo, hi):
    lane = lax.broadcasted_iota(jnp.int32, (1, LANES), 1)
    return jnp.where((lane >= lo) & (lane < hi), 1.0, 0.0).astype(F32)


def _rope(x, tables, q):
    cos, sin_a, sin_b = tables
    tiles = []
    for t in range(x.shape[1] // LANES):
        xt = x[:, LANES * t:LANES * (t + 1)]
        tiles.append(xt * cos + pltpu.roll(xt, LANES - q, 1) * sin_a + pltpu.roll(xt, q, 1) * sin_b)
    return tiles[0] if len(tiles) == 1 else jnp.concatenate(tiles, axis=1)


def _attend(q, dq, keys, kcat_ref, vm_ref, dka_ref, dvb_ref, lam, subln, lam_init):
    qr = q[:, 2 * LANES:3 * LANES]
    o_mla = []
    for t in range(2):
        qcat = jnp.concatenate([q[:, LANES * t:LANES * (t + 1)], qr], axis=1)
        acc = None
        for hh in range(2):
            h = 2 * t + hh
            o, rinv = _weighted(_exp_scores(_dot_nt(qcat, kcat_ref[h, keys, :])),
                                vm_ref[h, keys, :])
            acc = o * rinv if acc is None else acc + o * rinv
        o_mla.append(acc)
    o_diff = []
    for h in range(DIFF_HEADS):
        qt = dq[:, LANES * h:LANES * (h + 1)]
        o0, rinv0 = _weighted(_exp_scores(_dot_nt(qt, dka_ref[2 * h, keys, :])),
                              dvb_ref[h, keys, :])
        o1, rinv1 = _weighted(_exp_scores(_dot_nt(qt, dka_ref[2 * h + 1, keys, :])),
                              dvb_ref[h, keys, :])
        o = o0 * rinv0 - o1 * (lam * rinv1)
        o_diff.append(_rms(o, subln) * (1.0 - lam_init))
    return jnp.concatenate(o_mla, axis=1), jnp.concatenate(o_diff, axis=1)


def _proj(wl, w, scr, r0, r1):
    return _dot_nt(scr["hb"][...], w["w_in_t"][wl, r0:r1, :])


def _layer_front(l, wl, x, n_seq, mod, w, scr, rope, ctx, state):
    m_rows = x.shape[0]
    n_b = m_rows // n_seq
    n_ctx = 0 if ctx is None else ctx["ckv"].shape[0]
    nk = n_ctx + n_seq
    shift, scale, _ = mod
    hb_ref = scr["hb"]
    qb_ref, dqb_ref = scr["qb"], scr["dqb"]
    proj = functools.partial(_proj, wl, w, scr)
    kcat_ref, vm_ref, dka_ref, dvb_ref = scr["kcat"], scr["vm"], scr["dka"], scr["dvb"]
    w_ukv = w["w_ukv"]
    half = (_lane_mask(0, 64), _lane_mask(64, 128))
    new_keys, old_keys = pl.ds(n_ctx, n_seq), pl.ds(0, n_ctx)
    if rope is not None:
        rope_m = tuple(r[...] for r in rope[0])
        rope_d = tuple(r[...] for r in rope[1])

    def seq_rows(b):
        return slice(b * n_seq, (b + 1) * n_seq)

    gs = w["norm_g"][l:l + 1, :] * (1.0 + scale)
    ms = jnp.mean(x * x, axis=-1, keepdims=True)
    hb_ref[...] = (x * lax.rsqrt(ms + EPS) * gs + shift).astype(BF16)

    head = _dot_nt(hb_ref[...], w["w_head"][wl])

    cqn = _rms(head[:, 0:MLA_Q_LORA], w["q_norm"][l:l + 1, :]).astype(BF16)
    q = _dot(cqn, w["w_uq"][wl]) * MLA_SCALE
    if rope is not None:
        q = jnp.concatenate(
            [q[:, 0:2 * LANES], _rope(q[:, 2 * LANES:3 * LANES], rope_m, MLA_ROPE // 4)], axis=1)
    qb_ref[...] = q.astype(BF16)

    def fill_kv(b, dst, kv):
        for h in range(MLA_HEADS):
            t, hh = divmod(h, 2)
            kcat_ref[b, h, dst, 0:LANES] = (kv[:, LANES * t:LANES * (t + 1)] * half[hh]).astype(BF16)
            vm_ref[b, h, dst, 0:LANES] = (
                kv[:, 2 * LANES + LANES * t:2 * LANES + LANES * (t + 1)] * half[hh]).astype(BF16)

    def fill_kr(b, dst, kr):
        for h in range(MLA_HEADS):
            kcat_ref[b, h, dst, LANES:2 * LANES] = (
                kr * _lane_mask(MLA_ROPE * h, MLA_ROPE * (h + 1))).astype(BF16)

    def fill_dk(b, dst, dk):
        for h in range(DIFF_HEADS):
            for a in range(2):
                dka_ref[b, 2 * h + a, dst, :] = (
                    dk[:, LANES * h:LANES * (h + 1)] * half[a]).astype(BF16)

    ckv = _rms(head[:, R_KV:R_KR], w["kv_norm"][l:l + 1, :])
    kv = _dot(ckv.astype(BF16), w_ukv[wl])
    for b in range(n_b):
        if state is not None:
            state[0][b, l] = ckv[seq_rows(b)]
        fill_kv(b, new_keys, kv[seq_rows(b)])
        if ctx is not None:
            fill_kv(b, old_keys, _dot(ctx["ckv"][...].astype(BF16), w_ukv[wl]))

    krr = head[:, R_KR:R_KR + LANES]
    krr_pos = krr if rope is None else _rope(krr, rope_m, MLA_ROPE // 4)
    for b in range(n_b):
        if state is not None:
            state[1][b, l] = krr[seq_rows(b)].T[0:MLA_ROPE, :]
        fill_kr(b, new_keys, krr_pos[seq_rows(b)])
        if ctx is not None:
            fill_kr(b, old_keys, jnp.concatenate([ctx["kr"][...]] * (LANES // MLA_ROPE), axis=0).T)

    dq = proj(R_DQ, R_DK) * DIFF_SCALE
    if rope is not None:
        dq = _rope(dq, rope_d, DIFF_QK // 4)
    dqb_ref[...] = dq.astype(BF16)

    dk = proj(R_DK, R_DV)
    dk_pos = dk if rope is None else _rope(dk, rope_d, DIFF_QK // 4)
    for b in range(n_b):
        if state is not None:
            state[2][b, l] = dk[seq_rows(b)].T
        fill_dk(b, new_keys, dk_pos[seq_rows(b)])
        if ctx is not None:
            fill_dk(b, old_keys, ctx["dk"][...].T)

    dv = proj(R_DV, R_Z)
    for b in range(n_b):
        for h in range(DIFF_HEADS):
            dv_h = dv[seq_rows(b), LANES * h:LANES * (h + 1)]
            if state is not None:
                state[3][b, l, pl.ds(h, n_seq, stride=DIFF_HEADS), :] = dv_h
            dvb_ref[b, h, new_keys, 0:LANES] = dv_h.astype(BF16)
            if ctx is not None:
                dvb_ref[b, h, old_keys, 0:LANES] = (
                    ctx["dv"][pl.ds(h, n_ctx, stride=DIFF_HEADS), :].astype(BF16))
            ones = jnp.ones((nk, LANES), BF16)
            dvb_ref[b, h, :, LANES:2 * LANES] = ones
            vm_ref[b, h, :, LANES:2 * LANES] = ones


def _attend_block(l, b, start, nk, w, scr):
    lf = w["lam"][l]
    lam = (jnp.exp(jnp.sum(lf[0:1] * lf[1:2], axis=-1, keepdims=True))
           - jnp.exp(jnp.sum(lf[2:3] * lf[3:4], axis=-1, keepdims=True)) + _lambda_init(l))
    qrows = pl.ds(start, Q_TILE)
    return _attend(scr["qb"][qrows, :], scr["dqb"][qrows, :], pl.ds(0, nk),
                   scr["kcat"].at[b], scr["vm"].at[b], scr["dka"].at[b], scr["dvb"].at[b],
                   lam, w["subln"][l:l + 1, :], _lambda_init(l))


def _conv_branch(l, wl, m_rows, n_seq, w, scr):
    conv = _proj(wl, w, scr, R_CB, R_DQ)
    u = conv[:, W_CONV:2 * W_CONV] * conv[:, 2 * W_CONV:3 * W_CONV]
    pos = lax.broadcasted_iota(jnp.int32, (m_rows, 1), 0) % n_seq
    u_prev = jnp.where(pos == 0, 0.0, pltpu.roll(u, 1, 0))
    u_next = jnp.where(pos == n_seq - 1, 0.0, pltpu.roll(u, m_rows - 1, 0))
    cw = w["conv_w"][:, l, :]
    return conv[:, 0:W_CONV] * (u_prev * cw[0:1] + u * cw[1:2] + u_next * cw[2:3])


def _silu_gate(wl, w, scr, c0, c1):
    z = _proj(wl, w, scr, R_Z + c0, R_Z + c1)
    return z * jax.nn.sigmoid(z)


def _trunk_layer(l, x, n_seq, mod, w, scr, state):
    m_rows = x.shape[0]
    o_ref, mix_ref = scr["o"], scr["mix"]
    _layer_front(l, l, x, n_seq, mod, w, scr, None, None, state)
    for b in range(m_rows // n_seq):
        for start in range(b * n_seq, (b + 1) * n_seq, Q_TILE):
            o_mla, o_diff = _attend_block(l, b, start, n_seq, w, scr)
            o_ref[start:start + Q_TILE, 0:W_MLA] = o_mla
            o_ref[start:start + Q_TILE, W_MLA + W_CONV:D_MIX] = o_diff
    o_ref[:, W_MLA:W_MLA + W_CONV] = _conv_branch(l, l, m_rows, n_seq, w, scr)
    for c0 in range(0, D_MIX, 512):
        mix_ref[:, c0:c0 + 512] = (
            o_ref[:, c0:c0 + 512] * _silu_gate(l, w, scr, c0, c0 + 512)).astype(BF16)
    return x + mod[2] * _dot(mix_ref[...], w["w_out"][l])


_W_NAMES = ("norm_g", "w_in_t", "w_head", "q_norm", "w_uq", "kv_norm", "w_ukv", "conv_w", "lam",
            "subln", "w_out", "final_norm")
_PER_LAYER_BLOCKS = ("w_in_t", "w_head", "w_uq", "w_ukv", "w_out")
_SCR_NAMES = ("hb", "o", "mix", "qb", "dqb", "kcat", "vm", "dka", "dvb")
_CTX_NAMES = ("ckv", "kr", "dk", "dv")
N_ROPE = 6


def _ctx_kernel(x_ref, mod_ref, *refs, n_seq):
    w = dict(zip(_W_NAMES, refs[:len(_W_NAMES)]))
    rest = refs[len(_W_NAMES):]
    y_ref, state = rest[0], rest[1:5]
    scr = dict(zip(_SCR_NAMES, rest[5:]))
    n_b = x_ref.shape[0]
    x = x_ref[...].reshape(n_b * n_seq, D_MODEL)
    for l in range(DEPTH):
        mod = tuple(mod_ref[l, j, 0:1, :] for j in range(3))
        x = _trunk_layer(l, x, n_seq, mod, w, scr, state)
    y_ref[...] = _rms(x, w["final_norm"][...]).reshape(n_b, n_seq, D_MODEL)


def _lat_kernel(x_ref, mod_ref, *refs, n_seq, layer):
    ctx = {k: r.at[0, 0] for k, r in zip(_CTX_NAMES, refs[:4])}
    rope = (refs[4:7], refs[7:10])
    w = dict(zip(_W_NAMES, refs[4 + N_ROPE:4 + N_ROPE + len(_W_NAMES)]))
    rest = refs[4 + N_ROPE + len(_W_NAMES):]
    y_ref = rest[0]
    scr = dict(zip(_SCR_NAMES, rest[1:]))
    cond_row = pl.ds(1 + pl.program_id(0), 1)
    mod = tuple(mod_ref[layer, j, cond_row, :] for j in range(3))
    zg_ref, mix_ref = scr["o"], scr["mix"]
    n_ctx = ctx["ckv"].shape[0]
    conv_cols = slice(W_MLA, W_MLA + W_CONV)

    @pl.when(pl.program_id(1) == 0)
    def _():
        _layer_front(layer, 0, x_ref[0], n_seq, mod, w, scr, rope, ctx, None)
        for c0 in range(0, D_MIX, 512):
            zg_ref[:, c0:c0 + 512] = _silu_gate(0, w, scr, c0, c0 + 512)
        mix_ref[:, conv_cols] = (
            _conv_branch(layer, 0, n_seq, n_seq, w, scr) * zg_ref[:, conv_cols]).astype(BF16)

    start = pl.multiple_of(pl.program_id(1) * Q_TILE, Q_TILE)
    qrows = pl.ds(start, Q_TILE)
    o_mla, o_diff = _attend_block(layer, 0, start, n_ctx + n_seq, w, scr)
    mix_ref[qrows, 0:W_MLA] = (o_mla * zg_ref[qrows, 0:W_MLA]).astype(BF16)
    mix_ref[qrows, W_MLA + W_CONV:D_MIX] = (
        o_diff * zg_ref[qrows, W_MLA + W_CONV:D_MIX]).astype(BF16)
    x = x_ref[0, qrows, :] + mod[2] * _dot(mix_ref[qrows, :], w["w_out"][0])
    if layer == DEPTH - 1:
        x = _rms(x, w["final_norm"][...])
    y_ref[0] = x


def _prep_kernel(cctx_ref, c_ref, ada_w_ref, ada_b_ref, w_in_ref, w_out_ref, w_uq_ref, w_ukv_ref,
                 mod_ref, w_in_bf_ref, w_out_bf_ref, w_head_ref, w_uq_p_ref, w_ukv_p_ref):
    n_rows = mod_ref.shape[2]
    pad = jnp.zeros((n_rows - 1 - c_ref.shape[0], D_MODEL), F32)
    c = jnp.concatenate([cctx_ref[...], c_ref[...], pad], axis=0)
    s = (c * jax.nn.sigmoid(c)).astype(BF16)
    mod_ref[0, 0] = (_dot(s, ada_w_ref[0].astype(BF16))
                     + ada_b_ref[pl.ds(pl.program_id(0), 1), :])
    w_in_bf_ref[0] = w_in_ref[0].astype(BF16)

    @pl.when(pl.program_id(1) == 0)
    def _():
        w_out_bf_ref[0] = w_out_ref[0].astype(BF16)
        w_head_ref[0, R_Q:R_KR, :] = w_in_ref[0, R_Q:R_KR, :].astype(BF16)
        kr = w_in_ref[0, R_KR:R_CB, :].astype(BF16)
        for r in range(LANES // MLA_ROPE):
            w_head_ref[0, R_KR + MLA_ROPE * r:R_KR + MLA_ROPE * (r + 1), :] = kr
        for src, dst, first in ((w_uq_ref, w_uq_p_ref, MLA_NOPE), (w_ukv_ref, w_ukv_p_ref, MLA_NOPE)):
            a = src[0]
            per = a.shape[1] // MLA_HEADS
            cols = ([a[:, per * h:per * h + first] for h in range(MLA_HEADS)]
                    + [a[:, per * h + first:per * (h + 1)] for h in range(MLA_HEADS)])
            dst[0] = jnp.concatenate(cols, axis=1).astype(BF16)


def _const_spec(shape, layer=None):
    if layer is None:
        zeros = (0,) * len(shape)
        return pl.BlockSpec(shape, lambda *_: zeros, pipeline_mode=pl.Buffered(1))
    idx = (layer,) + (0,) * (len(shape) - 1)
    return pl.BlockSpec((1,) + tuple(shape[1:]), lambda *_: idx, pipeline_mode=pl.Buffered(1))


def _scratch(n_b, n_seq, nk):
    m_rows = n_b * n_seq
    return [
        pltpu.VMEM((m_rows, D_MODEL), BF16),
        pltpu.VMEM((m_rows, D_MIX), F32),
        pltpu.VMEM((m_rows, D_MIX), BF16),
        pltpu.VMEM((m_rows, 3 * LANES), BF16),
        pltpu.VMEM((m_rows, W_DIFF), BF16),
        pltpu.VMEM((n_b, MLA_HEADS, nk, 2 * LANES), BF16),
        pltpu.VMEM((n_b, MLA_HEADS, nk, 2 * LANES), BF16),
        pltpu.VMEM((n_b, 2 * DIFF_HEADS, nk, LANES), BF16),
        pltpu.VMEM((n_b, DIFF_HEADS, nk, 2 * LANES), BF16),
    ]


def _rope_tables(n, rot_dim):
    half = rot_dim // 2
    inv = ROPE_THETA ** (-(np.arange(0, half, 2, dtype=np.float32) / half))
    pos = np.arange(n)
    ang_r = (pos // GRID_W).astype(np.float32)[:, None] * inv
    ang_c = (pos % GRID_W).astype(np.float32)[:, None] * inv
    ang = np.concatenate([ang_r, ang_r, ang_c, ang_c], axis=-1)
    cos, sin = np.cos(ang), np.sin(ang)
    q = rot_dim // 4
    first = (np.arange(rot_dim) % (2 * q)) < q
    sin_a = np.where(first, -sin, 0.0)
    sin_b = np.where(first, 0.0, sin)
    reps = LANES // rot_dim
    return tuple(jnp.asarray(np.tile(t, (1, reps)), dtype=F32) for t in (cos, sin_a, sin_b))


def kernel(x_prompt, x_sample, c, cache_mla_ckv, cache_mla_krope, cache_diff_k, cache_diff_v,
           c_ctx, norm_g, ada_w, ada_b, w_in, mla_q_norm, w_uq, mla_kv_norm, w_ukv,
           conv_w, diff_lambda, diff_subln, w_out, final_norm):
    batch, seq, _ = x_prompt.shape
    dec_batch, dec_seq, _ = x_sample.shape
    past = cache_mla_ckv.shape[2]
    cparams = pltpu.CompilerParams(dimension_semantics=("arbitrary",),
                                   vmem_limit_bytes=VMEM_LIMIT_BYTES)

    n_cond = 8
    n_third = 3
    in_rows = D_IN // n_third
    head_rows = R_KR + LANES
    per_layer_spec = lambda a: pl.BlockSpec((1,) + a.shape[1:], lambda l, j: (l, 0, 0))
    mod, w_in_t, w_out_b, w_head, w_uq_p, w_ukv_p = pl.pallas_call(
        _prep_kernel,
        grid=(DEPTH, n_third),
        in_specs=[pl.BlockSpec((1, D_MODEL), lambda l, j: (0, 0)),
                  pl.BlockSpec((dec_batch, D_MODEL), lambda l, j: (0, 0)),
                  pl.BlockSpec((1, D_MODEL, D_MODEL), lambda l, j: (l, 0, j)),
                  pl.BlockSpec((DEPTH, D_MODEL), lambda l, j: (0, j)),
                  pl.BlockSpec((1, in_rows, D_MODEL), lambda l, j: (l, j, 0)),
                  per_layer_spec(w_out), per_layer_spec(w_uq), per_layer_spec(w_ukv)],
        out_specs=[pl.BlockSpec((1, 1, n_cond, D_MODEL), lambda l, j: (l, j, 0, 0)),
                   pl.BlockSpec((1, in_rows, D_MODEL), lambda l, j: (l, j, 0)),
                   per_layer_spec(w_out),
                   pl.BlockSpec((1, head_rows, D_MODEL), lambda l, j: (l, 0, 0)),
                   per_layer_spec(w_uq), per_layer_spec(w_ukv)],
        out_shape=[jax.ShapeDtypeStruct((DEPTH, 3, n_cond, D_MODEL), F32),
                   jax.ShapeDtypeStruct((DEPTH, D_IN, D_MODEL), BF16),
                   jax.ShapeDtypeStruct(w_out.shape, BF16),
                   jax.ShapeDtypeStruct((DEPTH, head_rows, D_MODEL), BF16),
                   jax.ShapeDtypeStruct(w_uq.shape, BF16),
                   jax.ShapeDtypeStruct(w_ukv.shape, BF16)],
        compiler_params=pltpu.CompilerParams(dimension_semantics=("arbitrary", "arbitrary"),
                                             vmem_limit_bytes=VMEM_LIMIT_BYTES),
        name="prep",
    )(c_ctx.reshape(1, D_MODEL), c, ada_w, ada_b, jnp.swapaxes(w_in, 1, 2), w_out, w_uq, w_ukv)

    weights = (norm_g, w_in_t, w_head, mla_q_norm, w_uq_p, mla_kv_norm, w_ukv_p,
               jnp.swapaxes(conv_w, 0, 1), diff_lambda, diff_subln, w_out_b,
               final_norm.reshape(1, D_MODEL))

    tb = 2
    state_shapes = ((seq, MLA_KV_LORA), (MLA_ROPE, seq), (W_DIFF, seq),
                    (DIFF_HEADS * seq, DIFF_V))
    y_prompt, s_ckv, s_kr, s_dk, s_dv = pl.pallas_call(
        functools.partial(_ctx_kernel, n_seq=seq),
        grid=(batch // tb,),
        in_specs=[pl.BlockSpec((tb, seq, D_MODEL), lambda i: (i, 0, 0)),
                  _const_spec(mod.shape)] + [_const_spec(a.shape) for a in weights],
        out_specs=[pl.BlockSpec((tb, seq, D_MODEL), lambda i: (i, 0, 0))]
        + [pl.BlockSpec((tb, DEPTH) + s, lambda i: (i, 0, 0, 0)) for s in state_shapes],
        out_shape=[jax.ShapeDtypeStruct((batch, seq, D_MODEL), F32)]
        + [jax.ShapeDtypeStruct((batch, DEPTH) + s, F32) for s in state_shapes],
        scratch_shapes=_scratch(tb, seq, seq),
        compiler_params=cparams,
        name="ctx",
    )(x_prompt, mod, *weights)

    lat_params = pltpu.CompilerParams(dimension_semantics=("arbitrary", "arbitrary"),
                                      vmem_limit_bytes=VMEM_LIMIT_BYTES)
    tables = _rope_tables(dec_seq, MLA_ROPE) + _rope_tables(dec_seq, DIFF_QK)
    ctx_in = (cache_mla_ckv,
              jnp.swapaxes(cache_mla_krope, 2, 3),
              jnp.transpose(cache_diff_k, (0, 1, 3, 4, 5, 2)).reshape(dec_batch, DEPTH, W_DIFF, past),
              cache_diff_v.reshape(dec_batch, DEPTH, past * DIFF_HEADS, DIFF_V))
    h_lat = x_sample
    for l in range(DEPTH):
        h_lat = pl.pallas_call(
            functools.partial(_lat_kernel, n_seq=dec_seq, layer=l),
            grid=(dec_batch, dec_seq // Q_TILE),
            in_specs=[pl.BlockSpec((1, dec_seq, D_MODEL), lambda i, j: (i, 0, 0),
                                   pipeline_mode=pl.Buffered(1)),
                      _const_spec(mod.shape)]
            + [pl.BlockSpec((1, 1) + a.shape[2:], lambda i, j, l=l: (i, l, 0, 0))
               for a in ctx_in]
            + [_const_spec(t.shape) for t in tables]
            + [_const_spec(a.shape, layer=l if name in _PER_LAYER_BLOCKS else None)
               for name, a in zip(_W_NAMES, weights)],
            out_specs=pl.BlockSpec((1, Q_TILE, D_MODEL), lambda i, j: (i, j, 0)),
            out_shape=jax.ShapeDtypeStruct((dec_batch, dec_seq, D_MODEL), F32),
            scratch_shapes=_scratch(1, dec_seq, past + dec_seq),
            compiler_params=lat_params,
            name=f"latent_{l}",
        )(h_lat, mod, *ctx_in, *tables, *weights)

    return (y_prompt, h_lat, s_ckv,
            jnp.swapaxes(s_kr, 2, 3),
            jnp.transpose(s_dk.reshape(batch, DEPTH, DIFF_HEADS, 2, DIFF_QK, seq),
                          (0, 1, 5, 2, 3, 4)),
            s_dv.reshape(batch, DEPTH, seq, DIFF_HEADS, DIFF_V))
```

```python
import functools
import math

import numpy as np
import jax
import jax.numpy as jnp
from jax import lax
from jax.experimental import pallas as pl
from jax.experimental.pallas import tpu as pltpu

D_MODEL = 1024
DEPTH = 2
GRID_W = 64
MLA_HEADS = 4
MLA_NOPE = 64
MLA_ROPE = 32
MLA_V = 64
MLA_Q_LORA = 256
MLA_KV_LORA = 128
W_MLA = MLA_HEADS * MLA_V
W_CONV = 256
DIFF_HEADS = 4
DIFF_QK = 64
DIFF_V = 2 * DIFF_QK
W_DIFF = DIFF_HEADS * DIFF_V
D_MIX = W_MLA + W_CONV + W_DIFF
ROPE_THETA = 10000.0
EPS = 1e-6

LANES = 128
Q_TILE = 256
LOG2E = math.log2(math.e)
MLA_SCALE = (MLA_NOPE + MLA_ROPE) ** -0.5 * LOG2E
DIFF_SCALE = DIFF_QK ** -0.5 * LOG2E
VMEM_LIMIT_BYTES = 54 * 1024 * 1024

R_Q = 0
R_KV = R_Q + MLA_Q_LORA
R_KR = R_KV + MLA_KV_LORA
R_CB = R_KR + MLA_ROPE
R_CC = R_CB + W_CONV
R_CX = R_CC + W_CONV
R_DQ = R_CX + W_CONV
R_DK = R_DQ + W_DIFF
R_DV = R_DK + W_DIFF
R_Z = R_DV + W_DIFF
D_IN = R_Z + D_MIX

BF16 = jnp.bfloat16
F32 = jnp.float32


def _lambda_init(layer):
    return 0.8 - 0.6 * math.exp(-0.3 * layer)


def _dot(a, b):
    return jnp.dot(a, b, preferred_element_type=F32)


def _dot_nt(a, b):
    return lax.dot_general(a, b, (((1,), (1,)), ((), ())), preferred_element_type=F32)


def _rms(x, g):
    ms = jnp.mean(x * x, axis=-1, keepdims=True)
    return x * lax.rsqrt(ms + EPS) * g


def _exp_scores(s):
    m = s[:, 0:LANES]
    for t in range(1, s.shape[1] // LANES):
        m = jnp.maximum(m, s[:, LANES * t:LANES * (t + 1)])
    return jnp.exp2(s - jnp.max(m, axis=-1, keepdims=True)).astype(BF16)


def _weighted(e, v_ones):
    r = _dot(e, v_ones)
    return r[:, 0:LANES], 1.0 / r[:, LANES:2 * LANES]


def _lane_mask(lo, hi):
    lane = lax.broadcasted_iota(jnp.int32, (1, LANES), 1)
    return jnp.where((lane >= lo) & (lane < hi), 1.0, 0.0).astype(F32)


def _rope(x, tables, q):
    cos, sin_a, sin_b = tables
    tiles = []
    for t in range(x.shape[1] // LANES):
        xt = x[:, LANES * t:LANES * (t + 1)]
        tiles.append(xt * cos + pltpu.roll(xt, LANES - q, 1) * sin_a + pltpu.roll(xt, q, 1) * sin_b)
    return tiles[0] if len(tiles) == 1 else jnp.concatenate(tiles, axis=1)


def _attend(q, dq, keys, kcat_ref, vm_ref, dka_ref, dvb_ref, lam, subln, lam_init):
    qr = q[:, 2 * LANES:3 * LANES]
    o_mla = []
    for t in range(2):
        qcat = jnp.concatenate([q[:, LANES * t:LANES * (t + 1)], qr], axis=1)
        acc = None
        for hh in range(2):
            h = 2 * t + hh
            o, rinv = _weighted(_exp_scores(_dot_nt(qcat, kcat_ref[h, keys, :])),
                                vm_ref[h, keys, :])
            acc = o * rinv if acc is None else acc + o * rinv
        o_mla.append(acc)
    o_diff = []
    for h in range(DIFF_HEADS):
        qt = dq[:, LANES * h:LANES * (h + 1)]
        o0, rinv0 = _weighted(_exp_scores(_dot_nt(qt, dka_ref[2 * h, keys, :])),
                              dvb_ref[h, keys, :])
        o1, rinv1 = _weighted(_exp_scores(_dot_nt(qt, dka_ref[2 * h + 1, keys, :])),
                              dvb_ref[h, keys, :])
        o = o0 * rinv0 - o1 * (lam * rinv1)
        o_diff.append(_rms(o, subln) * (1.0 - lam_init))
    return jnp.concatenate(o_mla, axis=1), jnp.concatenate(o_diff, axis=1)


def _proj(wl, w, scr, r0, r1):
    return _dot_nt(scr["hb"][...], w["w_in_t"][wl, r0:r1, :])


def _layer_front(l, wl, x, n_seq, mod, w, scr, rope, ctx, state):
    m_rows = x.shape[0]
    n_b = m_rows // n_seq
    n_ctx = 0 if ctx is None else ctx["ckv"].shape[0]
    nk = n_ctx + n_seq
    shift, scale, _ = mod
    hb_ref = scr["hb"]
    qb_ref, dqb_ref = scr["qb"], scr["dqb"]
    proj = functools.partial(_proj, wl, w, scr)
    kcat_ref, vm_ref, dka_ref, dvb_ref = scr["kcat"], scr["vm"], scr["dka"], scr["dvb"]
    w_ukv = w["w_ukv"]
    half = (_lane_mask(0, 64), _lane_mask(64, 128))
    new_keys, old_keys = pl.ds(n_ctx, n_seq), pl.ds(0, n_ctx)
    if rope is not None:
        rope_m = tuple(r[...] for r in rope[0])
        rope_d = tuple(r[...] for r in rope[1])

    def seq_rows(b):
        return slice(b * n_seq, (b + 1) * n_seq)

    gs = w["norm_g"][l:l + 1, :] * (1.0 + scale)
    ms = jnp.mean(x * x, axis=-1, keepdims=True)
    hb_ref[...] = (x * lax.rsqrt(ms + EPS) * gs + shift).astype(BF16)

    head = _dot_nt(hb_ref[...], w["w_head"][wl])

    cqn = _rms(head[:, 0:MLA_Q_LORA], w["q_norm"][l:l + 1, :]).astype(BF16)
    q = _dot(cqn, w["w_uq"][wl]) * MLA_SCALE
    if rope is not None:
        q = jnp.concatenate(
            [q[:, 0:2 * LANES], _rope(q[:, 2 * LANES:3 * LANES], rope_m, MLA_ROPE // 4)], axis=1)
    qb_ref[...] = q.astype(BF16)

    def fill_kv(b, dst, kv):
        for h in range(MLA_HEADS):
            t, hh = divmod(h, 2)
            kcat_ref[b, h, dst, 0:LANES] = (kv[:, LANES * t:LANES * (t + 1)] * half[hh]).astype(BF16)
            vm_ref[b, h, dst, 0:LANES] = (
                kv[:, 2 * LANES + LANES * t:2 * LANES + LANES * (t + 1)] * half[hh]).astype(BF16)

    def fill_kr(b, dst, kr):
        for h in range(MLA_HEADS):
            kcat_ref[b, h, dst, LANES:2 * LANES] = (
                kr * _lane_mask(MLA_ROPE * h, MLA_ROPE * (h + 1))).astype(BF16)

    def fill_dk(b, dst, dk):
        for h in range(DIFF_HEADS):
            for a in range(2):
                dka_ref[b, 2 * h + a, dst, :] = (
                    dk[:, LANES * h:LANES * (h + 1)] * half[a]).astype(BF16)

    ckv = _rms(head[:, R_KV:R_KR], w["kv_norm"][l:l + 1, :])
    kv = _dot(ckv.astype(BF16), w_ukv[wl])
    for b in range(n_b):
        if state is not None:
            state[0][b, l] = ckv[seq_rows(b)]
        fill_kv(b, new_keys, kv[seq_rows(b)])
        if ctx is not None:
            fill_kv(b, old_keys, _dot(ctx["ckv"][...].astype(BF16), w_ukv[wl]))

    krr = head[:, R_KR:R_KR + LANES]
    krr_pos = krr if rope is None else _rope(krr, rope_m, MLA_ROPE // 4)
    for b in range(n_b):
        if state is not None:
            state[1][b, l] = krr[seq_rows(b)].T[0:MLA_ROPE, :]
        fill_kr(b, new_keys, krr_pos[seq_rows(b)])
        if ctx is not None:
            fill_kr(b, old_keys, jnp.concatenate([ctx["kr"][...]] * (LANES // MLA_ROPE), axis=0).T)

    dq = proj(R_DQ, R_DK) * DIFF_SCALE
    if rope is not None:
        dq = _rope(dq, rope_d, DIFF_QK // 4)
    dqb_ref[...] = dq.astype(BF16)

    dk = proj(R_DK, R_DV)
    dk_pos = dk if rope is None else _rope(dk, rope_d, DIFF_QK // 4)
    for b in range(n_b):
        if state is not None:
            state[2][b, l] = dk[seq_rows(b)].T
        fill_dk(b, new_keys, dk_pos[seq_rows(b)])
        if ctx is not None:
            fill_dk(b, old_keys, ctx["dk"][...].T)

    dv = proj(R_DV, R_Z)
    for b in range(n_b):
        for h in range(DIFF_HEADS):
            dv_h = dv[seq_rows(b), LANES * h:LANES * (h + 1)]
            if state is not None:
                state[3][b, l, pl.ds(h, n_seq, stride=DIFF_HEADS), :] = dv_h
            dvb_ref[b, h, new_keys, 0:LANES] = dv_h.astype(BF16)
            if ctx is not None:
                dvb_ref[b, h, old_keys, 0:LANES] = (
                    ctx["dv"][pl.ds(h, n_ctx, stride=DIFF_HEADS), :].astype(BF16))
            ones = jnp.ones((nk, LANES), BF16)
            dvb_ref[b, h, :, LANES:2 * LANES] = ones
            vm_ref[b, h, :, LANES:2 * LANES] = ones


def _attend_block(l, b, start, nk, w, scr):
    lf = w["lam"][l]
    lam = (jnp.exp(jnp.sum(lf[0:1] * lf[1:2], axis=-1, keepdims=True))
           - jnp.exp(jnp.sum(lf[2:3] * lf[3:4], axis=-1, keepdims=True)) + _lambda_init(l))
    qrows = pl.ds(start, Q_TILE)
    return _attend(scr["qb"][qrows, :], scr["dqb"][qrows, :], pl.ds(0, nk),
                   scr["kcat"].at[b], scr["vm"].at[b], scr["dka"].at[b], scr["dvb"].at[b],
                   lam, w["subln"][l:l + 1, :], _lambda_init(l))


def _conv_branch(l, wl, m_rows, n_seq, w, scr):
    conv = _proj(wl, w, scr, R_CB, R_DQ)
    u = conv[:, W_CONV:2 * W_CONV] * conv[:, 2 * W_CONV:3 * W_CONV]
    pos = lax.broadcasted_iota(jnp.int32, (m_rows, 1), 0) % n_seq
    u_prev = jnp.where(pos == 0, 0.0, pltpu.roll(u, 1, 0))
    u_next = jnp.where(pos == n_seq - 1, 0.0, pltpu.roll(u, m_rows - 1, 0))
    cw = w["conv_w"][:, l, :]
    return conv[:, 0:W_CONV] * (u_prev * cw[0:1] + u * cw[1:2] + u_next * cw[2:3])


def _silu_gate(wl, w, scr, c0, c1):
    z = _proj(wl, w, scr, R_Z + c0, R_Z + c1)
    return z * jax.nn.sigmoid(z)


def _trunk_layer(l, x, n_seq, mod, w, scr, state):
    m_rows = x.shape[0]
    o_ref, mix_ref = scr["o"], scr["mix"]
    _layer_front(l, l, x, n_seq, mod, w, scr, None, None, state)
    for b in range(m_rows // n_seq):
        for start in range(b * n_seq, (b + 1) * n_seq, Q_TILE):
            o_mla, o_diff = _attend_block(l, b, start, n_seq, w, scr)
            o_ref[start:start + Q_TILE, 0:W_MLA] = o_mla
            o_ref[start:start + Q_TILE, W_MLA + W_CONV:D_MIX] = o_diff
    o_ref[:, W_MLA:W_MLA + W_CONV] = _conv_branch(l, l, m_rows, n_seq, w, scr)
    for c0 in range(0, D_MIX, 512):
        mix_ref[:, c0:c0 + 512] = (
            o_ref[:, c0:c0 + 512] * _silu_gate(l, w, scr, c0, c0 + 512)).astype(BF16)
    return x + mod[2] * _dot(mix_ref[...], w["w_out"][l])


_W_NAMES = ("norm_g", "w_in_t", "w_head", "q_norm", "w_uq", "kv_norm", "w_ukv", "conv_w", "lam",
            "subln", "w_out", "final_norm")
_PER_LAYER_BLOCKS = ("w_in_t", "w_head", "w_uq", "w_ukv", "w_out")
_SCR_NAMES = ("hb", "o", "mix", "qb", "dqb", "kcat", "vm", "dka", "dvb")
_CTX_NAMES = ("ckv", "kr", "dk", "dv")
N_ROPE = 6


def _ctx_kernel(x_ref, mod_ref, *refs, n_seq):
    w = dict(zip(_W_NAMES, refs[:len(_W_NAMES)]))
    rest = refs[len(_W_NAMES):]
    y_ref, state = rest[0], rest[1:5]
    scr = dict(zip(_SCR_NAMES, rest[5:]))
    n_b = x_ref.shape[0]
    x = x_ref[...].reshape(n_b * n_seq, D_MODEL)
    for l in range(DEPTH):
        mod = tuple(mod_ref[l, j, 0:1, :] for j in range(3))
        x = _trunk_layer(l, x, n_seq, mod, w, scr, state)
    y_ref[...] = _rms(x, w["final_norm"][...]).reshape(n_b, n_seq, D_MODEL)


def _lat_kernel(x_ref, mod_ref, *refs, n_seq, layer):
    ctx = {k: r.at[0, 0] for k, r in zip(_CTX_NAMES, refs[:4])}
    rope = (refs[4:7], refs[7:10])
    w = dict(zip(_W_NAMES, refs[4 + N_ROPE:4 + N_ROPE + len(_W_NAMES)]))
    rest = refs[4 + N_ROPE + len(_W_NAMES):]
    y_ref = rest[0]
    scr = dict(zip(_SCR_NAMES, rest[1:]))
    cond_row = pl.ds(1 + pl.program_id(0), 1)
    mod = tuple(mod_ref[layer, j, cond_row, :] for j in range(3))
    zg_ref, mix_ref = scr["o"], scr["mix"]
    n_ctx = ctx["ckv"].shape[0]
    conv_cols = slice(W_MLA, W_MLA + W_CONV)

    @pl.when(pl.program_id(1) == 0)
    def _():
        _layer_front(layer, 0, x_ref[0], n_seq, mod, w, scr, rope, ctx, None)
        for c0 in range(0, D_MIX, 512):
            zg_ref[:, c0:c0 + 512] = _silu_gate(0, w, scr, c0, c0 + 512)
        mix_ref[:, conv_cols] = (
            _conv_branch(layer, 0, n_seq, n_seq, w, scr) * zg_ref[:, conv_cols]).astype(BF16)

    start = pl.multiple_of(pl.program_id(1) * Q_TILE, Q_TILE)
    qrows = pl.ds(start, Q_TILE)
    o_mla, o_diff = _attend_block(layer, 0, start, n_ctx + n_seq, w, scr)
    mix_ref[qrows, 0:W_MLA] = (o_mla * zg_ref[qrows, 0:W_MLA]).astype(BF16)
    mix_ref[qrows, W_MLA + W_CONV:D_MIX] = (
        o_diff * zg_ref[qrows, W_MLA + W_CONV:D_MIX]).astype(BF16)
    x = x_ref[0, qrows, :] + mod[2] * _dot(mix_ref[qrows, :], w["w_out"][0])
    if layer == DEPTH - 1:
        x = _rms(x, w["final_norm"][...])
    y_ref[0] = x


def _prep_kernel(cctx_ref, c_ref, ada_w_ref, ada_b_ref, w_in_ref, w_out_ref, w_uq_ref, w_ukv_ref,
                 mod_ref, w_in_bf_ref, w_out_bf_ref, w_head_ref, w_uq_p_ref, w_ukv_p_ref):
    n_rows = mod_ref.shape[2]
    pad = jnp.zeros((n_rows - 1 - c_ref.shape[0], D_MODEL), F32)
    c = jnp.concatenate([cctx_ref[...], c_ref[...], pad], axis=0)
    s = (c * jax.nn.sigmoid(c)).astype(BF16)
    mod_ref[0, 0] = (_dot(s, ada_w_ref[0].astype(BF16))
                     + ada_b_ref[pl.ds(pl.program_id(0), 1), :])
    w_in_bf_ref[0] = w_in_ref[0].astype(BF16)

    @pl.when(pl.program_id(1) == 0)
    def _():
        w_out_bf_ref[0] = w_out_ref[0].astype(BF16)
        w_head_ref[0, R_Q:R_KR, :] = w_in_ref[0, R_Q:R_KR, :].astype(BF16)
        kr = w_in_ref[0, R_KR:R_CB, :].astype(BF16)
        for r in range(LANES // MLA_ROPE):
            w_head_ref[0, R_KR + MLA_ROPE * r:R_KR + MLA_ROPE * (r + 1), :] = kr
        for src, dst, first in ((w_uq_ref, w_uq_p_ref, MLA_NOPE), (w_ukv_ref, w_ukv_p_ref, MLA_NOPE)):
            a = src[0]
            per = a.shape[1] // MLA_HEADS
            cols = ([a[:, per * h:per * h + first] for h in range(MLA_HEADS)]
                    + [a[:, per * h + first:per * (h + 1)] for h in range(MLA_HEADS)])
            dst[0] = jnp.concatenate(cols, axis=1).astype(BF16)


def _const_spec(shape, layer=None):
    if layer is None:
        zeros = (0,) * len(shape)
        return pl.BlockSpec(shape, lambda *_: zeros, pipeline_mode=pl.Buffered(1))
    idx = (layer,) + (0,) * (len(shape) - 1)
    return pl.BlockSpec((1,) + tuple(shape[1:]), lambda *_: idx, pipeline_mode=pl.Buffered(1))


def _scratch(n_b, n_seq, nk):
    m_rows = n_b * n_seq
    return [
        pltpu.VMEM((m_rows, D_MODEL), BF16),
        pltpu.VMEM((m_rows, D_MIX), F32),
        pltpu.VMEM((m_rows, D_MIX), BF16),
        pltpu.VMEM((m_rows, 3 * LANES), BF16),
        pltpu.VMEM((m_rows, W_DIFF), BF16),
        pltpu.VMEM((n_b, MLA_HEADS, nk, 2 * LANES), BF16),
        pltpu.VMEM((n_b, MLA_HEADS, nk, 2 * LANES), BF16),
        pltpu.VMEM((n_b, 2 * DIFF_HEADS, nk, LANES), BF16),
        pltpu.VMEM((n_b, DIFF_HEADS, nk, 2 * LANES), BF16),
    ]


def _rope_tables(n, rot_dim):
    half = rot_dim // 2
    inv = ROPE_THETA ** (-(np.arange(0, half, 2, dtype=np.float32) / half))
    pos = np.arange(n)
    ang_r = (pos // GRID_W).astype(np.float32)[:, None] * inv
    ang_c = (pos % GRID_W).astype(np.float32)[:, None] * inv
    ang = np.concatenate([ang_r, ang_r, ang_c, ang_c], axis=-1)
    cos, sin = np.cos(ang), np.sin(ang)
    q = rot_dim // 4
    first = (np.arange(rot_dim) % (2 * q)) < q
    sin_a = np.where(first, -sin, 0.0)
    sin_b = np.where(first, 0.0, sin)
    reps = LANES // rot_dim
    return tuple(jnp.asarray(np.tile(t, (1, reps)), dtype=F32) for t in (cos, sin_a, sin_b))


def kernel(x_prompt, x_sample, c, cache_mla_ckv, cache_mla_krope, cache_diff_k, cache_diff_v,
           c_ctx, norm_g, ada_w, ada_b, w_in, mla_q_norm, w_uq, mla_kv_norm, w_ukv,
           conv_w, diff_lambda, diff_subln, w_out, final_norm):
    batch, seq, _ = x_prompt.shape
    dec_batch, dec_seq, _ = x_sample.shape
    past = cache_mla_ckv.shape[2]
    cparams = pltpu.CompilerParams(dimension_semantics=("arbitrary",),
                                   vmem_limit_bytes=VMEM_LIMIT_BYTES)

    n_cond = 8
    n_third = 3
    in_rows = D_IN // n_third
    head_rows = R_KR + LANES
    per_layer_spec = lambda a: pl.BlockSpec((1,) + a.shape[1:], lambda l, j: (l, 0, 0))
    mod, w_in_t, w_out_b, w_head, w_uq_p, w_ukv_p = pl.pallas_call(
        _prep_kernel,
        grid=(DEPTH, n_third),
        in_specs=[pl.BlockSpec((1, D_MODEL), lambda l, j: (0, 0)),
                  pl.BlockSpec((dec_batch, D_MODEL), lambda l, j: (0, 0)),
                  pl.BlockSpec((1, D_MODEL, D_MODEL), lambda l, j: (l, 0, j)),
                  pl.BlockSpec((DEPTH, D_MODEL), lambda l, j: (0, j)),
                  pl.BlockSpec((1, in_rows, D_MODEL), lambda l, j: (l, j, 0)),
                  per_layer_spec(w_out), per_layer_spec(w_uq), per_layer_spec(w_ukv)],
        out_specs=[pl.BlockSpec((1, 1, n_cond, D_MODEL), lambda l, j: (l, j, 0, 0)),
                   pl.BlockSpec((1, in_rows, D_MODEL), lambda l, j: (l, j, 0)),
                   per_layer_spec(w_out),
                   pl.BlockSpec((1, head_rows, D_MODEL), lambda l, j: (l, 0, 0)),
                   per_layer_spec(w_uq), per_layer_spec(w_ukv)],
        out_shape=[jax.ShapeDtypeStruct((DEPTH, 3, n_cond, D_MODEL), F32),
                   jax.ShapeDtypeStruct((DEPTH, D_IN, D_MODEL), BF16),
                   jax.ShapeDtypeStruct(w_out.shape, BF16),
                   jax.ShapeDtypeStruct((DEPTH, head_rows, D_MODEL), BF16),
                   jax.ShapeDtypeStruct(w_uq.shape, BF16),
                   jax.ShapeDtypeStruct(w_ukv.shape, BF16)],
        compiler_params=pltpu.CompilerParams(dimension_semantics=("arbitrary", "arbitrary"),
                                             vmem_limit_bytes=VMEM_LIMIT_BYTES),
        name="prep",
    )(c_ctx.reshape(1, D_MODEL), c, ada_w, ada_b, jnp.swapaxes(w_in, 1, 2), w_out, w_uq, w_ukv)

    weights = (norm_g, w_in_t, w_head, mla_q_norm, w_uq_p, mla_kv_norm, w_ukv_p,
               jnp.swapaxes(conv_w, 0, 1), diff_lambda, diff_subln, w_out_b,
               final_norm.reshape(1, D_MODEL))

    tb = 2
    state_shapes = ((seq, MLA_KV_LORA), (MLA_ROPE, seq), (W_DIFF, seq),
                    (DIFF_HEADS * seq, DIFF_V))
    y_prompt, s_ckv, s_kr, s_dk, s_dv = pl.pallas_call(
        functools.partial(_ctx_kernel, n_seq=seq),
        grid=(batch // tb,),
        in_specs=[pl.BlockSpec((tb, seq, D_MODEL), lambda i: (i, 0, 0)),
                  _const_spec(mod.shape)] + [_const_spec(a.shape) for a in weights],
        out_specs=[pl.BlockSpec((tb, seq, D_MODEL), lambda i: (i, 0, 0))]
        + [pl.BlockSpec((tb, DEPTH) + s, lambda i: (i, 0, 0, 0)) for s in state_shapes],
        out_shape=[jax.ShapeDtypeStruct((batch, seq, D_MODEL), F32)]
        + [jax.ShapeDtypeStruct((batch, DEPTH) + s, F32) for s in state_shapes],
        scratch_shapes=_scratch(tb, seq, seq),
        compiler_params=cparams,
        name="ctx",
    )(x_prompt, mod, *weights)

    lat_params = pltpu.CompilerParams(dimension_semantics=("arbitrary", "arbitrary"),
                                      vmem_limit_bytes=VMEM_LIMIT_BYTES)
    tables = _rope_tables(dec_seq, MLA_ROPE) + _rope_tables(dec_seq, DIFF_QK)
    ctx_in = (cache_mla_ckv,
              jnp.swapaxes(cache_mla_krope, 2, 3),
              jnp.transpose(cache_diff_k, (0, 1, 3, 4, 5, 2)).reshape(dec_batch, DEPTH, W_DIFF, past),
              cache_diff_v.reshape(dec_batch, DEPTH, past * DIFF_HEADS, DIFF_V))
    h_lat = x_sample
    for l in range(DEPTH):
        h_lat = pl.pallas_call(
            functools.partial(_lat_kernel, n_seq=dec_seq, layer=l),
            grid=(dec_batch, dec_seq // Q_TILE),
            in_specs=[pl.BlockSpec((1, dec_seq, D_MODEL), lambda i, j: (i, 0, 0)),
                      _const_spec(mod.shape)]
            + [pl.BlockSpec((1, 1) + a.shape[2:], lambda i, j, l=l: (i, l, 0, 0))
               for a in ctx_in]
            + [_const_spec(t.shape) for t in tables]
            + [_const_spec(a.shape, layer=l if name in _PER_LAYER_BLOCKS else None)
               for name, a in zip(_W_NAMES, weights)],
            out_specs=pl.BlockSpec((1, Q_TILE, D_MODEL), lambda i, j: (i, j, 0)),
            out_shape=jax.ShapeDtypeStruct((dec_batch, dec_seq, D_MODEL), F32),
            scratch_shapes=_scratch(1, dec_seq, past + dec_seq),
            compiler_params=lat_params,
            name=f"latent_{l}",
        )(h_lat, mod, *ctx_in, *tables, *weights)

    return (y_prompt, h_lat, s_ckv,
            jnp.swapaxes(s_kr, 2, 3),
            jnp.transpose(s_dk.reshape(batch, DEPTH, DIFF_HEADS, 2, DIFF_QK, seq),
                          (0, 1, 5, 2, 3, 4)),
            s_dv.reshape(batch, DEPTH, seq, DIFF_HEADS, DIFF_V))
```

```python
import functools
import math

import numpy as np
import jax
import jax.numpy as jnp
from jax import lax
from jax.experimental import pallas as pl
from jax.experimental.pallas import tpu as pltpu

D_MODEL = 1024
DEPTH = 2
GRID_W = 64
MLA_HEADS = 4
MLA_NOPE = 64
MLA_ROPE = 32
MLA_V = 64
MLA_Q_LORA = 256
MLA_KV_LORA = 128
W_MLA = MLA_HEADS * MLA_V
W_CONV = 256
DIFF_HEADS = 4
DIFF_QK = 64
DIFF_V = 2 * DIFF_QK
W_DIFF = DIFF_HEADS * DIFF_V
D_MIX = W_MLA + W_CONV + W_DIFF
ROPE_THETA = 10000.0
EPS = 1e-6

LANES = 128
Q_TILE = 256
LAT_Q_TILE = 256
LOG2E = math.log2(math.e)
MLA_SCALE = (MLA_NOPE + MLA_ROPE) ** -0.5 * LOG2E
DIFF_SCALE = DIFF_QK ** -0.5 * LOG2E
VMEM_LIMIT_BYTES = 54 * 1024 * 1024

R_Q = 0
R_KV = R_Q + MLA_Q_LORA
R_KR = R_KV + MLA_KV_LORA
R_CB = R_KR + MLA_ROPE
R_CC = R_CB + W_CONV
R_CX = R_CC + W_CONV
R_DQ = R_CX + W_CONV
R_DK = R_DQ + W_DIFF
R_DV = R_DK + W_DIFF
R_Z = R_DV + W_DIFF
D_IN = R_Z + D_MIX

BF16 = jnp.bfloat16
F32 = jnp.float32


def _lambda_init(layer):
    return 0.8 - 0.6 * math.exp(-0.3 * layer)


def _dot(a, b):
    return jnp.dot(a, b, preferred_element_type=F32)


def _dot_nt(a, b):
    return lax.dot_general(a, b, (((1,), (1,)), ((), ())), preferred_element_type=F32)


def _rms(x, g):
    ms = jnp.mean(x * x, axis=-1, keepdims=True)
    return x * lax.rsqrt(ms + EPS) * g


def _exp_scores(s):
    m = s[:, 0:LANES]
    for t in range(1, s.shape[1] // LANES):
        m = jnp.maximum(m, s[:, LANES * t:LANES * (t + 1)])
    return jnp.exp2(s - jnp.max(m, axis=-1, keepdims=True)).astype(BF16)


def _weighted(e, v_ones):
    r = _dot(e, v_ones)
    return r[:, 0:LANES], 1.0 / r[:, LANES:2 * LANES]


def _lane_mask(lo, hi):
    lane = lax.broadcasted_iota(jnp.int32, (1, LANES), 1)
    return jnp.where((lane >= lo) & (lane < hi), 1.0, 0.0).astype(F32)


def _rope(x, tables, q):
    cos, sin_a, sin_b = tables
    tiles = []
    for t in range(x.shape[1] // LANES):
        xt = x[:, LANES * t:LANES * (t + 1)]
        tiles.append(xt * cos + pltpu.roll(xt, LANES - q, 1) * sin_a + pltpu.roll(xt, q, 1) * sin_b)
    return tiles[0] if len(tiles) == 1 else jnp.concatenate(tiles, axis=1)


def _attend(q, dq, keys, kcat_ref, vm_ref, dka_ref, dvb_ref, lam, subln, lam_init):
    qr = q[:, 2 * LANES:3 * LANES]
    o_mla = []
    for t in range(2):
        qcat = jnp.concatenate([q[:, LANES * t:LANES * (t + 1)], qr], axis=1)
        acc = None
        for hh in range(2):
            h = 2 * t + hh
            o, rinv = _weighted(_exp_scores(_dot_nt(qcat, kcat_ref[h, keys, :])),
                                vm_ref[h, keys, :])
            acc = o * rinv if acc is None else acc + o * rinv
        o_mla.append(acc)
    o_diff = []
    for h in range(DIFF_HEADS):
        qt = dq[:, LANES * h:LANES * (h + 1)]
        o0, rinv0 = _weighted(_exp_scores(_dot_nt(qt, dka_ref[2 * h, keys, :])),
                              dvb_ref[h, keys, :])
        o1, rinv1 = _weighted(_exp_scores(_dot_nt(qt, dka_ref[2 * h + 1, keys, :])),
                              dvb_ref[h, keys, :])
        o = o0 * rinv0 - o1 * (lam * rinv1)
        o_diff.append(_rms(o, subln) * (1.0 - lam_init))
    return jnp.concatenate(o_mla, axis=1), jnp.concatenate(o_diff, axis=1)


def _proj(wl, w, scr, r0, r1):
    return _dot_nt(scr["hb"][...], w["w_in_t"][wl, r0:r1, :])


def _layer_front(l, wl, x, n_seq, mod, w, scr, rope, ctx, state):
    m_rows = x.shape[0]
    n_b = m_rows // n_seq
    n_ctx = 0 if ctx is None else ctx["ckv"].shape[0]
    nk = n_ctx + n_seq
    shift, scale, _ = mod
    hb_ref = scr["hb"]
    qb_ref, dqb_ref = scr["qb"], scr["dqb"]
    proj = functools.partial(_proj, wl, w, scr)
    kcat_ref, vm_ref, dka_ref, dvb_ref = scr["kcat"], scr["vm"], scr["dka"], scr["dvb"]
    w_ukv = w["w_ukv"]
    half = (_lane_mask(0, LANES // 2), _lane_mask(LANES // 2, LANES))
    new_keys, old_keys = pl.ds(n_ctx, n_seq), pl.ds(0, n_ctx)
    if rope is not None:
        rope_m = tuple(r[...] for r in rope[0])
        rope_d = tuple(r[...] for r in rope[1])

    def seq_rows(b):
        return slice(b * n_seq, (b + 1) * n_seq)

    gs = w["norm_g"][l:l + 1, :] * (1.0 + scale)
    ms = jnp.mean(x * x, axis=-1, keepdims=True)
    hb_ref[...] = (x * lax.rsqrt(ms + EPS) * gs + shift).astype(BF16)

    head = _dot_nt(hb_ref[...], w["w_head"][wl])

    cqn = _rms(head[:, 0:MLA_Q_LORA], w["q_norm"][l:l + 1, :]).astype(BF16)
    q = _dot(cqn, w["w_uq"][wl]) * MLA_SCALE
    if rope is not None:
        q = jnp.concatenate(
            [q[:, 0:2 * LANES], _rope(q[:, 2 * LANES:3 * LANES], rope_m, MLA_ROPE // 4)], axis=1)
    qb_ref[...] = q.astype(BF16)

    def fill_kv(b, dst, kv):
        for h in range(MLA_HEADS):
            t, hh = divmod(h, 2)
            kcat_ref[b, h, dst, 0:LANES] = (kv[:, LANES * t:LANES * (t + 1)] * half[hh]).astype(BF16)
            vm_ref[b, h, dst, 0:LANES] = (
                kv[:, 2 * LANES + LANES * t:2 * LANES + LANES * (t + 1)] * half[hh]).astype(BF16)

    def fill_kr(b, dst, kr):
        for h in range(MLA_HEADS):
            kcat_ref[b, h, dst, LANES:2 * LANES] = (
                kr * _lane_mask(MLA_ROPE * h, MLA_ROPE * (h + 1))).astype(BF16)

    def fill_dk(b, dst, dk):
        for h in range(DIFF_HEADS):
            for a in range(2):
                dka_ref[b, 2 * h + a, dst, :] = (
                    dk[:, LANES * h:LANES * (h + 1)] * half[a]).astype(BF16)

    ckv = _rms(head[:, R_KV:R_KR], w["kv_norm"][l:l + 1, :])
    kv = _dot(ckv.astype(BF16), w_ukv[wl])
    for b in range(n_b):
        if state is not None:
            state[0][b, l] = ckv[seq_rows(b)]
        fill_kv(b, new_keys, kv[seq_rows(b)])
        if ctx is not None:
            fill_kv(b, old_keys, _dot(ctx["ckv"][...].astype(BF16), w_ukv[wl]))

    krr = head[:, R_KR:R_KR + LANES]
    krr_pos = krr if rope is None else _rope(krr, rope_m, MLA_ROPE // 4)
    for b in range(n_b):
        if state is not None:
            state[1][b, l] = krr[seq_rows(b)].T[0:MLA_ROPE, :]
        fill_kr(b, new_keys, krr_pos[seq_rows(b)])
        if ctx is not None:
            fill_kr(b, old_keys, jnp.concatenate([ctx["kr"][...]] * (LANES // MLA_ROPE), axis=0).T)

    dq = proj(R_DQ, R_DK) * DIFF_SCALE
    if rope is not None:
        dq = _rope(dq, rope_d, DIFF_QK // 4)
    dqb_ref[...] = dq.astype(BF16)

    dk = proj(R_DK, R_DV)
    dk_pos = dk if rope is None else _rope(dk, rope_d, DIFF_QK // 4)
    for b in range(n_b):
        if state is not None:
            state[2][b, l] = dk[seq_rows(b)].T
        fill_dk(b, new_keys, dk_pos[seq_rows(b)])
        if ctx is not None:
            fill_dk(b, old_keys, ctx["dk"][...].T)

    dv = proj(R_DV, R_Z)
    for b in range(n_b):
        for h in range(DIFF_HEADS):
            dv_h = dv[seq_rows(b), LANES * h:LANES * (h + 1)]
            if state is not None:
                state[3][b, l, pl.ds(h, n_seq, stride=DIFF_HEADS), :] = dv_h
            dvb_ref[b, h, new_keys, 0:LANES] = dv_h.astype(BF16)
            if ctx is not None:
                dvb_ref[b, h, old_keys, 0:LANES] = (
                    ctx["dv"][pl.ds(h, n_ctx, stride=DIFF_HEADS), :].astype(BF16))
            ones = jnp.ones((nk, LANES), BF16)
            dvb_ref[b, h, :, LANES:2 * LANES] = ones
            vm_ref[b, h, :, LANES:2 * LANES] = ones


def _attend_block(l, b, start, tq, nk, w, scr):
    lf = w["lam"][l]
    lam = (jnp.exp(jnp.sum(lf[0:1] * lf[1:2], axis=-1, keepdims=True))
           - jnp.exp(jnp.sum(lf[2:3] * lf[3:4], axis=-1, keepdims=True)) + _lambda_init(l))
    qrows = pl.ds(start, tq)
    return _attend(scr["qb"][qrows, :], scr["dqb"][qrows, :], pl.ds(0, nk),
                   scr["kcat"].at[b], scr["vm"].at[b], scr["dka"].at[b], scr["dvb"].at[b],
                   lam, w["subln"][l:l + 1, :], _lambda_init(l))


def _conv_branch(l, wl, m_rows, n_seq, w, scr):
    conv = _proj(wl, w, scr, R_CB, R_DQ)
    u = conv[:, W_CONV:2 * W_CONV] * conv[:, 2 * W_CONV:3 * W_CONV]
    pos = lax.broadcasted_iota(jnp.int32, (m_rows, 1), 0) % n_seq
    u_prev = jnp.where(pos == 0, 0.0, pltpu.roll(u, 1, 0))
    u_next = jnp.where(pos == n_seq - 1, 0.0, pltpu.roll(u, m_rows - 1, 0))
    cw = w["conv_w"][:, l, :]
    return conv[:, 0:W_CONV] * (u_prev * cw[0:1] + u * cw[1:2] + u_next * cw[2:3])


def _silu_gate(wl, w, scr, c0, c1):
    z = _proj(wl, w, scr, R_Z + c0, R_Z + c1)
    return z * jax.nn.sigmoid(z)


def _trunk_layer(l, x, n_seq, mod, w, scr, state):
    m_rows = x.shape[0]
    o_ref, mix_ref = scr["o"], scr["mix"]
    _layer_front(l, l, x, n_seq, mod, w, scr, None, None, state)
    for b in range(m_rows // n_seq):
        for start in range(b * n_seq, (b + 1) * n_seq, Q_TILE):
            o_mla, o_diff = _attend_block(l, b, start, Q_TILE, n_seq, w, scr)
            o_ref[start:start + Q_TILE, 0:W_MLA] = o_mla
            o_ref[start:start + Q_TILE, W_MLA + W_CONV:D_MIX] = o_diff
    o_ref[:, W_MLA:W_MLA + W_CONV] = _conv_branch(l, l, m_rows, n_seq, w, scr)
    for c0 in range(0, D_MIX, 512):
        mix_ref[:, c0:c0 + 512] = (
            o_ref[:, c0:c0 + 512] * _silu_gate(l, w, scr, c0, c0 + 512)).astype(BF16)
    return x + mod[2] * _dot(mix_ref[...], w["w_out"][l])


_W_NAMES = ("norm_g", "w_in_t", "w_head", "q_norm", "w_uq", "kv_norm", "w_ukv", "conv_w", "lam",
            "subln", "w_out", "final_norm")
_PER_LAYER_BLOCKS = ("w_in_t", "w_head", "w_uq", "w_ukv", "w_out")
_SCR_NAMES = ("hb", "o", "mix", "qb", "dqb", "kcat", "vm", "dka", "dvb")
_CTX_NAMES = ("ckv", "kr", "dk", "dv")
N_ROPE = 6


def _ctx_kernel(x_ref, mod_ref, *refs, n_seq):
    w = dict(zip(_W_NAMES, refs[:len(_W_NAMES)]))
    rest = refs[len(_W_NAMES):]
    y_ref, state = rest[0], rest[1:5]
    scr = dict(zip(_SCR_NAMES, rest[5:]))
    n_b = x_ref.shape[0]
    x = x_ref[...].reshape(n_b * n_seq, D_MODEL)
    for l in range(DEPTH):
        mod = tuple(mod_ref[l, j, 0:1, :] for j in range(3))
        x = _trunk_layer(l, x, n_seq, mod, w, scr, state)
    y_ref[...] = _rms(x, w["final_norm"][...]).reshape(n_b, n_seq, D_MODEL)


def _lat_kernel(x_ref, mod_ref, *refs, n_seq, layer):
    ctx = {k: r.at[0, 0] for k, r in zip(_CTX_NAMES, refs[:4])}
    rope = (refs[4:7], refs[7:10])
    w = dict(zip(_W_NAMES, refs[4 + N_ROPE:4 + N_ROPE + len(_W_NAMES)]))
    rest = refs[4 + N_ROPE + len(_W_NAMES):]
    y_ref = rest[0]
    scr = dict(zip(_SCR_NAMES, rest[1:]))
    cond_row = pl.ds(1 + pl.program_id(0), 1)
    mod = tuple(mod_ref[layer, j, cond_row, :] for j in range(3))
    zg_ref, mix_ref = scr["o"], scr["mix"]
    n_ctx = ctx["ckv"].shape[0]
    conv_cols = slice(W_MLA, W_MLA + W_CONV)

    @pl.when(pl.program_id(1) == 0)
    def _():
        _layer_front(layer, 0, x_ref[0], n_seq, mod, w, scr, rope, ctx, None)
        for c0 in range(0, D_MIX, 512):
            zg_ref[:, c0:c0 + 512] = _silu_gate(0, w, scr, c0, c0 + 512)
        mix_ref[:, conv_cols] = (
            _conv_branch(layer, 0, n_seq, n_seq, w, scr) * zg_ref[:, conv_cols]).astype(BF16)

    start = pl.multiple_of(pl.program_id(1) * LAT_Q_TILE, LAT_Q_TILE)
    qrows = pl.ds(start, LAT_Q_TILE)
    o_mla, o_diff = _attend_block(layer, 0, start, LAT_Q_TILE, n_ctx + n_seq, w, scr)
    mix_ref[qrows, 0:W_MLA] = (o_mla * zg_ref[qrows, 0:W_MLA]).astype(BF16)
    mix_ref[qrows, W_MLA + W_CONV:D_MIX] = (
        o_diff * zg_ref[qrows, W_MLA + W_CONV:D_MIX]).astype(BF16)
    x = x_ref[0, qrows, :] + mod[2] * _dot(mix_ref[qrows, :], w["w_out"][0])
    if layer == DEPTH - 1:
        x = _rms(x, w["final_norm"][...])
    y_ref[0] = x


def _prep_kernel(cctx_ref, c_ref, ada_w_ref, ada_b_ref, w_in_ref, w_out_ref, w_uq_ref, w_ukv_ref,
                 mod_ref, w_in_bf_ref, w_out_bf_ref, w_head_ref, w_uq_p_ref, w_ukv_p_ref):
    n_rows = mod_ref.shape[2]
    pad = jnp.zeros((n_rows - 1 - c_ref.shape[0], D_MODEL), F32)
    c = jnp.concatenate([cctx_ref[...], c_ref[...], pad], axis=0)
    s = (c * jax.nn.sigmoid(c)).astype(BF16)
    mod_ref[0, 0] = (_dot(s, ada_w_ref[0].astype(BF16))
                     + ada_b_ref[pl.ds(pl.program_id(0), 1), :])
    w_in_bf_ref[0] = w_in_ref[0].astype(BF16)

    @pl.when(pl.program_id(1) == 0)
    def _():
        w_out_bf_ref[0] = w_out_ref[0].astype(BF16)
        w_head_ref[0, R_Q:R_KR, :] = w_in_ref[0, R_Q:R_KR, :].astype(BF16)
        kr = w_in_ref[0, R_KR:R_CB, :].astype(BF16)
        for r in range(LANES // MLA_ROPE):
            w_head_ref[0, R_KR + MLA_ROPE * r:R_KR + MLA_ROPE * (r + 1), :] = kr
        for src, dst, first in ((w_uq_ref, w_uq_p_ref, MLA_NOPE), (w_ukv_ref, w_ukv_p_ref, MLA_NOPE)):
            a = src[0]
            per = a.shape[1] // MLA_HEADS
            cols = ([a[:, per * h:per * h + first] for h in range(MLA_HEADS)]
                    + [a[:, per * h + first:per * (h + 1)] for h in range(MLA_HEADS)])
            dst[0] = jnp.concatenate(cols, axis=1).astype(BF16)


def _const_spec(shape, layer=None):
    if layer is None:
        zeros = (0,) * len(shape)
        return pl.BlockSpec(shape, lambda *_: zeros, pipeline_mode=pl.Buffered(1))
    idx = (layer,) + (0,) * (len(shape) - 1)
    return pl.BlockSpec((1,) + tuple(shape[1:]), lambda *_: idx, pipeline_mode=pl.Buffered(1))


def _scratch(n_b, n_seq, nk):
    m_rows = n_b * n_seq
    return [
        pltpu.VMEM((m_rows, D_MODEL), BF16),
        pltpu.VMEM((m_rows, D_MIX), F32),
        pltpu.VMEM((m_rows, D_MIX), BF16),
        pltpu.VMEM((m_rows, 3 * LANES), BF16),
        pltpu.VMEM((m_rows, W_DIFF), BF16),
        pltpu.VMEM((n_b, MLA_HEADS, nk, 2 * LANES), BF16),
        pltpu.VMEM((n_b, MLA_HEADS, nk, 2 * LANES), BF16),
        pltpu.VMEM((n_b, 2 * DIFF_HEADS, nk, LANES), BF16),
        pltpu.VMEM((n_b, DIFF_HEADS, nk, 2 * LANES), BF16),
    ]


def _rope_tables(n, rot_dim):
    half = rot_dim // 2
    inv = ROPE_THETA ** (-(np.arange(0, half, 2, dtype=np.float32) / half))
    pos = np.arange(n)
    ang_r = (pos // GRID_W).astype(np.float32)[:, None] * inv
    ang_c = (pos % GRID_W).astype(np.float32)[:, None] * inv
    ang = np.concatenate([ang_r, ang_r, ang_c, ang_c], axis=-1)
    cos, sin = np.cos(ang), np.sin(ang)
    q = rot_dim // 4
    first = (np.arange(rot_dim) % (2 * q)) < q
    sin_a = np.where(first, -sin, 0.0)
    sin_b = np.where(first, 0.0, sin)
    reps = LANES // rot_dim
    return tuple(jnp.asarray(np.tile(t, (1, reps)), dtype=F32) for t in (cos, sin_a, sin_b))


def kernel(x_prompt, x_sample, c, cache_mla_ckv, cache_mla_krope, cache_diff_k, cache_diff_v,
           c_ctx, norm_g, ada_w, ada_b, w_in, mla_q_norm, w_uq, mla_kv_norm, w_ukv,
           conv_w, diff_lambda, diff_subln, w_out, final_norm):
    batch, seq, _ = x_prompt.shape
    dec_batch, dec_seq, _ = x_sample.shape
    past = cache_mla_ckv.shape[2]
    cparams = pltpu.CompilerParams(dimension_semantics=("arbitrary",),
                                   vmem_limit_bytes=VMEM_LIMIT_BYTES)

    n_cond = 8
    n_third = 3
    in_rows = D_IN // n_third
    head_rows = R_KR + LANES
    per_layer_spec = lambda a: pl.BlockSpec((1,) + a.shape[1:], lambda l, j: (l, 0, 0))
    mod, w_in_t, w_out_b, w_head, w_uq_p, w_ukv_p = pl.pallas_call(
        _prep_kernel,
        grid=(DEPTH, n_third),
        in_specs=[pl.BlockSpec((1, D_MODEL), lambda l, j: (0, 0)),
                  pl.BlockSpec((dec_batch, D_MODEL), lambda l, j: (0, 0)),
                  pl.BlockSpec((1, D_MODEL, D_MODEL), lambda l, j: (l, 0, j)),
                  pl.BlockSpec((DEPTH, D_MODEL), lambda l, j: (0, j)),
                  pl.BlockSpec((1, in_rows, D_MODEL), lambda l, j: (l, j, 0)),
                  per_layer_spec(w_out), per_layer_spec(w_uq), per_layer_spec(w_ukv)],
        out_specs=[pl.BlockSpec((1, 1, n_cond, D_MODEL), lambda l, j: (l, j, 0, 0)),
                   pl.BlockSpec((1, in_rows, D_MODEL), lambda l, j: (l, j, 0)),
                   per_layer_spec(w_out),
                   pl.BlockSpec((1, head_rows, D_MODEL), lambda l, j: (l, 0, 0)),
                   per_layer_spec(w_uq), per_layer_spec(w_ukv)],
        out_shape=[jax.ShapeDtypeStruct((DEPTH, 3, n_cond, D_MODEL), F32),
                   jax.ShapeDtypeStruct((DEPTH, D_IN, D_MODEL), BF16),
                   jax.ShapeDtypeStruct(w_out.shape, BF16),
                   jax.ShapeDtypeStruct((DEPTH, head_rows, D_MODEL), BF16),
                   jax.ShapeDtypeStruct(w_uq.shape, BF16),
                   jax.ShapeDtypeStruct(w_ukv.shape, BF16)],
        compiler_params=pltpu.CompilerParams(dimension_semantics=("arbitrary", "arbitrary"),
                                             vmem_limit_bytes=VMEM_LIMIT_BYTES),
        name="prep",
    )(c_ctx.reshape(1, D_MODEL), c, ada_w, ada_b, jnp.swapaxes(w_in, 1, 2), w_out, w_uq, w_ukv)

    weights = (norm_g, w_in_t, w_head, mla_q_norm, w_uq_p, mla_kv_norm, w_ukv_p,
               jnp.swapaxes(conv_w, 0, 1), diff_lambda, diff_subln, w_out_b,
               final_norm.reshape(1, D_MODEL))

    tb = 2
    state_shapes = ((seq, MLA_KV_LORA), (MLA_ROPE, seq), (W_DIFF, seq),
                    (DIFF_HEADS * seq, DIFF_V))
    y_prompt, s_ckv, s_kr, s_dk, s_dv = pl.pallas_call(
        functools.partial(_ctx_kernel, n_seq=seq),
        grid=(batch // tb,),
        in_specs=[pl.BlockSpec((tb, seq, D_MODEL), lambda i: (i, 0, 0)),
                  _const_spec(mod.shape)] + [_const_spec(a.shape) for a in weights],
        out_specs=[pl.BlockSpec((tb, seq, D_MODEL), lambda i: (i, 0, 0))]
        + [pl.BlockSpec((tb, DEPTH) + s, lambda i: (i, 0, 0, 0)) for s in state_shapes],
        out_shape=[jax.ShapeDtypeStruct((batch, seq, D_MODEL), F32)]
        + [jax.ShapeDtypeStruct((batch, DEPTH) + s, F32) for s in state_shapes],
        scratch_shapes=_scratch(tb, seq, seq),
        compiler_params=cparams,
        name="ctx",
    )(x_prompt, mod, *weights)

    lat_params = pltpu.CompilerParams(dimension_semantics=("arbitrary", "arbitrary"),
                                      vmem_limit_bytes=VMEM_LIMIT_BYTES)
    tables = _rope_tables(dec_seq, MLA_ROPE) + _rope_tables(dec_seq, DIFF_QK)
    ctx_in = (cache_mla_ckv,
              jnp.swapaxes(cache_mla_krope, 2, 3),
              jnp.transpose(cache_diff_k, (0, 1, 3, 4, 5, 2)).reshape(dec_batch, DEPTH, W_DIFF, past),
              cache_diff_v.reshape(dec_batch, DEPTH, past * DIFF_HEADS, DIFF_V))
    h_lat = x_sample
    for l in range(DEPTH):
        h_lat = pl.pallas_call(
            functools.partial(_lat_kernel, n_seq=dec_seq, layer=l),
            grid=(dec_batch, dec_seq // LAT_Q_TILE),
            in_specs=[pl.BlockSpec((1, dec_seq, D_MODEL), lambda i, j: (i, 0, 0)),
                      _const_spec(mod.shape)]
            + [pl.BlockSpec((1, 1) + a.shape[2:], lambda i, j, l=l: (i, l, 0, 0))
               for a in ctx_in]
            + [_const_spec(t.shape) for t in tables]
            + [_const_spec(a.shape, layer=l if name in _PER_LAYER_BLOCKS else None)
               for name, a in zip(_W_NAMES, weights)],
            out_specs=pl.BlockSpec((1, LAT_Q_TILE, D_MODEL), lambda i, j: (i, j, 0)),
            out_shape=jax.ShapeDtypeStruct((dec_batch, dec_seq, D_MODEL), F32),
            scratch_shapes=_scratch(1, dec_seq, past + dec_seq),
            compiler_params=lat_params,
            name=f"latent_{l}",
        )(h_lat, mod, *ctx_in, *tables, *weights)

    return (y_prompt, h_lat, s_ckv,
            jnp.swapaxes(s_kr, 2, 3),
            jnp.transpose(s_dk.reshape(batch, DEPTH, DIFF_HEADS, 2, DIFF_QK, seq),
                          (0, 1, 5, 2, 3, 4)),
            s_dv.reshape(batch, DEPTH, seq, DIFF_HEADS, DIFF_V))
```

```python
import functools
import math

import numpy as np
import jax
import jax.numpy as jnp
from jax import lax
from jax.experimental import pallas as pl
from jax.experimental.pallas import tpu as pltpu

D_MODEL = 1024
DEPTH = 2
GRID_W = 64
MLA_HEADS = 4
MLA_NOPE = 64
MLA_ROPE = 32
MLA_V = 64
MLA_Q_LORA = 256
MLA_KV_LORA = 128
W_MLA = MLA_HEADS * MLA_V
W_CONV = 256
DIFF_HEADS = 4
DIFF_QK = 64
DIFF_V = 2 * DIFF_QK
W_DIFF = DIFF_HEADS * DIFF_V
D_MIX = W_MLA + W_CONV + W_DIFF
ROPE_THETA = 10000.0
EPS = 1e-6

LANES = 128
Q_TILE = 256
LAT_Q_TILE = 256
LAT_FRONT_ROWS = 512
LOG2E = math.log2(math.e)
MLA_SCALE = (MLA_NOPE + MLA_ROPE) ** -0.5 * LOG2E
DIFF_SCALE = DIFF_QK ** -0.5 * LOG2E
VMEM_LIMIT_BYTES = 54 * 1024 * 1024

R_Q = 0
R_KV = R_Q + MLA_Q_LORA
R_KR = R_KV + MLA_KV_LORA
R_CB = R_KR + MLA_ROPE
R_CC = R_CB + W_CONV
R_CX = R_CC + W_CONV
R_DQ = R_CX + W_CONV
R_DK = R_DQ + W_DIFF
R_DV = R_DK + W_DIFF
R_Z = R_DV + W_DIFF
D_IN = R_Z + D_MIX

BF16 = jnp.bfloat16
F32 = jnp.float32


def _lambda_init(layer):
    return 0.8 - 0.6 * math.exp(-0.3 * layer)


def _dot(a, b):
    return jnp.dot(a, b, preferred_element_type=F32)


def _dot_nt(a, b):
    return lax.dot_general(a, b, (((1,), (1,)), ((), ())), preferred_element_type=F32)


def _rms(x, g):
    ms = jnp.mean(x * x, axis=-1, keepdims=True)
    return x * lax.rsqrt(ms + EPS) * g


def _exp_scores(s):
    m = s[:, 0:LANES]
    for t in range(1, s.shape[1] // LANES):
        m = jnp.maximum(m, s[:, LANES * t:LANES * (t + 1)])
    return jnp.exp2(s - jnp.max(m, axis=-1, keepdims=True)).astype(BF16)


def _weighted(e, v_ones):
    r = _dot(e, v_ones)
    return r[:, 0:LANES], 1.0 / r[:, LANES:2 * LANES]


def _lane_mask(lo, hi):
    lane = lax.broadcasted_iota(jnp.int32, (1, LANES), 1)
    return jnp.where((lane >= lo) & (lane < hi), 1.0, 0.0).astype(F32)


def _rope(x, tables, q):
    cos, sin_signed = tables
    lane = lax.broadcasted_iota(jnp.int32, (1, LANES), 1)
    first = (lane & (2 * q - 1)) < q
    tiles = []
    for t in range(x.shape[1] // LANES):
        xt = x[:, LANES * t:LANES * (t + 1)]
        rot = jnp.where(first, pltpu.roll(xt, LANES - q, 1), pltpu.roll(xt, q, 1))
        tiles.append(xt * cos + rot * sin_signed)
    return tiles[0] if len(tiles) == 1 else jnp.concatenate(tiles, axis=1)


def _attend(q, dq, keys, kcat_ref, vm_ref, dka_ref, dvb_ref, lam, subln, lam_init):
    qr = q[:, 2 * LANES:3 * LANES]
    o_mla = []
    for t in range(2):
        qcat = jnp.concatenate([q[:, LANES * t:LANES * (t + 1)], qr], axis=1)
        acc = None
        for hh in range(2):
            h = 2 * t + hh
            o, rinv = _weighted(_exp_scores(_dot_nt(qcat, kcat_ref[h, keys, :])),
                                vm_ref[h, keys, :])
            acc = o * rinv if acc is None else acc + o * rinv
        o_mla.append(acc)
    o_diff = []
    for h in range(DIFF_HEADS):
        qt = dq[:, LANES * h:LANES * (h + 1)]
        o0, rinv0 = _weighted(_exp_scores(_dot_nt(qt, dka_ref[2 * h, keys, :])),
                              dvb_ref[h, keys, :])
        o1, rinv1 = _weighted(_exp_scores(_dot_nt(qt, dka_ref[2 * h + 1, keys, :])),
                              dvb_ref[h, keys, :])
        o = o0 * rinv0 - o1 * (lam * rinv1)
        o_diff.append(_rms(o, subln) * (1.0 - lam_init))
    return jnp.concatenate(o_mla, axis=1), jnp.concatenate(o_diff, axis=1)


def _proj(wl, w, scr, r0, r1):
    return _dot_nt(scr["hb"][...], w["w_in_t"][wl, r0:r1, :])


def _layer_front(l, wl, x, n_seq, mod, w, scr, rope, ctx, state, key_off=0, nk=None,
                 fill_ctx=True):
    m_rows = x.shape[0]
    n_b = m_rows // n_seq
    n_ctx = 0 if ctx is None else ctx["ckv"].shape[0]
    nk = n_ctx + n_seq if nk is None else nk
    shift, scale, _ = mod
    hb_ref = scr["hb"]
    qb_ref, dqb_ref = scr["qb"], scr["dqb"]
    proj = functools.partial(_proj, wl, w, scr)
    kcat_ref, vm_ref, dka_ref, dvb_ref = scr["kcat"], scr["vm"], scr["dka"], scr["dvb"]
    w_ukv = w["w_ukv"]
    half = (_lane_mask(0, LANES // 2), _lane_mask(LANES // 2, LANES))
    new_keys, old_keys = pl.ds(n_ctx + key_off, n_seq), pl.ds(0, n_ctx)
    fill_ctx = fill_ctx and ctx is not None
    if rope is not None:
        rope_m = tuple(r[...] for r in rope[0])
        rope_d = tuple(r[...] for r in rope[1])

    def seq_rows(b):
        return slice(b * n_seq, (b + 1) * n_seq)

    gs = w["norm_g"][l:l + 1, :] * (1.0 + scale)
    ms = jnp.mean(x * x, axis=-1, keepdims=True)
    hb_ref[...] = (x * lax.rsqrt(ms + EPS) * gs + shift).astype(BF16)

    head = _dot_nt(hb_ref[...], w["w_head"][wl])

    cqn = _rms(head[:, 0:MLA_Q_LORA], w["q_norm"][l:l + 1, :]).astype(BF16)
    q = _dot(cqn, w["w_uq"][wl]) * MLA_SCALE
    if rope is not None:
        q = jnp.concatenate(
            [q[:, 0:2 * LANES], _rope(q[:, 2 * LANES:3 * LANES], rope_m, MLA_ROPE // 4)], axis=1)
    qb_ref[...] = q.astype(BF16)

    def fill_kv(b, dst, kv):
        for h in range(MLA_HEADS):
            t, hh = divmod(h, 2)
            kcat_ref[b, h, dst, 0:LANES] = (kv[:, LANES * t:LANES * (t + 1)] * half[hh]).astype(BF16)
            vm_ref[b, h, dst, 0:LANES] = (
                kv[:, 2 * LANES + LANES * t:2 * LANES + LANES * (t + 1)] * half[hh]).astype(BF16)

    def fill_kr(b, dst, kr):
        for h in range(MLA_HEADS):
            kcat_ref[b, h, dst, LANES:2 * LANES] = (
                kr * _lane_mask(MLA_ROPE * h, MLA_ROPE * (h + 1))).astype(BF16)

    def fill_dk(b, dst, dk):
        for h in range(DIFF_HEADS):
            for a in range(2):
                dka_ref[b, 2 * h + a, dst, :] = (
                    dk[:, LANES * h:LANES * (h + 1)] * half[a]).astype(BF16)

    ckv = _rms(head[:, R_KV:R_KR], w["kv_norm"][l:l + 1, :])
    kv = _dot(ckv.astype(BF16), w_ukv[wl])
    for b in range(n_b):
        if state is not None:
            state[0][b, l] = ckv[seq_rows(b)]
        fill_kv(b, new_keys, kv[seq_rows(b)])
        if fill_ctx:
            fill_kv(b, old_keys, _dot(ctx["ckv"][...].astype(BF16), w_ukv[wl]))

    krr = head[:, R_KR:R_KR + LANES]
    krr_pos = krr if rope is None else _rope(krr, rope_m, MLA_ROPE // 4)
    for b in range(n_b):
        if state is not None:
            state[1][b, l] = krr[seq_rows(b)].T[0:MLA_ROPE, :]
        fill_kr(b, new_keys, krr_pos[seq_rows(b)])
        if fill_ctx:
            fill_kr(b, old_keys, jnp.concatenate([ctx["kr"][...]] * (LANES // MLA_ROPE), axis=0).T)

    dq = proj(R_DQ, R_DK) * DIFF_SCALE
    if rope is not None:
        dq = _rope(dq, rope_d, DIFF_QK // 4)
    dqb_ref[...] = dq.astype(BF16)

    dk = proj(R_DK, R_DV)
    dk_pos = dk if rope is None else _rope(dk, rope_d, DIFF_QK // 4)
    for b in range(n_b):
        if state is not None:
            state[2][b, l] = dk[seq_rows(b)].T
        fill_dk(b, new_keys, dk_pos[seq_rows(b)])
        if fill_ctx:
            fill_dk(b, old_keys, ctx["dk"][...].T)

    dv = proj(R_DV, R_Z)
    for b in range(n_b):
        for h in range(DIFF_HEADS):
            dv_h = dv[seq_rows(b), LANES * h:LANES * (h + 1)]
            if state is not None:
                state[3][b, l, pl.ds(h, n_seq, stride=DIFF_HEADS), :] = dv_h
            dvb_ref[b, h, new_keys, 0:LANES] = dv_h.astype(BF16)
            if fill_ctx:
                dvb_ref[b, h, old_keys, 0:LANES] = (
                    ctx["dv"][pl.ds(h, n_ctx, stride=DIFF_HEADS), :].astype(BF16))
            if fill_ctx or ctx is None:
                ones = jnp.ones((nk, LANES), BF16)
                dvb_ref[b, h, :, LANES:2 * LANES] = ones
                vm_ref[b, h, :, LANES:2 * LANES] = ones


def _attend_block(l, b, start, tq, nk, w, scr):
    lf = w["lam"][l]
    lam = (jnp.exp(jnp.sum(lf[0:1] * lf[1:2], axis=-1, keepdims=True))
           - jnp.exp(jnp.sum(lf[2:3] * lf[3:4], axis=-1, keepdims=True)) + _lambda_init(l))
    qrows = pl.ds(start, tq)
    return _attend(scr["qb"][qrows, :], scr["dqb"][qrows, :], pl.ds(0, nk),
                   scr["kcat"].at[b], scr["vm"].at[b], scr["dka"].at[b], scr["dvb"].at[b],
                   lam, w["subln"][l:l + 1, :], _lambda_init(l))


def _conv_branch(l, wl, m_rows, n_seq, w, scr):
    conv = _proj(wl, w, scr, R_CB, R_DQ)
    u = conv[:, W_CONV:2 * W_CONV] * conv[:, 2 * W_CONV:3 * W_CONV]
    pos = lax.broadcasted_iota(jnp.int32, (m_rows, 1), 0) % n_seq
    u_prev = jnp.where(pos == 0, 0.0, pltpu.roll(u, 1, 0))
    u_next = jnp.where(pos == n_seq - 1, 0.0, pltpu.roll(u, m_rows - 1, 0))
    cw = w["conv_w"][:, l, :]
    return conv[:, 0:W_CONV] * (u_prev * cw[0:1] + u * cw[1:2] + u_next * cw[2:3])


def _silu_gate(wl, w, scr, c0, c1):
    z = _proj(wl, w, scr, R_Z + c0, R_Z + c1)
    return z * jax.nn.sigmoid(z)


def _trunk_layer(l, x, n_seq, mod, w, scr, state):
    m_rows = x.shape[0]
    o_ref, mix_ref = scr["o"], scr["mix"]
    _layer_front(l, l, x, n_seq, mod, w, scr, None, None, state)
    for b in range(m_rows // n_seq):
        for start in range(b * n_seq, (b + 1) * n_seq, Q_TILE):
            o_mla, o_diff = _attend_block(l, b, start, Q_TILE, n_seq, w, scr)
            o_ref[start:start + Q_TILE, 0:W_MLA] = o_mla
            o_ref[start:start + Q_TILE, W_MLA + W_CONV:D_MIX] = o_diff
    o_ref[:, W_MLA:W_MLA + W_CONV] = _conv_branch(l, l, m_rows, n_seq, w, scr)
    for c0 in range(0, D_MIX, 512):
        mix_ref[:, c0:c0 + 512] = (
            o_ref[:, c0:c0 + 512] * _silu_gate(l, w, scr, c0, c0 + 512)).astype(BF16)
    return x + mod[2] * _dot(mix_ref[...], w["w_out"][l])


_W_NAMES = ("norm_g", "w_in_t", "w_head", "q_norm", "w_uq", "kv_norm", "w_ukv", "conv_w", "lam",
            "subln", "w_out", "final_norm")
_PER_LAYER_BLOCKS = ("w_in_t", "w_head", "w_uq", "w_ukv", "w_out")
_SCR_NAMES = ("hb", "o", "mix", "qb", "dqb", "kcat", "vm", "dka", "dvb")
_CTX_NAMES = ("ckv", "kr", "dk", "dv")
N_ROPE = 4


def _ctx_kernel(x_ref, mod_ref, *refs, n_seq):
    w = dict(zip(_W_NAMES, refs[:len(_W_NAMES)]))
    rest = refs[len(_W_NAMES):]
    y_ref, state = rest[0], rest[1:5]
    scr = dict(zip(_SCR_NAMES, rest[5:]))
    n_b = x_ref.shape[0]
    x = x_ref[...].reshape(n_b * n_seq, D_MODEL)
    for l in range(DEPTH):
        mod = tuple(mod_ref[l, j, 0:1, :] for j in range(3))
        x = _trunk_layer(l, x, n_seq, mod, w, scr, state)
    y_ref[...] = _rms(x, w["final_norm"][...]).reshape(n_b, n_seq, D_MODEL)


def _lat_kernel(x_ref, mod_ref, *refs, n_seq, layer):
    ctx = {k: r.at[0, 0] for k, r in zip(_CTX_NAMES, refs[:4])}
    rope = (refs[4:6], refs[6:8])
    w = dict(zip(_W_NAMES, refs[4 + N_ROPE:4 + N_ROPE + len(_W_NAMES)]))
    rest = refs[4 + N_ROPE + len(_W_NAMES):]
    y_ref = rest[0]
    scr = dict(zip(_SCR_NAMES, rest[1:]))
    cond_row = pl.ds(1 + pl.program_id(0), 1)
    mod = tuple(mod_ref[layer, j, cond_row, :] for j in range(3))
    zg_ref, mix_ref = scr["o"], scr["mix"]
    n_ctx = ctx["ckv"].shape[0]
    conv_cols = slice(W_MLA, W_MLA + W_CONV)

    @pl.when(pl.program_id(1) == 0)
    def _():
        for r0 in range(0, n_seq, LAT_FRONT_ROWS):
            rows = pl.ds(r0, LAT_FRONT_ROWS)
            scr_r = dict(scr, **{k: scr[k].at[rows] for k in ("hb", "qb", "dqb")})
            rope_r = tuple(tuple(t.at[rows] for t in kind) for kind in rope)
            _layer_front(layer, 0, x_ref[0, rows, :], LAT_FRONT_ROWS, mod, w, scr_r, rope_r, ctx,
                         None, key_off=r0, nk=n_ctx + n_seq, fill_ctx=(r0 == 0))
        for c0 in range(0, D_MIX, 512):
            zg_ref[:, c0:c0 + 512] = _silu_gate(0, w, scr, c0, c0 + 512)
        mix_ref[:, conv_cols] = (
            _conv_branch(layer, 0, n_seq, n_seq, w, scr) * zg_ref[:, conv_cols]).astype(BF16)

    start = pl.multiple_of(pl.program_id(1) * LAT_Q_TILE, LAT_Q_TILE)
    qrows = pl.ds(start, LAT_Q_TILE)
    o_mla, o_diff = _attend_block(layer, 0, start, LAT_Q_TILE, n_ctx + n_seq, w, scr)
    mix_ref[qrows, 0:W_MLA] = (o_mla * zg_ref[qrows, 0:W_MLA]).astype(BF16)
    mix_ref[qrows, W_MLA + W_CONV:D_MIX] = (
        o_diff * zg_ref[qrows, W_MLA + W_CONV:D_MIX]).astype(BF16)
    x = x_ref[0, qrows, :] + mod[2] * _dot(mix_ref[qrows, :], w["w_out"][0])
    if layer == DEPTH - 1:
        x = _rms(x, w["final_norm"][...])
    y_ref[0] = x


def _prep_kernel(cctx_ref, c_ref, ada_w_ref, ada_b_ref, w_in_ref, w_out_ref, w_uq_ref, w_ukv_ref,
                 mod_ref, w_in_bf_ref, w_out_bf_ref, w_head_ref, w_uq_p_ref, w_ukv_p_ref):
    n_rows = mod_ref.shape[2]
    pad = jnp.zeros((n_rows - 1 - c_ref.shape[0], D_MODEL), F32)
    c = jnp.concatenate([cctx_ref[...], c_ref[...], pad], axis=0)
    s = (c * jax.nn.sigmoid(c)).astype(BF16)
    mod_ref[0, 0] = (_dot(s, ada_w_ref[0].astype(BF16))
                     + ada_b_ref[pl.ds(pl.program_id(0), 1), :])
    w_in_bf_ref[0] = w_in_ref[0].astype(BF16)

    @pl.when(pl.program_id(1) == 0)
    def _():
        w_out_bf_ref[0] = w_out_ref[0].astype(BF16)
        w_head_ref[0, R_Q:R_KR, :] = w_in_ref[0, R_Q:R_KR, :].astype(BF16)
        kr = w_in_ref[0, R_KR:R_CB, :].astype(BF16)
        for r in range(LANES // MLA_ROPE):
            w_head_ref[0, R_KR + MLA_ROPE * r:R_KR + MLA_ROPE * (r + 1), :] = kr
        for src, dst, first in ((w_uq_ref, w_uq_p_ref, MLA_NOPE), (w_ukv_ref, w_ukv_p_ref, MLA_NOPE)):
            a = src[0]
            per = a.shape[1] // MLA_HEADS
            cols = ([a[:, per * h:per * h + first] for h in range(MLA_HEADS)]
                    + [a[:, per * h + first:per * (h + 1)] for h in range(MLA_HEADS)])
            dst[0] = jnp.concatenate(cols, axis=1).astype(BF16)


def _const_spec(shape, layer=None):
    if layer is None:
        zeros = (0,) * len(shape)
        return pl.BlockSpec(shape, lambda *_: zeros, pipeline_mode=pl.Buffered(1))
    idx = (layer,) + (0,) * (len(shape) - 1)
    return pl.BlockSpec((1,) + tuple(shape[1:]), lambda *_: idx, pipeline_mode=pl.Buffered(1))


def _scratch(n_b, n_seq, nk):
    m_rows = n_b * n_seq
    return [
        pltpu.VMEM((m_rows, D_MODEL), BF16),
        pltpu.VMEM((m_rows, D_MIX), F32),
        pltpu.VMEM((m_rows, D_MIX), BF16),
        pltpu.VMEM((m_rows, 3 * LANES), BF16),
        pltpu.VMEM((m_rows, W_DIFF), BF16),
        pltpu.VMEM((n_b, MLA_HEADS, nk, 2 * LANES), BF16),
        pltpu.VMEM((n_b, MLA_HEADS, nk, 2 * LANES), BF16),
        pltpu.VMEM((n_b, 2 * DIFF_HEADS, nk, LANES), BF16),
        pltpu.VMEM((n_b, DIFF_HEADS, nk, 2 * LANES), BF16),
    ]


def _rope_tables(n, rot_dim):
    half = rot_dim // 2
    inv = ROPE_THETA ** (-(np.arange(0, half, 2, dtype=np.float32) / half))
    pos = np.arange(n)
    ang_r = (pos // GRID_W).astype(np.float32)[:, None] * inv
    ang_c = (pos % GRID_W).astype(np.float32)[:, None] * inv
    ang = np.concatenate([ang_r, ang_r, ang_c, ang_c], axis=-1)
    cos, sin = np.cos(ang), np.sin(ang)
    q = rot_dim // 4
    first = (np.arange(rot_dim) % (2 * q)) < q
    sin_signed = np.where(first, -sin, sin)
    reps = LANES // rot_dim
    return tuple(jnp.asarray(np.tile(t, (1, reps)), dtype=F32) for t in (cos, sin_signed))


def kernel(x_prompt, x_sample, c, cache_mla_ckv, cache_mla_krope, cache_diff_k, cache_diff_v,
           c_ctx, norm_g, ada_w, ada_b, w_in, mla_q_norm, w_uq, mla_kv_norm, w_ukv,
           conv_w, diff_lambda, diff_subln, w_out, final_norm):
    batch, seq, _ = x_prompt.shape
    dec_batch, dec_seq, _ = x_sample.shape
    past = cache_mla_ckv.shape[2]
    cparams = pltpu.CompilerParams(dimension_semantics=("arbitrary",),
                                   vmem_limit_bytes=VMEM_LIMIT_BYTES)

    n_cond = 8
    n_third = 3
    in_rows = D_IN // n_third
    head_rows = R_KR + LANES
    per_layer_spec = lambda a: pl.BlockSpec((1,) + a.shape[1:], lambda l, j: (l, 0, 0))
    mod, w_in_t, w_out_b, w_head, w_uq_p, w_ukv_p = pl.pallas_call(
        _prep_kernel,
        grid=(DEPTH, n_third),
        in_specs=[pl.BlockSpec((1, D_MODEL), lambda l, j: (0, 0)),
                  pl.BlockSpec((dec_batch, D_MODEL), lambda l, j: (0, 0)),
                  pl.BlockSpec((1, D_MODEL, D_MODEL), lambda l, j: (l, 0, j)),
                  pl.BlockSpec((DEPTH, D_MODEL), lambda l, j: (0, j)),
                  pl.BlockSpec((1, in_rows, D_MODEL), lambda l, j: (l, j, 0)),
                  per_layer_spec(w_out), per_layer_spec(w_uq), per_layer_spec(w_ukv)],
        out_specs=[pl.BlockSpec((1, 1, n_cond, D_MODEL), lambda l, j: (l, j, 0, 0)),
                   pl.BlockSpec((1, in_rows, D_MODEL), lambda l, j: (l, j, 0)),
                   per_layer_spec(w_out),
                   pl.BlockSpec((1, head_rows, D_MODEL), lambda l, j: (l, 0, 0)),
                   per_layer_spec(w_uq), per_layer_spec(w_ukv)],
        out_shape=[jax.ShapeDtypeStruct((DEPTH, 3, n_cond, D_MODEL), F32),
                   jax.ShapeDtypeStruct((DEPTH, D_IN, D_MODEL), BF16),
                   jax.ShapeDtypeStruct(w_out.shape, BF16),
                   jax.ShapeDtypeStruct((DEPTH, head_rows, D_MODEL), BF16),
                   jax.ShapeDtypeStruct(w_uq.shape, BF16),
                   jax.ShapeDtypeStruct(w_ukv.shape, BF16)],
        compiler_params=pltpu.CompilerParams(dimension_semantics=("arbitrary", "arbitrary"),
                                             vmem_limit_bytes=VMEM_LIMIT_BYTES),
        name="prep",
    )(c_ctx.reshape(1, D_MODEL), c, ada_w, ada_b, jnp.swapaxes(w_in, 1, 2), w_out, w_uq, w_ukv)

    weights = (norm_g, w_in_t, w_head, mla_q_norm, w_uq_p, mla_kv_norm, w_ukv_p,
               jnp.swapaxes(conv_w, 0, 1), diff_lambda, diff_subln, w_out_b,
               final_norm.reshape(1, D_MODEL))

    tb = 2
    state_shapes = ((seq, MLA_KV_LORA), (MLA_ROPE, seq), (W_DIFF, seq),
                    (DIFF_HEADS * seq, DIFF_V))
    y_prompt, s_ckv, s_kr, s_dk, s_dv = pl.pallas_call(
        functools.partial(_ctx_kernel, n_seq=seq),
        grid=(batch // tb,),
        in_specs=[pl.BlockSpec((tb, seq, D_MODEL), lambda i: (i, 0, 0)),
                  _const_spec(mod.shape)] + [_const_spec(a.shape) for a in weights],
        out_specs=[pl.BlockSpec((tb, seq, D_MODEL), lambda i: (i, 0, 0))]
        + [pl.BlockSpec((tb, DEPTH) + s, lambda i: (i, 0, 0, 0)) for s in state_shapes],
        out_shape=[jax.ShapeDtypeStruct((batch, seq, D_MODEL), F32)]
        + [jax.ShapeDtypeStruct((batch, DEPTH) + s, F32) for s in state_shapes],
        scratch_shapes=_scratch(tb, seq, seq),
        compiler_params=cparams,
        name="ctx",
    )(x_prompt, mod, *weights)

    lat_params = pltpu.CompilerParams(dimension_semantics=("arbitrary", "arbitrary"),
                                      vmem_limit_bytes=VMEM_LIMIT_BYTES)
    tables = _rope_tables(dec_seq, MLA_ROPE) + _rope_tables(dec_seq, DIFF_QK)
    ctx_in = (cache_mla_ckv,
              jnp.swapaxes(cache_mla_krope, 2, 3),
              jnp.transpose(cache_diff_k, (0, 1, 3, 4, 5, 2)).reshape(dec_batch, DEPTH, W_DIFF, past),
              cache_diff_v.reshape(dec_batch, DEPTH, past * DIFF_HEADS, DIFF_V))
    h_lat = x_sample
    for l in range(DEPTH):
        h_lat = pl.pallas_call(
            functools.partial(_lat_kernel, n_seq=dec_seq, layer=l),
            grid=(dec_batch, dec_seq // LAT_Q_TILE),
            in_specs=[pl.BlockSpec((1, dec_seq, D_MODEL), lambda i, j: (i, 0, 0)),
                      _const_spec(mod.shape)]
            + [pl.BlockSpec((1, 1) + a.shape[2:], lambda i, j, l=l: (i, l, 0, 0))
               for a in ctx_in]
            + [_const_spec(t.shape) for t in tables]
            + [_const_spec(a.shape, layer=l if name in _PER_LAYER_BLOCKS else None)
               for name, a in zip(_W_NAMES, weights)],
            out_specs=pl.BlockSpec((1, LAT_Q_TILE, D_MODEL), lambda i, j: (i, j, 0)),
            out_shape=jax.ShapeDtypeStruct((dec_batch, dec_seq, D_MODEL), F32),
            scratch_shapes=_scratch(1, dec_seq, past + dec_seq),
            compiler_params=lat_params,
            name=f"latent_{l}",
        )(h_lat, mod, *ctx_in, *tables, *weights)

    return (y_prompt, h_lat, s_ckv,
            jnp.swapaxes(s_kr, 2, 3),
            jnp.transpose(s_dk.reshape(batch, DEPTH, DIFF_HEADS, 2, DIFF_QK, seq),
                          (0, 1, 5, 2, 3, 4)),
            s_dv.reshape(batch, DEPTH, seq, DIFF_HEADS, DIFF_V))
```

```python
import functools
import math

import numpy as np
import jax
import jax.numpy as jnp
from jax import lax
from jax.experimental import pallas as pl
from jax.experimental.pallas import tpu as pltpu

D_MODEL = 1024
DEPTH = 2
GRID_W = 64
MLA_HEADS = 4
MLA_NOPE = 64
MLA_ROPE = 32
MLA_V = 64
MLA_Q_LORA = 256
MLA_KV_LORA = 128
W_MLA = MLA_HEADS * MLA_V
W_CONV = 256
DIFF_HEADS = 4
DIFF_QK = 64
DIFF_V = 2 * DIFF_QK
W_DIFF = DIFF_HEADS * DIFF_V
D_MIX = W_MLA + W_CONV + W_DIFF
ROPE_THETA = 10000.0
EPS = 1e-6

LANES = 128
Q_TILE = 256
LAT_Q_TILE = 512
LOG2E = math.log2(math.e)
MLA_SCALE = (MLA_NOPE + MLA_ROPE) ** -0.5 * LOG2E
DIFF_SCALE = DIFF_QK ** -0.5 * LOG2E
VMEM_LIMIT_BYTES = 54 * 1024 * 1024

R_Q = 0
R_KV = R_Q + MLA_Q_LORA
R_KR = R_KV + MLA_KV_LORA
R_CB = R_KR + MLA_ROPE
R_CC = R_CB + W_CONV
R_CX = R_CC + W_CONV
R_DQ = R_CX + W_CONV
R_DK = R_DQ + W_DIFF
R_DV = R_DK + W_DIFF
R_Z = R_DV + W_DIFF
D_IN = R_Z + D_MIX

BF16 = jnp.bfloat16
F32 = jnp.float32


def _lambda_init(layer):
    return 0.8 - 0.6 * math.exp(-0.3 * layer)


def _dot(a, b):
    return jnp.dot(a, b, preferred_element_type=F32)


def _dot_nt(a, b):
    return lax.dot_general(a, b, (((1,), (1,)), ((), ())), preferred_element_type=F32)


def _rms(x, g):
    ms = jnp.mean(x * x, axis=-1, keepdims=True)
    return x * lax.rsqrt(ms + EPS) * g


def _exp_scores(s):
    m = s[:, 0:LANES]
    for t in range(1, s.shape[1] // LANES):
        m = jnp.maximum(m, s[:, LANES * t:LANES * (t + 1)])
    return jnp.exp2(s - jnp.max(m, axis=-1, keepdims=True)).astype(BF16)


def _weighted(e, v_ones):
    r = _dot(e, v_ones)
    return r[:, 0:LANES], 1.0 / r[:, LANES:2 * LANES]


def _lane_mask(lo, hi):
    lane = lax.broadcasted_iota(jnp.int32, (1, LANES), 1)
    return jnp.where((lane >= lo) & (lane < hi), 1.0, 0.0).astype(F32)


def _rope(x, tables, q):
    cos, sin_signed = tables
    lane = lax.broadcasted_iota(jnp.int32, (1, LANES), 1)
    first = (lane & (2 * q - 1)) < q
    tiles = []
    for t in range(x.shape[1] // LANES):
        xt = x[:, LANES * t:LANES * (t + 1)]
        rot = jnp.where(first, pltpu.roll(xt, LANES - q, 1), pltpu.roll(xt, q, 1))
        tiles.append(xt * cos + rot * sin_signed)
    return tiles[0] if len(tiles) == 1 else jnp.concatenate(tiles, axis=1)


def _attend(q, dq, keys, kcat_ref, vm_ref, dka_ref, dvb_ref, lam, subln, lam_init):
    qr = q[:, 2 * LANES:3 * LANES]
    o_mla = []
    for t in range(2):
        qcat = jnp.concatenate([q[:, LANES * t:LANES * (t + 1)], qr], axis=1)
        acc = None
        for hh in range(2):
            h = 2 * t + hh
            o, rinv = _weighted(_exp_scores(_dot_nt(qcat, kcat_ref[h, keys, :])),
                                vm_ref[h, keys, :])
            acc = o * rinv if acc is None else acc + o * rinv
        o_mla.append(acc)
    o_diff = []
    for h in range(DIFF_HEADS):
        qt = dq[:, LANES * h:LANES * (h + 1)]
        o0, rinv0 = _weighted(_exp_scores(_dot_nt(qt, dka_ref[2 * h, keys, :])),
                              dvb_ref[h, keys, :])
        o1, rinv1 = _weighted(_exp_scores(_dot_nt(qt, dka_ref[2 * h + 1, keys, :])),
                              dvb_ref[h, keys, :])
        o = o0 * rinv0 - o1 * (lam * rinv1)
        o_diff.append(_rms(o, subln) * (1.0 - lam_init))
    return jnp.concatenate(o_mla, axis=1), jnp.concatenate(o_diff, axis=1)


def _proj(wl, w, scr, r0, r1):
    return _dot_nt(scr["hb"][...], w["w_in_t"][wl, r0:r1, :])


def _layer_front(l, wl, x, n_seq, mod, w, scr, rope, ctx, state):
    m_rows = x.shape[0]
    n_b = m_rows // n_seq
    n_ctx = 0 if ctx is None else ctx["ckv"].shape[0]
    nk = n_ctx + n_seq
    shift, scale, _ = mod
    hb_ref = scr["hb"]
    qb_ref, dqb_ref = scr["qb"], scr["dqb"]
    proj = functools.partial(_proj, wl, w, scr)
    kcat_ref, vm_ref, dka_ref, dvb_ref = scr["kcat"], scr["vm"], scr["dka"], scr["dvb"]
    w_ukv = w["w_ukv"]
    half = (_lane_mask(0, LANES // 2), _lane_mask(LANES // 2, LANES))
    new_keys, old_keys = pl.ds(n_ctx, n_seq), pl.ds(0, n_ctx)
    if rope is not None:
        rope_m = tuple(r[...] for r in rope[0])
        rope_d = tuple(r[...] for r in rope[1])

    def seq_rows(b):
        return slice(b * n_seq, (b + 1) * n_seq)

    gs = w["norm_g"][l:l + 1, :] * (1.0 + scale)
    ms = jnp.mean(x * x, axis=-1, keepdims=True)
    hb_ref[...] = (x * lax.rsqrt(ms + EPS) * gs + shift).astype(BF16)

    head = _dot_nt(hb_ref[...], w["w_head"][wl])

    cqn = _rms(head[:, 0:MLA_Q_LORA], w["q_norm"][l:l + 1, :]).astype(BF16)
    q = _dot(cqn, w["w_uq"][wl]) * MLA_SCALE
    if rope is not None:
        q = jnp.concatenate(
            [q[:, 0:2 * LANES], _rope(q[:, 2 * LANES:3 * LANES], rope_m, MLA_ROPE // 4)], axis=1)
    qb_ref[...] = q.astype(BF16)

    def fill_kv(b, dst, kv):
        for h in range(MLA_HEADS):
            t, hh = divmod(h, 2)
            kcat_ref[b, h, dst, 0:LANES] = (kv[:, LANES * t:LANES * (t + 1)] * half[hh]).astype(BF16)
            vm_ref[b, h, dst, 0:LANES] = (
                kv[:, 2 * LANES + LANES * t:2 * LANES + LANES * (t + 1)] * half[hh]).astype(BF16)

    def fill_kr(b, dst, kr):
        for h in range(MLA_HEADS):
            kcat_ref[b, h, dst, LANES:2 * LANES] = (
                kr * _lane_mask(MLA_ROPE * h, MLA_ROPE * (h + 1))).astype(BF16)

    def fill_dk(b, dst, dk):
        for h in range(DIFF_HEADS):
            for a in range(2):
                dka_ref[b, 2 * h + a, dst, :] = (
                    dk[:, LANES * h:LANES * (h + 1)] * half[a]).astype(BF16)

    ckv = _rms(head[:, R_KV:R_KR], w["kv_norm"][l:l + 1, :])
    kv = _dot(ckv.astype(BF16), w_ukv[wl])
    for b in range(n_b):
        if state is not None:
            state[0][b, l] = ckv[seq_rows(b)]
        fill_kv(b, new_keys, kv[seq_rows(b)])
        if ctx is not None:
            fill_kv(b, old_keys, _dot(ctx["ckv"][...].astype(BF16), w_ukv[wl]))

    krr = head[:, R_KR:R_KR + LANES]
    krr_pos = krr if rope is None else _rope(krr, rope_m, MLA_ROPE // 4)
    for b in range(n_b):
        if state is not None:
            state[1][b, l] = krr[seq_rows(b)].T[0:MLA_ROPE, :]
        fill_kr(b, new_keys, krr_pos[seq_rows(b)])
        if ctx is not None:
            fill_kr(b, old_keys, jnp.concatenate([ctx["kr"][...]] * (LANES // MLA_ROPE), axis=0).T)

    dq = proj(R_DQ, R_DK) * DIFF_SCALE
    if rope is not None:
        dq = _rope(dq, rope_d, DIFF_QK // 4)
    dqb_ref[...] = dq.astype(BF16)

    dk = proj(R_DK, R_DV)
    dk_pos = dk if rope is None else _rope(dk, rope_d, DIFF_QK // 4)
    for b in range(n_b):
        if state is not None:
            state[2][b, l] = dk[seq_rows(b)].T
        fill_dk(b, new_keys, dk_pos[seq_rows(b)])
        if ctx is not None:
            fill_dk(b, old_keys, ctx["dk"][...].T)

    dv = proj(R_DV, R_Z)
    for b in range(n_b):
        for h in range(DIFF_HEADS):
            dv_h = dv[seq_rows(b), LANES * h:LANES * (h + 1)]
            if state is not None:
                state[3][b, l, pl.ds(h, n_seq, stride=DIFF_HEADS), :] = dv_h
            dvb_ref[b, h, new_keys, 0:LANES] = dv_h.astype(BF16)
            if ctx is not None:
                dvb_ref[b, h, old_keys, 0:LANES] = (
                    ctx["dv"][pl.ds(h, n_ctx, stride=DIFF_HEADS), :].astype(BF16))
            ones = jnp.ones((nk, LANES), BF16)
            dvb_ref[b, h, :, LANES:2 * LANES] = ones
            vm_ref[b, h, :, LANES:2 * LANES] = ones


def _attend_block(l, b, start, tq, nk, w, scr):
    lf = w["lam"][l]
    lam = (jnp.exp(jnp.sum(lf[0:1] * lf[1:2], axis=-1, keepdims=True))
           - jnp.exp(jnp.sum(lf[2:3] * lf[3:4], axis=-1, keepdims=True)) + _lambda_init(l))
    qrows = pl.ds(start, tq)
    return _attend(scr["qb"][qrows, :], scr["dqb"][qrows, :], pl.ds(0, nk),
                   scr["kcat"].at[b], scr["vm"].at[b], scr["dka"].at[b], scr["dvb"].at[b],
                   lam, w["subln"][l:l + 1, :], _lambda_init(l))


def _conv_branch(l, wl, m_rows, n_seq, w, scr):
    conv = _proj(wl, w, scr, R_CB, R_DQ)
    u = conv[:, W_CONV:2 * W_CONV] * conv[:, 2 * W_CONV:3 * W_CONV]
    pos = lax.broadcasted_iota(jnp.int32, (m_rows, 1), 0) % n_seq
    u_prev = jnp.where(pos == 0, 0.0, pltpu.roll(u, 1, 0))
    u_next = jnp.where(pos == n_seq - 1, 0.0, pltpu.roll(u, m_rows - 1, 0))
    cw = w["conv_w"][:, l, :]
    return conv[:, 0:W_CONV] * (u_prev * cw[0:1] + u * cw[1:2] + u_next * cw[2:3])


def _silu_gate(wl, w, scr, c0, c1):
    z = _proj(wl, w, scr, R_Z + c0, R_Z + c1)
    return z * jax.nn.sigmoid(z)


def _trunk_layer(l, x, n_seq, mod, w, scr, state):
    m_rows = x.shape[0]
    o_ref, mix_ref = scr["o"], scr["mix"]
    _layer_front(l, l, x, n_seq, mod, w, scr, None, None, state)
    for b in range(m_rows // n_seq):
        for start in range(b * n_seq, (b + 1) * n_seq, Q_TILE):
            o_mla, o_diff = _attend_block(l, b, start, Q_TILE, n_seq, w, scr)
            o_ref[start:start + Q_TILE, 0:W_MLA] = o_mla
            o_ref[start:start + Q_TILE, W_MLA + W_CONV:D_MIX] = o_diff
    o_ref[:, W_MLA:W_MLA + W_CONV] = _conv_branch(l, l, m_rows, n_seq, w, scr)
    for c0 in range(0, D_MIX, 512):
        mix_ref[:, c0:c0 + 512] = (
            o_ref[:, c0:c0 + 512] * _silu_gate(l, w, scr, c0, c0 + 512)).astype(BF16)
    return x + mod[2] * _dot(mix_ref[...], w["w_out"][l])


_W_NAMES = ("norm_g", "w_in_t", "w_head", "q_norm", "w_uq", "kv_norm", "w_ukv", "conv_w", "lam",
            "subln", "w_out", "final_norm")
_PER_LAYER_BLOCKS = ("w_in_t", "w_head", "w_uq", "w_ukv", "w_out")
_SCR_NAMES = ("hb", "o", "mix", "qb", "dqb", "kcat", "vm", "dka", "dvb")
_CTX_NAMES = ("ckv", "kr", "dk", "dv")
N_ROPE = 4


def _ctx_kernel(x_ref, mod_ref, *refs, n_seq):
    w = dict(zip(_W_NAMES, refs[:len(_W_NAMES)]))
    rest = refs[len(_W_NAMES):]
    y_ref, state = rest[0], rest[1:5]
    scr = dict(zip(_SCR_NAMES, rest[5:]))
    n_b = x_ref.shape[0]
    x = x_ref[...].reshape(n_b * n_seq, D_MODEL)
    for l in range(DEPTH):
        mod = tuple(mod_ref[l, j, 0:1, :] for j in range(3))
        x = _trunk_layer(l, x, n_seq, mod, w, scr, state)
    y_ref[...] = _rms(x, w["final_norm"][...]).reshape(n_b, n_seq, D_MODEL)


def _lat_kernel(x_ref, mod_ref, *refs, n_seq, layer):
    ctx = {k: r.at[0, 0] for k, r in zip(_CTX_NAMES, refs[:4])}
    rope = (refs[4:6], refs[6:8])
    w = dict(zip(_W_NAMES, refs[4 + N_ROPE:4 + N_ROPE + len(_W_NAMES)]))
    rest = refs[4 + N_ROPE + len(_W_NAMES):]
    y_ref = rest[0]
    scr = dict(zip(_SCR_NAMES, rest[1:]))
    cond_row = pl.ds(1 + pl.program_id(0), 1)
    mod = tuple(mod_ref[layer, j, cond_row, :] for j in range(3))
    zg_ref, mix_ref = scr["o"], scr["mix"]
    n_ctx = ctx["ckv"].shape[0]
    conv_cols = slice(W_MLA, W_MLA + W_CONV)

    @pl.when(pl.program_id(1) == 0)
    def _():
        _layer_front(layer, 0, x_ref[0], n_seq, mod, w, scr, rope, ctx, None)
        for c0 in range(0, D_MIX, 512):
            zg_ref[:, c0:c0 + 512] = _silu_gate(0, w, scr, c0, c0 + 512)
        mix_ref[:, conv_cols] = (
            _conv_branch(layer, 0, n_seq, n_seq, w, scr) * zg_ref[:, conv_cols]).astype(BF16)

    start = pl.multiple_of(pl.program_id(1) * LAT_Q_TILE, LAT_Q_TILE)
    qrows = pl.ds(start, LAT_Q_TILE)
    o_mla, o_diff = _attend_block(layer, 0, start, LAT_Q_TILE, n_ctx + n_seq, w, scr)
    mix_ref[qrows, 0:W_MLA] = (o_mla * zg_ref[qrows, 0:W_MLA]).astype(BF16)
    mix_ref[qrows, W_MLA + W_CONV:D_MIX] = (
        o_diff * zg_ref[qrows, W_MLA + W_CONV:D_MIX]).astype(BF16)
    x = x_ref[0, qrows, :] + mod[2] * _dot(mix_ref[qrows, :], w["w_out"][0])
    if layer == DEPTH - 1:
        x = _rms(x, w["final_norm"][...])
    y_ref[0] = x


def _prep_kernel(cctx_ref, c_ref, ada_w_ref, ada_b_ref, w_in_ref, w_out_ref, w_uq_ref, w_ukv_ref,
                 mod_ref, w_in_bf_ref, w_out_bf_ref, w_head_ref, w_uq_p_ref, w_ukv_p_ref):
    n_rows = mod_ref.shape[2]
    pad = jnp.zeros((n_rows - 1 - c_ref.shape[0], D_MODEL), F32)
    c = jnp.concatenate([cctx_ref[...], c_ref[...], pad], axis=0)
    s = (c * jax.nn.sigmoid(c)).astype(BF16)
    mod_ref[0, 0] = (_dot(s, ada_w_ref[0].astype(BF16))
                     + ada_b_ref[pl.ds(pl.program_id(0), 1), :])
    w_in_bf_ref[0] = w_in_ref[0].astype(BF16)

    @pl.when(pl.program_id(1) == 0)
    def _():
        w_out_bf_ref[0] = w_out_ref[0].astype(BF16)
        w_head_ref[0, R_Q:R_KR, :] = w_in_ref[0, R_Q:R_KR, :].astype(BF16)
        kr = w_in_ref[0, R_KR:R_CB, :].astype(BF16)
        for r in range(LANES // MLA_ROPE):
            w_head_ref[0, R_KR + MLA_ROPE * r:R_KR + MLA_ROPE * (r + 1), :] = kr
        for src, dst, first in ((w_uq_ref, w_uq_p_ref, MLA_NOPE), (w_ukv_ref, w_ukv_p_ref, MLA_NOPE)):
            a = src[0]
            per = a.shape[1] // MLA_HEADS
            cols = ([a[:, per * h:per * h + first] for h in range(MLA_HEADS)]
                    + [a[:, per * h + first:per * (h + 1)] for h in range(MLA_HEADS)])
            dst[0] = jnp.concatenate(cols, axis=1).astype(BF16)


def _const_spec(shape, layer=None):
    if layer is None:
        zeros = (0,) * len(shape)
        return pl.BlockSpec(shape, lambda *_: zeros, pipeline_mode=pl.Buffered(1))
    idx = (layer,) + (0,) * (len(shape) - 1)
    return pl.BlockSpec((1,) + tuple(shape[1:]), lambda *_: idx, pipeline_mode=pl.Buffered(1))


def _scratch(n_b, n_seq, nk):
    m_rows = n_b * n_seq
    return [
        pltpu.VMEM((m_rows, D_MODEL), BF16),
        pltpu.VMEM((m_rows, D_MIX), F32),
        pltpu.VMEM((m_rows, D_MIX), BF16),
        pltpu.VMEM((m_rows, 3 * LANES), BF16),
        pltpu.VMEM((m_rows, W_DIFF), BF16),
        pltpu.VMEM((n_b, MLA_HEADS, nk, 2 * LANES), BF16),
        pltpu.VMEM((n_b, MLA_HEADS, nk, 2 * LANES), BF16),
        pltpu.VMEM((n_b, 2 * DIFF_HEADS, nk, LANES), BF16),
        pltpu.VMEM((n_b, DIFF_HEADS, nk, 2 * LANES), BF16),
    ]


def _rope_tables(n, rot_dim):
    half = rot_dim // 2
    inv = ROPE_THETA ** (-(np.arange(0, half, 2, dtype=np.float32) / half))
    pos = np.arange(n)
    ang_r = (pos // GRID_W).astype(np.float32)[:, None] * inv
    ang_c = (pos % GRID_W).astype(np.float32)[:, None] * inv
    ang = np.concatenate([ang_r, ang_r, ang_c, ang_c], axis=-1)
    cos, sin = np.cos(ang), np.sin(ang)
    q = rot_dim // 4
    first = (np.arange(rot_dim) % (2 * q)) < q
    sin_signed = np.where(first, -sin, sin)
    reps = LANES // rot_dim
    return tuple(jnp.asarray(np.tile(t, (1, reps)), dtype=F32) for t in (cos, sin_signed))


def kernel(x_prompt, x_sample, c, cache_mla_ckv, cache_mla_krope, cache_diff_k, cache_diff_v,
           c_ctx, norm_g, ada_w, ada_b, w_in, mla_q_norm, w_uq, mla_kv_norm, w_ukv,
           conv_w, diff_lambda, diff_subln, w_out, final_norm):
    batch, seq, _ = x_prompt.shape
    dec_batch, dec_seq, _ = x_sample.shape
    past = cache_mla_ckv.shape[2]
    cparams = pltpu.CompilerParams(dimension_semantics=("arbitrary",),
                                   vmem_limit_bytes=VMEM_LIMIT_BYTES)

    n_cond = 8
    n_third = 3
    in_rows = D_IN // n_third
    head_rows = R_KR + LANES
    per_layer_spec = lambda a: pl.BlockSpec((1,) + a.shape[1:], lambda l, j: (l, 0, 0))
    mod, w_in_t, w_out_b, w_head, w_uq_p, w_ukv_p = pl.pallas_call(
        _prep_kernel,
        grid=(DEPTH, n_third),
        in_specs=[pl.BlockSpec((1, D_MODEL), lambda l, j: (0, 0)),
                  pl.BlockSpec((dec_batch, D_MODEL), lambda l, j: (0, 0)),
                  pl.BlockSpec((1, D_MODEL, D_MODEL), lambda l, j: (l, 0, j)),
                  pl.BlockSpec((DEPTH, D_MODEL), lambda l, j: (0, j)),
                  pl.BlockSpec((1, in_rows, D_MODEL), lambda l, j: (l, j, 0)),
                  per_layer_spec(w_out), per_layer_spec(w_uq), per_layer_spec(w_ukv)],
        out_specs=[pl.BlockSpec((1, 1, n_cond, D_MODEL), lambda l, j: (l, j, 0, 0)),
                   pl.BlockSpec((1, in_rows, D_MODEL), lambda l, j: (l, j, 0)),
                   per_layer_spec(w_out),
                   pl.BlockSpec((1, head_rows, D_MODEL), lambda l, j: (l, 0, 0)),
                   per_layer_spec(w_uq), per_layer_spec(w_ukv)],
        out_shape=[jax.ShapeDtypeStruct((DEPTH, 3, n_cond, D_MODEL), F32),
                   jax.ShapeDtypeStruct((DEPTH, D_IN, D_MODEL), BF16),
                   jax.ShapeDtypeStruct(w_out.shape, BF16),
                   jax.ShapeDtypeStruct((DEPTH, head_rows, D_MODEL), BF16),
                   jax.ShapeDtypeStruct(w_uq.shape, BF16),
                   jax.ShapeDtypeStruct(w_ukv.shape, BF16)],
        compiler_params=pltpu.CompilerParams(dimension_semantics=("arbitrary", "arbitrary"),
                                             vmem_limit_bytes=VMEM_LIMIT_BYTES),
        name="prep",
    )(c_ctx.reshape(1, D_MODEL), c, ada_w, ada_b, jnp.swapaxes(w_in, 1, 2), w_out, w_uq, w_ukv)

    weights = (norm_g, w_in_t, w_head, mla_q_norm, w_uq_p, mla_kv_norm, w_ukv_p,
               jnp.swapaxes(conv_w, 0, 1), diff_lambda, diff_subln, w_out_b,
               final_norm.reshape(1, D_MODEL))

    tb = 2
    state_shapes = ((seq, MLA_KV_LORA), (MLA_ROPE, seq), (W_DIFF, seq),
                    (DIFF_HEADS * seq, DIFF_V))
    y_prompt, s_ckv, s_kr, s_dk, s_dv = pl.pallas_call(
        functools.partial(_ctx_kernel, n_seq=seq),
        grid=(batch // tb,),
        in_specs=[pl.BlockSpec((tb, seq, D_MODEL), lambda i: (i, 0, 0)),
                  _const_spec(mod.shape)] + [_const_spec(a.shape) for a in weights],
        out_specs=[pl.BlockSpec((tb, seq, D_MODEL), lambda i: (i, 0, 0))]
        + [pl.BlockSpec((tb, DEPTH) + s, lambda i: (i, 0, 0, 0)) for s in state_shapes],
        out_shape=[jax.ShapeDtypeStruct((batch, seq, D_MODEL), F32)]
        + [jax.ShapeDtypeStruct((batch, DEPTH) + s, F32) for s in state_shapes],
        scratch_shapes=_scratch(tb, seq, seq),
        compiler_params=cparams,
        name="ctx",
    )(x_prompt, mod, *weights)

    lat_params = pltpu.CompilerParams(dimension_semantics=("arbitrary", "arbitrary"),
                                      vmem_limit_bytes=VMEM_LIMIT_BYTES)
    tables = _rope_tables(dec_seq, MLA_ROPE) + _rope_tables(dec_seq, DIFF_QK)
    ctx_in = (cache_mla_ckv,
              jnp.swapaxes(cache_mla_krope, 2, 3),
              jnp.transpose(cache_diff_k, (0, 1, 3, 4, 5, 2)).reshape(dec_batch, DEPTH, W_DIFF, past),
              cache_diff_v.reshape(dec_batch, DEPTH, past * DIFF_HEADS, DIFF_V))
    h_lat = x_sample
    for l in range(DEPTH):
        h_lat = pl.pallas_call(
            functools.partial(_lat_kernel, n_seq=dec_seq, layer=l),
            grid=(dec_batch, dec_seq // LAT_Q_TILE),
            in_specs=[pl.BlockSpec((1, dec_seq, D_MODEL), lambda i, j: (i, 0, 0),
                                   pipeline_mode=pl.Buffered(1)),
                      _const_spec(mod.shape)]
            + [pl.BlockSpec((1, 1) + a.shape[2:], lambda i, j, l=l: (i, l, 0, 0))
               for a in ctx_in]
            + [_const_spec(t.shape) for t in tables]
            + [_const_spec(a.shape, layer=l if name in _PER_LAYER_BLOCKS else None)
               for name, a in zip(_W_NAMES, weights)],
            out_specs=pl.BlockSpec((1, LAT_Q_TILE, D_MODEL), lambda i, j: (i, j, 0)),
            out_shape=jax.ShapeDtypeStruct((dec_batch, dec_seq, D_MODEL), F32),
            scratch_shapes=_scratch(1, dec_seq, past + dec_seq),
            compiler_params=lat_params,
            name=f"latent_{l}",
        )(h_lat, mod, *ctx_in, *tables, *weights)

    return (y_prompt, h_lat, s_ckv,
            jnp.swapaxes(s_kr, 2, 3),
            jnp.transpose(s_dk.reshape(batch, DEPTH, DIFF_HEADS, 2, DIFF_QK, seq),
                          (0, 1, 5, 2, 3, 4)),
            s_dv.reshape(batch, DEPTH, seq, DIFF_HEADS, DIFF_V))
```

```python
import functools
import math

import numpy as np
import jax
import jax.numpy as jnp
from jax import lax
from jax.experimental import pallas as pl
from jax.experimental.pallas import tpu as pltpu

D_MODEL = 1024
DEPTH = 2
GRID_W = 64
MLA_HEADS = 4
MLA_NOPE = 64
MLA_ROPE = 32
MLA_V = 64
MLA_Q_LORA = 256
MLA_KV_LORA = 128
W_MLA = MLA_HEADS * MLA_V
W_CONV = 256
DIFF_HEADS = 4
DIFF_QK = 64
DIFF_V = 2 * DIFF_QK
W_DIFF = DIFF_HEADS * DIFF_V
D_MIX = W_MLA + W_CONV + W_DIFF
ROPE_THETA = 10000.0
EPS = 1e-6

LANES = 128
Q_TILE = 256
LAT_Q_TILE = 256
LOG2E = math.log2(math.e)
MLA_SCALE = (MLA_NOPE + MLA_ROPE) ** -0.5 * LOG2E
DIFF_SCALE = DIFF_QK ** -0.5 * LOG2E
VMEM_LIMIT_BYTES = 54 * 1024 * 1024

R_Q = 0
R_KV = R_Q + MLA_Q_LORA
R_KR = R_KV + MLA_KV_LORA
R_CB = R_KR + MLA_ROPE
R_CC = R_CB + W_CONV
R_CX = R_CC + W_CONV
R_DQ = R_CX + W_CONV
R_DK = R_DQ + W_DIFF
R_DV = R_DK + W_DIFF
R_Z = R_DV + W_DIFF
D_IN = R_Z + D_MIX

BF16 = jnp.bfloat16
F32 = jnp.float32


def _lambda_init(layer):
    return 0.8 - 0.6 * math.exp(-0.3 * layer)


def _dot(a, b):
    return jnp.dot(a, b, preferred_element_type=F32)


def _dot_nt(a, b):
    return lax.dot_general(a, b, (((1,), (1,)), ((), ())), preferred_element_type=F32)


def _rms(x, g):
    ms = jnp.mean(x * x, axis=-1, keepdims=True)
    return x * lax.rsqrt(ms + EPS) * g


def _exp_scores(s):
    m = s[:, 0:LANES]
    for t in range(1, s.shape[1] // LANES):
        m = jnp.maximum(m, s[:, LANES * t:LANES * (t + 1)])
    return jnp.exp2(s - jnp.max(m, axis=-1, keepdims=True)).astype(BF16)


def _weighted(e, v_ones):
    r = _dot(e, v_ones)
    return r[:, 0:LANES], 1.0 / r[:, LANES:2 * LANES]


def _lane_mask(lo, hi):
    lane = lax.broadcasted_iota(jnp.int32, (1, LANES), 1)
    return jnp.where((lane >= lo) & (lane < hi), 1.0, 0.0).astype(F32)


def _rope(x, tables, q):
    cos, sin_signed = tables
    lane = lax.broadcasted_iota(jnp.int32, (1, LANES), 1)
    first = (lane & (2 * q - 1)) < q
    tiles = []
    for t in range(x.shape[1] // LANES):
        xt = x[:, LANES * t:LANES * (t + 1)]
        rot = jnp.where(first, pltpu.roll(xt, LANES - q, 1), pltpu.roll(xt, q, 1))
        tiles.append(xt * cos + rot * sin_signed)
    return tiles[0] if len(tiles) == 1 else jnp.concatenate(tiles, axis=1)


def _attend(q, dq, keys, kcat_ref, vm_ref, dka_ref, dvb_ref, lam, subln, lam_init):
    qr = q[:, 2 * LANES:3 * LANES]
    o_mla = []
    for t in range(2):
        qcat = jnp.concatenate([q[:, LANES * t:LANES * (t + 1)], qr], axis=1)
        acc = None
        for hh in range(2):
            h = 2 * t + hh
            o, rinv = _weighted(_exp_scores(_dot_nt(qcat, kcat_ref[h, keys, :])),
                                vm_ref[h, keys, :])
            acc = o * rinv if acc is None else acc + o * rinv
        o_mla.append(acc)
    o_diff = []
    for h in range(DIFF_HEADS):
        qt = dq[:, LANES * h:LANES * (h + 1)]
        o0, rinv0 = _weighted(_exp_scores(_dot_nt(qt, dka_ref[2 * h, keys, :])),
                              dvb_ref[h, keys, :])
        o1, rinv1 = _weighted(_exp_scores(_dot_nt(qt, dka_ref[2 * h + 1, keys, :])),
                              dvb_ref[h, keys, :])
        o = o0 * rinv0 - o1 * (lam * rinv1)
        o_diff.append(_rms(o, subln) * (1.0 - lam_init))
    return jnp.concatenate(o_mla, axis=1), jnp.concatenate(o_diff, axis=1)


def _proj(wl, w, scr, r0, r1):
    return _dot_nt(scr["hb"][...], w["w_in_t"][wl, r0 - R_CB:r1 - R_CB, :])


def _layer_front(l, wl, x, n_seq, mod, w, scr, rope, ctx, state):
    m_rows = x.shape[0]
    n_b = m_rows // n_seq
    n_ctx = 0 if ctx is None else ctx["ckv"].shape[0]
    nk = n_ctx + n_seq
    shift, scale, _ = mod
    hb_ref = scr["hb"]
    qb_ref, dqb_ref = scr["qb"], scr["dqb"]
    proj = functools.partial(_proj, wl, w, scr)
    kcat_ref, vm_ref, dka_ref, dvb_ref = scr["kcat"], scr["vm"], scr["dka"], scr["dvb"]
    w_ukv = w["w_ukv"]
    half = (_lane_mask(0, LANES // 2), _lane_mask(LANES // 2, LANES))
    new_keys, old_keys = pl.ds(n_ctx, n_seq), pl.ds(0, n_ctx)
    if rope is not None:
        rope_m = tuple(r[...] for r in rope[0])
        rope_d = tuple(r[...] for r in rope[1])

    def seq_rows(b):
        return slice(b * n_seq, (b + 1) * n_seq)

    gs = w["norm_g"][l:l + 1, :] * (1.0 + scale)
    ms = jnp.mean(x * x, axis=-1, keepdims=True)
    hb_ref[...] = (x * lax.rsqrt(ms + EPS) * gs + shift).astype(BF16)

    head = _dot_nt(hb_ref[...], w["w_head"][wl])

    cqn = _rms(head[:, 0:MLA_Q_LORA], w["q_norm"][l:l + 1, :]).astype(BF16)
    q = _dot(cqn, w["w_uq"][wl]) * MLA_SCALE
    if rope is not None:
        q = jnp.concatenate(
            [q[:, 0:2 * LANES], _rope(q[:, 2 * LANES:3 * LANES], rope_m, MLA_ROPE // 4)], axis=1)
    qb_ref[...] = q.astype(BF16)

    def fill_kv(b, dst, kv):
        for h in range(MLA_HEADS):
            t, hh = divmod(h, 2)
            kcat_ref[b, h, dst, 0:LANES] = (kv[:, LANES * t:LANES * (t + 1)] * half[hh]).astype(BF16)
            vm_ref[b, h, dst, 0:LANES] = (
                kv[:, 2 * LANES + LANES * t:2 * LANES + LANES * (t + 1)] * half[hh]).astype(BF16)

    def fill_kr(b, dst, kr):
        for h in range(MLA_HEADS):
            kcat_ref[b, h, dst, LANES:2 * LANES] = (
                kr * _lane_mask(MLA_ROPE * h, MLA_ROPE * (h + 1))).astype(BF16)

    def fill_dk(b, dst, dk):
        for h in range(DIFF_HEADS):
            for a in range(2):
                dka_ref[b, 2 * h + a, dst, :] = (
                    dk[:, LANES * h:LANES * (h + 1)] * half[a]).astype(BF16)

    ckv = _rms(head[:, R_KV:R_KR], w["kv_norm"][l:l + 1, :])
    kv = _dot(ckv.astype(BF16), w_ukv[wl])
    for b in range(n_b):
        if state is not None:
            state[0][b, l] = ckv[seq_rows(b)]
        fill_kv(b, new_keys, kv[seq_rows(b)])
        if ctx is not None:
            fill_kv(b, old_keys, _dot(ctx["ckv"][...].astype(BF16), w_ukv[wl]))

    krr = head[:, R_KR:R_KR + LANES]
    krr_pos = krr if rope is None else _rope(krr, rope_m, MLA_ROPE // 4)
    for b in range(n_b):
        if state is not None:
            state[1][b, l] = krr[seq_rows(b)].T[0:MLA_ROPE, :]
        fill_kr(b, new_keys, krr_pos[seq_rows(b)])
        if ctx is not None:
            fill_kr(b, old_keys, jnp.concatenate([ctx["kr"][...]] * (LANES // MLA_ROPE), axis=0).T)

    dq = proj(R_DQ, R_DK) * DIFF_SCALE
    if rope is not None:
        dq = _rope(dq, rope_d, DIFF_QK // 4)
    dqb_ref[...] = dq.astype(BF16)

    dk = proj(R_DK, R_DV)
    dk_pos = dk if rope is None else _rope(dk, rope_d, DIFF_QK // 4)
    for b in range(n_b):
        if state is not None:
            state[2][b, l] = dk[seq_rows(b)].T
        fill_dk(b, new_keys, dk_pos[seq_rows(b)])
        if ctx is not None:
            fill_dk(b, old_keys, ctx["dk"][...].T)

    dv = proj(R_DV, R_Z)
    for b in range(n_b):
        for h in range(DIFF_HEADS):
            dv_h = dv[seq_rows(b), LANES * h:LANES * (h + 1)]
            if state is not None:
                state[3][b, l, pl.ds(h, n_seq, stride=DIFF_HEADS), :] = dv_h
            dvb_ref[b, h, new_keys, 0:LANES] = dv_h.astype(BF16)
            if ctx is not None:
                dvb_ref[b, h, old_keys, 0:LANES] = (
                    ctx["dv"][pl.ds(h, n_ctx, stride=DIFF_HEADS), :].astype(BF16))
            ones = jnp.ones((nk, LANES), BF16)
            dvb_ref[b, h, :, LANES:2 * LANES] = ones
            vm_ref[b, h, :, LANES:2 * LANES] = ones


def _attend_block(l, b, start, tq, nk, w, scr):
    lf = w["lam"][l]
    lam = (jnp.exp(jnp.sum(lf[0:1] * lf[1:2], axis=-1, keepdims=True))
           - jnp.exp(jnp.sum(lf[2:3] * lf[3:4], axis=-1, keepdims=True)) + _lambda_init(l))
    qrows = pl.ds(start, tq)
    return _attend(scr["qb"][qrows, :], scr["dqb"][qrows, :], pl.ds(0, nk),
                   scr["kcat"].at[b], scr["vm"].at[b], scr["dka"].at[b], scr["dvb"].at[b],
                   lam, w["subln"][l:l + 1, :], _lambda_init(l))


def _conv_branch(l, wl, m_rows, n_seq, w, scr):
    conv = _proj(wl, w, scr, R_CB, R_DQ)
    u = conv[:, W_CONV:2 * W_CONV] * conv[:, 2 * W_CONV:3 * W_CONV]
    pos = lax.broadcasted_iota(jnp.int32, (m_rows, 1), 0) % n_seq
    u_prev = jnp.where(pos == 0, 0.0, pltpu.roll(u, 1, 0))
    u_next = jnp.where(pos == n_seq - 1, 0.0, pltpu.roll(u, m_rows - 1, 0))
    cw = w["conv_w"][:, l, :]
    return conv[:, 0:W_CONV] * (u_prev * cw[0:1] + u * cw[1:2] + u_next * cw[2:3])


def _silu_gate(wl, w, scr, c0, c1):
    z = _proj(wl, w, scr, R_Z + c0, R_Z + c1)
    return z * jax.nn.sigmoid(z)


def _trunk_layer(l, x, n_seq, mod, w, scr, state):
    m_rows = x.shape[0]
    o_ref, mix_ref = scr["o"], scr["mix"]
    _layer_front(l, l, x, n_seq, mod, w, scr, None, None, state)
    for b in range(m_rows // n_seq):
        for start in range(b * n_seq, (b + 1) * n_seq, Q_TILE):
            o_mla, o_diff = _attend_block(l, b, start, Q_TILE, n_seq, w, scr)
            o_ref[start:start + Q_TILE, 0:W_MLA] = o_mla
            o_ref[start:start + Q_TILE, W_MLA + W_CONV:D_MIX] = o_diff
    o_ref[:, W_MLA:W_MLA + W_CONV] = _conv_branch(l, l, m_rows, n_seq, w, scr)
    for c0 in range(0, D_MIX, 512):
        mix_ref[:, c0:c0 + 512] = (
            o_ref[:, c0:c0 + 512] * _silu_gate(l, w, scr, c0, c0 + 512)).astype(BF16)
    return x + mod[2] * _dot(mix_ref[...], w["w_out"][l])


_W_NAMES = ("norm_g", "w_in_t", "w_head", "q_norm", "w_uq", "kv_norm", "w_ukv", "conv_w", "lam",
            "subln", "w_out", "final_norm")
_PER_LAYER_BLOCKS = ("w_in_t", "w_head", "w_uq", "w_ukv", "w_out")
_SCR_NAMES = ("hb", "o", "mix", "qb", "dqb", "kcat", "vm", "dka", "dvb")
_CTX_NAMES = ("ckv", "kr", "dk", "dv")
N_ROPE = 4


_BIG_NAMES = ("w_in_t", "w_head", "w_out")
_SMALL_NAMES = tuple(n for n in _W_NAMES if n not in _BIG_NAMES)
W_IN_CHUNK = 208
STAGE_ROWS = 256
assert D_IN % W_IN_CHUNK == 0 and R_CB % W_IN_CHUNK == 0 and R_CB - R_KR <= W_IN_CHUNK


def _stage_weights(w_in_hbm, w_out_hbm, big, slots, sem):
    jobs = []
    for l in range(DEPTH):
        jobs += [(w_in_hbm, big["w_in_t"], l, r, W_IN_CHUNK) for r in range(0, D_IN, W_IN_CHUNK)]
        jobs += [(w_out_hbm, big["w_out"], l, r, STAGE_ROWS) for r in range(0, D_MIX, STAGE_ROWS)]
    n_slots = len(slots)

    def copy(k):
        src, _, l, r, n = jobs[k]
        return pltpu.make_async_copy(src.at[l, pl.ds(r, n)], slots[k % n_slots].at[pl.ds(0, n)],
                                     sem.at[k % n_slots])

    for k in range(n_slots):
        copy(k).start()
    head = big["w_head"]
    for k, (src, dst, l, r, n) in enumerate(jobs):
        copy(k).wait()
        slot = slots[k % n_slots]
        if src is w_out_hbm:
            dst[l, r:r + n, :] = slot[0:n, :].astype(BF16)
        elif r >= R_CB:
            dst[l, r - R_CB:r - R_CB + n, :] = slot[0:n, :].astype(BF16)
        else:
            n_main = min(r + n, R_KR) - r
            head[l, r:r + n_main, :] = slot[0:n_main, :].astype(BF16)
            if r + n > R_KR:
                k_rope = slot[R_KR - r:R_CB - r, :].astype(BF16)
                for i in range(LANES // MLA_ROPE):
                    head[l, R_KR + MLA_ROPE * i:R_KR + MLA_ROPE * (i + 1), :] = k_rope
        if k + n_slots < len(jobs):
            copy(k + n_slots).start()


def _ctx_kernel(x_ref, mod_ref, *refs, n_seq):
    n_small, n_big, n_scr = len(_SMALL_NAMES), len(_BIG_NAMES), len(_SCR_NAMES)
    w = dict(zip(_SMALL_NAMES, refs[:n_small]))
    w_in_hbm, w_out_hbm = refs[n_small:n_small + 2]
    rest = refs[n_small + 2:]
    y_ref, state = rest[0], rest[1:5]
    big_hbm = rest[5:5 + n_big]
    scr = dict(zip(_SCR_NAMES, rest[5 + n_big:5 + n_big + n_scr]))
    big = dict(zip(_BIG_NAMES, rest[5 + n_big + n_scr:5 + 2 * n_big + n_scr]))
    in_sem, out_sem = rest[5 + 2 * n_big + n_scr:]
    w.update(big)

    def hand_off(k):
        return pltpu.make_async_copy(big[_BIG_NAMES[k]], big_hbm[k], out_sem.at[k])

    first_step = pl.program_id(0) == 0

    @pl.when(first_step)
    def _():
        slots = [scr["o"].at[pl.ds(r, STAGE_ROWS)] for r in range(0, scr["o"].shape[0], STAGE_ROWS)]
        slots += [y_ref.at[b] for b in range(y_ref.shape[0])]
        _stage_weights(w_in_hbm, w_out_hbm, big, slots, in_sem)
        for k in range(n_big):
            hand_off(k).start()

    n_b = x_ref.shape[0]
    x = x_ref[...].reshape(n_b * n_seq, D_MODEL)
    for l in range(DEPTH):
        mod = tuple(mod_ref[l, j, 0:1, :] for j in range(3))
        x = _trunk_layer(l, x, n_seq, mod, w, scr, state)
    y_ref[...] = _rms(x, w["final_norm"][...]).reshape(n_b, n_seq, D_MODEL)

    @pl.when(first_step)
    def _():
        for k in range(n_big):
            hand_off(k).wait()


def _lat_kernel(x_ref, mod_ref, *refs, n_seq, layer):
    ctx = {k: r.at[0, 0] for k, r in zip(_CTX_NAMES, refs[:4])}
    rope = (refs[4:6], refs[6:8])
    w = dict(zip(_W_NAMES, refs[4 + N_ROPE:4 + N_ROPE + len(_W_NAMES)]))
    rest = refs[4 + N_ROPE + len(_W_NAMES):]
    y_ref = rest[0]
    scr = dict(zip(_SCR_NAMES, rest[1:]))
    cond_row = pl.ds(1 + pl.program_id(0), 1)
    mod = tuple(mod_ref[layer, j, cond_row, :] for j in range(3))
    zg_ref, mix_ref = scr["o"], scr["mix"]
    n_ctx = ctx["ckv"].shape[0]
    conv_cols = slice(W_MLA, W_MLA + W_CONV)

    @pl.when(pl.program_id(1) == 0)
    def _():
        _layer_front(layer, 0, x_ref[0], n_seq, mod, w, scr, rope, ctx, None)
        for c0 in range(0, D_MIX, 512):
            zg_ref[:, c0:c0 + 512] = _silu_gate(0, w, scr, c0, c0 + 512)
        mix_ref[:, conv_cols] = (
            _conv_branch(layer, 0, n_seq, n_seq, w, scr) * zg_ref[:, conv_cols]).astype(BF16)

    start = pl.multiple_of(pl.program_id(1) * LAT_Q_TILE, LAT_Q_TILE)
    qrows = pl.ds(start, LAT_Q_TILE)
    o_mla, o_diff = _attend_block(layer, 0, start, LAT_Q_TILE, n_ctx + n_seq, w, scr)
    mix_ref[qrows, 0:W_MLA] = (o_mla * zg_ref[qrows, 0:W_MLA]).astype(BF16)
    mix_ref[qrows, W_MLA + W_CONV:D_MIX] = (
        o_diff * zg_ref[qrows, W_MLA + W_CONV:D_MIX]).astype(BF16)
    x = x_ref[0, qrows, :] + mod[2] * _dot(mix_ref[qrows, :], w["w_out"][0])
    if layer == DEPTH - 1:
        x = _rms(x, w["final_norm"][...])
    y_ref[0] = x


def _prep_kernel(cctx_ref, c_ref, ada_w_ref, ada_b_ref, w_uq_ref, w_ukv_ref,
                 mod_ref, w_uq_p_ref, w_ukv_p_ref):
    n_rows = mod_ref.shape[2]
    pad = jnp.zeros((n_rows - 1 - c_ref.shape[0], D_MODEL), F32)
    c = jnp.concatenate([cctx_ref[...], c_ref[...], pad], axis=0)
    s = (c * jax.nn.sigmoid(c)).astype(BF16)
    mod_ref[0, 0] = (_dot(s, ada_w_ref[0].astype(BF16))
                     + ada_b_ref[pl.ds(pl.program_id(0), 1), :])

    @pl.when(pl.program_id(1) == 0)
    def _():
        for src, dst, first in ((w_uq_ref, w_uq_p_ref, MLA_NOPE), (w_ukv_ref, w_ukv_p_ref, MLA_NOPE)):
            a = src[0]
            per = a.shape[1] // MLA_HEADS
            cols = ([a[:, per * h:per * h + first] for h in range(MLA_HEADS)]
                    + [a[:, per * h + first:per * (h + 1)] for h in range(MLA_HEADS)])
            dst[0] = jnp.concatenate(cols, axis=1).astype(BF16)


def _const_spec(shape, layer=None):
    if layer is None:
        zeros = (0,) * len(shape)
        return pl.BlockSpec(shape, lambda *_: zeros, pipeline_mode=pl.Buffered(1))
    idx = (layer,) + (0,) * (len(shape) - 1)
    return pl.BlockSpec((1,) + tuple(shape[1:]), lambda *_: idx, pipeline_mode=pl.Buffered(1))


def _scratch(n_b, n_seq, nk):
    m_rows = n_b * n_seq
    return [
        pltpu.VMEM((m_rows, D_MODEL), BF16),
        pltpu.VMEM((m_rows, D_MIX), F32),
        pltpu.VMEM((m_rows, D_MIX), BF16),
        pltpu.VMEM((m_rows, 3 * LANES), BF16),
        pltpu.VMEM((m_rows, W_DIFF), BF16),
        pltpu.VMEM((n_b, MLA_HEADS, nk, 2 * LANES), BF16),
        pltpu.VMEM((n_b, MLA_HEADS, nk, 2 * LANES), BF16),
        pltpu.VMEM((n_b, 2 * DIFF_HEADS, nk, LANES), BF16),
        pltpu.VMEM((n_b, DIFF_HEADS, nk, 2 * LANES), BF16),
    ]


def _rope_tables(n, rot_dim):
    half = rot_dim // 2
    inv = ROPE_THETA ** (-(np.arange(0, half, 2, dtype=np.float32) / half))
    pos = np.arange(n)
    ang_r = (pos // GRID_W).astype(np.float32)[:, None] * inv
    ang_c = (pos % GRID_W).astype(np.float32)[:, None] * inv
    ang = np.concatenate([ang_r, ang_r, ang_c, ang_c], axis=-1)
    cos, sin = np.cos(ang), np.sin(ang)
    q = rot_dim // 4
    first = (np.arange(rot_dim) % (2 * q)) < q
    sin_signed = np.where(first, -sin, sin)
    reps = LANES // rot_dim
    return tuple(jnp.asarray(np.tile(t, (1, reps)), dtype=F32) for t in (cos, sin_signed))


def kernel(x_prompt, x_sample, c, cache_mla_ckv, cache_mla_krope, cache_diff_k, cache_diff_v,
           c_ctx, norm_g, ada_w, ada_b, w_in, mla_q_norm, w_uq, mla_kv_norm, w_ukv,
           conv_w, diff_lambda, diff_subln, w_out, final_norm):
    batch, seq, _ = x_prompt.shape
    dec_batch, dec_seq, _ = x_sample.shape
    past = cache_mla_ckv.shape[2]
    cparams = pltpu.CompilerParams(dimension_semantics=("arbitrary",),
                                   vmem_limit_bytes=VMEM_LIMIT_BYTES)

    n_cond = 8
    per_layer_spec = lambda a: pl.BlockSpec((1,) + a.shape[1:], lambda l, j: (l, 0, 0))
    mod, w_uq_p, w_ukv_p = pl.pallas_call(
        _prep_kernel,
        grid=(DEPTH, 3),
        in_specs=[pl.BlockSpec((1, D_MODEL), lambda l, j: (0, 0)),
                  pl.BlockSpec((dec_batch, D_MODEL), lambda l, j: (0, 0)),
                  pl.BlockSpec((1, D_MODEL, D_MODEL), lambda l, j: (l, 0, j)),
                  pl.BlockSpec((DEPTH, D_MODEL), lambda l, j: (0, j)),
                  per_layer_spec(w_uq), per_layer_spec(w_ukv)],
        out_specs=[pl.BlockSpec((1, 1, n_cond, D_MODEL), lambda l, j: (l, j, 0, 0)),
                   per_layer_spec(w_uq), per_layer_spec(w_ukv)],
        out_shape=[jax.ShapeDtypeStruct((DEPTH, 3, n_cond, D_MODEL), F32),
                   jax.ShapeDtypeStruct(w_uq.shape, BF16),
                   jax.ShapeDtypeStruct(w_ukv.shape, BF16)],
        compiler_params=pltpu.CompilerParams(dimension_semantics=("arbitrary", "arbitrary")),
        name="prep",
    )(c_ctx.reshape(1, D_MODEL), c, ada_w, ada_b, w_uq, w_ukv)

    tb = 2
    assert seq == STAGE_ROWS and (tb * seq) % STAGE_ROWS == 0
    n_slots = tb * seq // STAGE_ROWS + tb
    small = dict(zip(_SMALL_NAMES, (
        norm_g, mla_q_norm, w_uq_p, mla_kv_norm, w_ukv_p, jnp.swapaxes(conv_w, 0, 1), diff_lambda,
        diff_subln, final_norm.reshape(1, D_MODEL))))
    big_shapes = dict(zip(_BIG_NAMES, ((DEPTH, D_IN - R_CB, D_MODEL), (DEPTH, R_KR + LANES, D_MODEL),
                                       (DEPTH, D_MIX, D_MODEL))))
    hbm_spec = pl.BlockSpec(memory_space=pl.ANY)
    state_shapes = ((seq, MLA_KV_LORA), (MLA_ROPE, seq), (W_DIFF, seq),
                    (DIFF_HEADS * seq, DIFF_V))
    y_prompt, s_ckv, s_kr, s_dk, s_dv, w_in_t, w_head, w_out_b = pl.pallas_call(
        functools.partial(_ctx_kernel, n_seq=seq),
        grid=(batch // tb,),
        in_specs=[pl.BlockSpec((tb, seq, D_MODEL), lambda i: (i, 0, 0)), _const_spec(mod.shape)]
        + [_const_spec(a.shape) for a in small.values()] + [hbm_spec, hbm_spec],
        out_specs=[pl.BlockSpec((tb, seq, D_MODEL), lambda i: (i, 0, 0))]
        + [pl.BlockSpec((tb, DEPTH) + s, lambda i: (i, 0, 0, 0)) for s in state_shapes]
        + [hbm_spec] * len(_BIG_NAMES),
        out_shape=[jax.ShapeDtypeStruct((batch, seq, D_MODEL), F32)]
        + [jax.ShapeDtypeStruct((batch, DEPTH) + s, F32) for s in state_shapes]
        + [jax.ShapeDtypeStruct(big_shapes[n], BF16) for n in _BIG_NAMES],
        scratch_shapes=_scratch(tb, seq, seq)
        + [pltpu.VMEM(big_shapes[n], BF16) for n in _BIG_NAMES]
        + [pltpu.SemaphoreType.DMA((n_slots,)), pltpu.SemaphoreType.DMA((len(_BIG_NAMES),))],
        compiler_params=cparams,
        name="ctx",
    )(x_prompt, mod, *small.values(), jnp.swapaxes(w_in, 1, 2), w_out)

    weights = tuple(dict(small, w_in_t=w_in_t, w_head=w_head, w_out=w_out_b)[n] for n in _W_NAMES)

    lat_params = pltpu.CompilerParams(dimension_semantics=("arbitrary", "arbitrary"),
                                      vmem_limit_bytes=VMEM_LIMIT_BYTES)
    tables = _rope_tables(dec_seq, MLA_ROPE) + _rope_tables(dec_seq, DIFF_QK)
    ctx_in = (cache_mla_ckv,
              jnp.swapaxes(cache_mla_krope, 2, 3),
              jnp.transpose(cache_diff_k, (0, 1, 3, 4, 5, 2)).reshape(dec_batch, DEPTH, W_DIFF, past),
              cache_diff_v.reshape(dec_batch, DEPTH, past * DIFF_HEADS, DIFF_V))
    h_lat = x_sample
    for l in range(DEPTH):
        h_lat = pl.pallas_call(
            functools.partial(_lat_kernel, n_seq=dec_seq, layer=l),
            grid=(dec_batch, dec_seq // LAT_Q_TILE),
            in_specs=[pl.BlockSpec((1, dec_seq, D_MODEL), lambda i, j: (i, 0, 0)),
                      _const_spec(mod.shape)]
            + [pl.BlockSpec((1, 1) + a.shape[2:], lambda i, j, l=l: (i, l, 0, 0))
               for a in ctx_in]
            + [_const_spec(t.shape) for t in tables]
            + [_const_spec(a.shape, layer=l if name in _PER_LAYER_BLOCKS else None)
               for name, a in zip(_W_NAMES, weights)],
            out_specs=pl.BlockSpec((1, LAT_Q_TILE, D_MODEL), lambda i, j: (i, j, 0)),
            out_shape=jax.ShapeDtypeStruct((dec_batch, dec_seq, D_MODEL), F32),
            scratch_shapes=_scratch(1, dec_seq, past + dec_seq),
            compiler_params=lat_params,
            name=f"latent_{l}",
        )(h_lat, mod, *ctx_in, *tables, *weights)

    return (y_prompt, h_lat, s_ckv,
            jnp.swapaxes(s_kr, 2, 3),
            jnp.transpose(s_dk.reshape(batch, DEPTH, DIFF_HEADS, 2, DIFF_QK, seq),
                          (0, 1, 5, 2, 3, 4)),
            s_dv.reshape(batch, DEPTH, seq, DIFF_HEADS, DIFF_V))
```

```python
import functools
import math

import numpy as np
import jax
import jax.numpy as jnp
from jax import lax
from jax.experimental import pallas as pl
from jax.experimental.pallas import tpu as pltpu

D_MODEL = 1024
DEPTH = 2
GRID_W = 64
MLA_HEADS = 4
MLA_NOPE = 64
MLA_ROPE = 32
MLA_V = 64
MLA_Q_LORA = 256
MLA_KV_LORA = 128
W_MLA = MLA_HEADS * MLA_V
W_CONV = 256
DIFF_HEADS = 4
DIFF_QK = 64
DIFF_V = 2 * DIFF_QK
W_DIFF = DIFF_HEADS * DIFF_V
D_MIX = W_MLA + W_CONV + W_DIFF
ROPE_THETA = 10000.0
EPS = 1e-6

LANES = 128
Q_TILE = 256
LAT_Q_TILE = 256
LOG2E = math.log2(math.e)
MLA_SCALE = (MLA_NOPE + MLA_ROPE) ** -0.5 * LOG2E
DIFF_SCALE = DIFF_QK ** -0.5 * LOG2E
VMEM_LIMIT_BYTES = 54 * 1024 * 1024

R_Q = 0
R_KV = R_Q + MLA_Q_LORA
R_KR = R_KV + MLA_KV_LORA
R_CB = R_KR + MLA_ROPE
R_CC = R_CB + W_CONV
R_CX = R_CC + W_CONV
R_DQ = R_CX + W_CONV
R_DK = R_DQ + W_DIFF
R_DV = R_DK + W_DIFF
R_Z = R_DV + W_DIFF
D_IN = R_Z + D_MIX

BF16 = jnp.bfloat16
F32 = jnp.float32


def _lambda_init(layer):
    return 0.8 - 0.6 * math.exp(-0.3 * layer)


def _dot(a, b):
    return jnp.dot(a, b, preferred_element_type=F32)


def _dot_nt(a, b):
    return lax.dot_general(a, b, (((1,), (1,)), ((), ())), preferred_element_type=F32)


def _rms(x, g):
    ms = jnp.mean(x * x, axis=-1, keepdims=True)
    return x * lax.rsqrt(ms + EPS) * g


def _exp_scores(s):
    m = s[:, 0:LANES]
    for t in range(1, s.shape[1] // LANES):
        m = jnp.maximum(m, s[:, LANES * t:LANES * (t + 1)])
    return jnp.exp2(s - jnp.max(m, axis=-1, keepdims=True)).astype(BF16)


def _weighted(e, v_ones):
    r = _dot(e, v_ones)
    return r[:, 0:LANES], 1.0 / r[:, LANES:2 * LANES]


def _lane_mask(lo, hi):
    lane = lax.broadcasted_iota(jnp.int32, (1, LANES), 1)
    return jnp.where((lane >= lo) & (lane < hi), 1.0, 0.0).astype(F32)


def _rope(x, tables, q):
    cos, sin_signed = tables
    lane = lax.broadcasted_iota(jnp.int32, (1, LANES), 1)
    first = (lane & (2 * q - 1)) < q
    tiles = []
    for t in range(x.shape[1] // LANES):
        xt = x[:, LANES * t:LANES * (t + 1)]
        rot = jnp.where(first, pltpu.roll(xt, LANES - q, 1), pltpu.roll(xt, q, 1))
        tiles.append(xt * cos + rot * sin_signed)
    return tiles[0] if len(tiles) == 1 else jnp.concatenate(tiles, axis=1)


def _attend(q, dq, keys, kcat_ref, vm_ref, dka_ref, dvb_ref, lam, subln, lam_init):
    qr = q[:, 2 * LANES:3 * LANES]
    o_mla = []
    for t in range(2):
        qcat = jnp.concatenate([q[:, LANES * t:LANES * (t + 1)], qr], axis=1)
        acc = None
        for hh in range(2):
            h = 2 * t + hh
            o, rinv = _weighted(_exp_scores(_dot_nt(qcat, kcat_ref[h, keys, :])),
                                vm_ref[h, keys, :])
            acc = o * rinv if acc is None else acc + o * rinv
        o_mla.append(acc)
    o_diff = []
    for h in range(DIFF_HEADS):
        qt = dq[:, LANES * h:LANES * (h + 1)]
        o0, rinv0 = _weighted(_exp_scores(_dot_nt(qt, dka_ref[2 * h, keys, :])),
                              dvb_ref[h, keys, :])
        o1, rinv1 = _weighted(_exp_scores(_dot_nt(qt, dka_ref[2 * h + 1, keys, :])),
                              dvb_ref[h, keys, :])
        o = o0 * rinv0 - o1 * (lam * rinv1)
        o_diff.append(_rms(o, subln) * (1.0 - lam_init))
    return jnp.concatenate(o_mla, axis=1), jnp.concatenate(o_diff, axis=1)


def _proj(wl, w, scr, r0, r1):
    return _dot_nt(scr["hb"][...], w["w_in_t"][wl, r0 - R_CB:r1 - R_CB, :])


def _layer_front(l, wl, x, n_seq, mod, w, scr, rope, ctx, state):
    m_rows = x.shape[0]
    n_b = m_rows // n_seq
    n_ctx = 0 if ctx is None else ctx["ckv"].shape[0]
    nk = n_ctx + n_seq
    shift, scale, _ = mod
    hb_ref = scr["hb"]
    qb_ref, dqb_ref = scr["qb"], scr["dqb"]
    proj = functools.partial(_proj, wl, w, scr)
    kcat_ref, vm_ref, dka_ref, dvb_ref = scr["kcat"], scr["vm"], scr["dka"], scr["dvb"]
    w_ukv = w["w_ukv"]
    half = (_lane_mask(0, LANES // 2), _lane_mask(LANES // 2, LANES))
    new_keys, old_keys = pl.ds(n_ctx, n_seq), pl.ds(0, n_ctx)
    if rope is not None:
        rope_m = tuple(r[...] for r in rope[0])
        rope_d = tuple(r[...] for r in rope[1])

    def seq_rows(b):
        return slice(b * n_seq, (b + 1) * n_seq)

    gs = w["norm_g"][l:l + 1, :] * (1.0 + scale)
    ms = jnp.mean(x * x, axis=-1, keepdims=True)
    hb_ref[...] = (x * lax.rsqrt(ms + EPS) * gs + shift).astype(BF16)

    head = _dot_nt(hb_ref[...], w["w_head"][wl])

    cqn = _rms(head[:, 0:MLA_Q_LORA], w["q_norm"][l:l + 1, :]).astype(BF16)
    q = _dot(cqn, w["w_uq"][wl]) * MLA_SCALE
    if rope is not None:
        q = jnp.concatenate(
            [q[:, 0:2 * LANES], _rope(q[:, 2 * LANES:3 * LANES], rope_m, MLA_ROPE // 4)], axis=1)
    qb_ref[...] = q.astype(BF16)

    def fill_kv(b, dst, kv):
        for h in range(MLA_HEADS):
            t, hh = divmod(h, 2)
            kcat_ref[b, h, dst, 0:LANES] = (kv[:, LANES * t:LANES * (t + 1)] * half[hh]).astype(BF16)
            vm_ref[b, h, dst, 0:LANES] = (
                kv[:, 2 * LANES + LANES * t:2 * LANES + LANES * (t + 1)] * half[hh]).astype(BF16)

    def fill_kr(b, dst, kr):
        for h in range(MLA_HEADS):
            kcat_ref[b, h, dst, LANES:2 * LANES] = (
                kr * _lane_mask(MLA_ROPE * h, MLA_ROPE * (h + 1))).astype(BF16)

    def fill_dk(b, dst, dk):
        for h in range(DIFF_HEADS):
            for a in range(2):
                dka_ref[b, 2 * h + a, dst, :] = (
                    dk[:, LANES * h:LANES * (h + 1)] * half[a]).astype(BF16)

    ckv = _rms(head[:, R_KV:R_KR], w["kv_norm"][l:l + 1, :])
    kv = _dot(ckv.astype(BF16), w_ukv[wl])
    for b in range(n_b):
        if state is not None:
            state[0][b, l] = ckv[seq_rows(b)]
        fill_kv(b, new_keys, kv[seq_rows(b)])
        if ctx is not None:
            fill_kv(b, old_keys, _dot(ctx["ckv"][...].astype(BF16), w_ukv[wl]))

    krr = head[:, R_KR:R_KR + LANES]
    krr_pos = krr if rope is None else _rope(krr, rope_m, MLA_ROPE // 4)
    for b in range(n_b):
        if state is not None:
            state[1][b, l] = krr[seq_rows(b)].T[0:MLA_ROPE, :]
        fill_kr(b, new_keys, krr_pos[seq_rows(b)])
        if ctx is not None:
            fill_kr(b, old_keys, jnp.concatenate([ctx["kr"][...]] * (LANES // MLA_ROPE), axis=0).T)

    dq = proj(R_DQ, R_DK) * DIFF_SCALE
    if rope is not None:
        dq = _rope(dq, rope_d, DIFF_QK // 4)
    dqb_ref[...] = dq.astype(BF16)

    dk = proj(R_DK, R_DV)
    dk_pos = dk if rope is None else _rope(dk, rope_d, DIFF_QK // 4)
    for b in range(n_b):
        if state is not None:
            state[2][b, l] = dk[seq_rows(b)].T
        fill_dk(b, new_keys, dk_pos[seq_rows(b)])
        if ctx is not None:
            fill_dk(b, old_keys, ctx["dk"][...].T)

    dv = proj(R_DV, R_Z)
    for b in range(n_b):
        for h in range(DIFF_HEADS):
            dv_h = dv[seq_rows(b), LANES * h:LANES * (h + 1)]
            if state is not None:
                state[3][b, l, pl.ds(h, n_seq, stride=DIFF_HEADS), :] = dv_h
            dvb_ref[b, h, new_keys, 0:LANES] = dv_h.astype(BF16)
            if ctx is not None:
                dvb_ref[b, h, old_keys, 0:LANES] = (
                    ctx["dv"][pl.ds(h, n_ctx, stride=DIFF_HEADS), :].astype(BF16))
            ones = jnp.ones((nk, LANES), BF16)
            dvb_ref[b, h, :, LANES:2 * LANES] = ones
            vm_ref[b, h, :, LANES:2 * LANES] = ones


def _attend_block(l, b, start, tq, nk, w, scr):
    lf = w["lam"][l]
    lam = (jnp.exp(jnp.sum(lf[0:1] * lf[1:2], axis=-1, keepdims=True))
           - jnp.exp(jnp.sum(lf[2:3] * lf[3:4], axis=-1, keepdims=True)) + _lambda_init(l))
    qrows = pl.ds(start, tq)
    return _attend(scr["qb"][qrows, :], scr["dqb"][qrows, :], pl.ds(0, nk),
                   scr["kcat"].at[b], scr["vm"].at[b], scr["dka"].at[b], scr["dvb"].at[b],
                   lam, w["subln"][l:l + 1, :], _lambda_init(l))


def _conv_branch(l, wl, m_rows, n_seq, w, scr):
    conv = _proj(wl, w, scr, R_CB, R_DQ)
    u = conv[:, W_CONV:2 * W_CONV] * conv[:, 2 * W_CONV:3 * W_CONV]
    pos = lax.broadcasted_iota(jnp.int32, (m_rows, 1), 0) % n_seq
    u_prev = jnp.where(pos == 0, 0.0, pltpu.roll(u, 1, 0))
    u_next = jnp.where(pos == n_seq - 1, 0.0, pltpu.roll(u, m_rows - 1, 0))
    cw = w["conv_w"][:, l, :]
    return conv[:, 0:W_CONV] * (u_prev * cw[0:1] + u * cw[1:2] + u_next * cw[2:3])


def _silu_gate(wl, w, scr, c0, c1):
    z = _proj(wl, w, scr, R_Z + c0, R_Z + c1)
    return z * jax.nn.sigmoid(z)


def _trunk_layer(l, x, n_seq, mod, w, scr, state):
    m_rows = x.shape[0]
    o_ref, mix_ref = scr["o"], scr["mix"]
    _layer_front(l, l, x, n_seq, mod, w, scr, None, None, state)
    for b in range(m_rows // n_seq):
        for start in range(b * n_seq, (b + 1) * n_seq, Q_TILE):
            o_mla, o_diff = _attend_block(l, b, start, Q_TILE, n_seq, w, scr)
            o_ref[start:start + Q_TILE, 0:W_MLA] = o_mla
            o_ref[start:start + Q_TILE, W_MLA + W_CONV:D_MIX] = o_diff
    o_ref[:, W_MLA:W_MLA + W_CONV] = _conv_branch(l, l, m_rows, n_seq, w, scr)
    for c0 in range(0, D_MIX, 512):
        mix_ref[:, c0:c0 + 512] = (
            o_ref[:, c0:c0 + 512] * _silu_gate(l, w, scr, c0, c0 + 512)).astype(BF16)
    return x + mod[2] * _dot(mix_ref[...], w["w_out"][l])


_W_NAMES = ("norm_g", "w_in_t", "w_head", "q_norm", "w_uq", "kv_norm", "w_ukv", "conv_w", "lam",
            "subln", "w_out", "final_norm")
_PER_LAYER_BLOCKS = ("w_in_t", "w_head", "w_uq", "w_ukv", "w_out")
_SCR_NAMES = ("hb", "o", "mix", "qb", "dqb", "kcat", "vm", "dka", "dvb")
_CTX_NAMES = ("ckv", "kr", "dk", "dv")
N_ROPE = 4


_BIG_NAMES = ("w_in_t", "w_head", "w_out")
_SMALL_NAMES = tuple(n for n in _W_NAMES if n not in _BIG_NAMES)
W_IN_CHUNK = 208
STAGE_ROWS = 256
assert D_IN % W_IN_CHUNK == 0 and R_CB % W_IN_CHUNK == 0 and R_CB - R_KR <= W_IN_CHUNK


def _stage_weights(w_in_hbm, w_out_hbm, big, slots, sem):
    jobs = []
    for l in range(DEPTH):
        jobs += [(w_in_hbm, big["w_in_t"], l, r, W_IN_CHUNK) for r in range(0, D_IN, W_IN_CHUNK)]
        jobs += [(w_out_hbm, big["w_out"], l, r, STAGE_ROWS) for r in range(0, D_MIX, STAGE_ROWS)]
    n_slots = len(slots)

    def copy(k):
        src, _, l, r, n = jobs[k]
        return pltpu.make_async_copy(src.at[l, pl.ds(r, n)], slots[k % n_slots].at[pl.ds(0, n)],
                                     sem.at[k % n_slots])

    for k in range(n_slots):
        copy(k).start()
    head = big["w_head"]
    for k, (src, dst, l, r, n) in enumerate(jobs):
        copy(k).wait()
        slot = slots[k % n_slots]
        if src is w_out_hbm:
            dst[l, r:r + n, :] = slot[0:n, :].astype(BF16)
        elif r >= R_CB:
            dst[l, r - R_CB:r - R_CB + n, :] = slot[0:n, :].astype(BF16)
        else:
            n_main = min(r + n, R_KR) - r
            head[l, r:r + n_main, :] = slot[0:n_main, :].astype(BF16)
            if r + n > R_KR:
                k_rope = slot[R_KR - r:R_CB - r, :].astype(BF16)
                for i in range(LANES // MLA_ROPE):
                    head[l, R_KR + MLA_ROPE * i:R_KR + MLA_ROPE * (i + 1), :] = k_rope
        if k + n_slots < len(jobs):
            copy(k + n_slots).start()


def _ctx_kernel(x_ref, mod_ref, *refs, n_seq):
    n_small, n_big, n_scr = len(_SMALL_NAMES), len(_BIG_NAMES), len(_SCR_NAMES)
    w = dict(zip(_SMALL_NAMES, refs[:n_small]))
    w_in_hbm, w_out_hbm = refs[n_small:n_small + 2]
    rest = refs[n_small + 2:]
    y_ref, state = rest[0], rest[1:5]
    big_hbm = rest[5:5 + n_big]
    scr = dict(zip(_SCR_NAMES, rest[5 + n_big:5 + n_big + n_scr]))
    big = dict(zip(_BIG_NAMES, rest[5 + n_big + n_scr:5 + 2 * n_big + n_scr]))
    in_sem, out_sem = rest[5 + 2 * n_big + n_scr:]
    w.update(big)

    def hand_off(k):
        return pltpu.make_async_copy(big[_BIG_NAMES[k]], big_hbm[k], out_sem.at[k])

    first_step = pl.program_id(0) == 0

    @pl.when(first_step)
    def _():
        slots = [scr["o"].at[pl.ds(r, STAGE_ROWS)] for r in range(0, scr["o"].shape[0], STAGE_ROWS)]
        slots += [y_ref.at[b] for b in range(y_ref.shape[0])]
        _stage_weights(w_in_hbm, w_out_hbm, big, slots, in_sem)
        for k in range(n_big):
            hand_off(k).start()

    n_b = x_ref.shape[0]
    x = x_ref[...].reshape(n_b * n_seq, D_MODEL)
    for l in range(DEPTH):
        mod = tuple(mod_ref[l, j, 0:1, :] for j in range(3))
        x = _trunk_layer(l, x, n_seq, mod, w, scr, state)
    y_ref[...] = _rms(x, w["final_norm"][...]).reshape(n_b, n_seq, D_MODEL)

    @pl.when(first_step)
    def _():
        for k in range(n_big):
            hand_off(k).wait()


def _lat_kernel(x_ref, mod_ref, *refs, n_seq, layer):
    ctx = {k: r.at[0, 0] for k, r in zip(_CTX_NAMES, refs[:4])}
    rope = (refs[4:6], refs[6:8])
    w = dict(zip(_W_NAMES, refs[4 + N_ROPE:4 + N_ROPE + len(_W_NAMES)]))
    rest = refs[4 + N_ROPE + len(_W_NAMES):]
    y_ref = rest[0]
    scr = dict(zip(_SCR_NAMES, rest[1:]))
    cond_row = pl.ds(1 + pl.program_id(0), 1)
    mod = tuple(mod_ref[layer, j, cond_row, :] for j in range(3))
    zg_ref, mix_ref = scr["o"], scr["mix"]
    n_ctx = ctx["ckv"].shape[0]
    conv_cols = slice(W_MLA, W_MLA + W_CONV)

    @pl.when(pl.program_id(1) == 0)
    def _():
        _layer_front(layer, 0, x_ref[0], n_seq, mod, w, scr, rope, ctx, None)
        for c0 in range(0, D_MIX, 512):
            zg_ref[:, c0:c0 + 512] = _silu_gate(0, w, scr, c0, c0 + 512)
        mix_ref[:, conv_cols] = (
            _conv_branch(layer, 0, n_seq, n_seq, w, scr) * zg_ref[:, conv_cols]).astype(BF16)

    start = pl.multiple_of(pl.program_id(1) * LAT_Q_TILE, LAT_Q_TILE)
    qrows = pl.ds(start, LAT_Q_TILE)
    o_mla, o_diff = _attend_block(layer, 0, start, LAT_Q_TILE, n_ctx + n_seq, w, scr)
    mix_ref[qrows, 0:W_MLA] = (o_mla * zg_ref[qrows, 0:W_MLA]).astype(BF16)
    mix_ref[qrows, W_MLA + W_CONV:D_MIX] = (
        o_diff * zg_ref[qrows, W_MLA + W_CONV:D_MIX]).astype(BF16)
    x = x_ref[0, qrows, :] + mod[2] * _dot(mix_ref[qrows, :], w["w_out"][0])
    if layer == DEPTH - 1:
        x = _rms(x, w["final_norm"][...])
    y_ref[0] = x


def _prep_kernel(cctx_ref, c_ref, ada_w0_ref, ada_w1_ref, ada_b_ref, w_uq_ref, w_ukv_ref,
                 mod_ref, w_uq_p_ref, w_ukv_p_ref):
    n_rows = mod_ref.shape[2]
    pad = jnp.zeros((n_rows - 1 - c_ref.shape[0], D_MODEL), F32)
    c = jnp.concatenate([cctx_ref[...], c_ref[...], pad], axis=0)
    s = (c * jax.nn.sigmoid(c)).astype(BF16)
    bias = ada_b_ref[pl.ds(pl.program_id(0), 1), :]
    half = D_MODEL // 2
    mod_ref[0, 0, :, 0:half] = _dot(s, ada_w0_ref[0].astype(BF16)) + bias[:, 0:half]
    mod_ref[0, 0, :, half:D_MODEL] = _dot(s, ada_w1_ref[0].astype(BF16)) + bias[:, half:D_MODEL]

    @pl.when(pl.program_id(1) == 0)
    def _():
        for src, dst, first in ((w_uq_ref, w_uq_p_ref, MLA_NOPE), (w_ukv_ref, w_ukv_p_ref, MLA_NOPE)):
            a = src[0]
            per = a.shape[1] // MLA_HEADS
            cols = ([a[:, per * h:per * h + first] for h in range(MLA_HEADS)]
                    + [a[:, per * h + first:per * (h + 1)] for h in range(MLA_HEADS)])
            dst[0] = jnp.concatenate(cols, axis=1).astype(BF16)


def _const_spec(shape, layer=None):
    if layer is None:
        zeros = (0,) * len(shape)
        return pl.BlockSpec(shape, lambda *_: zeros, pipeline_mode=pl.Buffered(1))
    idx = (layer,) + (0,) * (len(shape) - 1)
    return pl.BlockSpec((1,) + tuple(shape[1:]), lambda *_: idx, pipeline_mode=pl.Buffered(1))


def _scratch(n_b, n_seq, nk):
    m_rows = n_b * n_seq
    return [
        pltpu.VMEM((m_rows, D_MODEL), BF16),
        pltpu.VMEM((m_rows, D_MIX), F32),
        pltpu.VMEM((m_rows, D_MIX), BF16),
        pltpu.VMEM((m_rows, 3 * LANES), BF16),
        pltpu.VMEM((m_rows, W_DIFF), BF16),
        pltpu.VMEM((n_b, MLA_HEADS, nk, 2 * LANES), BF16),
        pltpu.VMEM((n_b, MLA_HEADS, nk, 2 * LANES), BF16),
        pltpu.VMEM((n_b, 2 * DIFF_HEADS, nk, LANES), BF16),
        pltpu.VMEM((n_b, DIFF_HEADS, nk, 2 * LANES), BF16),
    ]


def _rope_tables(n, rot_dim):
    half = rot_dim // 2
    inv = ROPE_THETA ** (-(np.arange(0, half, 2, dtype=np.float32) / half))
    pos = np.arange(n)
    ang_r = (pos // GRID_W).astype(np.float32)[:, None] * inv
    ang_c = (pos % GRID_W).astype(np.float32)[:, None] * inv
    ang = np.concatenate([ang_r, ang_r, ang_c, ang_c], axis=-1)
    cos, sin = np.cos(ang), np.sin(ang)
    q = rot_dim // 4
    first = (np.arange(rot_dim) % (2 * q)) < q
    sin_signed = np.where(first, -sin, sin)
    reps = LANES // rot_dim
    return tuple(jnp.asarray(np.tile(t, (1, reps)), dtype=F32) for t in (cos, sin_signed))


def kernel(x_prompt, x_sample, c, cache_mla_ckv, cache_mla_krope, cache_diff_k, cache_diff_v,
           c_ctx, norm_g, ada_w, ada_b, w_in, mla_q_norm, w_uq, mla_kv_norm, w_ukv,
           conv_w, diff_lambda, diff_subln, w_out, final_norm):
    batch, seq, _ = x_prompt.shape
    dec_batch, dec_seq, _ = x_sample.shape
    past = cache_mla_ckv.shape[2]
    cparams = pltpu.CompilerParams(dimension_semantics=("arbitrary",),
                                   vmem_limit_bytes=VMEM_LIMIT_BYTES)

    n_cond = 8
    per_layer_spec = lambda a: pl.BlockSpec((1,) + a.shape[1:], lambda l, j: (l, 0, 0))
    mod, w_uq_p, w_ukv_p = pl.pallas_call(
        _prep_kernel,
        grid=(DEPTH, 3),
        in_specs=[pl.BlockSpec((1, D_MODEL), lambda l, j: (0, 0)),
                  pl.BlockSpec((dec_batch, D_MODEL), lambda l, j: (0, 0)),
                  pl.BlockSpec((1, D_MODEL, D_MODEL // 2), lambda l, j: (l, 0, 2 * j)),
                  pl.BlockSpec((1, D_MODEL, D_MODEL // 2), lambda l, j: (l, 0, 2 * j + 1)),
                  pl.BlockSpec((DEPTH, D_MODEL), lambda l, j: (0, j)),
                  per_layer_spec(w_uq), per_layer_spec(w_ukv)],
        out_specs=[pl.BlockSpec((1, 1, n_cond, D_MODEL), lambda l, j: (l, j, 0, 0)),
                   per_layer_spec(w_uq), per_layer_spec(w_ukv)],
        out_shape=[jax.ShapeDtypeStruct((DEPTH, 3, n_cond, D_MODEL), F32),
                   jax.ShapeDtypeStruct(w_uq.shape, BF16),
                   jax.ShapeDtypeStruct(w_ukv.shape, BF16)],
        compiler_params=pltpu.CompilerParams(dimension_semantics=("arbitrary", "arbitrary")),
        name="prep",
    )(c_ctx.reshape(1, D_MODEL), c, ada_w, ada_w, ada_b, w_uq, w_ukv)

    tb = 2
    assert seq == STAGE_ROWS and (tb * seq) % STAGE_ROWS == 0
    n_slots = tb * seq // STAGE_ROWS + tb
    small = dict(zip(_SMALL_NAMES, (
        norm_g, mla_q_norm, w_uq_p, mla_kv_norm, w_ukv_p, jnp.swapaxes(conv_w, 0, 1), diff_lambda,
        diff_subln, final_norm.reshape(1, D_MODEL))))
    big_shapes = dict(zip(_BIG_NAMES, ((DEPTH, D_IN - R_CB, D_MODEL), (DEPTH, R_KR + LANES, D_MODEL),
                                       (DEPTH, D_MIX, D_MODEL))))
    hbm_spec = pl.BlockSpec(memory_space=pl.ANY)
    state_shapes = ((seq, MLA_KV_LORA), (MLA_ROPE, seq), (W_DIFF, seq),
                    (DIFF_HEADS * seq, DIFF_V))
    y_prompt, s_ckv, s_kr, s_dk, s_dv, w_in_t, w_head, w_out_b = pl.pallas_call(
        functools.partial(_ctx_kernel, n_seq=seq),
        grid=(batch // tb,),
        in_specs=[pl.BlockSpec((tb, seq, D_MODEL), lambda i: (i, 0, 0)), _const_spec(mod.shape)]
        + [_const_spec(a.shape) for a in small.values()] + [hbm_spec, hbm_spec],
        out_specs=[pl.BlockSpec((tb, seq, D_MODEL), lambda i: (i, 0, 0))]
        + [pl.BlockSpec((tb, DEPTH) + s, lambda i: (i, 0, 0, 0)) for s in state_shapes]
        + [hbm_spec] * len(_BIG_NAMES),
        out_shape=[jax.ShapeDtypeStruct((batch, seq, D_MODEL), F32)]
        + [jax.ShapeDtypeStruct((batch, DEPTH) + s, F32) for s in state_shapes]
        + [jax.ShapeDtypeStruct(big_shapes[n], BF16) for n in _BIG_NAMES],
        scratch_shapes=_scratch(tb, seq, seq)
        + [pltpu.VMEM(big_shapes[n], BF16) for n in _BIG_NAMES]
        + [pltpu.SemaphoreType.DMA((n_slots,)), pltpu.SemaphoreType.DMA((len(_BIG_NAMES),))],
        compiler_params=cparams,
        name="ctx",
    )(x_prompt, mod, *small.values(), jnp.swapaxes(w_in, 1, 2), w_out)

    weights = tuple(dict(small, w_in_t=w_in_t, w_head=w_head, w_out=w_out_b)[n] for n in _W_NAMES)

    lat_params = pltpu.CompilerParams(dimension_semantics=("arbitrary", "arbitrary"),
                                      vmem_limit_bytes=VMEM_LIMIT_BYTES)
    tables = _rope_tables(dec_seq, MLA_ROPE) + _rope_tables(dec_seq, DIFF_QK)
    ctx_in = (cache_mla_ckv,
              jnp.swapaxes(cache_mla_krope, 2, 3),
              jnp.transpose(cache_diff_k, (0, 1, 3, 4, 5, 2)).reshape(dec_batch, DEPTH, W_DIFF, past),
              cache_diff_v.reshape(dec_batch, DEPTH, past * DIFF_HEADS, DIFF_V))
    h_lat = x_sample
    for l in range(DEPTH):
        h_lat = pl.pallas_call(
            functools.partial(_lat_kernel, n_seq=dec_seq, layer=l),
            grid=(dec_batch, dec_seq // LAT_Q_TILE),
            in_specs=[pl.BlockSpec((1, dec_seq, D_MODEL), lambda i, j: (i, 0, 0)),
                      _const_spec(mod.shape)]
            + [pl.BlockSpec((1, 1) + a.shape[2:], lambda i, j, l=l: (i, l, 0, 0))
               for a in ctx_in]
            + [_const_spec(t.shape) for t in tables]
            + [_const_spec(a.shape, layer=l if name in _PER_LAYER_BLOCKS else None)
               for name, a in zip(_W_NAMES, weights)],
            out_specs=pl.BlockSpec((1, LAT_Q_TILE, D_MODEL), lambda i, j: (i, j, 0)),
            out_shape=jax.ShapeDtypeStruct((dec_batch, dec_seq, D_MODEL), F32),
            scratch_shapes=_scratch(1, dec_seq, past + dec_seq),
            compiler_params=lat_params,
            name=f"latent_{l}",
        )(h_lat, mod, *ctx_in, *tables, *weights)

    return (y_prompt, h_lat, s_ckv,
            jnp.swapaxes(s_kr, 2, 3),
            jnp.transpose(s_dk.reshape(batch, DEPTH, DIFF_HEADS, 2, DIFF_QK, seq),
                          (0, 1, 5, 2, 3, 4)),
            s_dv.reshape(batch, DEPTH, seq, DIFF_HEADS, DIFF_V))
```

```python
import functools
import math

import numpy as np
import jax
import jax.numpy as jnp
from jax import lax
from jax.experimental import pallas as pl
from jax.experimental.pallas import tpu as pltpu

D_MODEL = 1024
DEPTH = 2
GRID_W = 64
MLA_HEADS = 4
MLA_NOPE = 64
MLA_ROPE = 32
MLA_V = 64
MLA_Q_LORA = 256
MLA_KV_LORA = 128
W_MLA = MLA_HEADS * MLA_V
W_CONV = 256
DIFF_HEADS = 4
DIFF_QK = 64
DIFF_V = 2 * DIFF_QK
W_DIFF = DIFF_HEADS * DIFF_V
D_MIX = W_MLA + W_CONV + W_DIFF
ROPE_THETA = 10000.0
EPS = 1e-6

LANES = 128
Q_TILE = 256
LAT_Q_TILE = 256
LOG2E = math.log2(math.e)
MLA_SCALE = (MLA_NOPE + MLA_ROPE) ** -0.5 * LOG2E
DIFF_SCALE = DIFF_QK ** -0.5 * LOG2E
VMEM_LIMIT_BYTES = 54 * 1024 * 1024

R_Q = 0
R_KV = R_Q + MLA_Q_LORA
R_KR = R_KV + MLA_KV_LORA
R_CB = R_KR + MLA_ROPE
R_CC = R_CB + W_CONV
R_CX = R_CC + W_CONV
R_DQ = R_CX + W_CONV
R_DK = R_DQ + W_DIFF
R_DV = R_DK + W_DIFF
R_Z = R_DV + W_DIFF
D_IN = R_Z + D_MIX

BF16 = jnp.bfloat16
F32 = jnp.float32


def _lambda_init(layer):
    return 0.8 - 0.6 * math.exp(-0.3 * layer)


def _dot(a, b):
    return jnp.dot(a, b, preferred_element_type=F32)


def _dot_nt(a, b):
    return lax.dot_general(a, b, (((1,), (1,)), ((), ())), preferred_element_type=F32)


def _rms(x, g):
    ms = jnp.mean(x * x, axis=-1, keepdims=True)
    return x * lax.rsqrt(ms + EPS) * g


def _exp_scores(s):
    m = s[:, 0:LANES]
    for t in range(1, s.shape[1] // LANES):
        m = jnp.maximum(m, s[:, LANES * t:LANES * (t + 1)])
    return jnp.exp2(s - jnp.max(m, axis=-1, keepdims=True)).astype(BF16)


def _weighted(e, v_ones):
    r = _dot(e, v_ones)
    return r[:, 0:LANES], 1.0 / r[:, LANES:2 * LANES]


def _lane_mask(lo, hi):
    lane = lax.broadcasted_iota(jnp.int32, (1, LANES), 1)
    return jnp.where((lane >= lo) & (lane < hi), 1.0, 0.0).astype(F32)


def _rope(x, tables, q):
    cos, sin_signed = tables
    lane = lax.broadcasted_iota(jnp.int32, (1, LANES), 1)
    first = (lane & (2 * q - 1)) < q
    tiles = []
    for t in range(x.shape[1] // LANES):
        xt = x[:, LANES * t:LANES * (t + 1)]
        rot = jnp.where(first, pltpu.roll(xt, LANES - q, 1), pltpu.roll(xt, q, 1))
        tiles.append(xt * cos + rot * sin_signed)
    return tiles[0] if len(tiles) == 1 else jnp.concatenate(tiles, axis=1)


def _attend(q, dq, keys, kcat_ref, vm_ref, dka_ref, dvb_ref, lam, subln, lam_init):
    qr = q[:, 2 * LANES:3 * LANES]
    o_mla = []
    for t in range(2):
        qcat = jnp.concatenate([q[:, LANES * t:LANES * (t + 1)], qr], axis=1)
        acc = None
        for hh in range(2):
            h = 2 * t + hh
            o, rinv = _weighted(_exp_scores(_dot_nt(qcat, kcat_ref[h, keys, :])),
                                vm_ref[h, keys, :])
            acc = o * rinv if acc is None else acc + o * rinv
        o_mla.append(acc)
    o_diff = []
    for h in range(DIFF_HEADS):
        qt = dq[:, LANES * h:LANES * (h + 1)]
        o0, rinv0 = _weighted(_exp_scores(_dot_nt(qt, dka_ref[2 * h, keys, :])),
                              dvb_ref[h, keys, :])
        o1, rinv1 = _weighted(_exp_scores(_dot_nt(qt, dka_ref[2 * h + 1, keys, :])),
                              dvb_ref[h, keys, :])
        o = o0 * rinv0 - o1 * (lam * rinv1)
        o_diff.append(_rms(o, subln) * (1.0 - lam_init))
    return jnp.concatenate(o_mla, axis=1), jnp.concatenate(o_diff, axis=1)


def _proj(wl, w, scr, r0, r1):
    return _dot_nt(scr["hb"][...], w["w_in_t"][wl, r0 - R_CB:r1 - R_CB, :])


def _layer_front(l, wl, x, n_seq, mod, w, scr, rope, ctx, state):
    m_rows = x.shape[0]
    n_b = m_rows // n_seq
    n_ctx = 0 if ctx is None else ctx["ckv"].shape[0]
    nk = n_ctx + n_seq
    shift, scale, _ = mod
    hb_ref = scr["hb"]
    qb_ref, dqb_ref = scr["qb"], scr["dqb"]
    proj = functools.partial(_proj, wl, w, scr)
    kcat_ref, vm_ref, dka_ref, dvb_ref = scr["kcat"], scr["vm"], scr["dka"], scr["dvb"]
    w_ukv = w["w_ukv"]
    half = (_lane_mask(0, LANES // 2), _lane_mask(LANES // 2, LANES))
    new_keys, old_keys = pl.ds(n_ctx, n_seq), pl.ds(0, n_ctx)
    if rope is not None:
        rope_m = tuple(r[...] for r in rope[0])
        rope_d = tuple(r[...] for r in rope[1])

    def seq_rows(b):
        return slice(b * n_seq, (b + 1) * n_seq)

    gs = w["norm_g"][l:l + 1, :] * (1.0 + scale)
    ms = jnp.mean(x * x, axis=-1, keepdims=True)
    hb_ref[...] = (x * lax.rsqrt(ms + EPS) * gs + shift).astype(BF16)

    head = _dot_nt(hb_ref[...], w["w_head"][wl])

    cqn = _rms(head[:, 0:MLA_Q_LORA], w["q_norm"][l:l + 1, :]).astype(BF16)
    q = _dot(cqn, w["w_uq"][wl]) * MLA_SCALE
    if rope is not None:
        q = jnp.concatenate(
            [q[:, 0:2 * LANES], _rope(q[:, 2 * LANES:3 * LANES], rope_m, MLA_ROPE // 4)], axis=1)
    qb_ref[...] = q.astype(BF16)

    def fill_kv(b, dst, kv):
        for h in range(MLA_HEADS):
            t, hh = divmod(h, 2)
            kcat_ref[b, h, dst, 0:LANES] = (kv[:, LANES * t:LANES * (t + 1)] * half[hh]).astype(BF16)
            vm_ref[b, h, dst, 0:LANES] = (
                kv[:, 2 * LANES + LANES * t:2 * LANES + LANES * (t + 1)] * half[hh]).astype(BF16)

    def fill_kr(b, dst, kr):
        for h in range(MLA_HEADS):
            kcat_ref[b, h, dst, LANES:2 * LANES] = (
                kr * _lane_mask(MLA_ROPE * h, MLA_ROPE * (h + 1))).astype(BF16)

    def fill_dk(b, dst, dk):
        for h in range(DIFF_HEADS):
            for a in range(2):
                dka_ref[b, 2 * h + a, dst, :] = (
                    dk[:, LANES * h:LANES * (h + 1)] * half[a]).astype(BF16)

    ckv = _rms(head[:, R_KV:R_KR], w["kv_norm"][l:l + 1, :])
    kv = _dot(ckv.astype(BF16), w_ukv[wl])
    for b in range(n_b):
        if state is not None:
            state[0][b, l] = ckv[seq_rows(b)]
        fill_kv(b, new_keys, kv[seq_rows(b)])
        if ctx is not None:
            fill_kv(b, old_keys, _dot(ctx["ckv"][...].astype(BF16), w_ukv[wl]))

    krr = head[:, R_KR:R_KR + LANES]
    krr_pos = krr if rope is None else _rope(krr, rope_m, MLA_ROPE // 4)
    for b in range(n_b):
        if state is not None:
            state[1][b, l] = krr[seq_rows(b)].T[0:MLA_ROPE, :]
        fill_kr(b, new_keys, krr_pos[seq_rows(b)])
        if ctx is not None:
            fill_kr(b, old_keys, jnp.concatenate([ctx["kr"][...]] * (LANES // MLA_ROPE), axis=0).T)

    dq = proj(R_DQ, R_DK) * DIFF_SCALE
    if rope is not None:
        dq = _rope(dq, rope_d, DIFF_QK // 4)
    dqb_ref[...] = dq.astype(BF16)

    dk = proj(R_DK, R_DV)
    dk_pos = dk if rope is None else _rope(dk, rope_d, DIFF_QK // 4)
    for b in range(n_b):
        if state is not None:
            state[2][b, l] = dk[seq_rows(b)].T
        fill_dk(b, new_keys, dk_pos[seq_rows(b)])
        if ctx is not None:
            fill_dk(b, old_keys, ctx["dk"][...].T)

    dv = proj(R_DV, R_Z)
    for b in range(n_b):
        for h in range(DIFF_HEADS):
            dv_h = dv[seq_rows(b), LANES * h:LANES * (h + 1)]
            if state is not None:
                state[3][b, l, pl.ds(h, n_seq, stride=DIFF_HEADS), :] = dv_h
            dvb_ref[b, h, new_keys, 0:LANES] = dv_h.astype(BF16)
            if ctx is not None:
                dvb_ref[b, h, old_keys, 0:LANES] = (
                    ctx["dv"][pl.ds(h, n_ctx, stride=DIFF_HEADS), :].astype(BF16))
            ones = jnp.ones((nk, LANES), BF16)
            dvb_ref[b, h, :, LANES:2 * LANES] = ones
            vm_ref[b, h, :, LANES:2 * LANES] = ones


def _attend_block(l, b, start, tq, nk, w, scr):
    lf = w["lam"][l]
    lam = (jnp.exp(jnp.sum(lf[0:1] * lf[1:2], axis=-1, keepdims=True))
           - jnp.exp(jnp.sum(lf[2:3] * lf[3:4], axis=-1, keepdims=True)) + _lambda_init(l))
    qrows = pl.ds(start, tq)
    return _attend(scr["qb"][qrows, :], scr["dqb"][qrows, :], pl.ds(0, nk),
                   scr["kcat"].at[b], scr["vm"].at[b], scr["dka"].at[b], scr["dvb"].at[b],
                   lam, w["subln"][l:l + 1, :], _lambda_init(l))


def _conv_branch(l, wl, m_rows, n_seq, w, scr):
    conv = _proj(wl, w, scr, R_CB, R_DQ)
    u = conv[:, W_CONV:2 * W_CONV] * conv[:, 2 * W_CONV:3 * W_CONV]
    pos = lax.broadcasted_iota(jnp.int32, (m_rows, 1), 0) % n_seq
    u_prev = jnp.where(pos == 0, 0.0, pltpu.roll(u, 1, 0))
    u_next = jnp.where(pos == n_seq - 1, 0.0, pltpu.roll(u, m_rows - 1, 0))
    cw = w["conv_w"][:, l, :]
    return conv[:, 0:W_CONV] * (u_prev * cw[0:1] + u * cw[1:2] + u_next * cw[2:3])


def _silu_gate(wl, w, scr, c0, c1):
    z = _proj(wl, w, scr, R_Z + c0, R_Z + c1)
    return z * jax.nn.sigmoid(z)


def _trunk_layer(l, x, n_seq, mod, w, scr, state):
    m_rows = x.shape[0]
    o_ref, mix_ref = scr["o"], scr["mix"]
    _layer_front(l, l, x, n_seq, mod, w, scr, None, None, state)
    for b in range(m_rows // n_seq):
        for start in range(b * n_seq, (b + 1) * n_seq, Q_TILE):
            o_mla, o_diff = _attend_block(l, b, start, Q_TILE, n_seq, w, scr)
            o_ref[start:start + Q_TILE, 0:W_MLA] = o_mla
            o_ref[start:start + Q_TILE, W_MLA + W_CONV:D_MIX] = o_diff
    o_ref[:, W_MLA:W_MLA + W_CONV] = _conv_branch(l, l, m_rows, n_seq, w, scr)
    for c0 in range(0, D_MIX, 512):
        mix_ref[:, c0:c0 + 512] = (
            o_ref[:, c0:c0 + 512] * _silu_gate(l, w, scr, c0, c0 + 512)).astype(BF16)
    return x + mod[2] * _dot(mix_ref[...], w["w_out"][l])


_W_NAMES = ("norm_g", "w_in_t", "w_head", "q_norm", "w_uq", "kv_norm", "w_ukv", "conv_w", "lam",
            "subln", "w_out", "final_norm")
_PER_LAYER_BLOCKS = ("w_in_t", "w_head", "w_uq", "w_ukv", "w_out")
_SCR_NAMES = ("hb", "o", "mix", "qb", "dqb", "kcat", "vm", "dka", "dvb")
_CTX_NAMES = ("ckv", "kr", "dk", "dv")
N_ROPE = 4


_BIG_NAMES = ("w_in_t", "w_head", "w_out")
_SMALL_NAMES = tuple(n for n in _W_NAMES if n not in _BIG_NAMES)
W_IN_CHUNK = 208
STAGE_ROWS = 256
assert D_IN % W_IN_CHUNK == 0 and R_CB % W_IN_CHUNK == 0 and R_CB - R_KR <= W_IN_CHUNK


def _stage_weights(w_in_hbm, w_out_hbm, big, slots, sem):
    jobs = []
    for l in range(DEPTH):
        jobs += [(w_in_hbm, big["w_in_t"], l, r, W_IN_CHUNK) for r in range(0, D_IN, W_IN_CHUNK)]
        jobs += [(w_out_hbm, big["w_out"], l, r, STAGE_ROWS) for r in range(0, D_MIX, STAGE_ROWS)]
    n_slots = len(slots)

    def copy(k):
        src, _, l, r, n = jobs[k]
        return pltpu.make_async_copy(src.at[l, pl.ds(r, n)], slots[k % n_slots].at[pl.ds(0, n)],
                                     sem.at[k % n_slots])

    for k in range(n_slots):
        copy(k).start()
    head = big["w_head"]
    for k, (src, dst, l, r, n) in enumerate(jobs):
        copy(k).wait()
        slot = slots[k % n_slots]
        if src is w_out_hbm:
            dst[l, r:r + n, :] = slot[0:n, :].astype(BF16)
        elif r >= R_CB:
            dst[l, r - R_CB:r - R_CB + n, :] = slot[0:n, :].astype(BF16)
        else:
            n_main = min(r + n, R_KR) - r
            head[l, r:r + n_main, :] = slot[0:n_main, :].astype(BF16)
            if r + n > R_KR:
                k_rope = slot[R_KR - r:R_CB - r, :].astype(BF16)
                for i in range(LANES // MLA_ROPE):
                    head[l, R_KR + MLA_ROPE * i:R_KR + MLA_ROPE * (i + 1), :] = k_rope
        if k + n_slots < len(jobs):
            copy(k + n_slots).start()


def _ctx_kernel(x_ref, mod_ref, *refs, n_seq):
    n_small, n_big, n_scr = len(_SMALL_NAMES), len(_BIG_NAMES), len(_SCR_NAMES)
    w = dict(zip(_SMALL_NAMES, refs[:n_small]))
    w_in_hbm, w_out_hbm = refs[n_small:n_small + 2]
    rest = refs[n_small + 2:]
    y_ref, state = rest[0], rest[1:5]
    big_hbm = rest[5:5 + n_big]
    scr = dict(zip(_SCR_NAMES, rest[5 + n_big:5 + n_big + n_scr]))
    big = dict(zip(_BIG_NAMES, rest[5 + n_big + n_scr:5 + 2 * n_big + n_scr]))
    in_sem, out_sem = rest[5 + 2 * n_big + n_scr:]
    w.update(big)

    def hand_off(k):
        return pltpu.make_async_copy(big[_BIG_NAMES[k]], big_hbm[k], out_sem.at[k])

    first_step = pl.program_id(0) == 0

    @pl.when(first_step)
    def _():
        slots = [scr["o"].at[pl.ds(r, STAGE_ROWS)] for r in range(0, scr["o"].shape[0], STAGE_ROWS)]
        slots += [y_ref.at[b] for b in range(y_ref.shape[0])]
        _stage_weights(w_in_hbm, w_out_hbm, big, slots, in_sem)
        for k in range(n_big):
            hand_off(k).start()

    n_b = x_ref.shape[0]
    x = x_ref[...].reshape(n_b * n_seq, D_MODEL)
    for l in range(DEPTH):
        mod = tuple(mod_ref[l, j, 0:1, :] for j in range(3))
        x = _trunk_layer(l, x, n_seq, mod, w, scr, state)
    y_ref[...] = _rms(x, w["final_norm"][...]).reshape(n_b, n_seq, D_MODEL)

    @pl.when(first_step)
    def _():
        for k in range(n_big):
            hand_off(k).wait()


def _lat_kernel(x_ref, mod_ref, *refs, n_seq, layer):
    ctx = {k: r.at[0, 0] for k, r in zip(_CTX_NAMES, refs[:4])}
    rope = (refs[4:6], refs[6:8])
    w = dict(zip(_W_NAMES, refs[4 + N_ROPE:4 + N_ROPE + len(_W_NAMES)]))
    rest = refs[4 + N_ROPE + len(_W_NAMES):]
    y_ref = rest[0]
    scr = dict(zip(_SCR_NAMES, rest[1:]))
    cond_row = pl.ds(1 + pl.program_id(0), 1)
    mod = tuple(mod_ref[layer, j, cond_row, :] for j in range(3))
    zg_ref, mix_ref = scr["o"], scr["mix"]
    n_ctx = ctx["ckv"].shape[0]
    conv_cols = slice(W_MLA, W_MLA + W_CONV)

    @pl.when(pl.program_id(1) == 0)
    def _():
        _layer_front(layer, 0, x_ref[0], n_seq, mod, w, scr, rope, ctx, None)
        for c0 in range(0, D_MIX, 512):
            zg_ref[:, c0:c0 + 512] = _silu_gate(0, w, scr, c0, c0 + 512)
        mix_ref[:, conv_cols] = (
            _conv_branch(layer, 0, n_seq, n_seq, w, scr) * zg_ref[:, conv_cols]).astype(BF16)

    start = pl.multiple_of(pl.program_id(1) * LAT_Q_TILE, LAT_Q_TILE)
    qrows = pl.ds(start, LAT_Q_TILE)
    o_mla, o_diff = _attend_block(layer, 0, start, LAT_Q_TILE, n_ctx + n_seq, w, scr)
    mix_ref[qrows, 0:W_MLA] = (o_mla * zg_ref[qrows, 0:W_MLA]).astype(BF16)
    mix_ref[qrows, W_MLA + W_CONV:D_MIX] = (
        o_diff * zg_ref[qrows, W_MLA + W_CONV:D_MIX]).astype(BF16)
    x = x_ref[0, qrows, :] + mod[2] * _dot(mix_ref[qrows, :], w["w_out"][0])
    if layer == DEPTH - 1:
        x = _rms(x, w["final_norm"][...])
    y_ref[0] = x


ADA_CHUNK = 256


def _prep_kernel(cctx_ref, c_ref, ada_w_ref, ada_b_ref, w_uq_ref, w_ukv_ref,
                 mod_ref, w_uq_p_ref, w_ukv_p_ref):
    n_rows = mod_ref.shape[2]
    pad = jnp.zeros((n_rows - 1 - c_ref.shape[0], c_ref.shape[1]), F32)
    c = jnp.concatenate([cctx_ref[...], c_ref[...], pad], axis=0)
    s = (c * jax.nn.sigmoid(c)).astype(BF16)
    part = _dot(s, ada_w_ref[0].astype(BF16))
    first = pl.program_id(1) == 0

    @pl.when(first)
    def _():
        bias = ada_b_ref[pl.ds(pl.program_id(0), 1), :]
        for j in range(3):
            mod_ref[0, j] = part[:, D_MODEL * j:D_MODEL * (j + 1)] + bias[:, D_MODEL * j:D_MODEL * (j + 1)]

    @pl.when(jnp.logical_not(first))
    def _():
        for j in range(3):
            mod_ref[0, j] += part[:, D_MODEL * j:D_MODEL * (j + 1)]

    @pl.when(first)
    def _():
        for src, dst, first in ((w_uq_ref, w_uq_p_ref, MLA_NOPE), (w_ukv_ref, w_ukv_p_ref, MLA_NOPE)):
            a = src[0]
            per = a.shape[1] // MLA_HEADS
            cols = ([a[:, per * h:per * h + first] for h in range(MLA_HEADS)]
                    + [a[:, per * h + first:per * (h + 1)] for h in range(MLA_HEADS)])
            dst[0] = jnp.concatenate(cols, axis=1).astype(BF16)


def _const_spec(shape, layer=None):
    if layer is None:
        zeros = (0,) * len(shape)
        return pl.BlockSpec(shape, lambda *_: zeros, pipeline_mode=pl.Buffered(1))
    idx = (layer,) + (0,) * (len(shape) - 1)
    return pl.BlockSpec((1,) + tuple(shape[1:]), lambda *_: idx, pipeline_mode=pl.Buffered(1))


def _scratch(n_b, n_seq, nk):
    m_rows = n_b * n_seq
    return [
        pltpu.VMEM((m_rows, D_MODEL), BF16),
        pltpu.VMEM((m_rows, D_MIX), F32),
        pltpu.VMEM((m_rows, D_MIX), BF16),
        pltpu.VMEM((m_rows, 3 * LANES), BF16),
        pltpu.VMEM((m_rows, W_DIFF), BF16),
        pltpu.VMEM((n_b, MLA_HEADS, nk, 2 * LANES), BF16),
        pltpu.VMEM((n_b, MLA_HEADS, nk, 2 * LANES), BF16),
        pltpu.VMEM((n_b, 2 * DIFF_HEADS, nk, LANES), BF16),
        pltpu.VMEM((n_b, DIFF_HEADS, nk, 2 * LANES), BF16),
    ]


def _rope_tables(n, rot_dim):
    half = rot_dim // 2
    inv = ROPE_THETA ** (-(np.arange(0, half, 2, dtype=np.float32) / half))
    pos = np.arange(n)
    ang_r = (pos // GRID_W).astype(np.float32)[:, None] * inv
    ang_c = (pos % GRID_W).astype(np.float32)[:, None] * inv
    ang = np.concatenate([ang_r, ang_r, ang_c, ang_c], axis=-1)
    cos, sin = np.cos(ang), np.sin(ang)
    q = rot_dim // 4
    first = (np.arange(rot_dim) % (2 * q)) < q
    sin_signed = np.where(first, -sin, sin)
    reps = LANES // rot_dim
    return tuple(jnp.asarray(np.tile(t, (1, reps)), dtype=F32) for t in (cos, sin_signed))


def kernel(x_prompt, x_sample, c, cache_mla_ckv, cache_mla_krope, cache_diff_k, cache_diff_v,
           c_ctx, norm_g, ada_w, ada_b, w_in, mla_q_norm, w_uq, mla_kv_norm, w_ukv,
           conv_w, diff_lambda, diff_subln, w_out, final_norm):
    batch, seq, _ = x_prompt.shape
    dec_batch, dec_seq, _ = x_sample.shape
    past = cache_mla_ckv.shape[2]
    cparams = pltpu.CompilerParams(dimension_semantics=("arbitrary",),
                                   vmem_limit_bytes=VMEM_LIMIT_BYTES)

    n_cond = 8
    per_layer_spec = lambda a: pl.BlockSpec((1,) + a.shape[1:], lambda l, j: (l, 0, 0))
    mod, w_uq_p, w_ukv_p = pl.pallas_call(
        _prep_kernel,
        grid=(DEPTH, D_MODEL // ADA_CHUNK),
        in_specs=[pl.BlockSpec((1, ADA_CHUNK), lambda l, j: (0, j)),
                  pl.BlockSpec((dec_batch, ADA_CHUNK), lambda l, j: (0, j)),
                  pl.BlockSpec((1, ADA_CHUNK, 3 * D_MODEL), lambda l, j: (l, j, 0)),
                  pl.BlockSpec((DEPTH, 3 * D_MODEL), lambda l, j: (0, 0)),
                  per_layer_spec(w_uq), per_layer_spec(w_ukv)],
        out_specs=[pl.BlockSpec((1, 3, n_cond, D_MODEL), lambda l, j: (l, 0, 0, 0)),
                   per_layer_spec(w_uq), per_layer_spec(w_ukv)],
        out_shape=[jax.ShapeDtypeStruct((DEPTH, 3, n_cond, D_MODEL), F32),
                   jax.ShapeDtypeStruct(w_uq.shape, BF16),
                   jax.ShapeDtypeStruct(w_ukv.shape, BF16)],
        compiler_params=pltpu.CompilerParams(dimension_semantics=("arbitrary", "arbitrary")),
        name="prep",
    )(c_ctx.reshape(1, D_MODEL), c, ada_w, ada_b, w_uq, w_ukv)

    tb = 2
    assert seq == STAGE_ROWS and (tb * seq) % STAGE_ROWS == 0
    n_slots = tb * seq // STAGE_ROWS + tb
    small = dict(zip(_SMALL_NAMES, (
        norm_g, mla_q_norm, w_uq_p, mla_kv_norm, w_ukv_p, jnp.swapaxes(conv_w, 0, 1), diff_lambda,
        diff_subln, final_norm.reshape(1, D_MODEL))))
    big_shapes = dict(zip(_BIG_NAMES, ((DEPTH, D_IN - R_CB, D_MODEL), (DEPTH, R_KR + LANES, D_MODEL),
                                       (DEPTH, D_MIX, D_MODEL))))
    hbm_spec = pl.BlockSpec(memory_space=pl.ANY)
    state_shapes = ((seq, MLA_KV_LORA), (MLA_ROPE, seq), (W_DIFF, seq),
                    (DIFF_HEADS * seq, DIFF_V))
    y_prompt, s_ckv, s_kr, s_dk, s_dv, w_in_t, w_head, w_out_b = pl.pallas_call(
        functools.partial(_ctx_kernel, n_seq=seq),
        grid=(batch // tb,),
        in_specs=[pl.BlockSpec((tb, seq, D_MODEL), lambda i: (i, 0, 0)), _const_spec(mod.shape)]
        + [_const_spec(a.shape) for a in small.values()] + [hbm_spec, hbm_spec],
        out_specs=[pl.BlockSpec((tb, seq, D_MODEL), lambda i: (i, 0, 0))]
        + [pl.BlockSpec((tb, DEPTH) + s, lambda i: (i, 0, 0, 0)) for s in state_shapes]
        + [hbm_spec] * len(_BIG_NAMES),
        out_shape=[jax.ShapeDtypeStruct((batch, seq, D_MODEL), F32)]
        + [jax.ShapeDtypeStruct((batch, DEPTH) + s, F32) for s in state_shapes]
        + [jax.ShapeDtypeStruct(big_shapes[n], BF16) for n in _BIG_NAMES],
        scratch_shapes=_scratch(tb, seq, seq)
        + [pltpu.VMEM(big_shapes[n], BF16) for n in _BIG_NAMES]
        + [pltpu.SemaphoreType.DMA((n_slots,)), pltpu.SemaphoreType.DMA((len(_BIG_NAMES),))],
        compiler_params=cparams,
        name="ctx",
    )(x_prompt, mod, *small.values(), jnp.swapaxes(w_in, 1, 2), w_out)

    weights = tuple(dict(small, w_in_t=w_in_t, w_head=w_head, w_out=w_out_b)[n] for n in _W_NAMES)

    lat_params = pltpu.CompilerParams(dimension_semantics=("arbitrary", "arbitrary"),
                                      vmem_limit_bytes=VMEM_LIMIT_BYTES)
    tables = _rope_tables(dec_seq, MLA_ROPE) + _rope_tables(dec_seq, DIFF_QK)
    ctx_in = (cache_mla_ckv,
              jnp.swapaxes(cache_mla_krope, 2, 3),
              jnp.transpose(cache_diff_k, (0, 1, 3, 4, 5, 2)).reshape(dec_batch, DEPTH, W_DIFF, past),
              cache_diff_v.reshape(dec_batch, DEPTH, past * DIFF_HEADS, DIFF_V))
    h_lat = x_sample
    for l in range(DEPTH):
        h_lat = pl.pallas_call(
            functools.partial(_lat_kernel, n_seq=dec_seq, layer=l),
            grid=(dec_batch, dec_seq // LAT_Q_TILE),
            in_specs=[pl.BlockSpec((1, dec_seq, D_MODEL), lambda i, j: (i, 0, 0)),
                      _const_spec(mod.shape)]
            + [pl.BlockSpec((1, 1) + a.shape[2:], lambda i, j, l=l: (i, l, 0, 0))
               for a in ctx_in]
            + [_const_spec(t.shape) for t in tables]
            + [_const_spec(a.shape, layer=l if name in _PER_LAYER_BLOCKS else None)
               for name, a in zip(_W_NAMES, weights)],
            out_specs=pl.BlockSpec((1, LAT_Q_TILE, D_MODEL), lambda i, j: (i, j, 0)),
            out_shape=jax.ShapeDtypeStruct((dec_batch, dec_seq, D_MODEL), F32),
            scratch_shapes=_scratch(1, dec_seq, past + dec_seq),
            compiler_params=lat_params,
            name=f"latent_{l}",
        )(h_lat, mod, *ctx_in, *tables, *weights)

    return (y_prompt, h_lat, s_ckv,
            jnp.swapaxes(s_kr, 2, 3),
            jnp.transpose(s_dk.reshape(batch, DEPTH, DIFF_HEADS, 2, DIFF_QK, seq),
                          (0, 1, 5, 2, 3, 4)),
            s_dv.reshape(batch, DEPTH, seq, DIFF_HEADS, DIFF_V))
```

```python
import functools
import math

import numpy as np
import jax
import jax.numpy as jnp
from jax import lax
from jax.experimental import pallas as pl
from jax.experimental.pallas import tpu as pltpu

D_MODEL = 1024
DEPTH = 2
GRID_W = 64
MLA_HEADS = 4
MLA_NOPE = 64
MLA_ROPE = 32
MLA_V = 64
MLA_Q_LORA = 256
MLA_KV_LORA = 128
W_MLA = MLA_HEADS * MLA_V
W_CONV = 256
DIFF_HEADS = 4
DIFF_QK = 64
DIFF_V = 2 * DIFF_QK
W_DIFF = DIFF_HEADS * DIFF_V
D_MIX = W_MLA + W_CONV + W_DIFF
ROPE_THETA = 10000.0
EPS = 1e-6

LANES = 128
Q_TILE = 256
LAT_Q_TILE = 512
LAT_LAST_Q_TILE = 256
LOG2E = math.log2(math.e)
MLA_SCALE = (MLA_NOPE + MLA_ROPE) ** -0.5 * LOG2E
DIFF_SCALE = DIFF_QK ** -0.5 * LOG2E
VMEM_LIMIT_BYTES = 54 * 1024 * 1024
BIG_TILE_VMEM_LIMIT_BYTES = 60 * 1024 * 1024

R_Q = 0
R_KV = R_Q + MLA_Q_LORA
R_KR = R_KV + MLA_KV_LORA
R_CB = R_KR + MLA_ROPE
R_CC = R_CB + W_CONV
R_CX = R_CC + W_CONV
R_DQ = R_CX + W_CONV
R_DK = R_DQ + W_DIFF
R_DV = R_DK + W_DIFF
R_Z = R_DV + W_DIFF
D_IN = R_Z + D_MIX

BF16 = jnp.bfloat16
F32 = jnp.float32


def _lambda_init(layer):
    return 0.8 - 0.6 * math.exp(-0.3 * layer)


def _dot(a, b):
    return jnp.dot(a, b, preferred_element_type=F32)


def _dot_nt(a, b):
    return lax.dot_general(a, b, (((1,), (1,)), ((), ())), preferred_element_type=F32)


def _rms(x, g):
    ms = jnp.mean(x * x, axis=-1, keepdims=True)
    return x * lax.rsqrt(ms + EPS) * g


def _exp_scores(s):
    m = s[:, 0:LANES]
    for t in range(1, s.shape[1] // LANES):
        m = jnp.maximum(m, s[:, LANES * t:LANES * (t + 1)])
    return jnp.exp2(s - jnp.max(m, axis=-1, keepdims=True)).astype(BF16)


def _weighted(e, v_ones):
    r = _dot(e, v_ones)
    return r[:, 0:LANES], 1.0 / r[:, LANES:2 * LANES]


def _lane_mask(lo, hi):
    lane = lax.broadcasted_iota(jnp.int32, (1, LANES), 1)
    return jnp.where((lane >= lo) & (lane < hi), 1.0, 0.0).astype(F32)


def _rope(x, tables, q):
    cos, sin_signed = tables
    lane = lax.broadcasted_iota(jnp.int32, (1, LANES), 1)
    first = (lane & (2 * q - 1)) < q
    tiles = []
    for t in range(x.shape[1] // LANES):
        xt = x[:, LANES * t:LANES * (t + 1)]
        rot = jnp.where(first, pltpu.roll(xt, LANES - q, 1), pltpu.roll(xt, q, 1))
        tiles.append(xt * cos + rot * sin_signed)
    return tiles[0] if len(tiles) == 1 else jnp.concatenate(tiles, axis=1)


def _attend(q, dq, keys, kcat_ref, vm_ref, dka_ref, dvb_ref, lam, subln, lam_init):
    qr = q[:, 2 * LANES:3 * LANES]
    o_mla = []
    for t in range(2):
        qcat = jnp.concatenate([q[:, LANES * t:LANES * (t + 1)], qr], axis=1)
        acc = None
        for hh in range(2):
            h = 2 * t + hh
            o, rinv = _weighted(_exp_scores(_dot_nt(qcat, kcat_ref[h, keys, :])),
                                vm_ref[h, keys, :])
            acc = o * rinv if acc is None else acc + o * rinv
        o_mla.append(acc)
    o_diff = []
    for h in range(DIFF_HEADS):
        qt = dq[:, LANES * h:LANES * (h + 1)]
        o0, rinv0 = _weighted(_exp_scores(_dot_nt(qt, dka_ref[2 * h, keys, :])),
                              dvb_ref[h, keys, :])
        o1, rinv1 = _weighted(_exp_scores(_dot_nt(qt, dka_ref[2 * h + 1, keys, :])),
                              dvb_ref[h, keys, :])
        o = o0 * rinv0 - o1 * (lam * rinv1)
        o_diff.append(_rms(o, subln) * (1.0 - lam_init))
    return jnp.concatenate(o_mla, axis=1), jnp.concatenate(o_diff, axis=1)


def _proj(wl, w, scr, r0, r1):
    return _dot_nt(scr["hb"][...], w["w_in_t"][wl, r0 - R_CB:r1 - R_CB, :])


def _layer_front(l, wl, x, n_seq, mod, w, scr, rope, ctx, state):
    m_rows = x.shape[0]
    n_b = m_rows // n_seq
    n_ctx = 0 if ctx is None else ctx["ckv"].shape[0]
    nk = n_ctx + n_seq
    shift, scale, _ = mod
    hb_ref = scr["hb"]
    qb_ref, dqb_ref = scr["qb"], scr["dqb"]
    proj = functools.partial(_proj, wl, w, scr)
    kcat_ref, vm_ref, dka_ref, dvb_ref = scr["kcat"], scr["vm"], scr["dka"], scr["dvb"]
    w_ukv = w["w_ukv"]
    half = (_lane_mask(0, LANES // 2), _lane_mask(LANES // 2, LANES))
    new_keys, old_keys = pl.ds(n_ctx, n_seq), pl.ds(0, n_ctx)
    if rope is not None:
        rope_m = tuple(r[...] for r in rope[0])
        rope_d = tuple(r[...] for r in rope[1])

    def seq_rows(b):
        return slice(b * n_seq, (b + 1) * n_seq)

    gs = w["norm_g"][l:l + 1, :] * (1.0 + scale)
    ms = jnp.mean(x * x, axis=-1, keepdims=True)
    hb_ref[...] = (x * lax.rsqrt(ms + EPS) * gs + shift).astype(BF16)

    head = _dot_nt(hb_ref[...], w["w_head"][wl])

    cqn = _rms(head[:, 0:MLA_Q_LORA], w["q_norm"][l:l + 1, :]).astype(BF16)
    q = _dot(cqn, w["w_uq"][wl]) * MLA_SCALE
    if rope is not None:
        q = jnp.concatenate(
            [q[:, 0:2 * LANES], _rope(q[:, 2 * LANES:3 * LANES], rope_m, MLA_ROPE // 4)], axis=1)
    qb_ref[...] = q.astype(BF16)

    def fill_kv(b, dst, kv):
        for h in range(MLA_HEADS):
            t, hh = divmod(h, 2)
            kcat_ref[b, h, dst, 0:LANES] = (kv[:, LANES * t:LANES * (t + 1)] * half[hh]).astype(BF16)
            vm_ref[b, h, dst, 0:LANES] = (
                kv[:, 2 * LANES + LANES * t:2 * LANES + LANES * (t + 1)] * half[hh]).astype(BF16)

    def fill_kr(b, dst, kr):
        for h in range(MLA_HEADS):
            kcat_ref[b, h, dst, LANES:2 * LANES] = (
                kr * _lane_mask(MLA_ROPE * h, MLA_ROPE * (h + 1))).astype(BF16)

    def fill_dk(b, dst, dk):
        for h in range(DIFF_HEADS):
            for a in range(2):
                dka_ref[b, 2 * h + a, dst, :] = (
                    dk[:, LANES * h:LANES * (h + 1)] * half[a]).astype(BF16)

    ckv = _rms(head[:, R_KV:R_KR], w["kv_norm"][l:l + 1, :])
    kv = _dot(ckv.astype(BF16), w_ukv[wl])
    for b in range(n_b):
        if state is not None:
            state[0][b, l] = ckv[seq_rows(b)]
        fill_kv(b, new_keys, kv[seq_rows(b)])
        if ctx is not None:
            fill_kv(b, old_keys, _dot(ctx["ckv"][...].astype(BF16), w_ukv[wl]))

    krr = head[:, R_KR:R_KR + LANES]
    krr_pos = krr if rope is None else _rope(krr, rope_m, MLA_ROPE // 4)
    for b in range(n_b):
        if state is not None:
            state[1][b, l] = krr[seq_rows(b)].T[0:MLA_ROPE, :]
        fill_kr(b, new_keys, krr_pos[seq_rows(b)])
        if ctx is not None:
            fill_kr(b, old_keys, jnp.concatenate([ctx["kr"][...]] * (LANES // MLA_ROPE), axis=0).T)

    dq = proj(R_DQ, R_DK) * DIFF_SCALE
    if rope is not None:
        dq = _rope(dq, rope_d, DIFF_QK // 4)
    dqb_ref[...] = dq.astype(BF16)

    dk = proj(R_DK, R_DV)
    dk_pos = dk if rope is None else _rope(dk, rope_d, DIFF_QK // 4)
    for b in range(n_b):
        if state is not None:
            state[2][b, l] = dk[seq_rows(b)].T
        fill_dk(b, new_keys, dk_pos[seq_rows(b)])
        if ctx is not None:
            fill_dk(b, old_keys, ctx["dk"][...].T)

    dv = proj(R_DV, R_Z)
    for b in range(n_b):
        for h in range(DIFF_HEADS):
            dv_h = dv[seq_rows(b), LANES * h:LANES * (h + 1)]
            if state is not None:
                state[3][b, l, pl.ds(h, n_seq, stride=DIFF_HEADS), :] = dv_h
            dvb_ref[b, h, new_keys, 0:LANES] = dv_h.astype(BF16)
            if ctx is not None:
                dvb_ref[b, h, old_keys, 0:LANES] = (
                    ctx["dv"][pl.ds(h, n_ctx, stride=DIFF_HEADS), :].astype(BF16))
            ones = jnp.ones((nk, LANES), BF16)
            dvb_ref[b, h, :, LANES:2 * LANES] = ones
            vm_ref[b, h, :, LANES:2 * LANES] = ones


def _attend_block(l, b, start, tq, nk, w, scr):
    lf = w["lam"][l]
    lam = (jnp.exp(jnp.sum(lf[0:1] * lf[1:2], axis=-1, keepdims=True))
           - jnp.exp(jnp.sum(lf[2:3] * lf[3:4], axis=-1, keepdims=True)) + _lambda_init(l))
    qrows = pl.ds(start, tq)
    return _attend(scr["qb"][qrows, :], scr["dqb"][qrows, :], pl.ds(0, nk),
                   scr["kcat"].at[b], scr["vm"].at[b], scr["dka"].at[b], scr["dvb"].at[b],
                   lam, w["subln"][l:l + 1, :], _lambda_init(l))


def _conv_branch(l, wl, m_rows, n_seq, w, scr):
    conv = _proj(wl, w, scr, R_CB, R_DQ)
    u = conv[:, W_CONV:2 * W_CONV] * conv[:, 2 * W_CONV:3 * W_CONV]
    pos = lax.broadcasted_iota(jnp.int32, (m_rows, 1), 0) % n_seq
    u_prev = jnp.where(pos == 0, 0.0, pltpu.roll(u, 1, 0))
    u_next = jnp.where(pos == n_seq - 1, 0.0, pltpu.roll(u, m_rows - 1, 0))
    cw = w["conv_w"][:, l, :]
    return conv[:, 0:W_CONV] * (u_prev * cw[0:1] + u * cw[1:2] + u_next * cw[2:3])


def _silu_gate(wl, w, scr, c0, c1):
    z = _proj(wl, w, scr, R_Z + c0, R_Z + c1)
    return z * jax.nn.sigmoid(z)


def _trunk_layer(l, x, n_seq, mod, w, scr, state):
    m_rows = x.shape[0]
    o_ref, mix_ref = scr["o"], scr["mix"]
    _layer_front(l, l, x, n_seq, mod, w, scr, None, None, state)
    for b in range(m_rows // n_seq):
        for start in range(b * n_seq, (b + 1) * n_seq, Q_TILE):
            o_mla, o_diff = _attend_block(l, b, start, Q_TILE, n_seq, w, scr)
            o_ref[start:start + Q_TILE, 0:W_MLA] = o_mla
            o_ref[start:start + Q_TILE, W_MLA + W_CONV:D_MIX] = o_diff
    o_ref[:, W_MLA:W_MLA + W_CONV] = _conv_branch(l, l, m_rows, n_seq, w, scr)
    for c0 in range(0, D_MIX, 512):
        mix_ref[:, c0:c0 + 512] = (
            o_ref[:, c0:c0 + 512] * _silu_gate(l, w, scr, c0, c0 + 512)).astype(BF16)
    return x + mod[2] * _dot(mix_ref[...], w["w_out"][l])


_W_NAMES = ("norm_g", "w_in_t", "w_head", "q_norm", "w_uq", "kv_norm", "w_ukv", "conv_w", "lam",
            "subln", "w_out", "final_norm")
_PER_LAYER_BLOCKS = ("w_in_t", "w_head", "w_uq", "w_ukv", "w_out")
_SCR_NAMES = ("hb", "o", "mix", "qb", "dqb", "kcat", "vm", "dka", "dvb")
_CTX_NAMES = ("ckv", "kr", "dk", "dv")
N_ROPE = 4


_BIG_NAMES = ("w_in_t", "w_head", "w_out")
_SMALL_NAMES = tuple(n for n in _W_NAMES if n not in _BIG_NAMES)
W_IN_CHUNK = 208
STAGE_ROWS = 256
assert D_IN % W_IN_CHUNK == 0 and R_CB % W_IN_CHUNK == 0 and R_CB - R_KR <= W_IN_CHUNK


def _stage_weights(w_in_hbm, w_out_hbm, big, slots, sem):
    jobs = []
    for l in range(DEPTH):
        jobs += [(w_in_hbm, big["w_in_t"], l, r, W_IN_CHUNK) for r in range(0, D_IN, W_IN_CHUNK)]
        jobs += [(w_out_hbm, big["w_out"], l, r, STAGE_ROWS) for r in range(0, D_MIX, STAGE_ROWS)]
    n_slots = len(slots)

    def copy(k):
        src, _, l, r, n = jobs[k]
        return pltpu.make_async_copy(src.at[l, pl.ds(r, n)], slots[k % n_slots].at[pl.ds(0, n)],
                                     sem.at[k % n_slots])

    for k in range(n_slots):
        copy(k).start()
    head = big["w_head"]
    for k, (src, dst, l, r, n) in enumerate(jobs):
        copy(k).wait()
        slot = slots[k % n_slots]
        if src is w_out_hbm:
            dst[l, r:r + n, :] = slot[0:n, :].astype(BF16)
        elif r >= R_CB:
            dst[l, r - R_CB:r - R_CB + n, :] = slot[0:n, :].astype(BF16)
        else:
            n_main = min(r + n, R_KR) - r
            head[l, r:r + n_main, :] = slot[0:n_main, :].astype(BF16)
            if r + n > R_KR:
                k_rope = slot[R_KR - r:R_CB - r, :].astype(BF16)
                for i in range(LANES // MLA_ROPE):
                    head[l, R_KR + MLA_ROPE * i:R_KR + MLA_ROPE * (i + 1), :] = k_rope
        if k + n_slots < len(jobs):
            copy(k + n_slots).start()


def _ctx_kernel(x_ref, mod_ref, *refs, n_seq):
    n_small, n_big, n_scr = len(_SMALL_NAMES), len(_BIG_NAMES), len(_SCR_NAMES)
    w = dict(zip(_SMALL_NAMES, refs[:n_small]))
    w_in_hbm, w_out_hbm = refs[n_small:n_small + 2]
    rest = refs[n_small + 2:]
    y_ref, state = rest[0], rest[1:5]
    big_hbm = rest[5:5 + n_big]
    scr = dict(zip(_SCR_NAMES, rest[5 + n_big:5 + n_big + n_scr]))
    big = dict(zip(_BIG_NAMES, rest[5 + n_big + n_scr:5 + 2 * n_big + n_scr]))
    in_sem, out_sem = rest[5 + 2 * n_big + n_scr:]
    w.update(big)

    def hand_off(k):
        return pltpu.make_async_copy(big[_BIG_NAMES[k]], big_hbm[k], out_sem.at[k])

    first_step = pl.program_id(0) == 0

    @pl.when(first_step)
    def _():
        slots = [scr["o"].at[pl.ds(r, STAGE_ROWS)] for r in range(0, scr["o"].shape[0], STAGE_ROWS)]
        slots += [y_ref.at[b] for b in range(y_ref.shape[0])]
        _stage_weights(w_in_hbm, w_out_hbm, big, slots, in_sem)
        for k in range(n_big):
            hand_off(k).start()

    n_b = x_ref.shape[0]
    x = x_ref[...].reshape(n_b * n_seq, D_MODEL)
    for l in range(DEPTH):
        mod = tuple(mod_ref[l, j, 0:1, :] for j in range(3))
        x = _trunk_layer(l, x, n_seq, mod, w, scr, state)
    y_ref[...] = _rms(x, w["final_norm"][...]).reshape(n_b, n_seq, D_MODEL)

    @pl.when(first_step)
    def _():
        for k in range(n_big):
            hand_off(k).wait()


def _lat_kernel(x_ref, mod_ref, *refs, n_seq, layer, q_tile):
    ctx = {k: r.at[0, 0] for k, r in zip(_CTX_NAMES, refs[:4])}
    rope = (refs[4:6], refs[6:8])
    w = dict(zip(_W_NAMES, refs[4 + N_ROPE:4 + N_ROPE + len(_W_NAMES)]))
    rest = refs[4 + N_ROPE + len(_W_NAMES):]
    y_ref = rest[0]
    scr = dict(zip(_SCR_NAMES, rest[1:]))
    cond_row = pl.ds(1 + pl.program_id(0), 1)
    mod = tuple(mod_ref[layer, j, cond_row, :] for j in range(3))
    zg_ref, mix_ref = scr["o"], scr["mix"]
    n_ctx = ctx["ckv"].shape[0]
    conv_cols = slice(W_MLA, W_MLA + W_CONV)

    @pl.when(pl.program_id(1) == 0)
    def _():
        _layer_front(layer, 0, x_ref[0], n_seq, mod, w, scr, rope, ctx, None)
        for c0 in range(0, D_MIX, 512):
            zg_ref[:, c0:c0 + 512] = _silu_gate(0, w, scr, c0, c0 + 512)
        mix_ref[:, conv_cols] = (
            _conv_branch(layer, 0, n_seq, n_seq, w, scr) * zg_ref[:, conv_cols]).astype(BF16)

    start = pl.multiple_of(pl.program_id(1) * q_tile, q_tile)
    qrows = pl.ds(start, q_tile)
    o_mla, o_diff = _attend_block(layer, 0, start, q_tile, n_ctx + n_seq, w, scr)
    mix_ref[qrows, 0:W_MLA] = (o_mla * zg_ref[qrows, 0:W_MLA]).astype(BF16)
    mix_ref[qrows, W_MLA + W_CONV:D_MIX] = (
        o_diff * zg_ref[qrows, W_MLA + W_CONV:D_MIX]).astype(BF16)
    x = x_ref[0, qrows, :] + mod[2] * _dot(mix_ref[qrows, :], w["w_out"][0])
    if layer == DEPTH - 1:
        x = _rms(x, w["final_norm"][...])
    y_ref[0] = x


def _prep_kernel(cctx_ref, c_ref, ada_w_ref, ada_b_ref, w_uq_ref, w_ukv_ref,
                 mod_ref, w_uq_p_ref, w_ukv_p_ref):
    n_rows = mod_ref.shape[2]
    pad = jnp.zeros((n_rows - 1 - c_ref.shape[0], D_MODEL), F32)
    c = jnp.concatenate([cctx_ref[...], c_ref[...], pad], axis=0)
    s = (c * jax.nn.sigmoid(c)).astype(BF16)
    mod_ref[0, 0] = (_dot(s, ada_w_ref[0].astype(BF16))
                     + ada_b_ref[pl.ds(pl.program_id(0), 1), :])

    @pl.when(pl.program_id(1) == 0)
    def _():
        for src, dst, first in ((w_uq_ref, w_uq_p_ref, MLA_NOPE), (w_ukv_ref, w_ukv_p_ref, MLA_NOPE)):
            a = src[0]
            per = a.shape[1] // MLA_HEADS
            cols = ([a[:, per * h:per * h + first] for h in range(MLA_HEADS)]
                    + [a[:, per * h + first:per * (h + 1)] for h in range(MLA_HEADS)])
            dst[0] = jnp.concatenate(cols, axis=1).astype(BF16)


def _const_spec(shape, layer=None):
    if layer is None:
        zeros = (0,) * len(shape)
        return pl.BlockSpec(shape, lambda *_: zeros, pipeline_mode=pl.Buffered(1))
    idx = (layer,) + (0,) * (len(shape) - 1)
    return pl.BlockSpec((1,) + tuple(shape[1:]), lambda *_: idx, pipeline_mode=pl.Buffered(1))


def _scratch(n_b, n_seq, nk):
    m_rows = n_b * n_seq
    return [
        pltpu.VMEM((m_rows, D_MODEL), BF16),
        pltpu.VMEM((m_rows, D_MIX), F32),
        pltpu.VMEM((m_rows, D_MIX), BF16),
        pltpu.VMEM((m_rows, 3 * LANES), BF16),
        pltpu.VMEM((m_rows, W_DIFF), BF16),
        pltpu.VMEM((n_b, MLA_HEADS, nk, 2 * LANES), BF16),
        pltpu.VMEM((n_b, MLA_HEADS, nk, 2 * LANES), BF16),
        pltpu.VMEM((n_b, 2 * DIFF_HEADS, nk, LANES), BF16),
        pltpu.VMEM((n_b, DIFF_HEADS, nk, 2 * LANES), BF16),
    ]


def _rope_tables(n, rot_dim):
    half = rot_dim // 2
    inv = ROPE_THETA ** (-(np.arange(0, half, 2, dtype=np.float32) / half))
    pos = np.arange(n)
    ang_r = (pos // GRID_W).astype(np.float32)[:, None] * inv
    ang_c = (pos % GRID_W).astype(np.float32)[:, None] * inv
    ang = np.concatenate([ang_r, ang_r, ang_c, ang_c], axis=-1)
    cos, sin = np.cos(ang), np.sin(ang)
    q = rot_dim // 4
    first = (np.arange(rot_dim) % (2 * q)) < q
    sin_signed = np.where(first, -sin, sin)
    reps = LANES // rot_dim
    return tuple(jnp.asarray(np.tile(t, (1, reps)), dtype=F32) for t in (cos, sin_signed))


def kernel(x_prompt, x_sample, c, cache_mla_ckv, cache_mla_krope, cache_diff_k, cache_diff_v,
           c_ctx, norm_g, ada_w, ada_b, w_in, mla_q_norm, w_uq, mla_kv_norm, w_ukv,
           conv_w, diff_lambda, diff_subln, w_out, final_norm):
    batch, seq, _ = x_prompt.shape
    dec_batch, dec_seq, _ = x_sample.shape
    past = cache_mla_ckv.shape[2]
    cparams = pltpu.CompilerParams(dimension_semantics=("arbitrary",),
                                   vmem_limit_bytes=VMEM_LIMIT_BYTES)

    n_cond = 8
    per_layer_spec = lambda a: pl.BlockSpec((1,) + a.shape[1:], lambda l, j: (l, 0, 0))
    mod, w_uq_p, w_ukv_p = pl.pallas_call(
        _prep_kernel,
        grid=(DEPTH, 3),
        in_specs=[pl.BlockSpec((1, D_MODEL), lambda l, j: (0, 0)),
                  pl.BlockSpec((dec_batch, D_MODEL), lambda l, j: (0, 0)),
                  pl.BlockSpec((1, D_MODEL, D_MODEL), lambda l, j: (l, 0, j)),
                  pl.BlockSpec((DEPTH, D_MODEL), lambda l, j: (0, j)),
                  per_layer_spec(w_uq), per_layer_spec(w_ukv)],
        out_specs=[pl.BlockSpec((1, 1, n_cond, D_MODEL), lambda l, j: (l, j, 0, 0)),
                   per_layer_spec(w_uq), per_layer_spec(w_ukv)],
        out_shape=[jax.ShapeDtypeStruct((DEPTH, 3, n_cond, D_MODEL), F32),
                   jax.ShapeDtypeStruct(w_uq.shape, BF16),
                   jax.ShapeDtypeStruct(w_ukv.shape, BF16)],
        compiler_params=pltpu.CompilerParams(dimension_semantics=("arbitrary", "arbitrary")),
        name="prep",
    )(c_ctx.reshape(1, D_MODEL), c, ada_w, ada_b, w_uq, w_ukv)

    tb = 2
    assert seq == STAGE_ROWS and (tb * seq) % STAGE_ROWS == 0
    n_slots = tb * seq // STAGE_ROWS + tb
    small = dict(zip(_SMALL_NAMES, (
        norm_g, mla_q_norm, w_uq_p, mla_kv_norm, w_ukv_p, jnp.swapaxes(conv_w, 0, 1), diff_lambda,
        diff_subln, final_norm.reshape(1, D_MODEL))))
    big_shapes = dict(zip(_BIG_NAMES, ((DEPTH, D_IN - R_CB, D_MODEL), (DEPTH, R_KR + LANES, D_MODEL),
                                       (DEPTH, D_MIX, D_MODEL))))
    hbm_spec = pl.BlockSpec(memory_space=pl.ANY)
    state_shapes = ((seq, MLA_KV_LORA), (MLA_ROPE, seq), (W_DIFF, seq),
                    (DIFF_HEADS * seq, DIFF_V))
    y_prompt, s_ckv, s_kr, s_dk, s_dv, w_in_t, w_head, w_out_b = pl.pallas_call(
        functools.partial(_ctx_kernel, n_seq=seq),
        grid=(batch // tb,),
        in_specs=[pl.BlockSpec((tb, seq, D_MODEL), lambda i: (i, 0, 0)), _const_spec(mod.shape)]
        + [_const_spec(a.shape) for a in small.values()] + [hbm_spec, hbm_spec],
        out_specs=[pl.BlockSpec((tb, seq, D_MODEL), lambda i: (i, 0, 0))]
        + [pl.BlockSpec((tb, DEPTH) + s, lambda i: (i, 0, 0, 0)) for s in state_shapes]
        + [hbm_spec] * len(_BIG_NAMES),
        out_shape=[jax.ShapeDtypeStruct((batch, seq, D_MODEL), F32)]
        + [jax.ShapeDtypeStruct((batch, DEPTH) + s, F32) for s in state_shapes]
        + [jax.ShapeDtypeStruct(big_shapes[n], BF16) for n in _BIG_NAMES],
        scratch_shapes=_scratch(tb, seq, seq)
        + [pltpu.VMEM(big_shapes[n], BF16) for n in _BIG_NAMES]
        + [pltpu.SemaphoreType.DMA((n_slots,)), pltpu.SemaphoreType.DMA((len(_BIG_NAMES),))],
        compiler_params=cparams,
        name="ctx",
    )(x_prompt, mod, *small.values(), jnp.swapaxes(w_in, 1, 2), w_out)

    weights = tuple(dict(small, w_in_t=w_in_t, w_head=w_head, w_out=w_out_b)[n] for n in _W_NAMES)

    tables = _rope_tables(dec_seq, MLA_ROPE) + _rope_tables(dec_seq, DIFF_QK)
    ctx_in = (cache_mla_ckv,
              jnp.swapaxes(cache_mla_krope, 2, 3),
              jnp.transpose(cache_diff_k, (0, 1, 3, 4, 5, 2)).reshape(dec_batch, DEPTH, W_DIFF, past),
              cache_diff_v.reshape(dec_batch, DEPTH, past * DIFF_HEADS, DIFF_V))
    h_lat = x_sample
    for l in range(DEPTH):
        q_tile = LAT_LAST_Q_TILE if l == DEPTH - 1 else LAT_Q_TILE
        lat_params = pltpu.CompilerParams(
            dimension_semantics=("arbitrary", "arbitrary"),
            vmem_limit_bytes=VMEM_LIMIT_BYTES if q_tile <= 256 else BIG_TILE_VMEM_LIMIT_BYTES)
        h_lat = pl.pallas_call(
            functools.partial(_lat_kernel, n_seq=dec_seq, layer=l, q_tile=q_tile),
            grid=(dec_batch, dec_seq // q_tile),
            in_specs=[pl.BlockSpec((1, dec_seq, D_MODEL), lambda i, j: (i, 0, 0)),
                      _const_spec(mod.shape)]
            + [pl.BlockSpec((1, 1) + a.shape[2:], lambda i, j, l=l: (i, l, 0, 0))
               for a in ctx_in]
            + [_const_spec(t.shape) for t in tables]
            + [_const_spec(a.shape, layer=l if name in _PER_LAYER_BLOCKS else None)
               for name, a in zip(_W_NAMES, weights)],
            out_specs=pl.BlockSpec((1, q_tile, D_MODEL), lambda i, j: (i, j, 0)),
            out_shape=jax.ShapeDtypeStruct((dec_batch, dec_seq, D_MODEL), F32),
            scratch_shapes=_scratch(1, dec_seq, past + dec_seq),
            compiler_params=lat_params,
            name=f"latent_{l}",
        )(h_lat, mod, *ctx_in, *tables, *weights)

    return (y_prompt, h_lat, s_ckv,
            jnp.swapaxes(s_kr, 2, 3),
            jnp.transpose(s_dk.reshape(batch, DEPTH, DIFF_HEADS, 2, DIFF_QK, seq),
                          (0, 1, 5, 2, 3, 4)),
            s_dv.reshape(batch, DEPTH, seq, DIFF_HEADS, DIFF_V))
```

```python
import functools
import math

import numpy as np
import jax
import jax.numpy as jnp
from jax import lax
from jax.experimental import pallas as pl
from jax.experimental.pallas import tpu as pltpu

D_MODEL = 1024
DEPTH = 2
GRID_W = 64
MLA_HEADS = 4
MLA_NOPE = 64
MLA_ROPE = 32
MLA_V = 64
MLA_Q_LORA = 256
MLA_KV_LORA = 128
W_MLA = MLA_HEADS * MLA_V
W_CONV = 256
DIFF_HEADS = 4
DIFF_QK = 64
DIFF_V = 2 * DIFF_QK
W_DIFF = DIFF_HEADS * DIFF_V
D_MIX = W_MLA + W_CONV + W_DIFF
ROPE_THETA = 10000.0
EPS = 1e-6

LANES = 128
Q_TILE = 256
LAT_Q_TILE = 256
LOG2E = math.log2(math.e)
MLA_SCALE = (MLA_NOPE + MLA_ROPE) ** -0.5 * LOG2E
DIFF_SCALE = DIFF_QK ** -0.5 * LOG2E
VMEM_LIMIT_BYTES = 54 * 1024 * 1024

R_Q = 0
R_KV = R_Q + MLA_Q_LORA
R_KR = R_KV + MLA_KV_LORA
R_CB = R_KR + MLA_ROPE
R_CC = R_CB + W_CONV
R_CX = R_CC + W_CONV
R_DQ = R_CX + W_CONV
R_DK = R_DQ + W_DIFF
R_DV = R_DK + W_DIFF
R_Z = R_DV + W_DIFF
D_IN = R_Z + D_MIX

BF16 = jnp.bfloat16
F32 = jnp.float32


def _lambda_init(layer):
    return 0.8 - 0.6 * math.exp(-0.3 * layer)


def _dot(a, b):
    return jnp.dot(a, b, preferred_element_type=F32)


def _dot_nt(a, b):
    return lax.dot_general(a, b, (((1,), (1,)), ((), ())), preferred_element_type=F32)


def _rms(x, g):
    ms = jnp.mean(x * x, axis=-1, keepdims=True)
    return x * lax.rsqrt(ms + EPS) * g


def _exp_scores(s):
    m = s[:, 0:LANES]
    for t in range(1, s.shape[1] // LANES):
        m = jnp.maximum(m, s[:, LANES * t:LANES * (t + 1)])
    return jnp.exp2(s - jnp.max(m, axis=-1, keepdims=True)).astype(BF16)


def _weighted(e, v_ones):
    r = _dot(e, v_ones)
    return r[:, 0:LANES], 1.0 / r[:, LANES:2 * LANES]


def _lane_mask(lo, hi):
    lane = lax.broadcasted_iota(jnp.int32, (1, LANES), 1)
    return jnp.where((lane >= lo) & (lane < hi), 1.0, 0.0).astype(F32)


def _rope(x, tables, q):
    cos, sin_signed = tables
    lane = lax.broadcasted_iota(jnp.int32, (1, LANES), 1)
    first = (lane & (2 * q - 1)) < q
    tiles = []
    for t in range(x.shape[1] // LANES):
        xt = x[:, LANES * t:LANES * (t + 1)]
        rot = jnp.where(first, pltpu.roll(xt, LANES - q, 1), pltpu.roll(xt, q, 1))
        tiles.append(xt * cos + rot * sin_signed)
    return tiles[0] if len(tiles) == 1 else jnp.concatenate(tiles, axis=1)


def _attend(q, dq, keys, kcat_ref, vm_ref, dka_ref, dvb_ref, lam, subln, lam_init):
    qr = q[:, 2 * LANES:3 * LANES]
    o_mla = []
    for t in range(2):
        qcat = jnp.concatenate([q[:, LANES * t:LANES * (t + 1)], qr], axis=1)
        acc = None
        for hh in range(2):
            h = 2 * t + hh
            o, rinv = _weighted(_exp_scores(_dot_nt(qcat, kcat_ref[h, keys, :])),
                                vm_ref[h, keys, :])
            acc = o * rinv if acc is None else acc + o * rinv
        o_mla.append(acc)
    o_diff = []
    for h in range(DIFF_HEADS):
        qt = dq[:, LANES * h:LANES * (h + 1)]
        o0, rinv0 = _weighted(_exp_scores(_dot_nt(qt, dka_ref[2 * h, keys, :])),
                              dvb_ref[h, keys, :])
        o1, rinv1 = _weighted(_exp_scores(_dot_nt(qt, dka_ref[2 * h + 1, keys, :])),
                              dvb_ref[h, keys, :])
        o = o0 * rinv0 - o1 * (lam * rinv1)
        o_diff.append(_rms(o, subln) * (1.0 - lam_init))
    return jnp.concatenate(o_mla, axis=1), jnp.concatenate(o_diff, axis=1)


def _proj(wl, w, scr, r0, r1):
    return _dot_nt(scr["hb"][...], w["w_in_t"][wl, r0 - R_CB:r1 - R_CB, :])


def _layer_front(l, wl, x, n_seq, mod, w, scr, rope, ctx, state):
    m_rows = x.shape[0]
    n_b = m_rows // n_seq
    n_ctx = 0 if ctx is None else ctx["ckv"].shape[0]
    nk = n_ctx + n_seq
    shift, scale, _ = mod
    hb_ref = scr["hb"]
    qb_ref, dqb_ref = scr["qb"], scr["dqb"]
    proj = functools.partial(_proj, wl, w, scr)
    kcat_ref, vm_ref, dka_ref, dvb_ref = scr["kcat"], scr["vm"], scr["dka"], scr["dvb"]
    w_ukv = w["w_ukv"]
    half = (_lane_mask(0, LANES // 2), _lane_mask(LANES // 2, LANES))
    new_keys, old_keys = pl.ds(n_ctx, n_seq), pl.ds(0, n_ctx)
    if rope is not None:
        rope_m = tuple(r[...] for r in rope[0])
        rope_d = tuple(r[...] for r in rope[1])

    def seq_rows(b):
        return slice(b * n_seq, (b + 1) * n_seq)

    gs = w["norm_g"][l:l + 1, :] * (1.0 + scale)
    ms = jnp.mean(x * x, axis=-1, keepdims=True)
    hb_ref[...] = (x * lax.rsqrt(ms + EPS) * gs + shift).astype(BF16)

    head = _dot_nt(hb_ref[...], w["w_head"][wl])

    cqn = _rms(head[:, 0:MLA_Q_LORA], w["q_norm"][l:l + 1, :]).astype(BF16)
    q = _dot(cqn, w["w_uq"][wl]) * MLA_SCALE
    if rope is not None:
        q = jnp.concatenate(
            [q[:, 0:2 * LANES], _rope(q[:, 2 * LANES:3 * LANES], rope_m, MLA_ROPE // 4)], axis=1)
    qb_ref[...] = q.astype(BF16)

    def fill_kv(b, dst, kv):
        for h in range(MLA_HEADS):
            t, hh = divmod(h, 2)
            kcat_ref[b, h, dst, 0:LANES] = (kv[:, LANES * t:LANES * (t + 1)] * half[hh]).astype(BF16)
            vm_ref[b, h, dst, 0:LANES] = (
                kv[:, 2 * LANES + LANES * t:2 * LANES + LANES * (t + 1)] * half[hh]).astype(BF16)

    def fill_kr(b, dst, kr):
        for h in range(MLA_HEADS):
            kcat_ref[b, h, dst, LANES:2 * LANES] = (
                kr * _lane_mask(MLA_ROPE * h, MLA_ROPE * (h + 1))).astype(BF16)

    def fill_dk(b, dst, dk):
        for h in range(DIFF_HEADS):
            for a in range(2):
                dka_ref[b, 2 * h + a, dst, :] = (
                    dk[:, LANES * h:LANES * (h + 1)] * half[a]).astype(BF16)

    ckv = _rms(head[:, R_KV:R_KR], w["kv_norm"][l:l + 1, :])
    kv = _dot(ckv.astype(BF16), w_ukv[wl])
    for b in range(n_b):
        if state is not None:
            state[0][b, l] = ckv[seq_rows(b)]
        fill_kv(b, new_keys, kv[seq_rows(b)])
        if ctx is not None:
            fill_kv(b, old_keys, _dot(ctx["ckv"][...].astype(BF16), w_ukv[wl]))

    krr = head[:, R_KR:R_KR + LANES]
    krr_pos = krr if rope is None else _rope(krr, rope_m, MLA_ROPE // 4)
    for b in range(n_b):
        if state is not None:
            state[1][b, l] = krr[seq_rows(b)].T[0:MLA_ROPE, :]
        fill_kr(b, new_keys, krr_pos[seq_rows(b)])
        if ctx is not None:
            fill_kr(b, old_keys, jnp.concatenate([ctx["kr"][...]] * (LANES // MLA_ROPE), axis=0).T)

    dq = proj(R_DQ, R_DK) * DIFF_SCALE
    if rope is not None:
        dq = _rope(dq, rope_d, DIFF_QK // 4)
    dqb_ref[...] = dq.astype(BF16)

    dk = proj(R_DK, R_DV)
    dk_pos = dk if rope is None else _rope(dk, rope_d, DIFF_QK // 4)
    for b in range(n_b):
        if state is not None:
            state[2][b, l] = dk[seq_rows(b)].T
        fill_dk(b, new_keys, dk_pos[seq_rows(b)])
        if ctx is not None:
            fill_dk(b, old_keys, ctx["dk"][...].T)

    dv = proj(R_DV, R_Z)
    for b in range(n_b):
        for h in range(DIFF_HEADS):
            dv_h = dv[seq_rows(b), LANES * h:LANES * (h + 1)]
            if state is not None:
                state[3][b, l, pl.ds(h, n_seq, stride=DIFF_HEADS), :] = dv_h
            dvb_ref[b, h, new_keys, 0:LANES] = dv_h.astype(BF16)
            if ctx is not None:
                dvb_ref[b, h, old_keys, 0:LANES] = (
                    ctx["dv"][pl.ds(h, n_ctx, stride=DIFF_HEADS), :].astype(BF16))
            ones = jnp.ones((nk, LANES), BF16)
            dvb_ref[b, h, :, LANES:2 * LANES] = ones
            vm_ref[b, h, :, LANES:2 * LANES] = ones


def _attend_block(l, b, start, tq, nk, w, scr):
    lf = w["lam"][l]
    lam = (jnp.exp(jnp.sum(lf[0:1] * lf[1:2], axis=-1, keepdims=True))
           - jnp.exp(jnp.sum(lf[2:3] * lf[3:4], axis=-1, keepdims=True)) + _lambda_init(l))
    qrows = pl.ds(start, tq)
    return _attend(scr["qb"][qrows, :], scr["dqb"][qrows, :], pl.ds(0, nk),
                   scr["kcat"].at[b], scr["vm"].at[b], scr["dka"].at[b], scr["dvb"].at[b],
                   lam, w["subln"][l:l + 1, :], _lambda_init(l))


def _conv_branch(l, wl, m_rows, n_seq, w, scr):
    conv = _proj(wl, w, scr, R_CB, R_DQ)
    u = conv[:, W_CONV:2 * W_CONV] * conv[:, 2 * W_CONV:3 * W_CONV]
    pos = lax.broadcasted_iota(jnp.int32, (m_rows, 1), 0) % n_seq
    u_prev = jnp.where(pos == 0, 0.0, pltpu.roll(u, 1, 0))
    u_next = jnp.where(pos == n_seq - 1, 0.0, pltpu.roll(u, m_rows - 1, 0))
    cw = w["conv_w"][:, l, :]
    return conv[:, 0:W_CONV] * (u_prev * cw[0:1] + u * cw[1:2] + u_next * cw[2:3])


def _silu_gate(wl, w, scr, c0, c1):
    z = _proj(wl, w, scr, R_Z + c0, R_Z + c1)
    return z * jax.nn.sigmoid(z)


def _trunk_layer(l, x, n_seq, mod, w, scr, state):
    m_rows = x.shape[0]
    o_ref, mix_ref = scr["o"], scr["mix"]
    _layer_front(l, l, x, n_seq, mod, w, scr, None, None, state)
    for b in range(m_rows // n_seq):
        for start in range(b * n_seq, (b + 1) * n_seq, Q_TILE):
            o_mla, o_diff = _attend_block(l, b, start, Q_TILE, n_seq, w, scr)
            o_ref[start:start + Q_TILE, 0:W_MLA] = o_mla
            o_ref[start:start + Q_TILE, W_MLA + W_CONV:D_MIX] = o_diff
    o_ref[:, W_MLA:W_MLA + W_CONV] = _conv_branch(l, l, m_rows, n_seq, w, scr)
    for c0 in range(0, D_MIX, 512):
        mix_ref[:, c0:c0 + 512] = (
            o_ref[:, c0:c0 + 512] * _silu_gate(l, w, scr, c0, c0 + 512)).astype(BF16)
    return x + mod[2] * _dot(mix_ref[...], w["w_out"][l])


_W_NAMES = ("norm_g", "w_in_t", "w_head", "q_norm", "w_uq", "kv_norm", "w_ukv", "conv_w", "lam",
            "subln", "w_out", "final_norm")
_PER_LAYER_BLOCKS = ("w_in_t", "w_head", "w_uq", "w_ukv", "w_out")
_SCR_NAMES = ("hb", "o", "mix", "qb", "dqb", "kcat", "vm", "dka", "dvb")
_CTX_NAMES = ("ckv", "kr", "dk", "dv")
N_ROPE = 4


_BIG_NAMES = ("w_in_t", "w_head", "w_out")
_SMALL_NAMES = tuple(n for n in _W_NAMES if n not in _BIG_NAMES)
W_IN_CHUNK = 208
STAGE_ROWS = 256
assert D_IN % W_IN_CHUNK == 0 and R_CB % W_IN_CHUNK == 0 and R_CB - R_KR <= W_IN_CHUNK


def _stage_weights(w_in_hbm, w_out_hbm, big, slots, sem):
    jobs = []
    for l in range(DEPTH):
        jobs += [(w_in_hbm, big["w_in_t"], l, r, W_IN_CHUNK) for r in range(0, D_IN, W_IN_CHUNK)]
        jobs += [(w_out_hbm, big["w_out"], l, r, STAGE_ROWS) for r in range(0, D_MIX, STAGE_ROWS)]
    n_slots = len(slots)

    def copy(k):
        src, _, l, r, n = jobs[k]
        return pltpu.make_async_copy(src.at[l, pl.ds(r, n)], slots[k % n_slots].at[pl.ds(0, n)],
                                     sem.at[k % n_slots])

    for k in range(n_slots):
        copy(k).start()
    head = big["w_head"]
    for k, (src, dst, l, r, n) in enumerate(jobs):
        copy(k).wait()
        slot = slots[k % n_slots]
        if src is w_out_hbm:
            dst[l, r:r + n, :] = slot[0:n, :].astype(BF16)
        elif r >= R_CB:
            dst[l, r - R_CB:r - R_CB + n, :] = slot[0:n, :].astype(BF16)
        else:
            n_main = min(r + n, R_KR) - r
            head[l, r:r + n_main, :] = slot[0:n_main, :].astype(BF16)
            if r + n > R_KR:
                k_rope = slot[R_KR - r:R_CB - r, :].astype(BF16)
                for i in range(LANES // MLA_ROPE):
                    head[l, R_KR + MLA_ROPE * i:R_KR + MLA_ROPE * (i + 1), :] = k_rope
        if k + n_slots < len(jobs):
            copy(k + n_slots).start()


def _ctx_kernel(x_ref, mod_ref, *refs, n_seq):
    n_small, n_big, n_scr = len(_SMALL_NAMES), len(_BIG_NAMES), len(_SCR_NAMES)
    w = dict(zip(_SMALL_NAMES, refs[:n_small]))
    w_in_hbm, w_out_hbm = refs[n_small:n_small + 2]
    rest = refs[n_small + 2:]
    y_ref, state = rest[0], rest[1:5]
    big_hbm = rest[5:5 + n_big]
    scr = dict(zip(_SCR_NAMES, rest[5 + n_big:5 + n_big + n_scr]))
    big = dict(zip(_BIG_NAMES, rest[5 + n_big + n_scr:5 + 2 * n_big + n_scr]))
    in_sem, out_sem = rest[5 + 2 * n_big + n_scr:]
    w.update(big)

    def hand_off(k):
        return pltpu.make_async_copy(big[_BIG_NAMES[k]], big_hbm[k], out_sem.at[k])

    first_step = pl.program_id(0) == 0

    @pl.when(first_step)
    def _():
        slots = [scr["o"].at[pl.ds(r, STAGE_ROWS)] for r in range(0, scr["o"].shape[0], STAGE_ROWS)]
        slots += [y_ref.at[b] for b in range(y_ref.shape[0])]
        _stage_weights(w_in_hbm, w_out_hbm, big, slots, in_sem)
        for k in range(n_big):
            hand_off(k).start()

    n_b = x_ref.shape[0]
    x = x_ref[...].reshape(n_b * n_seq, D_MODEL)
    for l in range(DEPTH):
        mod = tuple(mod_ref[l, j, 0:1, :] for j in range(3))
        x = _trunk_layer(l, x, n_seq, mod, w, scr, state)
    y_ref[...] = _rms(x, w["final_norm"][...]).reshape(n_b, n_seq, D_MODEL)

    @pl.when(first_step)
    def _():
        for k in range(n_big):
            hand_off(k).wait()


def _lat_kernel(x_ref, mod_ref, *refs, n_seq, layer):
    ctx = {k: r.at[0, 0] for k, r in zip(_CTX_NAMES, refs[:4])}
    rope = (refs[4:6], refs[6:8])
    w = dict(zip(_W_NAMES, refs[4 + N_ROPE:4 + N_ROPE + len(_W_NAMES)]))
    rest = refs[4 + N_ROPE + len(_W_NAMES):]
    y_ref = rest[0]
    scr = dict(zip(_SCR_NAMES, rest[1:]))
    cond_row = pl.ds(1 + pl.program_id(0), 1)
    mod = tuple(mod_ref[layer, j, cond_row, :] for j in range(3))
    zg_ref, mix_ref = scr["o"], scr["mix"]
    n_ctx = ctx["ckv"].shape[0]
    conv_cols = slice(W_MLA, W_MLA + W_CONV)

    @pl.when(pl.program_id(1) == 0)
    def _():
        _layer_front(layer, 0, x_ref[0], n_seq, mod, w, scr, rope, ctx, None)
        for c0 in range(0, D_MIX, 512):
            zg_ref[:, c0:c0 + 512] = _silu_gate(0, w, scr, c0, c0 + 512)
        mix_ref[:, conv_cols] = (
            _conv_branch(layer, 0, n_seq, n_seq, w, scr) * zg_ref[:, conv_cols]).astype(BF16)

    start = pl.multiple_of(pl.program_id(1) * LAT_Q_TILE, LAT_Q_TILE)
    qrows = pl.ds(start, LAT_Q_TILE)
    o_mla, o_diff = _attend_block(layer, 0, start, LAT_Q_TILE, n_ctx + n_seq, w, scr)
    mix_ref[qrows, 0:W_MLA] = (o_mla * zg_ref[qrows, 0:W_MLA]).astype(BF16)
    mix_ref[qrows, W_MLA + W_CONV:D_MIX] = (
        o_diff * zg_ref[qrows, W_MLA + W_CONV:D_MIX]).astype(BF16)
    x = x_ref[0, qrows, :] + mod[2] * _dot(mix_ref[qrows, :], w["w_out"][0])
    if layer == DEPTH - 1:
        x = _rms(x, w["final_norm"][...])
    y_ref[0] = x


def _prep_kernel(cctx_ref, c_ref, ada_w_ref, ada_b_ref, w_uq_ref, w_ukv_ref,
                 mod_ref, w_uq_p_ref, w_ukv_p_ref):
    n_rows = mod_ref.shape[2]
    pad = jnp.zeros((n_rows - 1 - c_ref.shape[0], D_MODEL), F32)
    c = jnp.concatenate([cctx_ref[...], c_ref[...], pad], axis=0)
    s = (c * jax.nn.sigmoid(c)).astype(BF16)
    mod_ref[0, 0] = (_dot(s, ada_w_ref[0].astype(BF16))
                     + ada_b_ref[pl.ds(pl.program_id(0), 1), :])

    @pl.when(pl.program_id(1) == 0)
    def _():
        for src, dst, first in ((w_uq_ref, w_uq_p_ref, MLA_NOPE), (w_ukv_ref, w_ukv_p_ref, MLA_NOPE)):
            a = src[0]
            per = a.shape[1] // MLA_HEADS
            cols = ([a[:, per * h:per * h + first] for h in range(MLA_HEADS)]
                    + [a[:, per * h + first:per * (h + 1)] for h in range(MLA_HEADS)])
            dst[0] = jnp.concatenate(cols, axis=1).astype(BF16)


def _const_spec(shape, layer=None):
    if layer is None:
        zeros = (0,) * len(shape)
        return pl.BlockSpec(shape, lambda *_: zeros, pipeline_mode=pl.Buffered(1))
    idx = (layer,) + (0,) * (len(shape) - 1)
    return pl.BlockSpec((1,) + tuple(shape[1:]), lambda *_: idx, pipeline_mode=pl.Buffered(1))


def _scratch(n_b, n_seq, nk):
    m_rows = n_b * n_seq
    return [
        pltpu.VMEM((m_rows, D_MODEL), BF16),
        pltpu.VMEM((m_rows, D_MIX), F32),
        pltpu.VMEM((m_rows, D_MIX), BF16),
        pltpu.VMEM((m_rows, 3 * LANES), BF16),
        pltpu.VMEM((m_rows, W_DIFF), BF16),
        pltpu.VMEM((n_b, MLA_HEADS, nk, 2 * LANES), BF16),
        pltpu.VMEM((n_b, MLA_HEADS, nk, 2 * LANES), BF16),
        pltpu.VMEM((n_b, 2 * DIFF_HEADS, nk, LANES), BF16),
        pltpu.VMEM((n_b, DIFF_HEADS, nk, 2 * LANES), BF16),
    ]


def _rope_tables(n, rot_dim):
    half = rot_dim // 2
    inv = ROPE_THETA ** (-(np.arange(0, half, 2, dtype=np.float32) / half))
    pos = np.arange(n)
    ang_r = (pos // GRID_W).astype(np.float32)[:, None] * inv
    ang_c = (pos % GRID_W).astype(np.float32)[:, None] * inv
    ang = np.concatenate([ang_r, ang_r, ang_c, ang_c], axis=-1)
    cos, sin = np.cos(ang), np.sin(ang)
    q = rot_dim // 4
    first = (np.arange(rot_dim) % (2 * q)) < q
    sin_signed = np.where(first, -sin, sin)
    reps = LANES // rot_dim
    return tuple(jnp.asarray(np.tile(t, (1, reps)), dtype=F32) for t in (cos, sin_signed))


def kernel(x_prompt, x_sample, c, cache_mla_ckv, cache_mla_krope, cache_diff_k, cache_diff_v,
           c_ctx, norm_g, ada_w, ada_b, w_in, mla_q_norm, w_uq, mla_kv_norm, w_ukv,
           conv_w, diff_lambda, diff_subln, w_out, final_norm):
    batch, seq, _ = x_prompt.shape
    dec_batch, dec_seq, _ = x_sample.shape
    past = cache_mla_ckv.shape[2]
    cparams = pltpu.CompilerParams(dimension_semantics=("arbitrary",),
                                   vmem_limit_bytes=VMEM_LIMIT_BYTES)

    n_cond = 8
    per_layer_spec = lambda a: pl.BlockSpec((1,) + a.shape[1:], lambda l, j: (l, 0, 0))
    mod, w_uq_p, w_ukv_p = pl.pallas_call(
        _prep_kernel,
        grid=(DEPTH, 3),
        in_specs=[pl.BlockSpec((1, D_MODEL), lambda l, j: (0, 0)),
                  pl.BlockSpec((dec_batch, D_MODEL), lambda l, j: (0, 0)),
                  pl.BlockSpec((1, D_MODEL, D_MODEL), lambda l, j: (l, 0, j)),
                  pl.BlockSpec((DEPTH, D_MODEL), lambda l, j: (0, j)),
                  per_layer_spec(w_uq), per_layer_spec(w_ukv)],
        out_specs=[pl.BlockSpec((1, 1, n_cond, D_MODEL), lambda l, j: (l, j, 0, 0)),
                   per_layer_spec(w_uq), per_layer_spec(w_ukv)],
        out_shape=[jax.ShapeDtypeStruct((DEPTH, 3, n_cond, D_MODEL), F32),
                   jax.ShapeDtypeStruct(w_uq.shape, BF16),
                   jax.ShapeDtypeStruct(w_ukv.shape, BF16)],
        compiler_params=pltpu.CompilerParams(dimension_semantics=("arbitrary", "arbitrary")),
        name="prep",
    )(c_ctx.reshape(1, D_MODEL), c, ada_w, ada_b, w_uq, w_ukv)

    tb = 2
    assert seq == STAGE_ROWS and (tb * seq) % STAGE_ROWS == 0
    n_slots = tb * seq // STAGE_ROWS + tb
    small = dict(zip(_SMALL_NAMES, (
        norm_g, mla_q_norm, w_uq_p, mla_kv_norm, w_ukv_p, jnp.swapaxes(conv_w, 0, 1), diff_lambda,
        diff_subln, final_norm.reshape(1, D_MODEL))))
    big_shapes = dict(zip(_BIG_NAMES, ((DEPTH, D_IN - R_CB, D_MODEL), (DEPTH, R_KR + LANES, D_MODEL),
                                       (DEPTH, D_MIX, D_MODEL))))
    hbm_spec = pl.BlockSpec(memory_space=pl.ANY)
    state_shapes = ((seq, MLA_KV_LORA), (MLA_ROPE, seq), (W_DIFF, seq),
                    (DIFF_HEADS * seq, DIFF_V))
    y_prompt, s_ckv, s_kr, s_dk, s_dv, w_in_t, w_head, w_out_b = pl.pallas_call(
        functools.partial(_ctx_kernel, n_seq=seq),
        grid=(batch // tb,),
        in_specs=[pl.BlockSpec((tb, seq, D_MODEL), lambda i: (i, 0, 0)), _const_spec(mod.shape)]
        + [_const_spec(a.shape) for a in small.values()] + [hbm_spec, hbm_spec],
        out_specs=[pl.BlockSpec((tb, seq, D_MODEL), lambda i: (i, 0, 0))]
        + [pl.BlockSpec((tb, DEPTH) + s, lambda i: (i, 0, 0, 0)) for s in state_shapes]
        + [hbm_spec] * len(_BIG_NAMES),
        out_shape=[jax.ShapeDtypeStruct((batch, seq, D_MODEL), F32)]
        + [jax.ShapeDtypeStruct((batch, DEPTH) + s, F32) for s in state_shapes]
        + [jax.ShapeDtypeStruct(big_shapes[n], BF16) for n in _BIG_NAMES],
        scratch_shapes=_scratch(tb, seq, seq)
        + [pltpu.VMEM(big_shapes[n], BF16) for n in _BIG_NAMES]
        + [pltpu.SemaphoreType.DMA((n_slots,)), pltpu.SemaphoreType.DMA((len(_BIG_NAMES),))],
        compiler_params=cparams,
        name="ctx",
    )(x_prompt, mod, *small.values(), jnp.swapaxes(w_in, 1, 2), w_out)

    weights = tuple(dict(small, w_in_t=w_in_t, w_head=w_head, w_out=w_out_b)[n] for n in _W_NAMES)

    lat_params = pltpu.CompilerParams(dimension_semantics=("arbitrary", "arbitrary"),
                                      vmem_limit_bytes=VMEM_LIMIT_BYTES)
    tables = _rope_tables(dec_seq, MLA_ROPE) + _rope_tables(dec_seq, DIFF_QK)
    ctx_in = (cache_mla_ckv,
              jnp.swapaxes(cache_mla_krope, 2, 3),
              jnp.transpose(cache_diff_k, (0, 1, 3, 4, 5, 2)).reshape(dec_batch, DEPTH, W_DIFF, past),
              cache_diff_v.reshape(dec_batch, DEPTH, past * DIFF_HEADS, DIFF_V))
    h_lat = x_sample
    for l in range(DEPTH):
        h_lat = pl.pallas_call(
            functools.partial(_lat_kernel, n_seq=dec_seq, layer=l),
            grid=(dec_batch, dec_seq // LAT_Q_TILE),
            in_specs=[pl.BlockSpec((1, dec_seq, D_MODEL), lambda i, j: (i, 0, 0)),
                      _const_spec(mod.shape)]
            + [pl.BlockSpec((1, 1) + a.shape[2:], lambda i, j, l=l: (i, l, 0, 0))
               for a in ctx_in]
            + [_const_spec(t.shape) for t in tables]
            + [_const_spec(a.shape, layer=l if name in _PER_LAYER_BLOCKS else None)
               for name, a in zip(_W_NAMES, weights)],
            out_specs=pl.BlockSpec((1, LAT_Q_TILE, D_MODEL), lambda i, j: (i, j, 0)),
            out_shape=jax.ShapeDtypeStruct((dec_batch, dec_seq, D_MODEL), F32),
            scratch_shapes=_scratch(1, dec_seq, past + dec_seq),
            compiler_params=lat_params,
            name=f"latent_{l}",
        )(h_lat, mod, *ctx_in, *tables, *weights)

    return (y_prompt, h_lat, s_ckv,
            jnp.swapaxes(s_kr, 2, 3),
            jnp.transpose(s_dk.reshape(batch, DEPTH, DIFF_HEADS, 2, DIFF_QK, seq),
                          (0, 1, 5, 2, 3, 4)),
            s_dv.reshape(batch, DEPTH, seq, DIFF_HEADS, DIFF_V))
```

```python
import functools
import math

import numpy as np
import jax
import jax.numpy as jnp
from jax import lax
from jax.experimental import pallas as pl
from jax.experimental.pallas import tpu as pltpu

D_MODEL = 1024
DEPTH = 2
GRID_W = 64
MLA_HEADS = 4
MLA_NOPE = 64
MLA_ROPE = 32
MLA_V = 64
MLA_Q_LORA = 256
MLA_KV_LORA = 128
W_MLA = MLA_HEADS * MLA_V
W_CONV = 256
DIFF_HEADS = 4
DIFF_QK = 64
DIFF_V = 2 * DIFF_QK
W_DIFF = DIFF_HEADS * DIFF_V
D_MIX = W_MLA + W_CONV + W_DIFF
ROPE_THETA = 10000.0
EPS = 1e-6

LANES = 128
Q_TILE = 256
LAT_Q_TILE = 256
LOG2E = math.log2(math.e)
MLA_SCALE = (MLA_NOPE + MLA_ROPE) ** -0.5 * LOG2E
DIFF_SCALE = DIFF_QK ** -0.5 * LOG2E
VMEM_LIMIT_BYTES = 54 * 1024 * 1024

R_Q = 0
R_KV = R_Q + MLA_Q_LORA
R_KR = R_KV + MLA_KV_LORA
R_CB = R_KR + MLA_ROPE
R_CC = R_CB + W_CONV
R_CX = R_CC + W_CONV
R_DQ = R_CX + W_CONV
R_DK = R_DQ + W_DIFF
R_DV = R_DK + W_DIFF
R_Z = R_DV + W_DIFF
D_IN = R_Z + D_MIX

BF16 = jnp.bfloat16
F32 = jnp.float32


def _lambda_init(layer):
    return 0.8 - 0.6 * math.exp(-0.3 * layer)


def _dot(a, b):
    return jnp.dot(a, b, preferred_element_type=F32)


def _dot_nt(a, b):
    return lax.dot_general(a, b, (((1,), (1,)), ((), ())), preferred_element_type=F32)


def _rms(x, g):
    ms = jnp.mean(x * x, axis=-1, keepdims=True)
    return x * lax.rsqrt(ms + EPS) * g


def _exp_scores(s):
    m = s[:, 0:LANES]
    for t in range(1, s.shape[1] // LANES):
        m = jnp.maximum(m, s[:, LANES * t:LANES * (t + 1)])
    return jnp.exp2(s - jnp.max(m, axis=-1, keepdims=True)).astype(BF16)


def _weighted(e, v_ones):
    r = _dot(e, v_ones)
    return r[:, 0:LANES], 1.0 / r[:, LANES:2 * LANES]


def _lane_mask(lo, hi):
    lane = lax.broadcasted_iota(jnp.int32, (1, LANES), 1)
    return jnp.where((lane >= lo) & (lane < hi), 1.0, 0.0).astype(F32)


def _rope(x, tables, q):
    cos, sin_signed = tables
    lane = lax.broadcasted_iota(jnp.int32, (1, LANES), 1)
    first = (lane & (2 * q - 1)) < q
    tiles = []
    for t in range(x.shape[1] // LANES):
        xt = x[:, LANES * t:LANES * (t + 1)]
        rot = jnp.where(first, pltpu.roll(xt, LANES - q, 1), pltpu.roll(xt, q, 1))
        tiles.append(xt * cos + rot * sin_signed)
    return tiles[0] if len(tiles) == 1 else jnp.concatenate(tiles, axis=1)


def _attend(q, dq, keys, kcat_ref, vm_ref, dka_ref, dvb_ref, lam, subln, lam_init):
    qr = q[:, 2 * LANES:3 * LANES]
    o_mla = []
    for t in range(2):
        qcat = jnp.concatenate([q[:, LANES * t:LANES * (t + 1)], qr], axis=1)
        acc = None
        for hh in range(2):
            h = 2 * t + hh
            o, rinv = _weighted(_exp_scores(_dot_nt(qcat, kcat_ref[h, keys, :])),
                                vm_ref[h, keys, :])
            acc = o * rinv if acc is None else acc + o * rinv
        o_mla.append(acc)
    o_diff = []
    for h in range(DIFF_HEADS):
        qt = dq[:, LANES * h:LANES * (h + 1)]
        o0, rinv0 = _weighted(_exp_scores(_dot_nt(qt, dka_ref[2 * h, keys, :])),
                              dvb_ref[h, keys, :])
        o1, rinv1 = _weighted(_exp_scores(_dot_nt(qt, dka_ref[2 * h + 1, keys, :])),
                              dvb_ref[h, keys, :])
        o = o0 * rinv0 - o1 * (lam * rinv1)
        o_diff.append(_rms(o, subln) * (1.0 - lam_init))
    return jnp.concatenate(o_mla, axis=1), jnp.concatenate(o_diff, axis=1)


def _proj(wl, w, scr, r0, r1):
    return _dot_nt(scr["hb"][...], w["w_in_t"][wl, r0 - R_CB:r1 - R_CB, :])


def _layer_front(l, wl, x, n_seq, mod, w, scr, rope, ctx, state):
    m_rows = x.shape[0]
    n_b = m_rows // n_seq
    n_ctx = 0 if ctx is None else ctx["ckv"].shape[0]
    nk = n_ctx + n_seq
    shift, scale, _ = mod
    hb_ref = scr["hb"]
    qb_ref, dqb_ref = scr["qb"], scr["dqb"]
    proj = functools.partial(_proj, wl, w, scr)
    kcat_ref, vm_ref, dka_ref, dvb_ref = scr["kcat"], scr["vm"], scr["dka"], scr["dvb"]
    w_ukv = w["w_ukv"]
    half = (_lane_mask(0, LANES // 2), _lane_mask(LANES // 2, LANES))
    new_keys, old_keys = pl.ds(n_ctx, n_seq), pl.ds(0, n_ctx)
    if rope is not None:
        rope_m = tuple(r[...] for r in rope[0])
        rope_d = tuple(r[...] for r in rope[1])

    def seq_rows(b):
        return slice(b * n_seq, (b + 1) * n_seq)

    gs = w["norm_g"][l:l + 1, :] * (1.0 + scale)
    ms = jnp.mean(x * x, axis=-1, keepdims=True)
    hb_ref[...] = (x * lax.rsqrt(ms + EPS) * gs + shift).astype(BF16)

    head = _dot_nt(hb_ref[...], w["w_head"][wl])

    cqn = _rms(head[:, 0:MLA_Q_LORA], w["q_norm"][l:l + 1, :]).astype(BF16)
    q = _dot(cqn, w["w_uq"][wl]) * MLA_SCALE
    if rope is not None:
        q = jnp.concatenate(
            [q[:, 0:2 * LANES], _rope(q[:, 2 * LANES:3 * LANES], rope_m, MLA_ROPE // 4)], axis=1)
    qb_ref[...] = q.astype(BF16)

    def fill_kv(b, dst, kv):
        for h in range(MLA_HEADS):
            t, hh = divmod(h, 2)
            kcat_ref[b, h, dst, 0:LANES] = (kv[:, LANES * t:LANES * (t + 1)] * half[hh]).astype(BF16)
            vm_ref[b, h, dst, 0:LANES] = (
                kv[:, 2 * LANES + LANES * t:2 * LANES + LANES * (t + 1)] * half[hh]).astype(BF16)

    def fill_kr(b, dst, kr):
        for h in range(MLA_HEADS):
            kcat_ref[b, h, dst, LANES:2 * LANES] = (
                kr * _lane_mask(MLA_ROPE * h, MLA_ROPE * (h + 1))).astype(BF16)

    def fill_dk(b, dst, dk):
        for h in range(DIFF_HEADS):
            for a in range(2):
                dka_ref[b, 2 * h + a, dst, :] = (
                    dk[:, LANES * h:LANES * (h + 1)] * half[a]).astype(BF16)

    ckv = _rms(head[:, R_KV:R_KR], w["kv_norm"][l:l + 1, :])
    kv = _dot(ckv.astype(BF16), w_ukv[wl])
    for b in range(n_b):
        if state is not None:
            state[0][b, l] = ckv[seq_rows(b)]
        fill_kv(b, new_keys, kv[seq_rows(b)])
        if ctx is not None:
            fill_kv(b, old_keys, _dot(ctx["ckv"][...].astype(BF16), w_ukv[wl]))

    krr = head[:, R_KR:R_KR + LANES]
    krr_pos = krr if rope is None else _rope(krr, rope_m, MLA_ROPE // 4)
    for b in range(n_b):
        if state is not None:
            state[1][b, l] = krr[seq_rows(b)].T[0:MLA_ROPE, :]
        fill_kr(b, new_keys, krr_pos[seq_rows(b)])
        if ctx is not None:
            fill_kr(b, old_keys, jnp.concatenate([ctx["kr"][...]] * (LANES // MLA_ROPE), axis=0).T)

    dq = proj(R_DQ, R_DK) * DIFF_SCALE
    if rope is not None:
        dq = _rope(dq, rope_d, DIFF_QK // 4)
    dqb_ref[...] = dq.astype(BF16)

    dk = proj(R_DK, R_DV)
    dk_pos = dk if rope is None else _rope(dk, rope_d, DIFF_QK // 4)
    for b in range(n_b):
        if state is not None:
            state[2][b, l] = dk[seq_rows(b)].T
        fill_dk(b, new_keys, dk_pos[seq_rows(b)])
        if ctx is not None:
            fill_dk(b, old_keys, ctx["dk"][...].T)

    dv = proj(R_DV, R_Z)
    for b in range(n_b):
        for h in range(DIFF_HEADS):
            dv_h = dv[seq_rows(b), LANES * h:LANES * (h + 1)]
            if state is not None:
                state[3][b, l, pl.ds(h, n_seq, stride=DIFF_HEADS), :] = dv_h
            dvb_ref[b, h, new_keys, 0:LANES] = dv_h.astype(BF16)
            if ctx is not None:
                dvb_ref[b, h, old_keys, 0:LANES] = (
                    ctx["dv"][pl.ds(h, n_ctx, stride=DIFF_HEADS), :].astype(BF16))
            ones = jnp.ones((nk, LANES), BF16)
            dvb_ref[b, h, :, LANES:2 * LANES] = ones
            vm_ref[b, h, :, LANES:2 * LANES] = ones


def _attend_block(l, b, start, tq, nk, w, scr):
    lf = w["lam"][l]
    lam = (jnp.exp(jnp.sum(lf[0:1] * lf[1:2], axis=-1, keepdims=True))
           - jnp.exp(jnp.sum(lf[2:3] * lf[3:4], axis=-1, keepdims=True)) + _lambda_init(l))
    qrows = pl.ds(start, tq)
    return _attend(scr["qb"][qrows, :], scr["dqb"][qrows, :], pl.ds(0, nk),
                   scr["kcat"].at[b], scr["vm"].at[b], scr["dka"].at[b], scr["dvb"].at[b],
                   lam, w["subln"][l:l + 1, :], _lambda_init(l))


def _conv_branch(l, wl, m_rows, n_seq, w, scr):
    conv = _proj(wl, w, scr, R_CB, R_DQ)
    u = conv[:, W_CONV:2 * W_CONV] * conv[:, 2 * W_CONV:3 * W_CONV]
    pos = lax.broadcasted_iota(jnp.int32, (m_rows, 1), 0) % n_seq
    u_prev = jnp.where(pos == 0, 0.0, pltpu.roll(u, 1, 0))
    u_next = jnp.where(pos == n_seq - 1, 0.0, pltpu.roll(u, m_rows - 1, 0))
    cw = w["conv_w"][:, l, :]
    return conv[:, 0:W_CONV] * (u_prev * cw[0:1] + u * cw[1:2] + u_next * cw[2:3])


def _silu_gate(wl, w, scr, c0, c1):
    z = _proj(wl, w, scr, R_Z + c0, R_Z + c1)
    return z * jax.nn.sigmoid(z)


def _trunk_layer(l, x, n_seq, mod, w, scr, state):
    m_rows = x.shape[0]
    o_ref, mix_ref = scr["o"], scr["mix"]
    _layer_front(l, l, x, n_seq, mod, w, scr, None, None, state)
    for b in range(m_rows // n_seq):
        for start in range(b * n_seq, (b + 1) * n_seq, Q_TILE):
            o_mla, o_diff = _attend_block(l, b, start, Q_TILE, n_seq, w, scr)
            o_ref[start:start + Q_TILE, 0:W_MLA] = o_mla
            o_ref[start:start + Q_TILE, W_MLA + W_CONV:D_MIX] = o_diff
    o_ref[:, W_MLA:W_MLA + W_CONV] = _conv_branch(l, l, m_rows, n_seq, w, scr)
    for c0 in range(0, D_MIX, 512):
        mix_ref[:, c0:c0 + 512] = (
            o_ref[:, c0:c0 + 512] * _silu_gate(l, w, scr, c0, c0 + 512)).astype(BF16)
    return x + mod[2] * _dot(mix_ref[...], w["w_out"][l])


_W_NAMES = ("norm_g", "w_in_t", "w_head", "q_norm", "w_uq", "kv_norm", "w_ukv", "conv_w", "lam",
            "subln", "w_out", "final_norm")
_PER_LAYER_BLOCKS = ("w_in_t", "w_head", "w_uq", "w_ukv", "w_out")
_SCR_NAMES = ("hb", "o", "mix", "qb", "dqb", "kcat", "vm", "dka", "dvb")
_CTX_NAMES = ("ckv", "kr", "dk", "dv")
N_ROPE = 4


_BIG_NAMES = ("w_in_t", "w_head", "w_out")
_SMALL_NAMES = tuple(n for n in _W_NAMES if n not in _BIG_NAMES)
W_IN_CHUNK = 208
STAGE_ROWS = 256
assert D_IN % W_IN_CHUNK == 0 and R_CB % W_IN_CHUNK == 0 and R_CB - R_KR <= W_IN_CHUNK


def _stage_weights(w_in_hbm, w_out_hbm, big, slots, sem):
    jobs = []
    for l in range(DEPTH):
        jobs += [(w_in_hbm, big["w_in_t"], l, r, W_IN_CHUNK) for r in range(0, D_IN, W_IN_CHUNK)]
        jobs += [(w_out_hbm, big["w_out"], l, r, STAGE_ROWS) for r in range(0, D_MIX, STAGE_ROWS)]
    n_slots = len(slots)

    def copy(k):
        src, _, l, r, n = jobs[k]
        return pltpu.make_async_copy(src.at[l, pl.ds(r, n)], slots[k % n_slots].at[pl.ds(0, n)],
                                     sem.at[k % n_slots])

    for k in range(n_slots):
        copy(k).start(priority=k % 2)
    head = big["w_head"]
    for k, (src, dst, l, r, n) in enumerate(jobs):
        copy(k).wait()
        slot = slots[k % n_slots]
        if src is w_out_hbm:
            dst[l, r:r + n, :] = slot[0:n, :].astype(BF16)
        elif r >= R_CB:
            dst[l, r - R_CB:r - R_CB + n, :] = slot[0:n, :].astype(BF16)
        else:
            n_main = min(r + n, R_KR) - r
            head[l, r:r + n_main, :] = slot[0:n_main, :].astype(BF16)
            if r + n > R_KR:
                k_rope = slot[R_KR - r:R_CB - r, :].astype(BF16)
                for i in range(LANES // MLA_ROPE):
                    head[l, R_KR + MLA_ROPE * i:R_KR + MLA_ROPE * (i + 1), :] = k_rope
        if k + n_slots < len(jobs):
            copy(k + n_slots).start(priority=(k + n_slots) % 2)


def _ctx_kernel(x_ref, mod_ref, *refs, n_seq):
    n_small, n_big, n_scr = len(_SMALL_NAMES), len(_BIG_NAMES), len(_SCR_NAMES)
    w = dict(zip(_SMALL_NAMES, refs[:n_small]))
    w_in_hbm, w_out_hbm = refs[n_small:n_small + 2]
    rest = refs[n_small + 2:]
    y_ref, state = rest[0], rest[1:5]
    big_hbm = rest[5:5 + n_big]
    scr = dict(zip(_SCR_NAMES, rest[5 + n_big:5 + n_big + n_scr]))
    big = dict(zip(_BIG_NAMES, rest[5 + n_big + n_scr:5 + 2 * n_big + n_scr]))
    in_sem, out_sem = rest[5 + 2 * n_big + n_scr:]
    w.update(big)

    def hand_off(k):
        return pltpu.make_async_copy(big[_BIG_NAMES[k]], big_hbm[k], out_sem.at[k])

    first_step = pl.program_id(0) == 0

    @pl.when(first_step)
    def _():
        slots = [scr["o"].at[pl.ds(r, STAGE_ROWS)] for r in range(0, scr["o"].shape[0], STAGE_ROWS)]
        slots += [y_ref.at[b] for b in range(y_ref.shape[0])]
        _stage_weights(w_in_hbm, w_out_hbm, big, slots, in_sem)
        for k in range(n_big):
            hand_off(k).start()

    n_b = x_ref.shape[0]
    x = x_ref[...].reshape(n_b * n_seq, D_MODEL)
    for l in range(DEPTH):
        mod = tuple(mod_ref[l, j, 0:1, :] for j in range(3))
        x = _trunk_layer(l, x, n_seq, mod, w, scr, state)
    y_ref[...] = _rms(x, w["final_norm"][...]).reshape(n_b, n_seq, D_MODEL)

    @pl.when(first_step)
    def _():
        for k in range(n_big):
            hand_off(k).wait()


def _lat_kernel(x_ref, mod_ref, *refs, n_seq, layer):
    ctx = {k: r.at[0, 0] for k, r in zip(_CTX_NAMES, refs[:4])}
    rope = (refs[4:6], refs[6:8])
    w = dict(zip(_W_NAMES, refs[4 + N_ROPE:4 + N_ROPE + len(_W_NAMES)]))
    rest = refs[4 + N_ROPE + len(_W_NAMES):]
    y_ref = rest[0]
    scr = dict(zip(_SCR_NAMES, rest[1:]))
    cond_row = pl.ds(1 + pl.program_id(0), 1)
    mod = tuple(mod_ref[layer, j, cond_row, :] for j in range(3))
    zg_ref, mix_ref = scr["o"], scr["mix"]
    n_ctx = ctx["ckv"].shape[0]
    conv_cols = slice(W_MLA, W_MLA + W_CONV)

    @pl.when(pl.program_id(1) == 0)
    def _():
        _layer_front(layer, 0, x_ref[0], n_seq, mod, w, scr, rope, ctx, None)
        for c0 in range(0, D_MIX, 512):
            zg_ref[:, c0:c0 + 512] = _silu_gate(0, w, scr, c0, c0 + 512)
        mix_ref[:, conv_cols] = (
            _conv_branch(layer, 0, n_seq, n_seq, w, scr) * zg_ref[:, conv_cols]).astype(BF16)

    start = pl.multiple_of(pl.program_id(1) * LAT_Q_TILE, LAT_Q_TILE)
    qrows = pl.ds(start, LAT_Q_TILE)
    o_mla, o_diff = _attend_block(layer, 0, start, LAT_Q_TILE, n_ctx + n_seq, w, scr)
    mix_ref[qrows, 0:W_MLA] = (o_mla * zg_ref[qrows, 0:W_MLA]).astype(BF16)
    mix_ref[qrows, W_MLA + W_CONV:D_MIX] = (
        o_diff * zg_ref[qrows, W_MLA + W_CONV:D_MIX]).astype(BF16)
    x = x_ref[0, qrows, :] + mod[2] * _dot(mix_ref[qrows, :], w["w_out"][0])
    if layer == DEPTH - 1:
        x = _rms(x, w["final_norm"][...])
    y_ref[0] = x


def _prep_kernel(cctx_ref, c_ref, ada_w_ref, ada_b_ref, w_uq_ref, w_ukv_ref,
                 mod_ref, w_uq_p_ref, w_ukv_p_ref):
    n_rows = mod_ref.shape[2]
    pad = jnp.zeros((n_rows - 1 - c_ref.shape[0], D_MODEL), F32)
    c = jnp.concatenate([cctx_ref[...], c_ref[...], pad], axis=0)
    s = (c * jax.nn.sigmoid(c)).astype(BF16)
    mod_ref[0, 0] = (_dot(s, ada_w_ref[0].astype(BF16))
                     + ada_b_ref[pl.ds(pl.program_id(0), 1), :])

    @pl.when(pl.program_id(1) == 0)
    def _():
        for src, dst, first in ((w_uq_ref, w_uq_p_ref, MLA_NOPE), (w_ukv_ref, w_ukv_p_ref, MLA_NOPE)):
            a = src[0]
            per = a.shape[1] // MLA_HEADS
            cols = ([a[:, per * h:per * h + first] for h in range(MLA_HEADS)]
                    + [a[:, per * h + first:per * (h + 1)] for h in range(MLA_HEADS)])
            dst[0] = jnp.concatenate(cols, axis=1).astype(BF16)


def _const_spec(shape, layer=None):
    if layer is None:
        zeros = (0,) * len(shape)
        return pl.BlockSpec(shape, lambda *_: zeros, pipeline_mode=pl.Buffered(1))
    idx = (layer,) + (0,) * (len(shape) - 1)
    return pl.BlockSpec((1,) + tuple(shape[1:]), lambda *_: idx, pipeline_mode=pl.Buffered(1))


def _scratch(n_b, n_seq, nk):
    m_rows = n_b * n_seq
    return [
        pltpu.VMEM((m_rows, D_MODEL), BF16),
        pltpu.VMEM((m_rows, D_MIX), F32),
        pltpu.VMEM((m_rows, D_MIX), BF16),
        pltpu.VMEM((m_rows, 3 * LANES), BF16),
        pltpu.VMEM((m_rows, W_DIFF), BF16),
        pltpu.VMEM((n_b, MLA_HEADS, nk, 2 * LANES), BF16),
        pltpu.VMEM((n_b, MLA_HEADS, nk, 2 * LANES), BF16),
        pltpu.VMEM((n_b, 2 * DIFF_HEADS, nk, LANES), BF16),
        pltpu.VMEM((n_b, DIFF_HEADS, nk, 2 * LANES), BF16),
    ]


def _rope_tables(n, rot_dim):
    half = rot_dim // 2
    inv = ROPE_THETA ** (-(np.arange(0, half, 2, dtype=np.float32) / half))
    pos = np.arange(n)
    ang_r = (pos // GRID_W).astype(np.float32)[:, None] * inv
    ang_c = (pos % GRID_W).astype(np.float32)[:, None] * inv
    ang = np.concatenate([ang_r, ang_r, ang_c, ang_c], axis=-1)
    cos, sin = np.cos(ang), np.sin(ang)
    q = rot_dim // 4
    first = (np.arange(rot_dim) % (2 * q)) < q
    sin_signed = np.where(first, -sin, sin)
    reps = LANES // rot_dim
    return tuple(jnp.asarray(np.tile(t, (1, reps)), dtype=F32) for t in (cos, sin_signed))


def kernel(x_prompt, x_sample, c, cache_mla_ckv, cache_mla_krope, cache_diff_k, cache_diff_v,
           c_ctx, norm_g, ada_w, ada_b, w_in, mla_q_norm, w_uq, mla_kv_norm, w_ukv,
           conv_w, diff_lambda, diff_subln, w_out, final_norm):
    batch, seq, _ = x_prompt.shape
    dec_batch, dec_seq, _ = x_sample.shape
    past = cache_mla_ckv.shape[2]
    cparams = pltpu.CompilerParams(dimension_semantics=("arbitrary",),
                                   vmem_limit_bytes=VMEM_LIMIT_BYTES)

    n_cond = 8
    per_layer_spec = lambda a: pl.BlockSpec((1,) + a.shape[1:], lambda l, j: (l, 0, 0))
    mod, w_uq_p, w_ukv_p = pl.pallas_call(
        _prep_kernel,
        grid=(DEPTH, 3),
        in_specs=[pl.BlockSpec((1, D_MODEL), lambda l, j: (0, 0)),
                  pl.BlockSpec((dec_batch, D_MODEL), lambda l, j: (0, 0)),
                  pl.BlockSpec((1, D_MODEL, D_MODEL), lambda l, j: (l, 0, j)),
                  pl.BlockSpec((DEPTH, D_MODEL), lambda l, j: (0, j)),
                  per_layer_spec(w_uq), per_layer_spec(w_ukv)],
        out_specs=[pl.BlockSpec((1, 1, n_cond, D_MODEL), lambda l, j: (l, j, 0, 0)),
                   per_layer_spec(w_uq), per_layer_spec(w_ukv)],
        out_shape=[jax.ShapeDtypeStruct((DEPTH, 3, n_cond, D_MODEL), F32),
                   jax.ShapeDtypeStruct(w_uq.shape, BF16),
                   jax.ShapeDtypeStruct(w_ukv.shape, BF16)],
        compiler_params=pltpu.CompilerParams(dimension_semantics=("arbitrary", "arbitrary")),
        name="prep",
    )(c_ctx.reshape(1, D_MODEL), c, ada_w, ada_b, w_uq, w_ukv)

    tb = 2
    assert seq == STAGE_ROWS and (tb * seq) % STAGE_ROWS == 0
    n_slots = tb * seq // STAGE_ROWS + tb
    small = dict(zip(_SMALL_NAMES, (
        norm_g, mla_q_norm, w_uq_p, mla_kv_norm, w_ukv_p, jnp.swapaxes(conv_w, 0, 1), diff_lambda,
        diff_subln, final_norm.reshape(1, D_MODEL))))
    big_shapes = dict(zip(_BIG_NAMES, ((DEPTH, D_IN - R_CB, D_MODEL), (DEPTH, R_KR + LANES, D_MODEL),
                                       (DEPTH, D_MIX, D_MODEL))))
    hbm_spec = pl.BlockSpec(memory_space=pl.ANY)
    state_shapes = ((seq, MLA_KV_LORA), (MLA_ROPE, seq), (W_DIFF, seq),
                    (DIFF_HEADS * seq, DIFF_V))
    y_prompt, s_ckv, s_kr, s_dk, s_dv, w_in_t, w_head, w_out_b = pl.pallas_call(
        functools.partial(_ctx_kernel, n_seq=seq),
        grid=(batch // tb,),
        in_specs=[pl.BlockSpec((tb, seq, D_MODEL), lambda i: (i, 0, 0)), _const_spec(mod.shape)]
        + [_const_spec(a.shape) for a in small.values()] + [hbm_spec, hbm_spec],
        out_specs=[pl.BlockSpec((tb, seq, D_MODEL), lambda i: (i, 0, 0))]
        + [pl.BlockSpec((tb, DEPTH) + s, lambda i: (i, 0, 0, 0)) for s in state_shapes]
        + [hbm_spec] * len(_BIG_NAMES),
        out_shape=[jax.ShapeDtypeStruct((batch, seq, D_MODEL), F32)]
        + [jax.ShapeDtypeStruct((batch, DEPTH) + s, F32) for s in state_shapes]
        + [jax.ShapeDtypeStruct(big_shapes[n], BF16) for n in _BIG_NAMES],
        scratch_shapes=_scratch(tb, seq, seq)
        + [pltpu.VMEM(big_shapes[n], BF16) for n in _BIG_NAMES]
        + [pltpu.SemaphoreType.DMA((n_slots,)), pltpu.SemaphoreType.DMA((len(_BIG_NAMES),))],
        compiler_params=cparams,
        name="ctx",
    )(x_prompt, mod, *small.values(), jnp.swapaxes(w_in, 1, 2), w_out)

    weights = tuple(dict(small, w_in_t=w_in_t, w_head=w_head, w_out=w_out_b)[n] for n in _W_NAMES)

    lat_params = pltpu.CompilerParams(dimension_semantics=("arbitrary", "arbitrary"),
                                      vmem_limit_bytes=VMEM_LIMIT_BYTES)
    tables = _rope_tables(dec_seq, MLA_ROPE) + _rope_tables(dec_seq, DIFF_QK)
    ctx_in = (cache_mla_ckv,
              jnp.swapaxes(cache_mla_krope, 2, 3),
              jnp.transpose(cache_diff_k, (0, 1, 3, 4, 5, 2)).reshape(dec_batch, DEPTH, W_DIFF, past),
              cache_diff_v.reshape(dec_batch, DEPTH, past * DIFF_HEADS, DIFF_V))
    h_lat = x_sample
    for l in range(DEPTH):
        h_lat = pl.pallas_call(
            functools.partial(_lat_kernel, n_seq=dec_seq, layer=l),
            grid=(dec_batch, dec_seq // LAT_Q_TILE),
            in_specs=[pl.BlockSpec((1, dec_seq, D_MODEL), lambda i, j: (i, 0, 0)),
                      _const_spec(mod.shape)]
            + [pl.BlockSpec((1, 1) + a.shape[2:], lambda i, j, l=l: (i, l, 0, 0))
               for a in ctx_in]
            + [_const_spec(t.shape) for t in tables]
            + [_const_spec(a.shape, layer=l if name in _PER_LAYER_BLOCKS else None)
               for name, a in zip(_W_NAMES, weights)],
            out_specs=pl.BlockSpec((1, LAT_Q_TILE, D_MODEL), lambda i, j: (i, j, 0)),
            out_shape=jax.ShapeDtypeStruct((dec_batch, dec_seq, D_MODEL), F32),
            scratch_shapes=_scratch(1, dec_seq, past + dec_seq),
            compiler_params=lat_params,
            name=f"latent_{l}",
        )(h_lat, mod, *ctx_in, *tables, *weights)

    return (y_prompt, h_lat, s_ckv,
            jnp.swapaxes(s_kr, 2, 3),
            jnp.transpose(s_dk.reshape(batch, DEPTH, DIFF_HEADS, 2, DIFF_QK, seq),
                          (0, 1, 5, 2, 3, 4)),
            s_dv.reshape(batch, DEPTH, seq, DIFF_HEADS, DIFF_V))
```

```python
import functools
import math

import numpy as np
import jax
import jax.numpy as jnp
from jax import lax
from jax.experimental import pallas as pl
from jax.experimental.pallas import tpu as pltpu

D_MODEL = 1024
DEPTH = 2
GRID_W = 64
MLA_HEADS = 4
MLA_NOPE = 64
MLA_ROPE = 32
MLA_V = 64
MLA_Q_LORA = 256
MLA_KV_LORA = 128
W_MLA = MLA_HEADS * MLA_V
W_CONV = 256
DIFF_HEADS = 4
DIFF_QK = 64
DIFF_V = 2 * DIFF_QK
W_DIFF = DIFF_HEADS * DIFF_V
D_MIX = W_MLA + W_CONV + W_DIFF
ROPE_THETA = 10000.0
EPS = 1e-6

LANES = 128
Q_TILE = 256
LAT_Q_TILE = 256
LOG2E = math.log2(math.e)
MLA_SCALE = (MLA_NOPE + MLA_ROPE) ** -0.5 * LOG2E
DIFF_SCALE = DIFF_QK ** -0.5 * LOG2E
VMEM_LIMIT_BYTES = 54 * 1024 * 1024

R_Q = 0
R_KV = R_Q + MLA_Q_LORA
R_KR = R_KV + MLA_KV_LORA
R_CB = R_KR + MLA_ROPE
R_CC = R_CB + W_CONV
R_CX = R_CC + W_CONV
R_DQ = R_CX + W_CONV
R_DK = R_DQ + W_DIFF
R_DV = R_DK + W_DIFF
R_Z = R_DV + W_DIFF
D_IN = R_Z + D_MIX

BF16 = jnp.bfloat16
F32 = jnp.float32


def _lambda_init(layer):
    return 0.8 - 0.6 * math.exp(-0.3 * layer)


def _dot(a, b):
    return jnp.dot(a, b, preferred_element_type=F32)


def _dot_nt(a, b):
    return lax.dot_general(a, b, (((1,), (1,)), ((), ())), preferred_element_type=F32)


def _rms(x, g):
    ms = jnp.mean(x * x, axis=-1, keepdims=True)
    return x * lax.rsqrt(ms + EPS) * g


def _exp_scores(s):
    m = s[:, 0:LANES]
    for t in range(1, s.shape[1] // LANES):
        m = jnp.maximum(m, s[:, LANES * t:LANES * (t + 1)])
    return jnp.exp2(s - jnp.max(m, axis=-1, keepdims=True)).astype(BF16)


def _weighted(e, v_ones):
    r = _dot(e, v_ones)
    return r[:, 0:LANES], 1.0 / r[:, LANES:2 * LANES]


def _lane_mask(lo, hi):
    lane = lax.broadcasted_iota(jnp.int32, (1, LANES), 1)
    return jnp.where((lane >= lo) & (lane < hi), 1.0, 0.0).astype(F32)


def _rope(x, tables, q):
    cos, sin_signed = tables
    lane = lax.broadcasted_iota(jnp.int32, (1, LANES), 1)
    first = (lane & (2 * q - 1)) < q
    tiles = []
    for t in range(x.shape[1] // LANES):
        xt = x[:, LANES * t:LANES * (t + 1)]
        rot = jnp.where(first, pltpu.roll(xt, LANES - q, 1), pltpu.roll(xt, q, 1))
        tiles.append(xt * cos + rot * sin_signed)
    return tiles[0] if len(tiles) == 1 else jnp.concatenate(tiles, axis=1)


def _attend(q, dq, keys, kcat_ref, vm_ref, dka_ref, dvb_ref, lam, subln, lam_init):
    qr = q[:, 2 * LANES:3 * LANES]
    o_mla = []
    for t in range(2):
        qcat = jnp.concatenate([q[:, LANES * t:LANES * (t + 1)], qr], axis=1)
        acc = None
        for hh in range(2):
            h = 2 * t + hh
            o, rinv = _weighted(_exp_scores(_dot_nt(qcat, kcat_ref[h, keys, :])),
                                vm_ref[h, keys, :])
            acc = o * rinv if acc is None else acc + o * rinv
        o_mla.append(acc)
    o_diff = []
    for h in range(DIFF_HEADS):
        qt = dq[:, LANES * h:LANES * (h + 1)]
        o0, rinv0 = _weighted(_exp_scores(_dot_nt(qt, dka_ref[2 * h, keys, :])),
                              dvb_ref[h, keys, :])
        o1, rinv1 = _weighted(_exp_scores(_dot_nt(qt, dka_ref[2 * h + 1, keys, :])),
                              dvb_ref[h, keys, :])
        o = o0 * rinv0 - o1 * (lam * rinv1)
        o_diff.append(_rms(o, subln) * (1.0 - lam_init))
    return jnp.concatenate(o_mla, axis=1), jnp.concatenate(o_diff, axis=1)


def _proj(wl, w, scr, r0, r1):
    return _dot_nt(scr["hb"][...], w["w_in_t"][wl, r0 - R_CB:r1 - R_CB, :])


def _layer_front(l, wl, x, n_seq, mod, w, scr, rope, ctx, state):
    m_rows = x.shape[0]
    n_b = m_rows // n_seq
    n_ctx = 0 if ctx is None else ctx["ckv"].shape[0]
    nk = n_ctx + n_seq
    shift, scale, _ = mod
    hb_ref = scr["hb"]
    qb_ref, dqb_ref = scr["qb"], scr["dqb"]
    proj = functools.partial(_proj, wl, w, scr)
    kcat_ref, vm_ref, dka_ref, dvb_ref = scr["kcat"], scr["vm"], scr["dka"], scr["dvb"]
    w_ukv = w["w_ukv"]
    half = (_lane_mask(0, LANES // 2), _lane_mask(LANES // 2, LANES))
    new_keys, old_keys = pl.ds(n_ctx, n_seq), pl.ds(0, n_ctx)
    if rope is not None:
        rope_m = tuple(r[...] for r in rope[0])
        rope_d = tuple(r[...] for r in rope[1])

    def seq_rows(b):
        return slice(b * n_seq, (b + 1) * n_seq)

    gs = w["norm_g"][l:l + 1, :] * (1.0 + scale)
    ms = jnp.mean(x * x, axis=-1, keepdims=True)
    hb_ref[...] = (x * lax.rsqrt(ms + EPS) * gs + shift).astype(BF16)

    head = _dot_nt(hb_ref[...], w["w_head"][wl])

    cqn = _rms(head[:, 0:MLA_Q_LORA], w["q_norm"][l:l + 1, :]).astype(BF16)
    q = _dot(cqn, w["w_uq"][wl]) * MLA_SCALE
    if rope is not None:
        q = jnp.concatenate(
            [q[:, 0:2 * LANES], _rope(q[:, 2 * LANES:3 * LANES], rope_m, MLA_ROPE // 4)], axis=1)
    qb_ref[...] = q.astype(BF16)

    def fill_kv(b, dst, kv):
        for h in range(MLA_HEADS):
            t, hh = divmod(h, 2)
            kcat_ref[b, h, dst, 0:LANES] = (kv[:, LANES * t:LANES * (t + 1)] * half[hh]).astype(BF16)
            vm_ref[b, h, dst, 0:LANES] = (
                kv[:, 2 * LANES + LANES * t:2 * LANES + LANES * (t + 1)] * half[hh]).astype(BF16)

    def fill_kr(b, dst, kr):
        for h in range(MLA_HEADS):
            kcat_ref[b, h, dst, LANES:2 * LANES] = (
                kr * _lane_mask(MLA_ROPE * h, MLA_ROPE * (h + 1))).astype(BF16)

    def fill_dk(b, dst, dk):
        for h in range(DIFF_HEADS):
            for a in range(2):
                dka_ref[b, 2 * h + a, dst, :] = (
                    dk[:, LANES * h:LANES * (h + 1)] * half[a]).astype(BF16)

    ckv = _rms(head[:, R_KV:R_KR], w["kv_norm"][l:l + 1, :])
    kv = _dot(ckv.astype(BF16), w_ukv[wl])
    for b in range(n_b):
        if state is not None:
            state[0][b, l] = ckv[seq_rows(b)]
        fill_kv(b, new_keys, kv[seq_rows(b)])
        if ctx is not None:
            fill_kv(b, old_keys, _dot(ctx["ckv"][...].astype(BF16), w_ukv[wl]))

    krr = head[:, R_KR:R_KR + LANES]
    krr_pos = krr if rope is None else _rope(krr, rope_m, MLA_ROPE // 4)
    for b in range(n_b):
        if state is not None:
            state[1][b, l] = krr[seq_rows(b)].T[0:MLA_ROPE, :]
        fill_kr(b, new_keys, krr_pos[seq_rows(b)])
        if ctx is not None:
            fill_kr(b, old_keys, jnp.concatenate([ctx["kr"][...]] * (LANES // MLA_ROPE), axis=0).T)

    dq = proj(R_DQ, R_DK) * DIFF_SCALE
    if rope is not None:
        dq = _rope(dq, rope_d, DIFF_QK // 4)
    dqb_ref[...] = dq.astype(BF16)

    dk = proj(R_DK, R_DV)
    dk_pos = dk if rope is None else _rope(dk, rope_d, DIFF_QK // 4)
    for b in range(n_b):
        if state is not None:
            state[2][b, l] = dk[seq_rows(b)].T
        fill_dk(b, new_keys, dk_pos[seq_rows(b)])
        if ctx is not None:
            fill_dk(b, old_keys, ctx["dk"][...].T)

    dv = proj(R_DV, R_Z)
    for b in range(n_b):
        for h in range(DIFF_HEADS):
            dv_h = dv[seq_rows(b), LANES * h:LANES * (h + 1)]
            if state is not None:
                state[3][b, l, pl.ds(h, n_seq, stride=DIFF_HEADS), :] = dv_h
            dvb_ref[b, h, new_keys, 0:LANES] = dv_h.astype(BF16)
            if ctx is not None:
                dvb_ref[b, h, old_keys, 0:LANES] = (
                    ctx["dv"][pl.ds(h, n_ctx, stride=DIFF_HEADS), :].astype(BF16))
            ones = jnp.ones((nk, LANES), BF16)
            dvb_ref[b, h, :, LANES:2 * LANES] = ones
            vm_ref[b, h, :, LANES:2 * LANES] = ones


def _attend_block(l, b, start, tq, nk, w, scr):
    lf = w["lam"][l]
    lam = (jnp.exp(jnp.sum(lf[0:1] * lf[1:2], axis=-1, keepdims=True))
           - jnp.exp(jnp.sum(lf[2:3] * lf[3:4], axis=-1, keepdims=True)) + _lambda_init(l))
    qrows = pl.ds(start, tq)
    return _attend(scr["qb"][qrows, :], scr["dqb"][qrows, :], pl.ds(0, nk),
                   scr["kcat"].at[b], scr["vm"].at[b], scr["dka"].at[b], scr["dvb"].at[b],
                   lam, w["subln"][l:l + 1, :], _lambda_init(l))


def _conv_branch(l, wl, m_rows, n_seq, w, scr):
    conv = _proj(wl, w, scr, R_CB, R_DQ)
    u = conv[:, W_CONV:2 * W_CONV] * conv[:, 2 * W_CONV:3 * W_CONV]
    pos = lax.broadcasted_iota(jnp.int32, (m_rows, 1), 0) % n_seq
    u_prev = jnp.where(pos == 0, 0.0, pltpu.roll(u, 1, 0))
    u_next = jnp.where(pos == n_seq - 1, 0.0, pltpu.roll(u, m_rows - 1, 0))
    cw = w["conv_w"][:, l, :]
    return conv[:, 0:W_CONV] * (u_prev * cw[0:1] + u * cw[1:2] + u_next * cw[2:3])


def _silu_gate(wl, w, scr, c0, c1):
    z = _proj(wl, w, scr, R_Z + c0, R_Z + c1)
    return z * jax.nn.sigmoid(z)


def _trunk_layer(l, x, n_seq, mod, w, scr, state):
    m_rows = x.shape[0]
    o_ref, mix_ref = scr["o"], scr["mix"]
    _layer_front(l, l, x, n_seq, mod, w, scr, None, None, state)
    for b in range(m_rows // n_seq):
        for start in range(b * n_seq, (b + 1) * n_seq, Q_TILE):
            o_mla, o_diff = _attend_block(l, b, start, Q_TILE, n_seq, w, scr)
            o_ref[start:start + Q_TILE, 0:W_MLA] = o_mla
            o_ref[start:start + Q_TILE, W_MLA + W_CONV:D_MIX] = o_diff
    o_ref[:, W_MLA:W_MLA + W_CONV] = _conv_branch(l, l, m_rows, n_seq, w, scr)
    for c0 in range(0, D_MIX, 512):
        mix_ref[:, c0:c0 + 512] = (
            o_ref[:, c0:c0 + 512] * _silu_gate(l, w, scr, c0, c0 + 512)).astype(BF16)
    return x + mod[2] * _dot(mix_ref[...], w["w_out"][l])


_W_NAMES = ("norm_g", "w_in_t", "w_head", "q_norm", "w_uq", "kv_norm", "w_ukv", "conv_w", "lam",
            "subln", "w_out", "final_norm")
_PER_LAYER_BLOCKS = ("w_in_t", "w_head", "w_uq", "w_ukv", "w_out")
_SCR_NAMES = ("hb", "o", "mix", "qb", "dqb", "kcat", "vm", "dka", "dvb")
_CTX_NAMES = ("ckv", "kr", "dk", "dv")
N_ROPE = 4


_BIG_NAMES = ("w_in_t", "w_head", "w_out")
_SMALL_NAMES = tuple(n for n in _W_NAMES if n not in _BIG_NAMES)
W_IN_CHUNK = 208
STAGE_ROWS = 256
assert D_IN % W_IN_CHUNK == 0 and R_CB % W_IN_CHUNK == 0 and R_CB - R_KR <= W_IN_CHUNK


def _stage_weights(w_in_hbm, w_out_hbm, big, slots, sem):
    jobs = []
    for l in range(DEPTH):
        jobs += [(w_in_hbm, big["w_in_t"], l, r, W_IN_CHUNK) for r in range(0, D_IN, W_IN_CHUNK)]
        jobs += [(w_out_hbm, big["w_out"], l, r, STAGE_ROWS) for r in range(0, D_MIX, STAGE_ROWS)]
    n_slots = len(slots)

    def copy(k):
        src, _, l, r, n = jobs[k]
        return pltpu.make_async_copy(src.at[l, pl.ds(r, n)], slots[k % n_slots].at[pl.ds(0, n)],
                                     sem.at[k % n_slots])

    for k in range(n_slots):
        copy(k).start()
    head = big["w_head"]
    for k, (src, dst, l, r, n) in enumerate(jobs):
        copy(k).wait()
        slot = slots[k % n_slots]
        if src is w_out_hbm:
            dst[l, r:r + n, :] = slot[0:n, :].astype(BF16)
        elif r >= R_CB:
            dst[l, r - R_CB:r - R_CB + n, :] = slot[0:n, :].astype(BF16)
        else:
            n_main = min(r + n, R_KR) - r
            head[l, r:r + n_main, :] = slot[0:n_main, :].astype(BF16)
            if r + n > R_KR:
                k_rope = slot[R_KR - r:R_CB - r, :].astype(BF16)
                for i in range(LANES // MLA_ROPE):
                    head[l, R_KR + MLA_ROPE * i:R_KR + MLA_ROPE * (i + 1), :] = k_rope
        if k + n_slots < len(jobs):
            copy(k + n_slots).start()


def _ctx_kernel(x_ref, mod_ref, *refs, n_seq):
    n_small, n_big, n_scr = len(_SMALL_NAMES), len(_BIG_NAMES), len(_SCR_NAMES)
    w = dict(zip(_SMALL_NAMES, refs[:n_small]))
    w_in_hbm, w_out_hbm = refs[n_small:n_small + 2]
    rest = refs[n_small + 2:]
    y_ref, state = rest[0], rest[1:5]
    big_hbm = rest[5:5 + n_big]
    scr = dict(zip(_SCR_NAMES, rest[5 + n_big:5 + n_big + n_scr]))
    big = dict(zip(_BIG_NAMES, rest[5 + n_big + n_scr:5 + 2 * n_big + n_scr]))
    in_sem, out_sem = rest[5 + 2 * n_big + n_scr:]
    w.update(big)

    def hand_off(k):
        return pltpu.make_async_copy(big[_BIG_NAMES[k]], big_hbm[k], out_sem.at[k])

    first_step = pl.program_id(0) == 0

    @pl.when(first_step)
    def _():
        slots = [scr["o"].at[pl.ds(r, STAGE_ROWS)] for r in range(0, scr["o"].shape[0], STAGE_ROWS)]
        slots += [y_ref.at[b] for b in range(y_ref.shape[0])]
        _stage_weights(w_in_hbm, w_out_hbm, big, slots, in_sem)
        for k in range(n_big):
            hand_off(k).start()

    n_b = x_ref.shape[0]
    x = x_ref[...].reshape(n_b * n_seq, D_MODEL)
    for l in range(DEPTH):
        mod = tuple(mod_ref[l, j, 0:1, :] for j in range(3))
        x = _trunk_layer(l, x, n_seq, mod, w, scr, state)
    y_ref[...] = _rms(x, w["final_norm"][...]).reshape(n_b, n_seq, D_MODEL)

    @pl.when(first_step)
    def _():
        for k in range(n_big):
            hand_off(k).wait()


def _lat_kernel(x_ref, mod_ref, *refs, n_seq, layer):
    ctx = {k: r.at[0, 0] for k, r in zip(_CTX_NAMES, refs[:4])}
    rope = (refs[4:6], refs[6:8])
    w = dict(zip(_W_NAMES, refs[4 + N_ROPE:4 + N_ROPE + len(_W_NAMES)]))
    rest = refs[4 + N_ROPE + len(_W_NAMES):]
    y_ref = rest[0]
    scr = dict(zip(_SCR_NAMES, rest[1:]))
    cond_row = pl.ds(1 + pl.program_id(0), 1)
    mod = tuple(mod_ref[layer, j, cond_row, :] for j in range(3))
    zg_ref, mix_ref = scr["o"], scr["mix"]
    n_ctx = ctx["ckv"].shape[0]
    conv_cols = slice(W_MLA, W_MLA + W_CONV)

    @pl.when(pl.program_id(1) == 0)
    def _():
        _layer_front(layer, 0, x_ref[0], n_seq, mod, w, scr, rope, ctx, None)
        for c0 in range(0, D_MIX, 512):
            zg_ref[:, c0:c0 + 512] = _silu_gate(0, w, scr, c0, c0 + 512)
        mix_ref[:, conv_cols] = (
            _conv_branch(layer, 0, n_seq, n_seq, w, scr) * zg_ref[:, conv_cols]).astype(BF16)

    start = pl.multiple_of(pl.program_id(1) * LAT_Q_TILE, LAT_Q_TILE)
    qrows = pl.ds(start, LAT_Q_TILE)
    o_mla, o_diff = _attend_block(layer, 0, start, LAT_Q_TILE, n_ctx + n_seq, w, scr)
    mix_ref[qrows, 0:W_MLA] = (o_mla * zg_ref[qrows, 0:W_MLA]).astype(BF16)
    mix_ref[qrows, W_MLA + W_CONV:D_MIX] = (
        o_diff * zg_ref[qrows, W_MLA + W_CONV:D_MIX]).astype(BF16)
    x = x_ref[0, qrows, :] + mod[2] * _dot(mix_ref[qrows, :], w["w_out"][0])
    if layer == DEPTH - 1:
        x = _rms(x, w["final_norm"][...])
    y_ref[0] = x


def _prep_kernel(cctx_ref, c_ref, ada_w_ref, ada_b_ref, w_uq_ref, w_ukv_ref,
                 mod_ref, w_uq_p_ref, w_ukv_p_ref):
    n_rows = mod_ref.shape[2]
    pad = jnp.zeros((n_rows - 1 - c_ref.shape[0], D_MODEL), F32)
    c = jnp.concatenate([cctx_ref[...], c_ref[...], pad], axis=0)
    s = (c * jax.nn.sigmoid(c)).astype(BF16)
    mod_ref[0, 0] = (_dot(s, ada_w_ref[0].astype(BF16))
                     + ada_b_ref[pl.ds(pl.program_id(0), 1), :])

    @pl.when(pl.program_id(1) == 0)
    def _():
        for src, dst, first in ((w_uq_ref, w_uq_p_ref, MLA_NOPE), (w_ukv_ref, w_ukv_p_ref, MLA_NOPE)):
            a = src[0]
            per = a.shape[1] // MLA_HEADS
            cols = ([a[:, per * h:per * h + first] for h in range(MLA_HEADS)]
                    + [a[:, per * h + first:per * (h + 1)] for h in range(MLA_HEADS)])
            dst[0] = jnp.concatenate(cols, axis=1).astype(BF16)


def _const_spec(shape, layer=None):
    if layer is None:
        zeros = (0,) * len(shape)
        return pl.BlockSpec(shape, lambda *_: zeros, pipeline_mode=pl.Buffered(1))
    idx = (layer,) + (0,) * (len(shape) - 1)
    return pl.BlockSpec((1,) + tuple(shape[1:]), lambda *_: idx, pipeline_mode=pl.Buffered(1))


def _scratch(n_b, n_seq, nk):
    m_rows = n_b * n_seq
    return [
        pltpu.VMEM((m_rows, D_MODEL), BF16),
        pltpu.VMEM((m_rows, D_MIX), F32),
        pltpu.VMEM((m_rows, D_MIX), BF16),
        pltpu.VMEM((m_rows, 3 * LANES), BF16),
        pltpu.VMEM((m_rows, W_DIFF), BF16),
        pltpu.VMEM((n_b, MLA_HEADS, nk, 2 * LANES), BF16),
        pltpu.VMEM((n_b, MLA_HEADS, nk, 2 * LANES), BF16),
        pltpu.VMEM((n_b, 2 * DIFF_HEADS, nk, LANES), BF16),
        pltpu.VMEM((n_b, DIFF_HEADS, nk, 2 * LANES), BF16),
    ]


def _rope_tables(n, rot_dim):
    half = rot_dim // 2
    inv = ROPE_THETA ** (-(np.arange(0, half, 2, dtype=np.float32) / half))
    pos = np.arange(n)
    ang_r = (pos // GRID_W).astype(np.float32)[:, None] * inv
    ang_c = (pos % GRID_W).astype(np.float32)[:, None] * inv
    ang = np.concatenate([ang_r, ang_r, ang_c, ang_c], axis=-1)
    cos, sin = np.cos(ang), np.sin(ang)
    q = rot_dim // 4
    first = (np.arange(rot_dim) % (2 * q)) < q
    sin_signed = np.where(first, -sin, sin)
    reps = LANES // rot_dim
    return tuple(jnp.asarray(np.tile(t, (1, reps)), dtype=F32) for t in (cos, sin_signed))


def kernel(x_prompt, x_sample, c, cache_mla_ckv, cache_mla_krope, cache_diff_k, cache_diff_v,
           c_ctx, norm_g, ada_w, ada_b, w_in, mla_q_norm, w_uq, mla_kv_norm, w_ukv,
           conv_w, diff_lambda, diff_subln, w_out, final_norm):
    batch, seq, _ = x_prompt.shape
    dec_batch, dec_seq, _ = x_sample.shape
    past = cache_mla_ckv.shape[2]
    cparams = pltpu.CompilerParams(dimension_semantics=("arbitrary",),
                                   vmem_limit_bytes=VMEM_LIMIT_BYTES)

    n_cond = 8
    per_layer_spec = lambda a: pl.BlockSpec((1,) + a.shape[1:], lambda l, j: (l, 0, 0))
    mod, w_uq_p, w_ukv_p = pl.pallas_call(
        _prep_kernel,
        grid=(DEPTH, 3),
        in_specs=[pl.BlockSpec((1, D_MODEL), lambda l, j: (0, 0)),
                  pl.BlockSpec((dec_batch, D_MODEL), lambda l, j: (0, 0)),
                  pl.BlockSpec((1, D_MODEL, D_MODEL), lambda l, j: (l, 0, j)),
                  pl.BlockSpec((DEPTH, D_MODEL), lambda l, j: (0, j)),
                  per_layer_spec(w_uq), per_layer_spec(w_ukv)],
        out_specs=[pl.BlockSpec((1, 1, n_cond, D_MODEL), lambda l, j: (l, j, 0, 0)),
                   per_layer_spec(w_uq), per_layer_spec(w_ukv)],
        out_shape=[jax.ShapeDtypeStruct((DEPTH, 3, n_cond, D_MODEL), F32),
                   jax.ShapeDtypeStruct(w_uq.shape, BF16),
                   jax.ShapeDtypeStruct(w_ukv.shape, BF16)],
        compiler_params=pltpu.CompilerParams(dimension_semantics=("arbitrary", "arbitrary")),
        name="prep",
    )(c_ctx.reshape(1, D_MODEL), c, ada_w, ada_b, w_uq, w_ukv)

    tb = 2
    assert seq == STAGE_ROWS and (tb * seq) % STAGE_ROWS == 0
    n_slots = tb * seq // STAGE_ROWS + tb
    small = dict(zip(_SMALL_NAMES, (
        norm_g, mla_q_norm, w_uq_p, mla_kv_norm, w_ukv_p, jnp.swapaxes(conv_w, 0, 1), diff_lambda,
        diff_subln, final_norm.reshape(1, D_MODEL))))
    big_shapes = dict(zip(_BIG_NAMES, ((DEPTH, D_IN - R_CB, D_MODEL), (DEPTH, R_KR + LANES, D_MODEL),
                                       (DEPTH, D_MIX, D_MODEL))))
    hbm_spec = pl.BlockSpec(memory_space=pl.ANY)
    state_shapes = ((seq, MLA_KV_LORA), (MLA_ROPE, seq), (W_DIFF, seq),
                    (DIFF_HEADS * seq, DIFF_V))
    y_prompt, s_ckv, s_kr, s_dk, s_dv, w_in_t, w_head, w_out_b = pl.pallas_call(
        functools.partial(_ctx_kernel, n_seq=seq),
        grid=(batch // tb,),
        in_specs=[pl.BlockSpec((tb, seq, D_MODEL), lambda i: (i, 0, 0)),
                  pl.BlockSpec(mod.shape, lambda i: (0,) * mod.ndim)]
        + [pl.BlockSpec(a.shape, lambda i, n=a.ndim: (0,) * n) for a in small.values()]
        + [hbm_spec, hbm_spec],
        out_specs=[pl.BlockSpec((tb, seq, D_MODEL), lambda i: (i, 0, 0))]
        + [pl.BlockSpec((tb, DEPTH) + s, lambda i: (i, 0, 0, 0)) for s in state_shapes]
        + [hbm_spec] * len(_BIG_NAMES),
        out_shape=[jax.ShapeDtypeStruct((batch, seq, D_MODEL), F32)]
        + [jax.ShapeDtypeStruct((batch, DEPTH) + s, F32) for s in state_shapes]
        + [jax.ShapeDtypeStruct(big_shapes[n], BF16) for n in _BIG_NAMES],
        scratch_shapes=_scratch(tb, seq, seq)
        + [pltpu.VMEM(big_shapes[n], BF16) for n in _BIG_NAMES]
        + [pltpu.SemaphoreType.DMA((n_slots,)), pltpu.SemaphoreType.DMA((len(_BIG_NAMES),))],
        compiler_params=cparams,
        name="ctx",
    )(x_prompt, mod, *small.values(), jnp.swapaxes(w_in, 1, 2), w_out)

    weights = tuple(dict(small, w_in_t=w_in_t, w_head=w_head, w_out=w_out_b)[n] for n in _W_NAMES)

    lat_params = pltpu.CompilerParams(dimension_semantics=("arbitrary", "arbitrary"),
                                      vmem_limit_bytes=VMEM_LIMIT_BYTES)
    tables = _rope_tables(dec_seq, MLA_ROPE) + _rope_tables(dec_seq, DIFF_QK)
    ctx_in = (cache_mla_ckv,
              jnp.swapaxes(cache_mla_krope, 2, 3),
              jnp.transpose(cache_diff_k, (0, 1, 3, 4, 5, 2)).reshape(dec_batch, DEPTH, W_DIFF, past),
              cache_diff_v.reshape(dec_batch, DEPTH, past * DIFF_HEADS, DIFF_V))
    h_lat = x_sample
    for l in range(DEPTH):
        h_lat = pl.pallas_call(
            functools.partial(_lat_kernel, n_seq=dec_seq, layer=l),
            grid=(dec_batch, dec_seq // LAT_Q_TILE),
            in_specs=[pl.BlockSpec((1, dec_seq, D_MODEL), lambda i, j: (i, 0, 0)),
                      _const_spec(mod.shape)]
            + [pl.BlockSpec((1, 1) + a.shape[2:], lambda i, j, l=l: (i, l, 0, 0))
               for a in ctx_in]
            + [_const_spec(t.shape) for t in tables]
            + [_const_spec(a.shape, layer=l if name in _PER_LAYER_BLOCKS else None)
               for name, a in zip(_W_NAMES, weights)],
            out_specs=pl.BlockSpec((1, LAT_Q_TILE, D_MODEL), lambda i, j: (i, j, 0)),
            out_shape=jax.ShapeDtypeStruct((dec_batch, dec_seq, D_MODEL), F32),
            scratch_shapes=_scratch(1, dec_seq, past + dec_seq),
            compiler_params=lat_params,
            name=f"latent_{l}",
        )(h_lat, mod, *ctx_in, *tables, *weights)

    return (y_prompt, h_lat, s_ckv,
            jnp.swapaxes(s_kr, 2, 3),
            jnp.transpose(s_dk.reshape(batch, DEPTH, DIFF_HEADS, 2, DIFF_QK, seq),
                          (0, 1, 5, 2, 3, 4)),
            s_dv.reshape(batch, DEPTH, seq, DIFF_HEADS, DIFF_V))
```

```python
import functools
import math

import numpy as np
import jax
import jax.numpy as jnp
from jax import lax
from jax.experimental import pallas as pl
from jax.experimental.pallas import tpu as pltpu

D_MODEL = 1024
DEPTH = 2
GRID_W = 64
MLA_HEADS = 4
MLA_NOPE = 64
MLA_ROPE = 32
MLA_V = 64
MLA_Q_LORA = 256
MLA_KV_LORA = 128
W_MLA = MLA_HEADS * MLA_V
W_CONV = 256
DIFF_HEADS = 4
DIFF_QK = 64
DIFF_V = 2 * DIFF_QK
W_DIFF = DIFF_HEADS * DIFF_V
D_MIX = W_MLA + W_CONV + W_DIFF
ROPE_THETA = 10000.0
EPS = 1e-6

LANES = 128
Q_TILE = 256
LAT_Q_TILE = 256
LOG2E = math.log2(math.e)
MLA_SCALE = (MLA_NOPE + MLA_ROPE) ** -0.5 * LOG2E
DIFF_SCALE = DIFF_QK ** -0.5 * LOG2E
VMEM_LIMIT_BYTES = 54 * 1024 * 1024

R_Q = 0
R_KV = R_Q + MLA_Q_LORA
R_KR = R_KV + MLA_KV_LORA
R_CB = R_KR + MLA_ROPE
R_CC = R_CB + W_CONV
R_CX = R_CC + W_CONV
R_DQ = R_CX + W_CONV
R_DK = R_DQ + W_DIFF
R_DV = R_DK + W_DIFF
R_Z = R_DV + W_DIFF
D_IN = R_Z + D_MIX

BF16 = jnp.bfloat16
F32 = jnp.float32


def _lambda_init(layer):
    return 0.8 - 0.6 * math.exp(-0.3 * layer)


def _dot(a, b):
    return jnp.dot(a, b, preferred_element_type=F32)


def _dot_nt(a, b):
    return lax.dot_general(a, b, (((1,), (1,)), ((), ())), preferred_element_type=F32)


def _rms(x, g):
    ms = jnp.mean(x * x, axis=-1, keepdims=True)
    return x * lax.rsqrt(ms + EPS) * g


def _exp_scores(s):
    m = s[:, 0:LANES]
    for t in range(1, s.shape[1] // LANES):
        m = jnp.maximum(m, s[:, LANES * t:LANES * (t + 1)])
    return jnp.exp2(s - jnp.max(m, axis=-1, keepdims=True)).astype(BF16)


def _weighted(e, v_ones):
    r = _dot(e, v_ones)
    return r[:, 0:LANES], 1.0 / r[:, LANES:2 * LANES]


def _lane_mask(lo, hi):
    lane = lax.broadcasted_iota(jnp.int32, (1, LANES), 1)
    return jnp.where((lane >= lo) & (lane < hi), 1.0, 0.0).astype(F32)


def _rope(x, tables, q):
    cos, sin_signed = tables
    lane = lax.broadcasted_iota(jnp.int32, (1, LANES), 1)
    first = (lane & (2 * q - 1)) < q
    tiles = []
    for t in range(x.shape[1] // LANES):
        xt = x[:, LANES * t:LANES * (t + 1)]
        rot = jnp.where(first, pltpu.roll(xt, LANES - q, 1), pltpu.roll(xt, q, 1))
        tiles.append(xt * cos + rot * sin_signed)
    return tiles[0] if len(tiles) == 1 else jnp.concatenate(tiles, axis=1)


def _attend(q, dq, keys, kcat_ref, vm_ref, dka_ref, dvb_ref, lam, subln, lam_init):
    qr = q[:, 2 * LANES:3 * LANES]
    o_mla = []
    for t in range(2):
        qcat = jnp.concatenate([q[:, LANES * t:LANES * (t + 1)], qr], axis=1)
        acc = None
        for hh in range(2):
            h = 2 * t + hh
            o, rinv = _weighted(_exp_scores(_dot_nt(qcat, kcat_ref[h, keys, :])),
                                vm_ref[h, keys, :])
            acc = o * rinv if acc is None else acc + o * rinv
        o_mla.append(acc)
    o_diff = []
    for h in range(DIFF_HEADS):
        qt = dq[:, LANES * h:LANES * (h + 1)]
        o0, rinv0 = _weighted(_exp_scores(_dot_nt(qt, dka_ref[2 * h, keys, :])),
                              dvb_ref[h, keys, :])
        o1, rinv1 = _weighted(_exp_scores(_dot_nt(qt, dka_ref[2 * h + 1, keys, :])),
                              dvb_ref[h, keys, :])
        o = o0 * rinv0 - o1 * (lam * rinv1)
        o_diff.append(_rms(o, subln) * (1.0 - lam_init))
    return jnp.concatenate(o_mla, axis=1), jnp.concatenate(o_diff, axis=1)


def _proj(wl, w, scr, r0, r1):
    return _dot_nt(scr["hb"][...], w["w_in_t"][wl, r0 - R_CB:r1 - R_CB, :])


def _fill_ones(scr, b, h):
    vm_ref, dvb_ref = scr["vm"], scr["dvb"]
    ones = jnp.ones((vm_ref.shape[2], LANES), BF16)
    dvb_ref[b, h, :, LANES:2 * LANES] = ones
    vm_ref[b, h, :, LANES:2 * LANES] = ones


def _layer_front(l, wl, x, n_seq, mod, w, scr, rope, ctx, state, fill_ones=True):
    m_rows = x.shape[0]
    n_b = m_rows // n_seq
    n_ctx = 0 if ctx is None else ctx["ckv"].shape[0]
    nk = n_ctx + n_seq
    shift, scale, _ = mod
    hb_ref = scr["hb"]
    qb_ref, dqb_ref = scr["qb"], scr["dqb"]
    proj = functools.partial(_proj, wl, w, scr)
    kcat_ref, vm_ref, dka_ref, dvb_ref = scr["kcat"], scr["vm"], scr["dka"], scr["dvb"]
    w_ukv = w["w_ukv"]
    half = (_lane_mask(0, LANES // 2), _lane_mask(LANES // 2, LANES))
    new_keys, old_keys = pl.ds(n_ctx, n_seq), pl.ds(0, n_ctx)
    if rope is not None:
        rope_m = tuple(r[...] for r in rope[0])
        rope_d = tuple(r[...] for r in rope[1])

    def seq_rows(b):
        return slice(b * n_seq, (b + 1) * n_seq)

    gs = w["norm_g"][l:l + 1, :] * (1.0 + scale)
    ms = jnp.mean(x * x, axis=-1, keepdims=True)
    hb_ref[...] = (x * lax.rsqrt(ms + EPS) * gs + shift).astype(BF16)

    head = _dot_nt(hb_ref[...], w["w_head"][wl])

    cqn = _rms(head[:, 0:MLA_Q_LORA], w["q_norm"][l:l + 1, :]).astype(BF16)
    q = _dot(cqn, w["w_uq"][wl]) * MLA_SCALE
    if rope is not None:
        q = jnp.concatenate(
            [q[:, 0:2 * LANES], _rope(q[:, 2 * LANES:3 * LANES], rope_m, MLA_ROPE // 4)], axis=1)
    qb_ref[...] = q.astype(BF16)

    def fill_kv(b, dst, kv):
        for h in range(MLA_HEADS):
            t, hh = divmod(h, 2)
            kcat_ref[b, h, dst, 0:LANES] = (kv[:, LANES * t:LANES * (t + 1)] * half[hh]).astype(BF16)
            vm_ref[b, h, dst, 0:LANES] = (
                kv[:, 2 * LANES + LANES * t:2 * LANES + LANES * (t + 1)] * half[hh]).astype(BF16)

    def fill_kr(b, dst, kr):
        for h in range(MLA_HEADS):
            kcat_ref[b, h, dst, LANES:2 * LANES] = (
                kr * _lane_mask(MLA_ROPE * h, MLA_ROPE * (h + 1))).astype(BF16)

    def fill_dk(b, dst, dk):
        for h in range(DIFF_HEADS):
            for a in range(2):
                dka_ref[b, 2 * h + a, dst, :] = (
                    dk[:, LANES * h:LANES * (h + 1)] * half[a]).astype(BF16)

    ckv = _rms(head[:, R_KV:R_KR], w["kv_norm"][l:l + 1, :])
    kv = _dot(ckv.astype(BF16), w_ukv[wl])
    for b in range(n_b):
        if state is not None:
            state[0][b, l] = ckv[seq_rows(b)]
        fill_kv(b, new_keys, kv[seq_rows(b)])
        if ctx is not None:
            fill_kv(b, old_keys, _dot(ctx["ckv"][...].astype(BF16), w_ukv[wl]))

    krr = head[:, R_KR:R_KR + LANES]
    krr_pos = krr if rope is None else _rope(krr, rope_m, MLA_ROPE // 4)
    for b in range(n_b):
        if state is not None:
            state[1][b, l] = krr[seq_rows(b)].T[0:MLA_ROPE, :]
        fill_kr(b, new_keys, krr_pos[seq_rows(b)])
        if ctx is not None:
            fill_kr(b, old_keys, jnp.concatenate([ctx["kr"][...]] * (LANES // MLA_ROPE), axis=0).T)

    dq = proj(R_DQ, R_DK) * DIFF_SCALE
    if rope is not None:
        dq = _rope(dq, rope_d, DIFF_QK // 4)
    dqb_ref[...] = dq.astype(BF16)

    dk = proj(R_DK, R_DV)
    dk_pos = dk if rope is None else _rope(dk, rope_d, DIFF_QK // 4)
    for b in range(n_b):
        if state is not None:
            state[2][b, l] = dk[seq_rows(b)].T
        fill_dk(b, new_keys, dk_pos[seq_rows(b)])
        if ctx is not None:
            fill_dk(b, old_keys, ctx["dk"][...].T)

    dv = proj(R_DV, R_Z)
    for b in range(n_b):
        for h in range(DIFF_HEADS):
            dv_h = dv[seq_rows(b), LANES * h:LANES * (h + 1)]
            if state is not None:
                state[3][b, l, pl.ds(h, n_seq, stride=DIFF_HEADS), :] = dv_h
            dvb_ref[b, h, new_keys, 0:LANES] = dv_h.astype(BF16)
            if ctx is not None:
                dvb_ref[b, h, old_keys, 0:LANES] = (
                    ctx["dv"][pl.ds(h, n_ctx, stride=DIFF_HEADS), :].astype(BF16))
            if fill_ones:
                _fill_ones(scr, b, h)


def _attend_block(l, b, start, tq, nk, w, scr):
    lf = w["lam"][l]
    lam = (jnp.exp(jnp.sum(lf[0:1] * lf[1:2], axis=-1, keepdims=True))
           - jnp.exp(jnp.sum(lf[2:3] * lf[3:4], axis=-1, keepdims=True)) + _lambda_init(l))
    qrows = pl.ds(start, tq)
    return _attend(scr["qb"][qrows, :], scr["dqb"][qrows, :], pl.ds(0, nk),
                   scr["kcat"].at[b], scr["vm"].at[b], scr["dka"].at[b], scr["dvb"].at[b],
                   lam, w["subln"][l:l + 1, :], _lambda_init(l))


def _conv_branch(l, wl, m_rows, n_seq, w, scr):
    conv = _proj(wl, w, scr, R_CB, R_DQ)
    u = conv[:, W_CONV:2 * W_CONV] * conv[:, 2 * W_CONV:3 * W_CONV]
    pos = lax.broadcasted_iota(jnp.int32, (m_rows, 1), 0) % n_seq
    u_prev = jnp.where(pos == 0, 0.0, pltpu.roll(u, 1, 0))
    u_next = jnp.where(pos == n_seq - 1, 0.0, pltpu.roll(u, m_rows - 1, 0))
    cw = w["conv_w"][:, l, :]
    return conv[:, 0:W_CONV] * (u_prev * cw[0:1] + u * cw[1:2] + u_next * cw[2:3])


def _silu_gate(wl, w, scr, c0, c1):
    z = _proj(wl, w, scr, R_Z + c0, R_Z + c1)
    return z * jax.nn.sigmoid(z)


def _trunk_layer(l, x, n_seq, mod, w, scr, state):
    m_rows = x.shape[0]
    o_ref, mix_ref = scr["o"], scr["mix"]
    _layer_front(l, l, x, n_seq, mod, w, scr, None, None, state, fill_ones=False)
    for b in range(m_rows // n_seq):
        for start in range(b * n_seq, (b + 1) * n_seq, Q_TILE):
            o_mla, o_diff = _attend_block(l, b, start, Q_TILE, n_seq, w, scr)
            o_ref[start:start + Q_TILE, 0:W_MLA] = o_mla
            o_ref[start:start + Q_TILE, W_MLA + W_CONV:D_MIX] = o_diff
    o_ref[:, W_MLA:W_MLA + W_CONV] = _conv_branch(l, l, m_rows, n_seq, w, scr)
    for c0 in range(0, D_MIX, 512):
        mix_ref[:, c0:c0 + 512] = (
            o_ref[:, c0:c0 + 512] * _silu_gate(l, w, scr, c0, c0 + 512)).astype(BF16)
    return x + mod[2] * _dot(mix_ref[...], w["w_out"][l])


_W_NAMES = ("norm_g", "w_in_t", "w_head", "q_norm", "w_uq", "kv_norm", "w_ukv", "conv_w", "lam",
            "subln", "w_out", "final_norm")
_PER_LAYER_BLOCKS = ("w_in_t", "w_head", "w_uq", "w_ukv", "w_out")
_SCR_NAMES = ("hb", "o", "mix", "qb", "dqb", "kcat", "vm", "dka", "dvb")
_CTX_NAMES = ("ckv", "kr", "dk", "dv")
N_ROPE = 4


_BIG_NAMES = ("w_in_t", "w_head", "w_out")
_SMALL_NAMES = tuple(n for n in _W_NAMES if n not in _BIG_NAMES)
W_IN_CHUNK = 208
STAGE_ROWS = 256
assert D_IN % W_IN_CHUNK == 0 and R_CB % W_IN_CHUNK == 0 and R_CB - R_KR <= W_IN_CHUNK


def _stage_weights(w_in_hbm, w_out_hbm, big, slots, sem):
    jobs = []
    for l in range(DEPTH):
        jobs += [(w_in_hbm, big["w_in_t"], l, r, W_IN_CHUNK) for r in range(0, D_IN, W_IN_CHUNK)]
        jobs += [(w_out_hbm, big["w_out"], l, r, STAGE_ROWS) for r in range(0, D_MIX, STAGE_ROWS)]
    n_slots = len(slots)

    def copy(k):
        src, _, l, r, n = jobs[k]
        return pltpu.make_async_copy(src.at[l, pl.ds(r, n)], slots[k % n_slots].at[pl.ds(0, n)],
                                     sem.at[k % n_slots])

    for k in range(n_slots):
        copy(k).start()
    head = big["w_head"]
    for k, (src, dst, l, r, n) in enumerate(jobs):
        copy(k).wait()
        slot = slots[k % n_slots]
        if src is w_out_hbm:
            dst[l, r:r + n, :] = slot[0:n, :].astype(BF16)
        elif r >= R_CB:
            dst[l, r - R_CB:r - R_CB + n, :] = slot[0:n, :].astype(BF16)
        else:
            n_main = min(r + n, R_KR) - r
            head[l, r:r + n_main, :] = slot[0:n_main, :].astype(BF16)
            if r + n > R_KR:
                k_rope = slot[R_KR - r:R_CB - r, :].astype(BF16)
                for i in range(LANES // MLA_ROPE):
                    head[l, R_KR + MLA_ROPE * i:R_KR + MLA_ROPE * (i + 1), :] = k_rope
        if k + n_slots < len(jobs):
            copy(k + n_slots).start()


def _ctx_kernel(x_ref, mod_ref, *refs, n_seq):
    n_small, n_big, n_scr = len(_SMALL_NAMES), len(_BIG_NAMES), len(_SCR_NAMES)
    w = dict(zip(_SMALL_NAMES, refs[:n_small]))
    w_in_hbm, w_out_hbm = refs[n_small:n_small + 2]
    rest = refs[n_small + 2:]
    y_ref, state = rest[0], rest[1:5]
    big_hbm = rest[5:5 + n_big]
    scr = dict(zip(_SCR_NAMES, rest[5 + n_big:5 + n_big + n_scr]))
    big = dict(zip(_BIG_NAMES, rest[5 + n_big + n_scr:5 + 2 * n_big + n_scr]))
    in_sem, out_sem = rest[5 + 2 * n_big + n_scr:]
    w.update(big)

    def hand_off(k):
        return pltpu.make_async_copy(big[_BIG_NAMES[k]], big_hbm[k], out_sem.at[k])

    first_step = pl.program_id(0) == 0

    @pl.when(first_step)
    def _():
        slots = [scr["o"].at[pl.ds(r, STAGE_ROWS)] for r in range(0, scr["o"].shape[0], STAGE_ROWS)]
        slots += [y_ref.at[b] for b in range(y_ref.shape[0])]
        _stage_weights(w_in_hbm, w_out_hbm, big, slots, in_sem)
        for k in range(n_big):
            hand_off(k).start()
        for b in range(x_ref.shape[0]):
            for h in range(DIFF_HEADS):
                _fill_ones(scr, b, h)

    n_b = x_ref.shape[0]
    x = x_ref[...].reshape(n_b * n_seq, D_MODEL)
    for l in range(DEPTH):
        mod = tuple(mod_ref[l, j, 0:1, :] for j in range(3))
        x = _trunk_layer(l, x, n_seq, mod, w, scr, state)
    y_ref[...] = _rms(x, w["final_norm"][...]).reshape(n_b, n_seq, D_MODEL)

    @pl.when(first_step)
    def _():
        for k in range(n_big):
            hand_off(k).wait()


def _lat_kernel(x_ref, mod_ref, *refs, n_seq, layer):
    ctx = {k: r.at[0, 0] for k, r in zip(_CTX_NAMES, refs[:4])}
    rope = (refs[4:6], refs[6:8])
    w = dict(zip(_W_NAMES, refs[4 + N_ROPE:4 + N_ROPE + len(_W_NAMES)]))
    rest = refs[4 + N_ROPE + len(_W_NAMES):]
    y_ref = rest[0]
    scr = dict(zip(_SCR_NAMES, rest[1:]))
    cond_row = pl.ds(1 + pl.program_id(0), 1)
    mod = tuple(mod_ref[layer, j, cond_row, :] for j in range(3))
    zg_ref, mix_ref = scr["o"], scr["mix"]
    n_ctx = ctx["ckv"].shape[0]
    conv_cols = slice(W_MLA, W_MLA + W_CONV)

    @pl.when(pl.program_id(1) == 0)
    def _():
        _layer_front(layer, 0, x_ref[0], n_seq, mod, w, scr, rope, ctx, None)
        for c0 in range(0, D_MIX, 512):
            zg_ref[:, c0:c0 + 512] = _silu_gate(0, w, scr, c0, c0 + 512)
        mix_ref[:, conv_cols] = (
            _conv_branch(layer, 0, n_seq, n_seq, w, scr) * zg_ref[:, conv_cols]).astype(BF16)

    start = pl.multiple_of(pl.program_id(1) * LAT_Q_TILE, LAT_Q_TILE)
    qrows = pl.ds(start, LAT_Q_TILE)
    o_mla, o_diff = _attend_block(layer, 0, start, LAT_Q_TILE, n_ctx + n_seq, w, scr)
    mix_ref[qrows, 0:W_MLA] = (o_mla * zg_ref[qrows, 0:W_MLA]).astype(BF16)
    mix_ref[qrows, W_MLA + W_CONV:D_MIX] = (
        o_diff * zg_ref[qrows, W_MLA + W_CONV:D_MIX]).astype(BF16)
    x = x_ref[0, qrows, :] + mod[2] * _dot(mix_ref[qrows, :], w["w_out"][0])
    if layer == DEPTH - 1:
        x = _rms(x, w["final_norm"][...])
    y_ref[0] = x


def _prep_kernel(cctx_ref, c_ref, ada_w_ref, ada_b_ref, w_uq_ref, w_ukv_ref,
                 mod_ref, w_uq_p_ref, w_ukv_p_ref):
    n_rows = mod_ref.shape[2]
    pad = jnp.zeros((n_rows - 1 - c_ref.shape[0], D_MODEL), F32)
    c = jnp.concatenate([cctx_ref[...], c_ref[...], pad], axis=0)
    s = (c * jax.nn.sigmoid(c)).astype(BF16)
    mod_ref[0, 0] = (_dot(s, ada_w_ref[0].astype(BF16))
                     + ada_b_ref[pl.ds(pl.program_id(0), 1), :])

    @pl.when(pl.program_id(1) == 0)
    def _():
        for src, dst, first in ((w_uq_ref, w_uq_p_ref, MLA_NOPE), (w_ukv_ref, w_ukv_p_ref, MLA_NOPE)):
            a = src[0]
            per = a.shape[1] // MLA_HEADS
            cols = ([a[:, per * h:per * h + first] for h in range(MLA_HEADS)]
                    + [a[:, per * h + first:per * (h + 1)] for h in range(MLA_HEADS)])
            dst[0] = jnp.concatenate(cols, axis=1).astype(BF16)


def _const_spec(shape, layer=None):
    if layer is None:
        zeros = (0,) * len(shape)
        return pl.BlockSpec(shape, lambda *_: zeros, pipeline_mode=pl.Buffered(1))
    idx = (layer,) + (0,) * (len(shape) - 1)
    return pl.BlockSpec((1,) + tuple(shape[1:]), lambda *_: idx, pipeline_mode=pl.Buffered(1))


def _scratch(n_b, n_seq, nk):
    m_rows = n_b * n_seq
    return [
        pltpu.VMEM((m_rows, D_MODEL), BF16),
        pltpu.VMEM((m_rows, D_MIX), F32),
        pltpu.VMEM((m_rows, D_MIX), BF16),
        pltpu.VMEM((m_rows, 3 * LANES), BF16),
        pltpu.VMEM((m_rows, W_DIFF), BF16),
        pltpu.VMEM((n_b, MLA_HEADS, nk, 2 * LANES), BF16),
        pltpu.VMEM((n_b, MLA_HEADS, nk, 2 * LANES), BF16),
        pltpu.VMEM((n_b, 2 * DIFF_HEADS, nk, LANES), BF16),
        pltpu.VMEM((n_b, DIFF_HEADS, nk, 2 * LANES), BF16),
    ]


def _rope_tables(n, rot_dim):
    half = rot_dim // 2
    inv = ROPE_THETA ** (-(np.arange(0, half, 2, dtype=np.float32) / half))
    pos = np.arange(n)
    ang_r = (pos // GRID_W).astype(np.float32)[:, None] * inv
    ang_c = (pos % GRID_W).astype(np.float32)[:, None] * inv
    ang = np.concatenate([ang_r, ang_r, ang_c, ang_c], axis=-1)
    cos, sin = np.cos(ang), np.sin(ang)
    q = rot_dim // 4
    first = (np.arange(rot_dim) % (2 * q)) < q
    sin_signed = np.where(first, -sin, sin)
    reps = LANES // rot_dim
    return tuple(jnp.asarray(np.tile(t, (1, reps)), dtype=F32) for t in (cos, sin_signed))


def kernel(x_prompt, x_sample, c, cache_mla_ckv, cache_mla_krope, cache_diff_k, cache_diff_v,
           c_ctx, norm_g, ada_w, ada_b, w_in, mla_q_norm, w_uq, mla_kv_norm, w_ukv,
           conv_w, diff_lambda, diff_subln, w_out, final_norm):
    batch, seq, _ = x_prompt.shape
    dec_batch, dec_seq, _ = x_sample.shape
    past = cache_mla_ckv.shape[2]
    cparams = pltpu.CompilerParams(dimension_semantics=("arbitrary",),
                                   vmem_limit_bytes=VMEM_LIMIT_BYTES)

    n_cond = 8
    per_layer_spec = lambda a: pl.BlockSpec((1,) + a.shape[1:], lambda l, j: (l, 0, 0))
    mod, w_uq_p, w_ukv_p = pl.pallas_call(
        _prep_kernel,
        grid=(DEPTH, 3),
        in_specs=[pl.BlockSpec((1, D_MODEL), lambda l, j: (0, 0)),
                  pl.BlockSpec((dec_batch, D_MODEL), lambda l, j: (0, 0)),
                  pl.BlockSpec((1, D_MODEL, D_MODEL), lambda l, j: (l, 0, j)),
                  pl.BlockSpec((DEPTH, D_MODEL), lambda l, j: (0, j)),
                  per_layer_spec(w_uq), per_layer_spec(w_ukv)],
        out_specs=[pl.BlockSpec((1, 1, n_cond, D_MODEL), lambda l, j: (l, j, 0, 0)),
                   per_layer_spec(w_uq), per_layer_spec(w_ukv)],
        out_shape=[jax.ShapeDtypeStruct((DEPTH, 3, n_cond, D_MODEL), F32),
                   jax.ShapeDtypeStruct(w_uq.shape, BF16),
                   jax.ShapeDtypeStruct(w_ukv.shape, BF16)],
        compiler_params=pltpu.CompilerParams(dimension_semantics=("arbitrary", "arbitrary")),
        name="prep",
    )(c_ctx.reshape(1, D_MODEL), c, ada_w, ada_b, w_uq, w_ukv)

    tb = 2
    assert seq == STAGE_ROWS and (tb * seq) % STAGE_ROWS == 0
    n_slots = tb * seq // STAGE_ROWS + tb
    small = dict(zip(_SMALL_NAMES, (
        norm_g, mla_q_norm, w_uq_p, mla_kv_norm, w_ukv_p, jnp.swapaxes(conv_w, 0, 1), diff_lambda,
        diff_subln, final_norm.reshape(1, D_MODEL))))
    big_shapes = dict(zip(_BIG_NAMES, ((DEPTH, D_IN - R_CB, D_MODEL), (DEPTH, R_KR + LANES, D_MODEL),
                                       (DEPTH, D_MIX, D_MODEL))))
    hbm_spec = pl.BlockSpec(memory_space=pl.ANY)
    state_shapes = ((seq, MLA_KV_LORA), (MLA_ROPE, seq), (W_DIFF, seq),
                    (DIFF_HEADS * seq, DIFF_V))
    y_prompt, s_ckv, s_kr, s_dk, s_dv, w_in_t, w_head, w_out_b = pl.pallas_call(
        functools.partial(_ctx_kernel, n_seq=seq),
        grid=(batch // tb,),
        in_specs=[pl.BlockSpec((tb, seq, D_MODEL), lambda i: (i, 0, 0)), _const_spec(mod.shape)]
        + [_const_spec(a.shape) for a in small.values()] + [hbm_spec, hbm_spec],
        out_specs=[pl.BlockSpec((tb, seq, D_MODEL), lambda i: (i, 0, 0))]
        + [pl.BlockSpec((tb, DEPTH) + s, lambda i: (i, 0, 0, 0)) for s in state_shapes]
        + [hbm_spec] * len(_BIG_NAMES),
        out_shape=[jax.ShapeDtypeStruct((batch, seq, D_MODEL), F32)]
        + [jax.ShapeDtypeStruct((batch, DEPTH) + s, F32) for s in state_shapes]
        + [jax.ShapeDtypeStruct(big_shapes[n], BF16) for n in _BIG_NAMES],
        scratch_shapes=_scratch(tb, seq, seq)
        + [pltpu.VMEM(big_shapes[n], BF16) for n in _BIG_NAMES]
        + [pltpu.SemaphoreType.DMA((n_slots,)), pltpu.SemaphoreType.DMA((len(_BIG_NAMES),))],
        compiler_params=cparams,
        name="ctx",
    )(x_prompt, mod, *small.values(), jnp.swapaxes(w_in, 1, 2), w_out)

    weights = tuple(dict(small, w_in_t=w_in_t, w_head=w_head, w_out=w_out_b)[n] for n in _W_NAMES)

    lat_params = pltpu.CompilerParams(dimension_semantics=("arbitrary", "arbitrary"),
                                      vmem_limit_bytes=VMEM_LIMIT_BYTES)
    tables = _rope_tables(dec_seq, MLA_ROPE) + _rope_tables(dec_seq, DIFF_QK)
    ctx_in = (cache_mla_ckv,
              jnp.swapaxes(cache_mla_krope, 2, 3),
              jnp.transpose(cache_diff_k, (0, 1, 3, 4, 5, 2)).reshape(dec_batch, DEPTH, W_DIFF, past),
              cache_diff_v.reshape(dec_batch, DEPTH, past * DIFF_HEADS, DIFF_V))
    h_lat = x_sample
    for l in range(DEPTH):
        h_lat = pl.pallas_call(
            functools.partial(_lat_kernel, n_seq=dec_seq, layer=l),
            grid=(dec_batch, dec_seq // LAT_Q_TILE),
            in_specs=[pl.BlockSpec((1, dec_seq, D_MODEL), lambda i, j: (i, 0, 0)),
                      _const_spec(mod.shape)]
            + [pl.BlockSpec((1, 1) + a.shape[2:], lambda i, j, l=l: (i, l, 0, 0))
               for a in ctx_in]
            + [_const_spec(t.shape) for t in tables]
            + [_const_spec(a.shape, layer=l if name in _PER_LAYER_BLOCKS else None)
               for name, a in zip(_W_NAMES, weights)],
            out_specs=pl.BlockSpec((1, LAT_Q_TILE, D_MODEL), lambda i, j: (i, j, 0)),
            out_shape=jax.ShapeDtypeStruct((dec_batch, dec_seq, D_MODEL), F32),
            scratch_shapes=_scratch(1, dec_seq, past + dec_seq),
            compiler_params=lat_params,
            name=f"latent_{l}",
        )(h_lat, mod, *ctx_in, *tables, *weights)

    return (y_prompt, h_lat, s_ckv,
            jnp.swapaxes(s_kr, 2, 3),
            jnp.transpose(s_dk.reshape(batch, DEPTH, DIFF_HEADS, 2, DIFF_QK, seq),
                          (0, 1, 5, 2, 3, 4)),
            s_dv.reshape(batch, DEPTH, seq, DIFF_HEADS, DIFF_V))
```

```python
import functools
import math

import numpy as np
import jax
import jax.numpy as jnp
from jax import lax
from jax.experimental import pallas as pl
from jax.experimental.pallas import tpu as pltpu

D_MODEL = 1024
DEPTH = 2
GRID_W = 64
MLA_HEADS = 4
MLA_NOPE = 64
MLA_ROPE = 32
MLA_V = 64
MLA_Q_LORA = 256
MLA_KV_LORA = 128
W_MLA = MLA_HEADS * MLA_V
W_CONV = 256
DIFF_HEADS = 4
DIFF_QK = 64
DIFF_V = 2 * DIFF_QK
W_DIFF = DIFF_HEADS * DIFF_V
D_MIX = W_MLA + W_CONV + W_DIFF
ROPE_THETA = 10000.0
EPS = 1e-6

LANES = 128
Q_TILE = 256
LAT_Q_TILE = 256
LOG2E = math.log2(math.e)
MLA_SCALE = (MLA_NOPE + MLA_ROPE) ** -0.5 * LOG2E
DIFF_SCALE = DIFF_QK ** -0.5 * LOG2E
VMEM_LIMIT_BYTES = 54 * 1024 * 1024

R_Q = 0
R_KV = R_Q + MLA_Q_LORA
R_KR = R_KV + MLA_KV_LORA
R_CB = R_KR + MLA_ROPE
R_CC = R_CB + W_CONV
R_CX = R_CC + W_CONV
R_DQ = R_CX + W_CONV
R_DK = R_DQ + W_DIFF
R_DV = R_DK + W_DIFF
R_Z = R_DV + W_DIFF
D_IN = R_Z + D_MIX

BF16 = jnp.bfloat16
F32 = jnp.float32


def _lambda_init(layer):
    return 0.8 - 0.6 * math.exp(-0.3 * layer)


def _dot(a, b):
    return jnp.dot(a, b, preferred_element_type=F32)


def _dot_nt(a, b):
    return lax.dot_general(a, b, (((1,), (1,)), ((), ())), preferred_element_type=F32)


def _rms(x, g):
    ms = jnp.mean(x * x, axis=-1, keepdims=True)
    return x * lax.rsqrt(ms + EPS) * g


def _exp_scores(s):
    m = s[:, 0:LANES]
    for t in range(1, s.shape[1] // LANES):
        m = jnp.maximum(m, s[:, LANES * t:LANES * (t + 1)])
    return jnp.exp2(s - jnp.max(m, axis=-1, keepdims=True)).astype(BF16)


def _weighted(e, v_ones):
    r = _dot(e, v_ones)
    return r[:, 0:LANES], 1.0 / r[:, LANES:2 * LANES]


def _lane_mask(lo, hi):
    lane = lax.broadcasted_iota(jnp.int32, (1, LANES), 1)
    return jnp.where((lane >= lo) & (lane < hi), 1.0, 0.0).astype(F32)


def _rope(x, tables, q):
    cos, sin_signed = tables
    lane = lax.broadcasted_iota(jnp.int32, (1, LANES), 1)
    first = (lane & (2 * q - 1)) < q
    tiles = []
    for t in range(x.shape[1] // LANES):
        xt = x[:, LANES * t:LANES * (t + 1)]
        rot = jnp.where(first, pltpu.roll(xt, LANES - q, 1), pltpu.roll(xt, q, 1))
        tiles.append(xt * cos + rot * sin_signed)
    return tiles[0] if len(tiles) == 1 else jnp.concatenate(tiles, axis=1)


def _attend(q, dq, keys, kcat_ref, vm_ref, dka_ref, dvb_ref, lam, subln, lam_init):
    qr = q[:, 2 * LANES:3 * LANES]
    o_mla = []
    for t in range(2):
        qcat = jnp.concatenate([q[:, LANES * t:LANES * (t + 1)], qr], axis=1)
        acc = None
        for hh in range(2):
            h = 2 * t + hh
            o, rinv = _weighted(_exp_scores(_dot_nt(qcat, kcat_ref[h, keys, :])),
                                vm_ref[h, keys, :])
            acc = o * rinv if acc is None else acc + o * rinv
        o_mla.append(acc)
    o_diff = []
    for h in range(DIFF_HEADS):
        qt = dq[:, LANES * h:LANES * (h + 1)]
        o0, rinv0 = _weighted(_exp_scores(_dot_nt(qt, dka_ref[2 * h, keys, :])),
                              dvb_ref[h, keys, :])
        o1, rinv1 = _weighted(_exp_scores(_dot_nt(qt, dka_ref[2 * h + 1, keys, :])),
                              dvb_ref[h, keys, :])
        o = o0 * rinv0 - o1 * (lam * rinv1)
        o_diff.append(_rms(o, subln) * (1.0 - lam_init))
    return jnp.concatenate(o_mla, axis=1), jnp.concatenate(o_diff, axis=1)


def _proj(wl, w, scr, r0, r1):
    return _dot_nt(scr["hb"][...], w["w_in_t"][wl, r0 - R_CB:r1 - R_CB, :])


def _fill_ones(scr, b, h):
    vm_ref, dvb_ref = scr["vm"], scr["dvb"]
    ones = jnp.ones((vm_ref.shape[2], LANES), BF16)
    dvb_ref[b, h, :, LANES:2 * LANES] = ones
    vm_ref[b, h, :, LANES:2 * LANES] = ones


def _layer_front(l, wl, x, n_seq, mod, w, scr, rope, ctx, state, fill_ones=True):
    m_rows = x.shape[0]
    n_b = m_rows // n_seq
    n_ctx = 0 if ctx is None else ctx["ckv"].shape[0]
    nk = n_ctx + n_seq
    shift, scale, _ = mod
    hb_ref = scr["hb"]
    qb_ref, dqb_ref = scr["qb"], scr["dqb"]
    proj = functools.partial(_proj, wl, w, scr)
    kcat_ref, vm_ref, dka_ref, dvb_ref = scr["kcat"], scr["vm"], scr["dka"], scr["dvb"]
    w_ukv = w["w_ukv"]
    half = (_lane_mask(0, LANES // 2), _lane_mask(LANES // 2, LANES))
    new_keys, old_keys = pl.ds(n_ctx, n_seq), pl.ds(0, n_ctx)
    if rope is not None:
        rope_m = tuple(r[...] for r in rope[0])
        rope_d = tuple(r[...] for r in rope[1])

    def seq_rows(b):
        return slice(b * n_seq, (b + 1) * n_seq)

    gs = w["norm_g"][l:l + 1, :] * (1.0 + scale)
    ms = jnp.mean(x * x, axis=-1, keepdims=True)
    hb_ref[...] = (x * lax.rsqrt(ms + EPS) * gs + shift).astype(BF16)

    head = _dot_nt(hb_ref[...], w["w_head"][wl])

    cqn = _rms(head[:, 0:MLA_Q_LORA], w["q_norm"][l:l + 1, :]).astype(BF16)
    q = _dot(cqn, w["w_uq"][wl]) * MLA_SCALE
    if rope is not None:
        q = jnp.concatenate(
            [q[:, 0:2 * LANES], _rope(q[:, 2 * LANES:3 * LANES], rope_m, MLA_ROPE // 4)], axis=1)
    qb_ref[...] = q.astype(BF16)

    def fill_kv(b, dst, kv):
        for h in range(MLA_HEADS):
            t, hh = divmod(h, 2)
            kcat_ref[b, h, dst, 0:LANES] = (kv[:, LANES * t:LANES * (t + 1)] * half[hh]).astype(BF16)
            vm_ref[b, h, dst, 0:LANES] = (
                kv[:, 2 * LANES + LANES * t:2 * LANES + LANES * (t + 1)] * half[hh]).astype(BF16)

    def fill_kr(b, dst, kr):
        for h in range(MLA_HEADS):
            kcat_ref[b, h, dst, LANES:2 * LANES] = (
                kr * _lane_mask(MLA_ROPE * h, MLA_ROPE * (h + 1))).astype(BF16)

    def fill_dk(b, dst, dk):
        for h in range(DIFF_HEADS):
            for a in range(2):
                dka_ref[b, 2 * h + a, dst, :] = (
                    dk[:, LANES * h:LANES * (h + 1)] * half[a]).astype(BF16)

    ckv = _rms(head[:, R_KV:R_KR], w["kv_norm"][l:l + 1, :])
    kv = _dot(ckv.astype(BF16), w_ukv[wl])
    for b in range(n_b):
        if state is not None:
            state[0][b, l] = ckv[seq_rows(b)]
        fill_kv(b, new_keys, kv[seq_rows(b)])
        if ctx is not None:
            fill_kv(b, old_keys, _dot(ctx["ckv"][...].astype(BF16), w_ukv[wl]))

    krr = head[:, R_KR:R_KR + LANES]
    krr_pos = krr if rope is None else _rope(krr, rope_m, MLA_ROPE // 4)
    for b in range(n_b):
        if state is not None:
            state[1][b, l] = krr[seq_rows(b)].T[0:MLA_ROPE, :]
        fill_kr(b, new_keys, krr_pos[seq_rows(b)])
        if ctx is not None:
            fill_kr(b, old_keys, jnp.concatenate([ctx["kr"][...]] * (LANES // MLA_ROPE), axis=0).T)

    dq = proj(R_DQ, R_DK) * DIFF_SCALE
    if rope is not None:
        dq = _rope(dq, rope_d, DIFF_QK // 4)
    dqb_ref[...] = dq.astype(BF16)

    dk = proj(R_DK, R_DV)
    dk_pos = dk if rope is None else _rope(dk, rope_d, DIFF_QK // 4)
    for b in range(n_b):
        if state is not None:
            state[2][b, l] = dk[seq_rows(b)].T
        fill_dk(b, new_keys, dk_pos[seq_rows(b)])
        if ctx is not None:
            fill_dk(b, old_keys, ctx["dk"][...].T)

    dv = proj(R_DV, R_Z)
    for b in range(n_b):
        for h in range(DIFF_HEADS):
            dv_h = dv[seq_rows(b), LANES * h:LANES * (h + 1)]
            if state is not None:
                state[3][b, l, pl.ds(h, n_seq, stride=DIFF_HEADS), :] = dv_h
            dvb_ref[b, h, new_keys, 0:LANES] = dv_h.astype(BF16)
            if ctx is not None:
                dvb_ref[b, h, old_keys, 0:LANES] = (
                    ctx["dv"][pl.ds(h, n_ctx, stride=DIFF_HEADS), :].astype(BF16))
            if fill_ones:
                _fill_ones(scr, b, h)


def _attend_block(l, b, start, tq, nk, w, scr):
    lf = w["lam"][l]
    lam = (jnp.exp(jnp.sum(lf[0:1] * lf[1:2], axis=-1, keepdims=True))
           - jnp.exp(jnp.sum(lf[2:3] * lf[3:4], axis=-1, keepdims=True)) + _lambda_init(l))
    qrows = pl.ds(start, tq)
    return _attend(scr["qb"][qrows, :], scr["dqb"][qrows, :], pl.ds(0, nk),
                   scr["kcat"].at[b], scr["vm"].at[b], scr["dka"].at[b], scr["dvb"].at[b],
                   lam, w["subln"][l:l + 1, :], _lambda_init(l))


def _conv_branch(l, wl, m_rows, n_seq, w, scr):
    conv = _proj(wl, w, scr, R_CB, R_DQ)
    u = conv[:, W_CONV:2 * W_CONV] * conv[:, 2 * W_CONV:3 * W_CONV]
    pos = lax.broadcasted_iota(jnp.int32, (m_rows, 1), 0) % n_seq
    u_prev = jnp.where(pos == 0, 0.0, pltpu.roll(u, 1, 0))
    u_next = jnp.where(pos == n_seq - 1, 0.0, pltpu.roll(u, m_rows - 1, 0))
    cw = w["conv_w"][:, l, :]
    return conv[:, 0:W_CONV] * (u_prev * cw[0:1] + u * cw[1:2] + u_next * cw[2:3])


def _silu_gate(wl, w, scr, c0, c1):
    z = _proj(wl, w, scr, R_Z + c0, R_Z + c1)
    return z * jax.nn.sigmoid(z)


def _trunk_layer(l, x, n_seq, mod, w, scr, state):
    m_rows = x.shape[0]
    o_ref, mix_ref = scr["o"], scr["mix"]
    _layer_front(l, l, x, n_seq, mod, w, scr, None, None, state, fill_ones=False)
    for b in range(m_rows // n_seq):
        for start in range(b * n_seq, (b + 1) * n_seq, Q_TILE):
            o_mla, o_diff = _attend_block(l, b, start, Q_TILE, n_seq, w, scr)
            o_ref[start:start + Q_TILE, 0:W_MLA] = o_mla
            o_ref[start:start + Q_TILE, W_MLA + W_CONV:D_MIX] = o_diff
    o_ref[:, W_MLA:W_MLA + W_CONV] = _conv_branch(l, l, m_rows, n_seq, w, scr)
    for c0 in range(0, D_MIX, 512):
        mix_ref[:, c0:c0 + 512] = (
            o_ref[:, c0:c0 + 512] * _silu_gate(l, w, scr, c0, c0 + 512)).astype(BF16)
    return x + mod[2] * _dot(mix_ref[...], w["w_out"][l])


_W_NAMES = ("norm_g", "w_in_t", "w_head", "q_norm", "w_uq", "kv_norm", "w_ukv", "conv_w", "lam",
            "subln", "w_out", "final_norm")
_PER_LAYER_BLOCKS = ("w_in_t", "w_head", "w_uq", "w_ukv", "w_out")
_SCR_NAMES = ("hb", "o", "mix", "qb", "dqb", "kcat", "vm", "dka", "dvb")
_CTX_NAMES = ("ckv", "kr", "dk", "dv")
N_ROPE = 4


_BIG_NAMES = ("w_in_t", "w_head", "w_out")
_SMALL_NAMES = tuple(n for n in _W_NAMES if n not in _BIG_NAMES)
W_IN_CHUNK = 208
STAGE_ROWS = 256
assert D_IN % W_IN_CHUNK == 0 and R_CB % W_IN_CHUNK == 0 and R_CB - R_KR <= W_IN_CHUNK


def _stage_weights(w_in_hbm, w_out_hbm, big, slots, sem):
    jobs = []
    for l in range(DEPTH):
        jobs += [(w_in_hbm, big["w_in_t"], l, r, W_IN_CHUNK) for r in range(0, D_IN, W_IN_CHUNK)]
        jobs += [(w_out_hbm, big["w_out"], l, r, STAGE_ROWS) for r in range(0, D_MIX, STAGE_ROWS)]
    n_slots = len(slots)

    def copy(k):
        src, _, l, r, n = jobs[k]
        return pltpu.make_async_copy(src.at[l, pl.ds(r, n)], slots[k % n_slots].at[pl.ds(0, n)],
                                     sem.at[k % n_slots])

    for k in range(n_slots):
        copy(k).start()
    head = big["w_head"]
    for k, (src, dst, l, r, n) in enumerate(jobs):
        copy(k).wait()
        slot = slots[k % n_slots]
        if src is w_out_hbm:
            dst[l, r:r + n, :] = slot[0:n, :].astype(BF16)
        elif r >= R_CB:
            dst[l, r - R_CB:r - R_CB + n, :] = slot[0:n, :].astype(BF16)
        else:
            n_main = min(r + n, R_KR) - r
            head[l, r:r + n_main, :] = slot[0:n_main, :].astype(BF16)
            if r + n > R_KR:
                k_rope = slot[R_KR - r:R_CB - r, :].astype(BF16)
                for i in range(LANES // MLA_ROPE):
                    head[l, R_KR + MLA_ROPE * i:R_KR + MLA_ROPE * (i + 1), :] = k_rope
        if k + n_slots < len(jobs):
            copy(k + n_slots).start()


def _ctx_kernel(x_ref, mod_ref, *refs, n_seq):
    n_small, n_big, n_scr = len(_SMALL_NAMES), len(_BIG_NAMES), len(_SCR_NAMES)
    w = dict(zip(_SMALL_NAMES, refs[:n_small]))
    w_in_hbm, w_out_hbm = refs[n_small:n_small + 2]
    rest = refs[n_small + 2:]
    y_ref, state = rest[0], rest[1:5]
    big_hbm = rest[5:5 + n_big]
    scr = dict(zip(_SCR_NAMES, rest[5 + n_big:5 + n_big + n_scr]))
    big = dict(zip(_BIG_NAMES, rest[5 + n_big + n_scr:5 + 2 * n_big + n_scr]))
    in_sem, out_sem = rest[5 + 2 * n_big + n_scr:]
    w.update(big)

    def hand_off(k):
        return pltpu.make_async_copy(big[_BIG_NAMES[k]], big_hbm[k], out_sem.at[k])

    first_step = pl.program_id(0) == 0

    @pl.when(first_step)
    def _():
        slots = [scr["o"].at[pl.ds(r, STAGE_ROWS)] for r in range(0, scr["o"].shape[0], STAGE_ROWS)]
        slots += [y_ref.at[b] for b in range(y_ref.shape[0])]
        _stage_weights(w_in_hbm, w_out_hbm, big, slots, in_sem)
        for k in range(n_big):
            hand_off(k).start()
        for b in range(x_ref.shape[0]):
            for h in range(DIFF_HEADS):
                _fill_ones(scr, b, h)

    n_b = x_ref.shape[0]
    x = x_ref[...].reshape(n_b * n_seq, D_MODEL)
    for l in range(DEPTH):
        mod = tuple(mod_ref[l, j, 0:1, :] for j in range(3))
        x = _trunk_layer(l, x, n_seq, mod, w, scr, state)
    y_ref[...] = _rms(x, w["final_norm"][...]).reshape(n_b, n_seq, D_MODEL)

    @pl.when(first_step)
    def _():
        for k in range(n_big):
            hand_off(k).wait()


def _lat_kernel(x_ref, mod_ref, *refs, n_seq, layer):
    ctx = {k: r.at[0, 0] for k, r in zip(_CTX_NAMES, refs[:4])}
    rope = (refs[4:6], refs[6:8])
    w = dict(zip(_W_NAMES, refs[4 + N_ROPE:4 + N_ROPE + len(_W_NAMES)]))
    rest = refs[4 + N_ROPE + len(_W_NAMES):]
    y_ref = rest[0]
    scr = dict(zip(_SCR_NAMES, rest[1:]))
    cond_row = pl.ds(1 + pl.program_id(0), 1)
    mod = tuple(mod_ref[layer, j, cond_row, :] for j in range(3))
    zg_ref, mix_ref = scr["o"], scr["mix"]
    n_ctx = ctx["ckv"].shape[0]
    conv_cols = slice(W_MLA, W_MLA + W_CONV)

    @pl.when((pl.program_id(0) == 0) & (pl.program_id(1) == 0))
    def _():
        for h in range(DIFF_HEADS):
            _fill_ones(scr, 0, h)

    @pl.when(pl.program_id(1) == 0)
    def _():
        _layer_front(layer, 0, x_ref[0], n_seq, mod, w, scr, rope, ctx, None, fill_ones=False)
        for c0 in range(0, D_MIX, 512):
            zg_ref[:, c0:c0 + 512] = _silu_gate(0, w, scr, c0, c0 + 512)
        mix_ref[:, conv_cols] = (
            _conv_branch(layer, 0, n_seq, n_seq, w, scr) * zg_ref[:, conv_cols]).astype(BF16)

    start = pl.multiple_of(pl.program_id(1) * LAT_Q_TILE, LAT_Q_TILE)
    qrows = pl.ds(start, LAT_Q_TILE)
    o_mla, o_diff = _attend_block(layer, 0, start, LAT_Q_TILE, n_ctx + n_seq, w, scr)
    mix_ref[qrows, 0:W_MLA] = (o_mla * zg_ref[qrows, 0:W_MLA]).astype(BF16)
    mix_ref[qrows, W_MLA + W_CONV:D_MIX] = (
        o_diff * zg_ref[qrows, W_MLA + W_CONV:D_MIX]).astype(BF16)
    x = x_ref[0, qrows, :] + mod[2] * _dot(mix_ref[qrows, :], w["w_out"][0])
    if layer == DEPTH - 1:
        x = _rms(x, w["final_norm"][...])
    y_ref[0] = x


def _prep_kernel(cctx_ref, c_ref, ada_w_ref, ada_b_ref, w_uq_ref, w_ukv_ref,
                 mod_ref, w_uq_p_ref, w_ukv_p_ref):
    n_rows = mod_ref.shape[2]
    pad = jnp.zeros((n_rows - 1 - c_ref.shape[0], D_MODEL), F32)
    c = jnp.concatenate([cctx_ref[...], c_ref[...], pad], axis=0)
    s = (c * jax.nn.sigmoid(c)).astype(BF16)
    mod_ref[0, 0] = (_dot(s, ada_w_ref[0].astype(BF16))
                     + ada_b_ref[pl.ds(pl.program_id(0), 1), :])

    @pl.when(pl.program_id(1) == 0)
    def _():
        for src, dst, first in ((w_uq_ref, w_uq_p_ref, MLA_NOPE), (w_ukv_ref, w_ukv_p_ref, MLA_NOPE)):
            a = src[0]
            per = a.shape[1] // MLA_HEADS
            cols = ([a[:, per * h:per * h + first] for h in range(MLA_HEADS)]
                    + [a[:, per * h + first:per * (h + 1)] for h in range(MLA_HEADS)])
            dst[0] = jnp.concatenate(cols, axis=1).astype(BF16)


def _const_spec(shape, layer=None):
    if layer is None:
        zeros = (0,) * len(shape)
        return pl.BlockSpec(shape, lambda *_: zeros, pipeline_mode=pl.Buffered(1))
    idx = (layer,) + (0,) * (len(shape) - 1)
    return pl.BlockSpec((1,) + tuple(shape[1:]), lambda *_: idx, pipeline_mode=pl.Buffered(1))


def _scratch(n_b, n_seq, nk):
    m_rows = n_b * n_seq
    return [
        pltpu.VMEM((m_rows, D_MODEL), BF16),
        pltpu.VMEM((m_rows, D_MIX), F32),
        pltpu.VMEM((m_rows, D_MIX), BF16),
        pltpu.VMEM((m_rows, 3 * LANES), BF16),
        pltpu.VMEM((m_rows, W_DIFF), BF16),
        pltpu.VMEM((n_b, MLA_HEADS, nk, 2 * LANES), BF16),
        pltpu.VMEM((n_b, MLA_HEADS, nk, 2 * LANES), BF16),
        pltpu.VMEM((n_b, 2 * DIFF_HEADS, nk, LANES), BF16),
        pltpu.VMEM((n_b, DIFF_HEADS, nk, 2 * LANES), BF16),
    ]


def _rope_tables(n, rot_dim):
    half = rot_dim // 2
    inv = ROPE_THETA ** (-(np.arange(0, half, 2, dtype=np.float32) / half))
    pos = np.arange(n)
    ang_r = (pos // GRID_W).astype(np.float32)[:, None] * inv
    ang_c = (pos % GRID_W).astype(np.float32)[:, None] * inv
    ang = np.concatenate([ang_r, ang_r, ang_c, ang_c], axis=-1)
    cos, sin = np.cos(ang), np.sin(ang)
    q = rot_dim // 4
    first = (np.arange(rot_dim) % (2 * q)) < q
    sin_signed = np.where(first, -sin, sin)
    reps = LANES // rot_dim
    return tuple(jnp.asarray(np.tile(t, (1, reps)), dtype=F32) for t in (cos, sin_signed))


def kernel(x_prompt, x_sample, c, cache_mla_ckv, cache_mla_krope, cache_diff_k, cache_diff_v,
           c_ctx, norm_g, ada_w, ada_b, w_in, mla_q_norm, w_uq, mla_kv_norm, w_ukv,
           conv_w, diff_lambda, diff_subln, w_out, final_norm):
    batch, seq, _ = x_prompt.shape
    dec_batch, dec_seq, _ = x_sample.shape
    past = cache_mla_ckv.shape[2]
    cparams = pltpu.CompilerParams(dimension_semantics=("arbitrary",),
                                   vmem_limit_bytes=VMEM_LIMIT_BYTES)

    n_cond = 8
    per_layer_spec = lambda a: pl.BlockSpec((1,) + a.shape[1:], lambda l, j: (l, 0, 0))
    mod, w_uq_p, w_ukv_p = pl.pallas_call(
        _prep_kernel,
        grid=(DEPTH, 3),
        in_specs=[pl.BlockSpec((1, D_MODEL), lambda l, j: (0, 0)),
                  pl.BlockSpec((dec_batch, D_MODEL), lambda l, j: (0, 0)),
                  pl.BlockSpec((1, D_MODEL, D_MODEL), lambda l, j: (l, 0, j)),
                  pl.BlockSpec((DEPTH, D_MODEL), lambda l, j: (0, j)),
                  per_layer_spec(w_uq), per_layer_spec(w_ukv)],
        out_specs=[pl.BlockSpec((1, 1, n_cond, D_MODEL), lambda l, j: (l, j, 0, 0)),
                   per_layer_spec(w_uq), per_layer_spec(w_ukv)],
        out_shape=[jax.ShapeDtypeStruct((DEPTH, 3, n_cond, D_MODEL), F32),
                   jax.ShapeDtypeStruct(w_uq.shape, BF16),
                   jax.ShapeDtypeStruct(w_ukv.shape, BF16)],
        compiler_params=pltpu.CompilerParams(dimension_semantics=("arbitrary", "arbitrary")),
        name="prep",
    )(c_ctx.reshape(1, D_MODEL), c, ada_w, ada_b, w_uq, w_ukv)

    tb = 2
    assert seq == STAGE_ROWS and (tb * seq) % STAGE_ROWS == 0
    n_slots = tb * seq // STAGE_ROWS + tb
    small = dict(zip(_SMALL_NAMES, (
        norm_g, mla_q_norm, w_uq_p, mla_kv_norm, w_ukv_p, jnp.swapaxes(conv_w, 0, 1), diff_lambda,
        diff_subln, final_norm.reshape(1, D_MODEL))))
    big_shapes = dict(zip(_BIG_NAMES, ((DEPTH, D_IN - R_CB, D_MODEL), (DEPTH, R_KR + LANES, D_MODEL),
                                       (DEPTH, D_MIX, D_MODEL))))
    hbm_spec = pl.BlockSpec(memory_space=pl.ANY)
    state_shapes = ((seq, MLA_KV_LORA), (MLA_ROPE, seq), (W_DIFF, seq),
                    (DIFF_HEADS * seq, DIFF_V))
    y_prompt, s_ckv, s_kr, s_dk, s_dv, w_in_t, w_head, w_out_b = pl.pallas_call(
        functools.partial(_ctx_kernel, n_seq=seq),
        grid=(batch // tb,),
        in_specs=[pl.BlockSpec((tb, seq, D_MODEL), lambda i: (i, 0, 0)), _const_spec(mod.shape)]
        + [_const_spec(a.shape) for a in small.values()] + [hbm_spec, hbm_spec],
        out_specs=[pl.BlockSpec((tb, seq, D_MODEL), lambda i: (i, 0, 0))]
        + [pl.BlockSpec((tb, DEPTH) + s, lambda i: (i, 0, 0, 0)) for s in state_shapes]
        + [hbm_spec] * len(_BIG_NAMES),
        out_shape=[jax.ShapeDtypeStruct((batch, seq, D_MODEL), F32)]
        + [jax.ShapeDtypeStruct((batch, DEPTH) + s, F32) for s in state_shapes]
        + [jax.ShapeDtypeStruct(big_shapes[n], BF16) for n in _BIG_NAMES],
        scratch_shapes=_scratch(tb, seq, seq)
        + [pltpu.VMEM(big_shapes[n], BF16) for n in _BIG_NAMES]
        + [pltpu.SemaphoreType.DMA((n_slots,)), pltpu.SemaphoreType.DMA((len(_BIG_NAMES),))],
        compiler_params=cparams,
        name="ctx",
    )(x_prompt, mod, *small.values(), jnp.swapaxes(w_in, 1, 2), w_out)

    weights = tuple(dict(small, w_in_t=w_in_t, w_head=w_head, w_out=w_out_b)[n] for n in _W_NAMES)

    lat_params = pltpu.CompilerParams(dimension_semantics=("arbitrary", "arbitrary"),
                                      vmem_limit_bytes=VMEM_LIMIT_BYTES)
    tables = _rope_tables(dec_seq, MLA_ROPE) + _rope_tables(dec_seq, DIFF_QK)
    ctx_in = (cache_mla_ckv,
              jnp.swapaxes(cache_mla_krope, 2, 3),
              jnp.transpose(cache_diff_k, (0, 1, 3, 4, 5, 2)).reshape(dec_batch, DEPTH, W_DIFF, past),
              cache_diff_v.reshape(dec_batch, DEPTH, past * DIFF_HEADS, DIFF_V))
    h_lat = x_sample
    for l in range(DEPTH):
        h_lat = pl.pallas_call(
            functools.partial(_lat_kernel, n_seq=dec_seq, layer=l),
            grid=(dec_batch, dec_seq // LAT_Q_TILE),
            in_specs=[pl.BlockSpec((1, dec_seq, D_MODEL), lambda i, j: (i, 0, 0)),
                      _const_spec(mod.shape)]
            + [pl.BlockSpec((1, 1) + a.shape[2:], lambda i, j, l=l: (i, l, 0, 0))
               for a in ctx_in]
            + [_const_spec(t.shape) for t in tables]
            + [_const_spec(a.shape, layer=l if name in _PER_LAYER_BLOCKS else None)
               for name, a in zip(_W_NAMES, weights)],
            out_specs=pl.BlockSpec((1, LAT_Q_TILE, D_MODEL), lambda i, j: (i, j, 0)),
            out_shape=jax.ShapeDtypeStruct((dec_batch, dec_seq, D_MODEL), F32),
            scratch_shapes=_scratch(1, dec_seq, past + dec_seq),
            compiler_params=lat_params,
            name=f"latent_{l}",
        )(h_lat, mod, *ctx_in, *tables, *weights)

    return (y_prompt, h_lat, s_ckv,
            jnp.swapaxes(s_kr, 2, 3),
            jnp.transpose(s_dk.reshape(batch, DEPTH, DIFF_HEADS, 2, DIFF_QK, seq),
                          (0, 1, 5, 2, 3, 4)),
            s_dv.reshape(batch, DEPTH, seq, DIFF_HEADS, DIFF_V))
```

```python
import functools
import math

import numpy as np
import jax
import jax.numpy as jnp
from jax import lax
from jax.experimental import pallas as pl
from jax.experimental.pallas import tpu as pltpu

D_MODEL = 1024
DEPTH = 2
GRID_W = 64
MLA_HEADS = 4
MLA_NOPE = 64
MLA_ROPE = 32
MLA_V = 64
MLA_Q_LORA = 256
MLA_KV_LORA = 128
W_MLA = MLA_HEADS * MLA_V
W_CONV = 256
DIFF_HEADS = 4
DIFF_QK = 64
DIFF_V = 2 * DIFF_QK
W_DIFF = DIFF_HEADS * DIFF_V
D_MIX = W_MLA + W_CONV + W_DIFF
ROPE_THETA = 10000.0
EPS = 1e-6

LANES = 128
Q_TILE = 256
LAT_Q_TILE = 256
LOG2E = math.log2(math.e)
MLA_SCALE = (MLA_NOPE + MLA_ROPE) ** -0.5 * LOG2E
DIFF_SCALE = DIFF_QK ** -0.5 * LOG2E
VMEM_LIMIT_BYTES = 54 * 1024 * 1024

R_Q = 0
R_KV = R_Q + MLA_Q_LORA
R_KR = R_KV + MLA_KV_LORA
R_CB = R_KR + MLA_ROPE
R_CC = R_CB + W_CONV
R_CX = R_CC + W_CONV
R_DQ = R_CX + W_CONV
R_DK = R_DQ + W_DIFF
R_DV = R_DK + W_DIFF
R_Z = R_DV + W_DIFF
D_IN = R_Z + D_MIX

BF16 = jnp.bfloat16
F32 = jnp.float32


def _lambda_init(layer):
    return 0.8 - 0.6 * math.exp(-0.3 * layer)


def _dot(a, b):
    return jnp.dot(a, b, preferred_element_type=F32)


def _dot_nt(a, b):
    return lax.dot_general(a, b, (((1,), (1,)), ((), ())), preferred_element_type=F32)


def _rms(x, g):
    ms = jnp.mean(x * x, axis=-1, keepdims=True)
    return x * lax.rsqrt(ms + EPS) * g


def _exp_scores(s):
    m = s[:, 0:LANES]
    for t in range(1, s.shape[1] // LANES):
        m = jnp.maximum(m, s[:, LANES * t:LANES * (t + 1)])
    return jnp.exp2(s - jnp.max(m, axis=-1, keepdims=True)).astype(BF16)


def _weighted(e, v_ones):
    r = _dot(e, v_ones)
    return r[:, 0:LANES], 1.0 / r[:, LANES:2 * LANES]


def _lane_mask(lo, hi):
    lane = lax.broadcasted_iota(jnp.int32, (1, LANES), 1)
    return jnp.where((lane >= lo) & (lane < hi), 1.0, 0.0).astype(F32)


def _rope(x, tables, q):
    cos, sin_signed = tables
    lane = lax.broadcasted_iota(jnp.int32, (1, LANES), 1)
    first = (lane & (2 * q - 1)) < q
    tiles = []
    for t in range(x.shape[1] // LANES):
        xt = x[:, LANES * t:LANES * (t + 1)]
        rot = jnp.where(first, pltpu.roll(xt, LANES - q, 1), pltpu.roll(xt, q, 1))
        tiles.append(xt * cos + rot * sin_signed)
    return tiles[0] if len(tiles) == 1 else jnp.concatenate(tiles, axis=1)


def _attend(q, dq, keys, kcat_ref, vm_ref, dka_ref, dvb_ref, lam, subln, lam_init):
    qr = q[:, 2 * LANES:3 * LANES]
    o_mla = []
    for t in range(2):
        qcat = jnp.concatenate([q[:, LANES * t:LANES * (t + 1)], qr], axis=1)
        acc = None
        for hh in range(2):
            h = 2 * t + hh
            o, rinv = _weighted(_exp_scores(_dot_nt(qcat, kcat_ref[h, keys, :])),
                                vm_ref[h, keys, :])
            acc = o * rinv if acc is None else acc + o * rinv
        o_mla.append(acc)
    o_diff = []
    for h in range(DIFF_HEADS):
        qt = dq[:, LANES * h:LANES * (h + 1)]
        first_half = lax.broadcasted_iota(jnp.int32, (1, LANES), 1) < LANES // 2
        zero = jnp.zeros_like(qt)
        o0, rinv0 = _weighted(_exp_scores(_dot_nt(jnp.where(first_half, qt, zero),
                                                  dka_ref[h, keys, :])), dvb_ref[h, keys, :])
        o1, rinv1 = _weighted(_exp_scores(_dot_nt(jnp.where(first_half, zero, qt),
                                                  dka_ref[h, keys, :])), dvb_ref[h, keys, :])
        o = o0 * rinv0 - o1 * (lam * rinv1)
        o_diff.append(_rms(o, subln) * (1.0 - lam_init))
    return jnp.concatenate(o_mla, axis=1), jnp.concatenate(o_diff, axis=1)


def _proj(wl, w, scr, r0, r1):
    return _dot_nt(scr["hb"][...], w["w_in_t"][wl, r0 - R_CB:r1 - R_CB, :])


def _fill_ones(scr, b, h):
    vm_ref, dvb_ref = scr["vm"], scr["dvb"]
    ones = jnp.ones((vm_ref.shape[2], LANES), BF16)
    dvb_ref[b, h, :, LANES:2 * LANES] = ones
    vm_ref[b, h, :, LANES:2 * LANES] = ones


def _layer_front(l, wl, x, n_seq, mod, w, scr, rope, ctx, state, fill_ones=True):
    m_rows = x.shape[0]
    n_b = m_rows // n_seq
    n_ctx = 0 if ctx is None else ctx["ckv"].shape[0]
    nk = n_ctx + n_seq
    shift, scale, _ = mod
    hb_ref = scr["hb"]
    qb_ref, dqb_ref = scr["qb"], scr["dqb"]
    proj = functools.partial(_proj, wl, w, scr)
    kcat_ref, vm_ref, dka_ref, dvb_ref = scr["kcat"], scr["vm"], scr["dka"], scr["dvb"]
    w_ukv = w["w_ukv"]
    half = (_lane_mask(0, LANES // 2), _lane_mask(LANES // 2, LANES))
    new_keys, old_keys = pl.ds(n_ctx, n_seq), pl.ds(0, n_ctx)
    if rope is not None:
        rope_m = tuple(r[...] for r in rope[0])
        rope_d = tuple(r[...] for r in rope[1])

    def seq_rows(b):
        return slice(b * n_seq, (b + 1) * n_seq)

    gs = w["norm_g"][l:l + 1, :] * (1.0 + scale)
    ms = jnp.mean(x * x, axis=-1, keepdims=True)
    hb_ref[...] = (x * lax.rsqrt(ms + EPS) * gs + shift).astype(BF16)

    head = _dot_nt(hb_ref[...], w["w_head"][wl])

    cqn = _rms(head[:, 0:MLA_Q_LORA], w["q_norm"][l:l + 1, :]).astype(BF16)
    q = _dot(cqn, w["w_uq"][wl]) * MLA_SCALE
    if rope is not None:
        q = jnp.concatenate(
            [q[:, 0:2 * LANES], _rope(q[:, 2 * LANES:3 * LANES], rope_m, MLA_ROPE // 4)], axis=1)
    qb_ref[...] = q.astype(BF16)

    def fill_kv(b, dst, kv):
        for h in range(MLA_HEADS):
            t, hh = divmod(h, 2)
            kcat_ref[b, h, dst, 0:LANES] = (kv[:, LANES * t:LANES * (t + 1)] * half[hh]).astype(BF16)
            vm_ref[b, h, dst, 0:LANES] = (
                kv[:, 2 * LANES + LANES * t:2 * LANES + LANES * (t + 1)] * half[hh]).astype(BF16)

    def fill_kr(b, dst, kr):
        for h in range(MLA_HEADS):
            kcat_ref[b, h, dst, LANES:2 * LANES] = (
                kr * _lane_mask(MLA_ROPE * h, MLA_ROPE * (h + 1))).astype(BF16)

    def fill_dk(b, dst, dk):
        for h in range(DIFF_HEADS):
            dka_ref[b, h, dst, :] = dk[:, LANES * h:LANES * (h + 1)].astype(BF16)

    ckv = _rms(head[:, R_KV:R_KR], w["kv_norm"][l:l + 1, :])
    kv = _dot(ckv.astype(BF16), w_ukv[wl])
    for b in range(n_b):
        if state is not None:
            state[0][b, l] = ckv[seq_rows(b)]
        fill_kv(b, new_keys, kv[seq_rows(b)])
        if ctx is not None:
            fill_kv(b, old_keys, _dot(ctx["ckv"][...].astype(BF16), w_ukv[wl]))

    krr = head[:, R_KR:R_KR + LANES]
    krr_pos = krr if rope is None else _rope(krr, rope_m, MLA_ROPE // 4)
    for b in range(n_b):
        if state is not None:
            state[1][b, l] = krr[seq_rows(b)].T[0:MLA_ROPE, :]
        fill_kr(b, new_keys, krr_pos[seq_rows(b)])
        if ctx is not None:
            fill_kr(b, old_keys, jnp.concatenate([ctx["kr"][...]] * (LANES // MLA_ROPE), axis=0).T)

    dq = proj(R_DQ, R_DK) * DIFF_SCALE
    if rope is not None:
        dq = _rope(dq, rope_d, DIFF_QK // 4)
    dqb_ref[...] = dq.astype(BF16)

    dk = proj(R_DK, R_DV)
    dk_pos = dk if rope is None else _rope(dk, rope_d, DIFF_QK // 4)
    for b in range(n_b):
        if state is not None:
            state[2][b, l] = dk[seq_rows(b)].T
        fill_dk(b, new_keys, dk_pos[seq_rows(b)])
        if ctx is not None:
            fill_dk(b, old_keys, ctx["dk"][...].T)

    dv = proj(R_DV, R_Z)
    for b in range(n_b):
        for h in range(DIFF_HEADS):
            dv_h = dv[seq_rows(b), LANES * h:LANES * (h + 1)]
            if state is not None:
                state[3][b, l, pl.ds(h, n_seq, stride=DIFF_HEADS), :] = dv_h
            dvb_ref[b, h, new_keys, 0:LANES] = dv_h.astype(BF16)
            if ctx is not None:
                dvb_ref[b, h, old_keys, 0:LANES] = (
                    ctx["dv"][pl.ds(h, n_ctx, stride=DIFF_HEADS), :].astype(BF16))
            if fill_ones:
                _fill_ones(scr, b, h)


def _attend_block(l, b, start, tq, nk, w, scr):
    lf = w["lam"][l]
    lam = (jnp.exp(jnp.sum(lf[0:1] * lf[1:2], axis=-1, keepdims=True))
           - jnp.exp(jnp.sum(lf[2:3] * lf[3:4], axis=-1, keepdims=True)) + _lambda_init(l))
    qrows = pl.ds(start, tq)
    return _attend(scr["qb"][qrows, :], scr["dqb"][qrows, :], pl.ds(0, nk),
                   scr["kcat"].at[b], scr["vm"].at[b], scr["dka"].at[b], scr["dvb"].at[b],
                   lam, w["subln"][l:l + 1, :], _lambda_init(l))


def _conv_branch(l, wl, m_rows, n_seq, w, scr):
    conv = _proj(wl, w, scr, R_CB, R_DQ)
    u = conv[:, W_CONV:2 * W_CONV] * conv[:, 2 * W_CONV:3 * W_CONV]
    pos = lax.broadcasted_iota(jnp.int32, (m_rows, 1), 0) % n_seq
    u_prev = jnp.where(pos == 0, 0.0, pltpu.roll(u, 1, 0))
    u_next = jnp.where(pos == n_seq - 1, 0.0, pltpu.roll(u, m_rows - 1, 0))
    cw = w["conv_w"][:, l, :]
    return conv[:, 0:W_CONV] * (u_prev * cw[0:1] + u * cw[1:2] + u_next * cw[2:3])


def _silu_gate(wl, w, scr, c0, c1):
    z = _proj(wl, w, scr, R_Z + c0, R_Z + c1)
    return z * jax.nn.sigmoid(z)


def _trunk_layer(l, x, n_seq, mod, w, scr, state):
    m_rows = x.shape[0]
    o_ref, mix_ref = scr["o"], scr["mix"]
    _layer_front(l, l, x, n_seq, mod, w, scr, None, None, state, fill_ones=False)
    for b in range(m_rows // n_seq):
        for start in range(b * n_seq, (b + 1) * n_seq, Q_TILE):
            o_mla, o_diff = _attend_block(l, b, start, Q_TILE, n_seq, w, scr)
            o_ref[start:start + Q_TILE, 0:W_MLA] = o_mla
            o_ref[start:start + Q_TILE, W_MLA + W_CONV:D_MIX] = o_diff
    o_ref[:, W_MLA:W_MLA + W_CONV] = _conv_branch(l, l, m_rows, n_seq, w, scr)
    for c0 in range(0, D_MIX, 512):
        mix_ref[:, c0:c0 + 512] = (
            o_ref[:, c0:c0 + 512] * _silu_gate(l, w, scr, c0, c0 + 512)).astype(BF16)
    return x + mod[2] * _dot(mix_ref[...], w["w_out"][l])


_W_NAMES = ("norm_g", "w_in_t", "w_head", "q_norm", "w_uq", "kv_norm", "w_ukv", "conv_w", "lam",
            "subln", "w_out", "final_norm")
_PER_LAYER_BLOCKS = ("w_in_t", "w_head", "w_uq", "w_ukv", "w_out")
_SCR_NAMES = ("hb", "o", "mix", "qb", "dqb", "kcat", "vm", "dka", "dvb")
_CTX_NAMES = ("ckv", "kr", "dk", "dv")
N_ROPE = 4


_BIG_NAMES = ("w_in_t", "w_head", "w_out")
_SMALL_NAMES = tuple(n for n in _W_NAMES if n not in _BIG_NAMES)
W_IN_CHUNK = 208
STAGE_ROWS = 256
assert D_IN % W_IN_CHUNK == 0 and R_CB % W_IN_CHUNK == 0 and R_CB - R_KR <= W_IN_CHUNK


def _stage_weights(w_in_hbm, w_out_hbm, big, slots, sem):
    jobs = []
    for l in range(DEPTH):
        jobs += [(w_in_hbm, big["w_in_t"], l, r, W_IN_CHUNK) for r in range(0, D_IN, W_IN_CHUNK)]
        jobs += [(w_out_hbm, big["w_out"], l, r, STAGE_ROWS) for r in range(0, D_MIX, STAGE_ROWS)]
    n_slots = len(slots)

    def copy(k):
        src, _, l, r, n = jobs[k]
        return pltpu.make_async_copy(src.at[l, pl.ds(r, n)], slots[k % n_slots].at[pl.ds(0, n)],
                                     sem.at[k % n_slots])

    for k in range(n_slots):
        copy(k).start()
    head = big["w_head"]
    for k, (src, dst, l, r, n) in enumerate(jobs):
        copy(k).wait()
        slot = slots[k % n_slots]
        if src is w_out_hbm:
            dst[l, r:r + n, :] = slot[0:n, :].astype(BF16)
        elif r >= R_CB:
            dst[l, r - R_CB:r - R_CB + n, :] = slot[0:n, :].astype(BF16)
        else:
            n_main = min(r + n, R_KR) - r
            head[l, r:r + n_main, :] = slot[0:n_main, :].astype(BF16)
            if r + n > R_KR:
                k_rope = slot[R_KR - r:R_CB - r, :].astype(BF16)
                for i in range(LANES // MLA_ROPE):
                    head[l, R_KR + MLA_ROPE * i:R_KR + MLA_ROPE * (i + 1), :] = k_rope
        if k + n_slots < len(jobs):
            copy(k + n_slots).start()


def _ctx_kernel(x_ref, mod_ref, *refs, n_seq):
    n_small, n_big, n_scr = len(_SMALL_NAMES), len(_BIG_NAMES), len(_SCR_NAMES)
    w = dict(zip(_SMALL_NAMES, refs[:n_small]))
    w_in_hbm, w_out_hbm = refs[n_small:n_small + 2]
    rest = refs[n_small + 2:]
    y_ref, state = rest[0], rest[1:5]
    big_hbm = rest[5:5 + n_big]
    scr = dict(zip(_SCR_NAMES, rest[5 + n_big:5 + n_big + n_scr]))
    big = dict(zip(_BIG_NAMES, rest[5 + n_big + n_scr:5 + 2 * n_big + n_scr]))
    in_sem, out_sem = rest[5 + 2 * n_big + n_scr:]
    w.update(big)

    def hand_off(k):
        return pltpu.make_async_copy(big[_BIG_NAMES[k]], big_hbm[k], out_sem.at[k])

    first_step = pl.program_id(0) == 0

    @pl.when(first_step)
    def _():
        slots = [scr["o"].at[pl.ds(r, STAGE_ROWS)] for r in range(0, scr["o"].shape[0], STAGE_ROWS)]
        slots += [y_ref.at[b] for b in range(y_ref.shape[0])]
        _stage_weights(w_in_hbm, w_out_hbm, big, slots, in_sem)
        for k in range(n_big):
            hand_off(k).start()
        for b in range(x_ref.shape[0]):
            for h in range(DIFF_HEADS):
                _fill_ones(scr, b, h)

    n_b = x_ref.shape[0]
    x = x_ref[...].reshape(n_b * n_seq, D_MODEL)
    for l in range(DEPTH):
        mod = tuple(mod_ref[l, j, 0:1, :] for j in range(3))
        x = _trunk_layer(l, x, n_seq, mod, w, scr, state)
    y_ref[...] = _rms(x, w["final_norm"][...]).reshape(n_b, n_seq, D_MODEL)

    @pl.when(first_step)
    def _():
        for k in range(n_big):
            hand_off(k).wait()


def _lat_kernel(x_ref, mod_ref, *refs, n_seq, layer):
    ctx = {k: r.at[0, 0] for k, r in zip(_CTX_NAMES, refs[:4])}
    rope = (refs[4:6], refs[6:8])
    w = dict(zip(_W_NAMES, refs[4 + N_ROPE:4 + N_ROPE + len(_W_NAMES)]))
    rest = refs[4 + N_ROPE + len(_W_NAMES):]
    y_ref = rest[0]
    scr = dict(zip(_SCR_NAMES, rest[1:]))
    cond_row = pl.ds(1 + pl.program_id(0), 1)
    mod = tuple(mod_ref[layer, j, cond_row, :] for j in range(3))
    zg_ref, mix_ref = scr["o"], scr["mix"]
    n_ctx = ctx["ckv"].shape[0]
    conv_cols = slice(W_MLA, W_MLA + W_CONV)

    @pl.when((pl.program_id(0) == 0) & (pl.program_id(1) == 0))
    def _():
        for h in range(DIFF_HEADS):
            _fill_ones(scr, 0, h)

    @pl.when(pl.program_id(1) == 0)
    def _():
        _layer_front(layer, 0, x_ref[0], n_seq, mod, w, scr, rope, ctx, None, fill_ones=False)
        for c0 in range(0, D_MIX, 512):
            zg_ref[:, c0:c0 + 512] = _silu_gate(0, w, scr, c0, c0 + 512)
        mix_ref[:, conv_cols] = (
            _conv_branch(layer, 0, n_seq, n_seq, w, scr) * zg_ref[:, conv_cols]).astype(BF16)

    start = pl.multiple_of(pl.program_id(1) * LAT_Q_TILE, LAT_Q_TILE)
    qrows = pl.ds(start, LAT_Q_TILE)
    o_mla, o_diff = _attend_block(layer, 0, start, LAT_Q_TILE, n_ctx + n_seq, w, scr)
    mix_ref[qrows, 0:W_MLA] = (o_mla * zg_ref[qrows, 0:W_MLA]).astype(BF16)
    mix_ref[qrows, W_MLA + W_CONV:D_MIX] = (
        o_diff * zg_ref[qrows, W_MLA + W_CONV:D_MIX]).astype(BF16)
    x = x_ref[0, qrows, :] + mod[2] * _dot(mix_ref[qrows, :], w["w_out"][0])
    if layer == DEPTH - 1:
        x = _rms(x, w["final_norm"][...])
    y_ref[0] = x


def _prep_kernel(cctx_ref, c_ref, ada_w_ref, ada_b_ref, w_uq_ref, w_ukv_ref,
                 mod_ref, w_uq_p_ref, w_ukv_p_ref):
    n_rows = mod_ref.shape[2]
    pad = jnp.zeros((n_rows - 1 - c_ref.shape[0], D_MODEL), F32)
    c = jnp.concatenate([cctx_ref[...], c_ref[...], pad], axis=0)
    s = (c * jax.nn.sigmoid(c)).astype(BF16)
    mod_ref[0, 0] = (_dot(s, ada_w_ref[0].astype(BF16))
                     + ada_b_ref[pl.ds(pl.program_id(0), 1), :])

    @pl.when(pl.program_id(1) == 0)
    def _():
        for src, dst, first in ((w_uq_ref, w_uq_p_ref, MLA_NOPE), (w_ukv_ref, w_ukv_p_ref, MLA_NOPE)):
            a = src[0]
            per = a.shape[1] // MLA_HEADS
            cols = ([a[:, per * h:per * h + first] for h in range(MLA_HEADS)]
                    + [a[:, per * h + first:per * (h + 1)] for h in range(MLA_HEADS)])
            dst[0] = jnp.concatenate(cols, axis=1).astype(BF16)


def _const_spec(shape, layer=None):
    if layer is None:
        zeros = (0,) * len(shape)
        return pl.BlockSpec(shape, lambda *_: zeros, pipeline_mode=pl.Buffered(1))
    idx = (layer,) + (0,) * (len(shape) - 1)
    return pl.BlockSpec((1,) + tuple(shape[1:]), lambda *_: idx, pipeline_mode=pl.Buffered(1))


def _scratch(n_b, n_seq, nk):
    m_rows = n_b * n_seq
    return [
        pltpu.VMEM((m_rows, D_MODEL), BF16),
        pltpu.VMEM((m_rows, D_MIX), F32),
        pltpu.VMEM((m_rows, D_MIX), BF16),
        pltpu.VMEM((m_rows, 3 * LANES), BF16),
        pltpu.VMEM((m_rows, W_DIFF), BF16),
        pltpu.VMEM((n_b, MLA_HEADS, nk, 2 * LANES), BF16),
        pltpu.VMEM((n_b, MLA_HEADS, nk, 2 * LANES), BF16),
        pltpu.VMEM((n_b, 2 * DIFF_HEADS, nk, LANES), BF16),
        pltpu.VMEM((n_b, DIFF_HEADS, nk, 2 * LANES), BF16),
    ]


def _rope_tables(n, rot_dim):
    half = rot_dim // 2
    inv = ROPE_THETA ** (-(np.arange(0, half, 2, dtype=np.float32) / half))
    pos = np.arange(n)
    ang_r = (pos // GRID_W).astype(np.float32)[:, None] * inv
    ang_c = (pos % GRID_W).astype(np.float32)[:, None] * inv
    ang = np.concatenate([ang_r, ang_r, ang_c, ang_c], axis=-1)
    cos, sin = np.cos(ang), np.sin(ang)
    q = rot_dim // 4
    first = (np.arange(rot_dim) % (2 * q)) < q
    sin_signed = np.where(first, -sin, sin)
    reps = LANES // rot_dim
    return tuple(jnp.asarray(np.tile(t, (1, reps)), dtype=F32) for t in (cos, sin_signed))


def kernel(x_prompt, x_sample, c, cache_mla_ckv, cache_mla_krope, cache_diff_k, cache_diff_v,
           c_ctx, norm_g, ada_w, ada_b, w_in, mla_q_norm, w_uq, mla_kv_norm, w_ukv,
           conv_w, diff_lambda, diff_subln, w_out, final_norm):
    batch, seq, _ = x_prompt.shape
    dec_batch, dec_seq, _ = x_sample.shape
    past = cache_mla_ckv.shape[2]
    cparams = pltpu.CompilerParams(dimension_semantics=("arbitrary",),
                                   vmem_limit_bytes=VMEM_LIMIT_BYTES)

    n_cond = 8
    per_layer_spec = lambda a: pl.BlockSpec((1,) + a.shape[1:], lambda l, j: (l, 0, 0))
    mod, w_uq_p, w_ukv_p = pl.pallas_call(
        _prep_kernel,
        grid=(DEPTH, 3),
        in_specs=[pl.BlockSpec((1, D_MODEL), lambda l, j: (0, 0)),
                  pl.BlockSpec((dec_batch, D_MODEL), lambda l, j: (0, 0)),
                  pl.BlockSpec((1, D_MODEL, D_MODEL), lambda l, j: (l, 0, j)),
                  pl.BlockSpec((DEPTH, D_MODEL), lambda l, j: (0, j)),
                  per_layer_spec(w_uq), per_layer_spec(w_ukv)],
        out_specs=[pl.BlockSpec((1, 1, n_cond, D_MODEL), lambda l, j: (l, j, 0, 0)),
                   per_layer_spec(w_uq), per_layer_spec(w_ukv)],
        out_shape=[jax.ShapeDtypeStruct((DEPTH, 3, n_cond, D_MODEL), F32),
                   jax.ShapeDtypeStruct(w_uq.shape, BF16),
                   jax.ShapeDtypeStruct(w_ukv.shape, BF16)],
        compiler_params=pltpu.CompilerParams(dimension_semantics=("arbitrary", "arbitrary")),
        name="prep",
    )(c_ctx.reshape(1, D_MODEL), c, ada_w, ada_b, w_uq, w_ukv)

    tb = 2
    assert seq == STAGE_ROWS and (tb * seq) % STAGE_ROWS == 0
    n_slots = tb * seq // STAGE_ROWS + tb
    small = dict(zip(_SMALL_NAMES, (
        norm_g, mla_q_norm, w_uq_p, mla_kv_norm, w_ukv_p, jnp.swapaxes(conv_w, 0, 1), diff_lambda,
        diff_subln, final_norm.reshape(1, D_MODEL))))
    big_shapes = dict(zip(_BIG_NAMES, ((DEPTH, D_IN - R_CB, D_MODEL), (DEPTH, R_KR + LANES, D_MODEL),
                                       (DEPTH, D_MIX, D_MODEL))))
    hbm_spec = pl.BlockSpec(memory_space=pl.ANY)
    state_shapes = ((seq, MLA_KV_LORA), (MLA_ROPE, seq), (W_DIFF, seq),
                    (DIFF_HEADS * seq, DIFF_V))
    y_prompt, s_ckv, s_kr, s_dk, s_dv, w_in_t, w_head, w_out_b = pl.pallas_call(
        functools.partial(_ctx_kernel, n_seq=seq),
        grid=(batch // tb,),
        in_specs=[pl.BlockSpec((tb, seq, D_MODEL), lambda i: (i, 0, 0)), _const_spec(mod.shape)]
        + [_const_spec(a.shape) for a in small.values()] + [hbm_spec, hbm_spec],
        out_specs=[pl.BlockSpec((tb, seq, D_MODEL), lambda i: (i, 0, 0))]
        + [pl.BlockSpec((tb, DEPTH) + s, lambda i: (i, 0, 0, 0)) for s in state_shapes]
        + [hbm_spec] * len(_BIG_NAMES),
        out_shape=[jax.ShapeDtypeStruct((batch, seq, D_MODEL), F32)]
        + [jax.ShapeDtypeStruct((batch, DEPTH) + s, F32) for s in state_shapes]
        + [jax.ShapeDtypeStruct(big_shapes[n], BF16) for n in _BIG_NAMES],
        scratch_shapes=_scratch(tb, seq, seq)
        + [pltpu.VMEM(big_shapes[n], BF16) for n in _BIG_NAMES]
        + [pltpu.SemaphoreType.DMA((n_slots,)), pltpu.SemaphoreType.DMA((len(_BIG_NAMES),))],
        compiler_params=cparams,
        name="ctx",
    )(x_prompt, mod, *small.values(), jnp.swapaxes(w_in, 1, 2), w_out)

    weights = tuple(dict(small, w_in_t=w_in_t, w_head=w_head, w_out=w_out_b)[n] for n in _W_NAMES)

    lat_params = pltpu.CompilerParams(dimension_semantics=("arbitrary", "arbitrary"),
                                      vmem_limit_bytes=VMEM_LIMIT_BYTES)
    tables = _rope_tables(dec_seq, MLA_ROPE) + _rope_tables(dec_seq, DIFF_QK)
    ctx_in = (cache_mla_ckv,
              jnp.swapaxes(cache_mla_krope, 2, 3),
              jnp.transpose(cache_diff_k, (0, 1, 3, 4, 5, 2)).reshape(dec_batch, DEPTH, W_DIFF, past),
              cache_diff_v.reshape(dec_batch, DEPTH, past * DIFF_HEADS, DIFF_V))
    h_lat = x_sample
    for l in range(DEPTH):
        h_lat = pl.pallas_call(
            functools.partial(_lat_kernel, n_seq=dec_seq, layer=l),
            grid=(dec_batch, dec_seq // LAT_Q_TILE),
            in_specs=[pl.BlockSpec((1, dec_seq, D_MODEL), lambda i, j: (i, 0, 0)),
                      _const_spec(mod.shape)]
            + [pl.BlockSpec((1, 1) + a.shape[2:], lambda i, j, l=l: (i, l, 0, 0))
               for a in ctx_in]
            + [_const_spec(t.shape) for t in tables]
            + [_const_spec(a.shape, layer=l if name in _PER_LAYER_BLOCKS else None)
               for name, a in zip(_W_NAMES, weights)],
            out_specs=pl.BlockSpec((1, LAT_Q_TILE, D_MODEL), lambda i, j: (i, j, 0)),
            out_shape=jax.ShapeDtypeStruct((dec_batch, dec_seq, D_MODEL), F32),
            scratch_shapes=_scratch(1, dec_seq, past + dec_seq),
            compiler_params=lat_params,
            name=f"latent_{l}",
        )(h_lat, mod, *ctx_in, *tables, *weights)

    return (y_prompt, h_lat, s_ckv,
            jnp.swapaxes(s_kr, 2, 3),
            jnp.transpose(s_dk.reshape(batch, DEPTH, DIFF_HEADS, 2, DIFF_QK, seq),
                          (0, 1, 5, 2, 3, 4)),
            s_dv.reshape(batch, DEPTH, seq, DIFF_HEADS, DIFF_V))
```
